```python
import jax, jax.numpy as jnp
from jax import lax
import numpy as np

D_MODEL = 1024
BATCH = 8
SEQ = 16384
DEPTH = 1

W_A = 3 * D_MODEL // 2
CONV_K = 3
CHUNK = 128
DH_B = 128
H_B = D_MODEL // DH_B
W_B = H_B * DH_B
D_FF = 4 * D_MODEL
N_PROJ = 3 * W_A + 2 * W_B + 2 * D_MODEL
LN_EPS = 1e-5
ALPHA = (2.0 * DEPTH) ** 0.25
BETA = (8.0 * DEPTH) ** -0.25

OFF_BA = 0
OFF_CA = OFF_BA + W_A
OFF_HA = OFF_CA + W_A
OFF_UB = OFF_HA + W_A
OFF_VB = OFF_UB + W_B
OFF_GA = OFF_VB + W_B
OFF_GB = OFF_GA + D_MODEL

kernel_name = "hybrid_shortconv_gmlp_deepnorm_encoder"


def _layernorm(x, g, b):
    xf = x.astype(jnp.float32)
    mu = jnp.mean(xf, axis=-1, keepdims=True)
    var = jnp.mean(jnp.square(xf - mu), axis=-1, keepdims=True)
    y = (xf - mu) * lax.rsqrt(var + LN_EPS) * g.astype(jnp.float32) + b.astype(jnp.float32)
    return y.astype(x.dtype)


def _centred_depthwise_conv(h, w):
    hp = jnp.pad(h, ((0, 0), (1, 1), (0, 0)))
    return hp[:, :-2, :] * w[0] + hp[:, 1:-1, :] * w[1] + hp[:, 2:, :] * w[2]


def _spatial_gate(u, v, g, b, w_s, b_s):
    bsz = v.shape[0]
    s = v.shape[1]
    n_chunks = s // CHUNK
    v = _layernorm(v, g, b)
    vc = v.reshape(bsz, n_chunks, CHUNK, H_B, DH_B)
    mixed = jnp.einsum('hij,bcjhd->bcihd', w_s, vc)
    mixed = mixed + jnp.transpose(b_s)[None, None, :, :, None]
    return u * mixed.reshape(bsz, s, W_B)


def _fwd_setup_inputs(seed: int = 0) -> dict:
    key = jax.random.key(seed)
    ks = jax.random.split(key, 20)
    L = DEPTH

    def nrm(k, shape, scale):
        return jax.random.normal(k, shape, jnp.float32) * scale

    return {
        "x": nrm(ks[0], (BATCH, SEQ, D_MODEL), 1.0),
        "w_in": nrm(ks[1], (L, D_MODEL, N_PROJ), D_MODEL ** -0.5),
        "b_gate": nrm(ks[2], (L, 2 * D_MODEL), 0.02),
        "conv_w": nrm(ks[3], (L, CONV_K, W_A), CONV_K ** -0.5),
        "v_norm_g": 1.0 + nrm(ks[4], (L, W_B), 0.02),
        "v_norm_b": nrm(ks[5], (L, W_B), 0.02),
        "w_s": nrm(ks[6], (L, H_B, CHUNK, CHUNK), CHUNK ** -0.5),
        "b_s": 1.0 + nrm(ks[7], (L, H_B, CHUNK), 0.02),
        "w_pa": nrm(ks[8], (L, W_A, D_MODEL), W_A ** -0.5),
        "w_pb": nrm(ks[9], (L, W_B, D_MODEL), W_B ** -0.5),
        "w_o": nrm(ks[10], (L, D_MODEL, D_MODEL), BETA * D_MODEL ** -0.5),
        "ln1_g": 1.0 + nrm(ks[11], (L, D_MODEL), 0.02),
        "ln1_b": nrm(ks[12], (L, D_MODEL), 0.02),
        "w_ff1": nrm(ks[13], (L, D_MODEL, D_FF), BETA * D_MODEL ** -0.5),
        "w_ff2": nrm(ks[14], (L, D_FF, D_MODEL), BETA * D_FF ** -0.5),
        "ln2_g": 1.0 + nrm(ks[15], (L, D_MODEL), 0.02),
        "ln2_b": nrm(ks[16], (L, D_MODEL), 0.02),
    }


def _fwd_reference(x, w_in, b_gate, conv_w, v_norm_g, v_norm_b, w_s, b_s, w_pa, w_pb, w_o,
              ln1_g, ln1_b, w_ff1, w_ff2, ln2_g, ln2_b):
    for l in range(DEPTH):
        p = jnp.einsum('bsd,dn->bsn', x, w_in[l])
        b_a = p[:, :, OFF_BA:OFF_CA]
        c_a = p[:, :, OFF_CA:OFF_HA]
        h_a = p[:, :, OFF_HA:OFF_UB]
        u_b = p[:, :, OFF_UB:OFF_VB]
        v_b = p[:, :, OFF_VB:OFF_GA]
        gates = jax.nn.sigmoid(p[:, :, OFF_GA:N_PROJ] + b_gate[l])
        g_a = gates[:, :, :D_MODEL]
        g_b = gates[:, :, D_MODEL:]
        a = b_a * _centred_depthwise_conv(c_a * h_a, conv_w[l])
        bb = _spatial_gate(jax.nn.gelu(u_b), jax.nn.gelu(v_b), v_norm_g[l], v_norm_b[l],
                           w_s[l], b_s[l])
        y_a = jnp.einsum('bsc,cd->bsd', a, w_pa[l])
        y_b = jnp.einsum('bsc,cd->bsd', bb, w_pb[l])
        mix = jnp.einsum('bsd,de->bse', g_a * y_a + g_b * y_b, w_o[l])
        x = _layernorm(ALPHA * x + mix, ln1_g[l], ln1_b[l])
        hid = jnp.square(jax.nn.relu(jnp.einsum('bsd,df->bsf', x, w_ff1[l])))
        ffn = jnp.einsum('bsf,fd->bsd', hid, w_ff2[l])
        x = _layernorm(ALPHA * x + ffn, ln2_g[l], ln2_b[l])
    return x


import jax as _jax
import jax.numpy as _jnp

TWIN_FORMAT = 'train_step'
FWD_PARAMS = ['x', 'w_in', 'b_gate', 'conv_w', 'v_norm_g', 'v_norm_b', 'w_s', 'b_s', 'w_pa', 'w_pb', 'w_o', 'ln1_g', 'ln1_b', 'w_ff1', 'w_ff2', 'ln2_g', 'ln2_b']
TWIN_WEIGHTS = ['w_in', 'b_gate', 'conv_w', 'v_norm_g', 'v_norm_b', 'w_s', 'b_s', 'w_pa', 'w_pb', 'w_o', 'ln1_g', 'ln1_b', 'w_ff1', 'w_ff2', 'ln2_g', 'ln2_b']
TWIN_DIFF_INPUT = 'x'
TWIN_INPUTS = ['x', 'w_in', 'b_gate', 'conv_w', 'v_norm_g', 'v_norm_b', 'w_s', 'b_s', 'w_pa', 'w_pb', 'w_o', 'ln1_g', 'ln1_b', 'w_ff1', 'w_ff2', 'ln2_g', 'ln2_b', 'loss_target', 'm_w_in', 'm_b_gate', 'm_conv_w', 'm_v_norm_g', 'm_v_norm_b', 'm_w_s', 'm_b_s', 'm_w_pa', 'm_w_pb', 'm_w_o', 'm_ln1_g', 'm_ln1_b', 'm_w_ff1', 'm_w_ff2', 'm_ln2_g', 'm_ln2_b', 'v_w_in', 'v_b_gate', 'v_conv_w', 'v_v_norm_g', 'v_v_norm_b', 'v_w_s', 'v_b_s', 'v_w_pa', 'v_w_pb', 'v_w_o', 'v_ln1_g', 'v_ln1_b', 'v_w_ff1', 'v_w_ff2', 'v_ln2_g', 'v_ln2_b']
TWIN_OUTPUTS = ['loss', 'grad_x', 'grad_w_in', 'grad_b_gate', 'grad_conv_w', 'grad_v_norm_g', 'grad_v_norm_b', 'grad_w_s', 'grad_b_s', 'grad_w_pa', 'grad_w_pb', 'grad_w_o', 'grad_ln1_g', 'grad_ln1_b', 'grad_w_ff1', 'grad_w_ff2', 'grad_ln2_g', 'grad_ln2_b', 'delta_w_in', 'delta_b_gate', 'delta_conv_w', 'delta_v_norm_g', 'delta_v_norm_b', 'delta_w_s', 'delta_b_s', 'delta_w_pa', 'delta_w_pb', 'delta_w_o', 'delta_ln1_g', 'delta_ln1_b', 'delta_w_ff1', 'delta_w_ff2', 'delta_ln2_g', 'delta_ln2_b', 'new_m_w_in', 'new_m_b_gate', 'new_m_conv_w', 'new_m_v_norm_g', 'new_m_v_norm_b', 'new_m_w_s', 'new_m_b_s', 'new_m_w_pa', 'new_m_w_pb', 'new_m_w_o', 'new_m_ln1_g', 'new_m_ln1_b', 'new_m_w_ff1', 'new_m_w_ff2', 'new_m_ln2_g', 'new_m_ln2_b', 'new_v_w_in', 'new_v_b_gate', 'new_v_conv_w', 'new_v_v_norm_g', 'new_v_v_norm_b', 'new_v_w_s', 'new_v_b_s', 'new_v_w_pa', 'new_v_w_pb', 'new_v_w_o', 'new_v_ln1_g', 'new_v_ln1_b', 'new_v_w_ff1', 'new_v_w_ff2', 'new_v_ln2_g', 'new_v_ln2_b']
TWIN_LEAF_KINDS = {'loss': 'loss', 'grad_x': 'grad_x', 'grad_w_in': 'grad_w', 'grad_b_gate': 'grad_w', 'grad_conv_w': 'grad_w', 'grad_v_norm_g': 'grad_w', 'grad_v_norm_b': 'grad_w', 'grad_w_s': 'grad_w', 'grad_b_s': 'grad_w', 'grad_w_pa': 'grad_w', 'grad_w_pb': 'grad_w', 'grad_w_o': 'grad_w', 'grad_ln1_g': 'grad_w', 'grad_ln1_b': 'grad_w', 'grad_w_ff1': 'grad_w', 'grad_w_ff2': 'grad_w', 'grad_ln2_g': 'grad_w', 'grad_ln2_b': 'grad_w', 'delta_w_in': 'delta_w', 'delta_b_gate': 'delta_w', 'delta_conv_w': 'delta_w', 'delta_v_norm_g': 'delta_w', 'delta_v_norm_b': 'delta_w', 'delta_w_s': 'delta_w', 'delta_b_s': 'delta_w', 'delta_w_pa': 'delta_w', 'delta_w_pb': 'delta_w', 'delta_w_o': 'delta_w', 'delta_ln1_g': 'delta_w', 'delta_ln1_b': 'delta_w', 'delta_w_ff1': 'delta_w', 'delta_w_ff2': 'delta_w', 'delta_ln2_g': 'delta_w', 'delta_ln2_b': 'delta_w', 'new_m_w_in': 'new_m', 'new_m_b_gate': 'new_m', 'new_m_conv_w': 'new_m', 'new_m_v_norm_g': 'new_m', 'new_m_v_norm_b': 'new_m', 'new_m_w_s': 'new_m', 'new_m_b_s': 'new_m', 'new_m_w_pa': 'new_m', 'new_m_w_pb': 'new_m', 'new_m_w_o': 'new_m', 'new_m_ln1_g': 'new_m', 'new_m_ln1_b': 'new_m', 'new_m_w_ff1': 'new_m', 'new_m_w_ff2': 'new_m', 'new_m_ln2_g': 'new_m', 'new_m_ln2_b': 'new_m', 'new_v_w_in': 'new_v', 'new_v_b_gate': 'new_v', 'new_v_conv_w': 'new_v', 'new_v_v_norm_g': 'new_v', 'new_v_v_norm_b': 'new_v', 'new_v_w_s': 'new_v', 'new_v_b_s': 'new_v', 'new_v_w_pa': 'new_v', 'new_v_w_pb': 'new_v', 'new_v_w_o': 'new_v', 'new_v_ln1_g': 'new_v', 'new_v_ln1_b': 'new_v', 'new_v_w_ff1': 'new_v', 'new_v_w_ff2': 'new_v', 'new_v_ln2_g': 'new_v', 'new_v_ln2_b': 'new_v'}


def _forward(args):
    return _fwd_reference(*[args[k] for k in FWD_PARAMS])


def _output_shape():
    def fwd():
        inp = _fwd_setup_inputs(0)
        return _fwd_reference(*[inp[k] for k in FWD_PARAMS])
    out = _jax.eval_shape(fwd)
    return out.shape, out.dtype

N_MICROBATCH = 1
ADAM_LR = 0.001
ADAM_B1 = 0.9
ADAM_B2 = 0.999
ADAM_EPS = 1e-08
ADAM_WD = 0.01
ADAM_STEP = 10
PER_EXAMPLE_BATCH_AXIS = {'x': 0, 'loss_target': 0}
SHARED_INPUTS = []
_WEIGHT_DTYPES = {'w_in': _jnp.float32, 'b_gate': _jnp.float32, 'conv_w': _jnp.float32, 'v_norm_g': _jnp.float32, 'v_norm_b': _jnp.float32, 'w_s': _jnp.float32, 'b_s': _jnp.float32, 'w_pa': _jnp.float32, 'w_pb': _jnp.float32, 'w_o': _jnp.float32, 'ln1_g': _jnp.float32, 'ln1_b': _jnp.float32, 'w_ff1': _jnp.float32, 'w_ff2': _jnp.float32, 'ln2_g': _jnp.float32, 'ln2_b': _jnp.float32}
MOMENT_SCALE = {'w_in': 6.757485e-02, 'b_gate': 3.986771e-02, 'conv_w': 7.816228e-02, 'v_norm_g': 6.015427e-02, 'v_norm_b': 6.087237e-02, 'w_s': 5.897094e-02, 'b_s': 6.440707e-02, 'w_pa': 9.069210e-02, 'w_pb': 1.123992e-01, 'w_o': 2.396200e-01, 'ln1_g': 4.209693e+00, 'ln1_b': 1.774789e+00, 'w_ff1': 7.349911e-02, 'w_ff2': 2.183442e-01, 'ln2_g': 1.279395e+02, 'ln2_b': 1.313346e+01}


def _to_microbatches(a, axis):
    t = _jnp.moveaxis(a, axis, 0)
    t = t.reshape((N_MICROBATCH, t.shape[0] // N_MICROBATCH) + t.shape[1:])
    return _jnp.moveaxis(t, 1, axis + 1)


def setup_inputs(seed: int = 0) -> dict:
    inp = _fwd_setup_inputs(seed)
    key = _jax.random.fold_in(_jax.random.key(seed), 7919)
    shape, _ = _output_shape()
    out = dict(inp)
    out["loss_target"] = _jax.random.normal(_jax.random.fold_in(key, 0), shape, _jnp.float32)
    for i, name in enumerate(TWIN_WEIGHTS):
        w = inp[name].astype(_jnp.float32)
        if MOMENT_SCALE is None:
            s = _jnp.sqrt(_jnp.mean(_jnp.square(w)) + 1e-30)
        else:
            s = MOMENT_SCALE[name]
        km, kv = _jax.random.split(_jax.random.fold_in(key, i + 1))
        out[name] = w
        out["m_" + name] = s * _jax.random.normal(km, w.shape, _jnp.float32)
        out["v_" + name] = (s * s) * _jax.random.uniform(kv, w.shape, _jnp.float32, 0.5, 1.5)
    if N_MICROBATCH > 1:
        for name, axis in PER_EXAMPLE_BATCH_AXIS.items():
            out[name] = _to_microbatches(out[name], axis)
    return {'x': out['x'], 'w_in': out['w_in'], 'b_gate': out['b_gate'], 'conv_w': out['conv_w'], 'v_norm_g': out['v_norm_g'], 'v_norm_b': out['v_norm_b'], 'w_s': out['w_s'], 'b_s': out['b_s'], 'w_pa': out['w_pa'], 'w_pb': out['w_pb'], 'w_o': out['w_o'], 'ln1_g': out['ln1_g'], 'ln1_b': out['ln1_b'], 'w_ff1': out['w_ff1'], 'w_ff2': out['w_ff2'], 'ln2_g': out['ln2_g'], 'ln2_b': out['ln2_b'], 'loss_target': out['loss_target'], 'm_w_in': out['m_w_in'], 'm_b_gate': out['m_b_gate'], 'm_conv_w': out['m_conv_w'], 'm_v_norm_g': out['m_v_norm_g'], 'm_v_norm_b': out['m_v_norm_b'], 'm_w_s': out['m_w_s'], 'm_b_s': out['m_b_s'], 'm_w_pa': out['m_w_pa'], 'm_w_pb': out['m_w_pb'], 'm_w_o': out['m_w_o'], 'm_ln1_g': out['m_ln1_g'], 'm_ln1_b': out['m_ln1_b'], 'm_w_ff1': out['m_w_ff1'], 'm_w_ff2': out['m_w_ff2'], 'm_ln2_g': out['m_ln2_g'], 'm_ln2_b': out['m_ln2_b'], 'v_w_in': out['v_w_in'], 'v_b_gate': out['v_b_gate'], 'v_conv_w': out['v_conv_w'], 'v_v_norm_g': out['v_v_norm_g'], 'v_v_norm_b': out['v_v_norm_b'], 'v_w_s': out['v_w_s'], 'v_b_s': out['v_b_s'], 'v_w_pa': out['v_w_pa'], 'v_w_pb': out['v_w_pb'], 'v_w_o': out['v_w_o'], 'v_ln1_g': out['v_ln1_g'], 'v_ln1_b': out['v_ln1_b'], 'v_w_ff1': out['v_w_ff1'], 'v_w_ff2': out['v_w_ff2'], 'v_ln2_g': out['v_ln2_g'], 'v_ln2_b': out['v_ln2_b']}


def _loss(weights, diff, rest, loss_target):
    with _jax.named_scope("forward"):
        args = {**rest, TWIN_DIFF_INPUT: diff, **{k: w.astype(_WEIGHT_DTYPES[k]) for k, w in weights.items()}}
        y = _forward(args)
    with _jax.named_scope("loss_head"):
        err = _jnp.square(y.astype(_jnp.float32) - loss_target)
        return 0.5 * _jnp.sum(_jnp.mean(err, axis=-1)) if err.ndim else 0.5 * err


def _adamw(w, g, m, v):
    m = ADAM_B1 * m + (1.0 - ADAM_B1) * g
    v = ADAM_B2 * v + (1.0 - ADAM_B2) * _jnp.square(g)
    m_hat = m / (1.0 - ADAM_B1 ** ADAM_STEP)
    v_hat = v / (1.0 - ADAM_B2 ** ADAM_STEP)
    delta = -ADAM_LR * (m_hat / (_jnp.sqrt(v_hat) + ADAM_EPS) + ADAM_WD * w)
    return delta, m, v


def reference(x, w_in, b_gate, conv_w, v_norm_g, v_norm_b, w_s, b_s, w_pa, w_pb, w_o, ln1_g, ln1_b, w_ff1, w_ff2, ln2_g, ln2_b, loss_target, m_w_in, m_b_gate, m_conv_w, m_v_norm_g, m_v_norm_b, m_w_s, m_b_s, m_w_pa, m_w_pb, m_w_o, m_ln1_g, m_ln1_b, m_w_ff1, m_w_ff2, m_ln2_g, m_ln2_b, v_w_in, v_b_gate, v_conv_w, v_v_norm_g, v_v_norm_b, v_w_s, v_b_s, v_w_pa, v_w_pb, v_w_o, v_ln1_g, v_ln1_b, v_w_ff1, v_w_ff2, v_ln2_g, v_ln2_b):
    given = dict(x=x, w_in=w_in, b_gate=b_gate, conv_w=conv_w, v_norm_g=v_norm_g, v_norm_b=v_norm_b, w_s=w_s, b_s=b_s, w_pa=w_pa, w_pb=w_pb, w_o=w_o, ln1_g=ln1_g, ln1_b=ln1_b, w_ff1=w_ff1, w_ff2=w_ff2, ln2_g=ln2_g, ln2_b=ln2_b, loss_target=loss_target, m_w_in=m_w_in, m_b_gate=m_b_gate, m_conv_w=m_conv_w, m_v_norm_g=m_v_norm_g, m_v_norm_b=m_v_norm_b, m_w_s=m_w_s, m_b_s=m_b_s, m_w_pa=m_w_pa, m_w_pb=m_w_pb, m_w_o=m_w_o, m_ln1_g=m_ln1_g, m_ln1_b=m_ln1_b, m_w_ff1=m_w_ff1, m_w_ff2=m_w_ff2, m_ln2_g=m_ln2_g, m_ln2_b=m_ln2_b, v_w_in=v_w_in, v_b_gate=v_b_gate, v_conv_w=v_conv_w, v_v_norm_g=v_v_norm_g, v_v_norm_b=v_v_norm_b, v_w_s=v_w_s, v_b_s=v_b_s, v_w_pa=v_w_pa, v_w_pb=v_w_pb, v_w_o=v_w_o, v_ln1_g=v_ln1_g, v_ln1_b=v_ln1_b, v_w_ff1=v_w_ff1, v_w_ff2=v_w_ff2, v_ln2_g=v_ln2_g, v_ln2_b=v_ln2_b)
    weights = {n: given[n] for n in TWIN_WEIGHTS}
    shared = {n: given[n] for n in SHARED_INPUTS}
    per_example = {n: given[n] for n in ['x']}
    grad_fn = _jax.value_and_grad(_loss, argnums=(0, 1))

    def one_microbatch(ex, loss_target):
        ex = dict(ex)
        diff = ex.pop(TWIN_DIFF_INPUT)
        return grad_fn(weights, diff, {**shared, **ex}, loss_target)

    if N_MICROBATCH == 1:
        loss, (grad_w, grad_x) = one_microbatch(per_example, given["loss_target"])
    else:
        def body(carry, xs):
            loss_sum, grad_sum = carry
            l_k, (gw_k, gx_k) = one_microbatch(xs[0], xs[1])
            with _jax.named_scope("update"):
                return (loss_sum + l_k, _jax.tree.map(_jnp.add, grad_sum, gw_k)), gx_k

        init = (_jnp.zeros((), _jnp.float32), _jax.tree.map(_jnp.zeros_like, weights))
        (loss, grad_w), grad_x = _jax.lax.scan(body, init, (per_example, given["loss_target"]))
    with _jax.named_scope("update"):
        delta_w, new_m, new_v = {}, {}, {}
        for n in TWIN_WEIGHTS:
            delta_w[n], new_m[n], new_v[n] = _adamw(weights[n], grad_w[n], given["m_" + n], given["v_" + n])
    return (loss, grad_x, *[grad_w[n] for n in TWIN_WEIGHTS], *[delta_w[n] for n in TWIN_WEIGHTS],
            *[new_m[n] for n in TWIN_WEIGHTS], *[new_v[n] for n in TWIN_WEIGHTS])
```

```python
import math

import jax
import jax.numpy as jnp
from jax import lax
from jax.experimental import pallas as pl
from jax.experimental.pallas import tpu as pltpu

F32 = jnp.float32
BF16 = jnp.bfloat16
N_DEV = 8
CHUNK = 128
LN_EPS = 1e-5
ALPHA = 2.0 ** 0.25
ADAM_LR, ADAM_B1, ADAM_B2, ADAM_EPS, ADAM_WD, ADAM_STEP = 0.001, 0.9, 0.999, 1e-08, 0.01, 10
F32_SUBLANES = 8
BF16_SUBLANES = 16
LANES = 128
VMEM_LIMIT = 56 * 1024 * 1024
MESH = pl.DeviceIdType.MESH
NT_DIMS = (((1,), (1,)), ((), ()))
TN_DIMS = (((0,), (0,)), ((), ()))
HBM_SPEC = pl.BlockSpec(memory_space=pltpu.HBM)


class _Dims:
    def __init__(self, t, d):
        self.T, self.D = t, d
        self.WA = 3 * d // 2
        self.NP = 3 * self.WA + 4 * d
        self.DFF = 4 * d
        self.H = d // CHUNK
        self.OFF_CA, self.OFF_HA = self.WA, 2 * self.WA
        self.OFF_UB = 3 * self.WA
        self.OFF_VB = self.OFF_UB + d
        self.OFF_GA = self.OFF_VB + d
        self.OFF_GB = self.OFF_GA + d
        self.shard_rows = (self.NP // N_DEV, self.WA // N_DEV, d // N_DEV, d // N_DEV, self.DFF // N_DEV, self.DFF // N_DEV)
        self.grad_rows = sum(self.shard_rows)
        self.conv_rows = BF16_SUBLANES * max(1, -(-(3 * (self.WA // N_DEV) * 2) // (BF16_SUBLANES * d)))
        self.pack_rows = self.grad_rows + self.conv_rows

    def shard_offset(self, i):
        return sum(self.shard_rows[:i])


def _params(sem=("arbitrary",), vmem=VMEM_LIMIT):
    return pltpu.CompilerParams(dimension_semantics=sem, vmem_limit_bytes=vmem)


def _mesh_pos():
    return lax.axis_index("x"), lax.axis_index("y"), lax.axis_index("c")


def _all_gather(shard):
    rows, cols = shard.shape

    def body(x_ref, out_ref, send_sems, recv_sems, local_sem):
        x, y, c = _mesh_pos()
        me, sibling = (x, y, c), (x, y, 1 - c)
        chips = [(1 - x, y), (x, 1 - y), (1 - x, 1 - y)]

        def slot(px, py, pc):
            return out_ref.at[4 * px + 2 * py + pc]

        def copy(k, block, to, src=None):
            return pltpu.make_async_remote_copy(
                src_ref=slot(*block) if src is None else src, dst_ref=slot(*block),
                send_sem=send_sems.at[k], recv_sem=recv_sems.at[k], device_id=to, device_id_type=MESH)

        mine = pltpu.make_async_copy(x_ref, slot(*me), local_sem)
        mine.start()
        first = [copy(0, me, sibling, src=x_ref)]
        first += [copy(1 + j, me, (*chip, c), src=x_ref) for j, chip in enumerate(chips)]
        for cp in first:
            cp.start()
        passed = [copy(4 + j, (*chip, c), sibling) for j, chip in enumerate(chips)]
        for j, chip in enumerate(chips):
            copy(1 + j, (*chip, c), me).wait_recv()
            passed[j].start()
        copy(0, sibling, me).wait_recv()
        for j, chip in enumerate(chips):
            copy(4 + j, (*chip, 1 - c), me).wait_recv()
        for cp in first + passed:
            cp.wait_send()
        mine.wait()

    return pl.pallas_call(
        body, name="all_gather_weights",
        out_shape=jax.ShapeDtypeStruct((N_DEV, rows, cols), shard.dtype),
        in_specs=[HBM_SPEC], out_specs=HBM_SPEC,
        scratch_shapes=[pltpu.SemaphoreType.DMA((7,)), pltpu.SemaphoreType.DMA((7,)), pltpu.SemaphoreType.DMA],
    )(shard)


def _exchange(grads, small):
    _, rows, cols = grads.shape
    srows, scols = small.shape

    def body(g_ref, s_ref, og_ref, os_ref, send_sems, recv_sems, local_sems):
        x, y, c = _mesh_pos()
        me = 4 * x + 2 * y + c
        own = [pltpu.make_async_copy(g_ref.at[me], og_ref.at[me], local_sems.at[0]),
               pltpu.make_async_copy(s_ref, os_ref.at[me], local_sems.at[1])]
        for cp in own:
            cp.start()
        sends, recvs = [], []
        for r in range(1, N_DEV):
            px = 1 - x if r & 4 else x
            py = 1 - y if r & 2 else y
            pc = 1 - c if r & 1 else c
            peer = 4 * px + 2 * py + pc
            k = r - 1
            sends.append(pltpu.make_async_remote_copy(
                src_ref=g_ref.at[peer], dst_ref=og_ref.at[me], send_sem=send_sems.at[k], recv_sem=recv_sems.at[k],
                device_id=(px, py, pc), device_id_type=MESH))
            sends.append(pltpu.make_async_remote_copy(
                src_ref=s_ref, dst_ref=os_ref.at[me], send_sem=send_sems.at[7 + k], recv_sem=recv_sems.at[7 + k],
                device_id=(px, py, pc), device_id_type=MESH))
            recvs.append(pltpu.make_async_remote_copy(
                src_ref=g_ref.at[me], dst_ref=og_ref.at[peer], send_sem=send_sems.at[k], recv_sem=recv_sems.at[k],
                device_id=(px, py, pc), device_id_type=MESH))
            recvs.append(pltpu.make_async_remote_copy(
                src_ref=s_ref, dst_ref=os_ref.at[peer], send_sem=send_sems.at[7 + k], recv_sem=recv_sems.at[7 + k],
                device_id=(px, py, pc), device_id_type=MESH))
        for cp in sends:
            cp.start()
        for cp in recvs:
            cp.wait_recv()
        for cp in sends:
            cp.wait_send()
        for cp in own:
            cp.wait()

    return pl.pallas_call(
        body, name="exchange_grads",
        out_shape=(jax.ShapeDtypeStruct((N_DEV, rows, cols), grads.dtype),
                   jax.ShapeDtypeStruct((N_DEV, srows, scols), small.dtype)),
        in_specs=[HBM_SPEC, HBM_SPEC], out_specs=(HBM_SPEC, HBM_SPEC),
        scratch_shapes=[pltpu.SemaphoreType.DMA((14,)), pltpu.SemaphoreType.DMA((14,)), pltpu.SemaphoreType.DMA((2,))],
    )(grads, small)


def _sum_slots(slots, tile_rows, name):
    _, rows, cols = slots.shape
    tr = tile_rows if rows % tile_rows == 0 else rows

    def body(s_ref, o_ref):
        acc = s_ref[0].astype(F32)
        for k in range(1, N_DEV):
            acc = acc + s_ref[k].astype(F32)
        o_ref[...] = acc

    return pl.pallas_call(
        body, name=name, grid=(rows // tr,),
        in_specs=[pl.BlockSpec((N_DEV, tr, cols), lambda i: (0, i, 0))],
        out_specs=pl.BlockSpec((tr, cols), lambda i: (i, 0)),
        out_shape=jax.ShapeDtypeStruct((rows, cols), F32),
        compiler_params=_params(),
    )(slots)


def _load_weight(wall_ref, dst_ref, sems, sem0, row_off, rows):
    copies = [pltpu.make_async_copy(wall_ref.at[k, pl.ds(row_off, rows), :], dst_ref.at[pl.ds(k * rows, rows), :],
                                    sems.at[sem0 + k]) for k in range(N_DEV)]
    for cp in copies:
        cp.start()
    return copies


def _gelu_and_grad(x):
    k0 = math.sqrt(2.0 / math.pi)
    k1 = 0.044715
    x2 = x * x
    th = jnp.tanh(k0 * x * (1.0 + k1 * x2))
    half = 0.5 * (1.0 + th)
    return x * half, half + 0.5 * x * (1.0 - th * th) * (k0 * (1.0 + 3.0 * k1 * x2))


def _ln_stats(r):
    mu = jnp.mean(r, axis=-1, keepdims=True)
    rc = r - mu
    var = jnp.mean(rc * rc, axis=-1, keepdims=True)
    rstd = lax.rsqrt(var + LN_EPS)
    return rc * rstd, rstd


def _ln_bwd(dxh, xh, rstd):
    return rstd * (dxh - jnp.mean(dxh, axis=-1, keepdims=True) - xh * jnp.mean(dxh * xh, axis=-1, keepdims=True))


def _colsum(a):
    return jnp.sum(a, axis=0, keepdims=True)


def _halo_maps(tm, t, unit):
    per, last = tm // unit, t // unit - 1
    return (lambda i: (jnp.maximum(i * per - 1, 0), 0)), (lambda i: (jnp.minimum((i + 1) * per, last), 0))


def _proj_in(x2, wall, b_gate, dm, tm):
    t, d, npj = dm.T, dm.D, dm.NP
    cw = d // 2
    rows = dm.shard_rows[0]

    def body(x_ref, wall_ref, bg_ref, p_ref, w_ref, sems):
        @pl.when(pl.program_id(0) == 0)
        def _():
            for cp in _load_weight(wall_ref, w_ref, sems, 0, dm.shard_offset(0), rows):
                cp.wait()

        xb = x_ref[...].astype(BF16)
        for n in range(npj // cw):
            lo = n * cw
            acc = lax.dot_general(xb, w_ref[lo:lo + cw, :], NT_DIMS, preferred_element_type=F32)
            if lo >= dm.OFF_GA:
                acc = jax.nn.sigmoid(acc + bg_ref[:, lo - dm.OFF_GA:lo - dm.OFF_GA + cw])
            p_ref[:, lo:lo + cw] = acc.astype(BF16)

    return pl.pallas_call(
        body, name="proj_in", grid=(t // tm,),
        in_specs=[pl.BlockSpec((tm, d), lambda i: (i, 0)), HBM_SPEC, pl.BlockSpec((1, 2 * d), lambda i: (0, 0))],
        out_specs=pl.BlockSpec((tm, npj), lambda i: (i, 0)),
        out_shape=jax.ShapeDtypeStruct((t, npj), BF16),
        scratch_shapes=[pltpu.VMEM((npj, d), BF16), pltpu.SemaphoreType.DMA((N_DEV,))],
        compiler_params=_params(),
    )(x2, wall, b_gate)


def _conv_taps(ext_ref, center, prev_blk, next_blk, first, last, tm):
    h = F32_SUBLANES
    ext_ref[0:h, :] = jnp.where(first, 0.0, prev_blk)
    ext_ref[h:h + tm, :] = center
    ext_ref[h + tm:h + tm + h, :] = jnp.where(last, 0.0, next_blk)
    return ext_ref[pl.ds(h - 1, tm), :], ext_ref[pl.ds(h + 1, tm), :]


def _spatial_mix(vn, ws_ref, bias_ref, mixed_ref, dm, tm):
    vb = vn.astype(BF16)
    for cc in range(tm // CHUNK):
        r0 = cc * CHUNK
        for h in range(dm.H):
            c0 = h * CHUNK
            m = jnp.dot(ws_ref[h], vb[r0:r0 + CHUNK, c0:c0 + CHUNK], preferred_element_type=F32)
            mixed_ref[r0:r0 + CHUNK, c0:c0 + CHUNK] = m + bias_ref[:, c0:c0 + CHUNK]
    return mixed_ref[...]


def _mixer_fwd(p, x2, wall, conv_w8, vng, vnb, ws_b, bias_s, dm, tm):
    t, d, wa, npj = dm.T, dm.D, dm.WA, dm.NP
    nt = t // tm
    hb = BF16_SUBLANES
    prev_map, next_map = _halo_maps(tm, t, hb)

    def body(p_ref, pp_ref, pn_ref, x_ref, wall_ref, cw_ref, vng_ref, vnb_ref, ws_ref, bias_ref,
             ya_ref, yb_ref, r1_ref, wpa_ref, wpb_ref, wo_ref, ext_ref, mixed_ref, sems):
        i = pl.program_id(0)

        @pl.when(i == 0)
        def _():
            cps = _load_weight(wall_ref, wpa_ref, sems, 0, dm.shard_offset(1), dm.shard_rows[1])
            cps += _load_weight(wall_ref, wpb_ref, sems, 8, dm.shard_offset(2), dm.shard_rows[2])
            cps += _load_weight(wall_ref, wo_ref, sems, 16, dm.shard_offset(3), dm.shard_rows[3])
            for cp in cps:
                cp.wait()

        def col(ref, lo, width):
            return ref[:, lo:lo + width].astype(F32)

        ch = col(p_ref, dm.OFF_CA, wa) * col(p_ref, dm.OFF_HA, wa)
        chp = (col(pp_ref, dm.OFF_CA, wa) * col(pp_ref, dm.OFF_HA, wa))[hb - F32_SUBLANES:hb]
        chn = (col(pn_ref, dm.OFF_CA, wa) * col(pn_ref, dm.OFF_HA, wa))[0:F32_SUBLANES]
        up, dn = _conv_taps(ext_ref, ch, chp, chn, i == 0, i == nt - 1, tm)
        a = col(p_ref, 0, wa) * (cw_ref[0:1, :] * up + cw_ref[1:2, :] * ch + cw_ref[2:3, :] * dn)
        ya = jnp.dot(a.astype(BF16), wpa_ref[...], preferred_element_type=F32)
        gv, _ = _gelu_and_grad(col(p_ref, dm.OFF_VB, d))
        xhv, _ = _ln_stats(gv)
        mixed = _spatial_mix(xhv * vng_ref[...] + vnb_ref[...], ws_ref, bias_ref, mixed_ref, dm, tm)
        gu, _ = _gelu_and_grad(col(p_ref, dm.OFF_UB, d))
        yb = jnp.dot((gu * mixed).astype(BF16), wpb_ref[...], preferred_element_type=F32)
        s = col(p_ref, dm.OFF_GA, d) * ya + col(p_ref, dm.OFF_GB, d) * yb
        mix = jnp.dot(s.astype(BF16), wo_ref[...], preferred_element_type=F32)
        ya_ref[...] = ya.astype(BF16)
        yb_ref[...] = yb.astype(BF16)
        r1_ref[...] = ALPHA * x_ref[...] + mix

    full = lambda i: (0, 0)
    tile = lambda i: (i, 0)
    return pl.pallas_call(
        body, name="mixer_fwd", grid=(nt,),
        in_specs=[pl.BlockSpec((tm, npj), tile), pl.BlockSpec((hb, npj), prev_map), pl.BlockSpec((hb, npj), next_map),
                  pl.BlockSpec((tm, d), tile), HBM_SPEC, pl.BlockSpec((F32_SUBLANES, wa), full),
                  pl.BlockSpec((1, d), full), pl.BlockSpec((1, d), full),
                  pl.BlockSpec((dm.H, CHUNK, CHUNK), lambda i: (0, 0, 0)), pl.BlockSpec((CHUNK, d), full)],
        out_specs=(pl.BlockSpec((tm, d), tile), pl.BlockSpec((tm, d), tile), pl.BlockSpec((tm, d), tile)),
        out_shape=(jax.ShapeDtypeStruct((t, d), BF16), jax.ShapeDtypeStruct((t, d), BF16), jax.ShapeDtypeStruct((t, d), F32)),
        scratch_shapes=[pltpu.VMEM((wa, d), BF16), pltpu.VMEM((d, d), BF16), pltpu.VMEM((d, d), BF16),
                        pltpu.VMEM((tm + 2 * F32_SUBLANES, wa), F32), pltpu.VMEM((tm, d), F32),
                        pltpu.SemaphoreType.DMA((3 * N_DEV,))],
        compiler_params=_params(),
    )(p, p, p, x2, wall, conv_w8, vng, vnb, ws_b, bias_s)


def _ffn_fwd_bwd(r1, tgt, wall, ln1g, ln1b, ln2g, ln2b, dm, tm):
    t, d, dff = dm.T, dm.D, dm.DFF
    fc = dff // N_DEV

    def body(r1_ref, tgt_ref, wall_ref, g1_ref, b1_ref, g2_ref, b2_ref,
             dr1_ref, act_ref, dh_ref, x1_ref, dr2_ref, sums_ref, w1_ref, w2_ref, relu_ref, sems):
        i = pl.program_id(0)

        @pl.when(i == 0)
        def _():
            cps = _load_weight(wall_ref, w1_ref, sems, 0, dm.shard_offset(4), dm.shard_rows[4])
            cps += _load_weight(wall_ref, w2_ref, sems, 8, dm.shard_offset(5), dm.shard_rows[5])
            sums_ref[...] = jnp.zeros_like(sums_ref)
            for cp in cps:
                cp.wait()

        xh1, rstd1 = _ln_stats(r1_ref[...])
        x1 = xh1 * g1_ref[...] + b1_ref[...]
        x1b = x1.astype(BF16)
        x1_ref[...] = x1b
        ffn = jnp.zeros((tm, d), F32)
        for k in range(dff // fc):
            ks = slice(k * fc, (k + 1) * fc)
            r = jnp.maximum(lax.dot_general(x1b, w1_ref[ks, :], NT_DIMS, preferred_element_type=F32), 0.0)
            ab = (r * r).astype(BF16)
            relu_ref[:, ks] = r.astype(BF16)
            act_ref[:, ks] = ab
            ffn = ffn + jnp.dot(ab, w2_ref[ks, :], preferred_element_type=F32)
        xh2, rstd2 = _ln_stats(ALPHA * x1 + ffn)
        diff = xh2 * g2_ref[...] + b2_ref[...] - tgt_ref[...]
        dy = diff * (1.0 / d)
        dr2 = _ln_bwd(dy * g2_ref[...], xh2, rstd2)
        dr2b = dr2.astype(BF16)
        dr2_ref[...] = dr2b
        dx1 = ALPHA * dr2
        for k in range(dff // fc):
            ks = slice(k * fc, (k + 1) * fc)
            dact = lax.dot_general(dr2b, w2_ref[ks, :], NT_DIMS, preferred_element_type=F32)
            dhb = (dact * (2.0 * relu_ref[:, ks].astype(F32))).astype(BF16)
            dh_ref[:, ks] = dhb
            dx1 = dx1 + jnp.dot(dhb, w1_ref[ks, :], preferred_element_type=F32)
        dr1_ref[...] = _ln_bwd(dx1 * g1_ref[...], xh1, rstd1)
        sums_ref[0:1, :] += _colsum(diff * diff)
        sums_ref[1:2, :] += _colsum(dy * xh2)
        sums_ref[2:3, :] += _colsum(dy)
        sums_ref[3:4, :] += _colsum(dx1 * xh1)
        sums_ref[4:5, :] += _colsum(dx1)

    full = lambda i: (0, 0)
    tile = lambda i: (i, 0)
    vec = pl.BlockSpec((1, d), full)
    return pl.pallas_call(
        body, name="ffn_fwd_bwd", grid=(t // tm,),
        in_specs=[pl.BlockSpec((tm, d), tile), pl.BlockSpec((tm, d), tile), HBM_SPEC, vec, vec, vec, vec],
        out_specs=(pl.BlockSpec((tm, d), tile), pl.BlockSpec((tm, dff), tile), pl.BlockSpec((tm, dff), tile),
                   pl.BlockSpec((tm, d), tile), pl.BlockSpec((tm, d), tile), pl.BlockSpec((F32_SUBLANES, d), full)),
        out_shape=(jax.ShapeDtypeStruct((t, d), F32), jax.ShapeDtypeStruct((t, dff), BF16), jax.ShapeDtypeStruct((t, dff), BF16),
                   jax.ShapeDtypeStruct((t, d), BF16), jax.ShapeDtypeStruct((t, d), BF16),
                   jax.ShapeDtypeStruct((F32_SUBLANES, d), F32)),
        scratch_shapes=[pltpu.VMEM((dff, d), BF16), pltpu.VMEM((dff, d), BF16), pltpu.VMEM((tm, dff), BF16),
                        pltpu.SemaphoreType.DMA((2 * N_DEV,))],
        compiler_params=_params(),
    )(r1, tgt, wall, ln1g, ln1b, ln2g, ln2b)


def _wgrad(lhs_a, rhs_a, lhs_b, rhs_b, tt, fb, name):
    t, f = lhs_a.shape
    d = rhs_a.shape[1]

    def body(la_ref, ra_ref, lb_ref, rb_ref, oa_ref, ob_ref):
        @pl.when(pl.program_id(1) == 0)
        def _():
            oa_ref[...] = jnp.zeros_like(oa_ref)
            ob_ref[...] = jnp.zeros_like(ob_ref)

        oa_ref[...] += lax.dot_general(la_ref[...], ra_ref[...].astype(BF16), TN_DIMS, preferred_element_type=F32)
        ob_ref[...] += lax.dot_general(lb_ref[...], rb_ref[...].astype(BF16), TN_DIMS, preferred_element_type=F32)

    lhs_spec = pl.BlockSpec((tt, fb), lambda j, i: (i, j))
    rhs_spec = pl.BlockSpec((tt, d), lambda j, i: (i, 0))
    out_spec = pl.BlockSpec((fb, d), lambda j, i: (j, 0))
    return pl.pallas_call(
        body, name=name, grid=(f // fb, t // tt),
        in_specs=[lhs_spec, rhs_spec, lhs_spec, rhs_spec], out_specs=(out_spec, out_spec),
        out_shape=(jax.ShapeDtypeStruct((f, d), F32), jax.ShapeDtypeStruct((f, d), F32)),
        compiler_params=_params(("arbitrary", "arbitrary")),
    )(lhs_a, rhs_a, lhs_b, rhs_b)


def _wgrad1(lhs, rhs, tt, fb, name):
    t, f = lhs.shape
    d = rhs.shape[1]

    def body(l_ref, r_ref, o_ref):
        @pl.when(pl.program_id(1) == 0)
        def _():
            o_ref[...] = jnp.zeros_like(o_ref)

        o_ref[...] += lax.dot_general(l_ref[...], r_ref[...].astype(BF16), TN_DIMS, preferred_element_type=F32)

    return pl.pallas_call(
        body, name=name, grid=(f // fb, t // tt),
        in_specs=[pl.BlockSpec((tt, fb), lambda j, i: (i, j)), pl.BlockSpec((tt, d), lambda j, i: (i, 0))],
        out_specs=pl.BlockSpec((fb, d), lambda j, i: (j, 0)),
        out_shape=jax.ShapeDtypeStruct((f, d), F32),
        compiler_params=_params(("arbitrary", "arbitrary")),
    )(lhs, rhs)


def _mixer_bwd(dr1, p, ya, yb, wall, conv_w8, vng, vnb, ws_b, wst_b, bias_s, head_sel, dm, tm):
    t, d, wa, npj = dm.T, dm.D, dm.WA, dm.NP
    nt = t // tm
    h8, hb = F32_SUBLANES, BF16_SUBLANES
    ext = tm + 2 * h8
    prev_f, next_f = _halo_maps(tm, t, h8)
    prev_b, next_b = _halo_maps(tm, t, hb)

    def body(dr_ref, drp_ref, drn_ref, p_ref, pp_ref, pn_ref, ya_ref, yb_ref, wall_ref, cw_ref, vng_ref, vnb_ref,
             ws_ref, wst_ref, bias_ref, sel_ref,
             dp_ref, dwo_hbm, dwpa_hbm, dwpb_hbm, dws_ref, dbs_ref, dcw_ref, dbg_ref, dvn_ref,
             wpa_ref, wpb_ref, wo_ref, dwo_ref, dwpa_ref, dwpb_ref, ext_ref, ext2_ref, mixed_ref, dvnm_ref, sems, osems):
        i = pl.program_id(0)

        @pl.when(i == 0)
        def _():
            cps = _load_weight(wall_ref, wpa_ref, sems, 0, dm.shard_offset(1), dm.shard_rows[1])
            cps += _load_weight(wall_ref, wpb_ref, sems, 8, dm.shard_offset(2), dm.shard_rows[2])
            cps += _load_weight(wall_ref, wo_ref, sems, 16, dm.shard_offset(3), dm.shard_rows[3])
            for ref in (dwo_ref, dwpa_ref, dwpb_ref, dws_ref, dbs_ref, dcw_ref, dbg_ref, dvn_ref):
                ref[...] = jnp.zeros_like(ref)
            for cp in cps:
                cp.wait()

        def col(ref, lo, width):
            return ref[:, lo:lo + width].astype(F32)

        def ext_rows(prev_blk, center, next_blk):
            return jnp.concatenate([prev_blk, center, next_blk], axis=0)

        def ext_col(lo, width):
            return ext_rows(col(pp_ref, lo, width)[hb - h8:hb], col(p_ref, lo, width), col(pn_ref, lo, width)[0:h8])

        row = lax.broadcasted_iota(jnp.int32, (ext, 1), 0) + (i * tm - h8)
        inside = jnp.logical_and(row >= 0, row < t)
        dr_e = ext_rows(drp_ref[...], dr_ref[...], drn_ref[...])
        drb = dr_e[h8:h8 + tm].astype(BF16)
        ds_e = lax.dot_general(dr_e.astype(BF16), wo_ref[...], NT_DIMS, preferred_element_type=F32)
        dya_e = ds_e * ext_col(dm.OFF_GA, d)
        da_e = lax.dot_general(dya_e.astype(BF16), wpa_ref[...], NT_DIMS, preferred_element_type=F32)
        dcv_e = jnp.where(inside, da_e * ext_col(0, wa), 0.0)
        ch_e = jnp.where(inside, ext_col(dm.OFF_CA, wa) * ext_col(dm.OFF_HA, wa), 0.0)
        ext_ref[...] = ch_e
        ext2_ref[...] = dcv_e
        ch, ch_up, ch_dn = ch_e[h8:h8 + tm], ext_ref[pl.ds(h8 - 1, tm), :], ext_ref[pl.ds(h8 + 1, tm), :]
        dcv, dcv_up, dcv_dn = dcv_e[h8:h8 + tm], ext2_ref[pl.ds(h8 - 1, tm), :], ext2_ref[pl.ds(h8 + 1, tm), :]
        w0, w1, w2 = cw_ref[0:1, :], cw_ref[1:2, :], cw_ref[2:3, :]
        cv = w0 * ch_up + w1 * ch + w2 * ch_dn
        da = da_e[h8:h8 + tm]
        b_a = col(p_ref, 0, wa)
        dp_ref[:, 0:wa] = (da * cv).astype(BF16)
        dch = w0 * dcv_dn + w1 * dcv + w2 * dcv_up
        dp_ref[:, dm.OFF_CA:dm.OFF_CA + wa] = (dch * col(p_ref, dm.OFF_HA, wa)).astype(BF16)
        dp_ref[:, dm.OFF_HA:dm.OFF_HA + wa] = (dch * col(p_ref, dm.OFF_CA, wa)).astype(BF16)
        dcw_ref[0:1, :] += _colsum(dcv * ch_up)
        dcw_ref[1:2, :] += _colsum(dcv * ch)
        dcw_ref[2:3, :] += _colsum(dcv * ch_dn)
        dya = dya_e[h8:h8 + tm]
        dwpa_ref[...] += lax.dot_general((b_a * cv).astype(BF16), dya.astype(BF16), TN_DIMS, preferred_element_type=F32)
        ds = ds_e[h8:h8 + tm]
        g_a, g_b = col(p_ref, dm.OFF_GA, d), col(p_ref, dm.OFF_GB, d)
        y_a, y_b = ya_ref[...].astype(F32), yb_ref[...].astype(F32)
        dwo_ref[...] += lax.dot_general((g_a * y_a + g_b * y_b).astype(BF16), drb, TN_DIMS, preferred_element_type=F32)
        dzga = ds * y_a * g_a * (1.0 - g_a)
        dzgb = ds * y_b * g_b * (1.0 - g_b)
        dp_ref[:, dm.OFF_GA:dm.OFF_GA + d] = dzga.astype(BF16)
        dp_ref[:, dm.OFF_GB:dm.OFF_GB + d] = dzgb.astype(BF16)
        dbg_ref[0:1, 0:d] += _colsum(dzga)
        dbg_ref[0:1, d:2 * d] += _colsum(dzgb)
        dyb = (ds * g_b).astype(BF16)
        gv, dgelu_v = _gelu_and_grad(col(p_ref, dm.OFF_VB, d))
        xhv, rstdv = _ln_stats(gv)
        vn = xhv * vng_ref[...] + vnb_ref[...]
        mixed = _spatial_mix(vn, ws_ref, bias_ref, mixed_ref, dm, tm)
        gu, dgelu_u = _gelu_and_grad(col(p_ref, dm.OFF_UB, d))
        dwpb_ref[...] += lax.dot_general((gu * mixed).astype(BF16), dyb, TN_DIMS, preferred_element_type=F32)
        dbb = lax.dot_general(dyb, wpb_ref[...], NT_DIMS, preferred_element_type=F32)
        dp_ref[:, dm.OFF_UB:dm.OFF_UB + d] = (dbb * mixed * dgelu_u).astype(BF16)
        dmb = (dbb * gu).astype(BF16)
        vb = vn.astype(BF16)
        dbs = jnp.zeros((CHUNK, CHUNK), F32)
        for cc in range(tm // CHUNK):
            r0 = cc * CHUNK
            dbs = dbs + jnp.dot(dmb[r0:r0 + CHUNK, :], sel_ref[...], preferred_element_type=F32)
            for h in range(dm.H):
                c0 = h * CHUNK
                blk = dmb[r0:r0 + CHUNK, c0:c0 + CHUNK]
                dvnm_ref[r0:r0 + CHUNK, c0:c0 + CHUNK] = jnp.dot(wst_ref[h], blk, preferred_element_type=F32)
                dws_ref[h] += lax.dot_general(blk, vb[r0:r0 + CHUNK, c0:c0 + CHUNK], NT_DIMS, preferred_element_type=F32)
        dbs_ref[...] += dbs
        dvn = dvnm_ref[...]
        dvn_ref[0:1, :] += _colsum(dvn * xhv)
        dvn_ref[1:2, :] += _colsum(dvn)
        dp_ref[:, dm.OFF_VB:dm.OFF_VB + d] = (_ln_bwd(dvn * vng_ref[...], xhv, rstdv) * dgelu_v).astype(BF16)

        @pl.when(i == nt - 1)
        def _():
            outs = [pltpu.make_async_copy(src, dst, osems.at[k])
                    for k, (src, dst) in enumerate(((dwo_ref, dwo_hbm), (dwpa_ref, dwpa_hbm), (dwpb_ref, dwpb_hbm)))]
            for cp in outs:
                cp.start()
            for cp in outs:
                cp.wait()

    full = lambda i: (0, 0)
    tile = lambda i: (i, 0)
    vec = pl.BlockSpec((1, d), full)
    hcc = pl.BlockSpec((dm.H, CHUNK, CHUNK), lambda i: (0, 0, 0))
    return pl.pallas_call(
        body, name="mixer_bwd", grid=(nt,),
        in_specs=[pl.BlockSpec((tm, d), tile), pl.BlockSpec((h8, d), prev_f), pl.BlockSpec((h8, d), next_f),
                  pl.BlockSpec((tm, npj), tile), pl.BlockSpec((hb, npj), prev_b), pl.BlockSpec((hb, npj), next_b),
                  pl.BlockSpec((tm, d), tile), pl.BlockSpec((tm, d), tile), HBM_SPEC,
                  pl.BlockSpec((h8, wa), full), vec, vec, hcc, hcc, pl.BlockSpec((CHUNK, d), full),
                  pl.BlockSpec((d, CHUNK), full)],
        out_specs=(pl.BlockSpec((tm, npj), tile), HBM_SPEC, HBM_SPEC, HBM_SPEC, hcc, pl.BlockSpec((CHUNK, CHUNK), full),
                   pl.BlockSpec((h8, wa), full), pl.BlockSpec((h8, 2 * d), full), pl.BlockSpec((h8, d), full)),
        out_shape=(jax.ShapeDtypeStruct((t, npj), BF16), jax.ShapeDtypeStruct((d, d), F32), jax.ShapeDtypeStruct((wa, d), F32),
                   jax.ShapeDtypeStruct((d, d), F32), jax.ShapeDtypeStruct((dm.H, CHUNK, CHUNK), F32),
                   jax.ShapeDtypeStruct((CHUNK, CHUNK), F32), jax.ShapeDtypeStruct((h8, wa), F32),
                   jax.ShapeDtypeStruct((h8, 2 * d), F32), jax.ShapeDtypeStruct((h8, d), F32)),
        scratch_shapes=[pltpu.VMEM((wa, d), BF16), pltpu.VMEM((d, d), BF16), pltpu.VMEM((d, d), BF16),
                        pltpu.VMEM((d, d), F32), pltpu.VMEM((wa, d), F32), pltpu.VMEM((d, d), F32),
                        pltpu.VMEM((ext, wa), F32), pltpu.VMEM((ext, wa), F32), pltpu.VMEM((tm, d), F32),
                        pltpu.VMEM((tm, d), F32), pltpu.SemaphoreType.DMA((3 * N_DEV,)), pltpu.SemaphoreType.DMA((3,))],
        compiler_params=_params(),
    )(dr1, dr1, dr1, p, p, p, ya, yb, wall, conv_w8, vng, vnb, ws_b, wst_b, bias_s, head_sel)


def _input_grad(dp, dr1, wall, dm, tm):
    t, d, npj = dm.T, dm.D, dm.NP
    rows = dm.shard_rows[0]

    def body(dp_ref, dr_ref, wall_ref, dx_ref, w_ref, sems):
        @pl.when(pl.program_id(0) == 0)
        def _():
            for cp in _load_weight(wall_ref, w_ref, sems, 0, dm.shard_offset(0), rows):
                cp.wait()

        dx_ref[...] = ALPHA * dr_ref[...] + jnp.dot(dp_ref[...], w_ref[...], preferred_element_type=F32)

    return pl.pallas_call(
        body, name="input_grad", grid=(t // tm,),
        in_specs=[pl.BlockSpec((tm, npj), lambda i: (i, 0)), pl.BlockSpec((tm, d), lambda i: (i, 0)), HBM_SPEC],
        out_specs=pl.BlockSpec((tm, d), lambda i: (i, 0)),
        out_shape=jax.ShapeDtypeStruct((t, d), F32),
        scratch_shapes=[pltpu.VMEM((npj, d), BF16), pltpu.SemaphoreType.DMA((N_DEV,))],
        compiler_params=_params(),
    )(dp, dr1, wall)


def _adamw(w, g, m, v, name):
    rows, cols = w.shape
    tr = 256 if rows % 256 == 0 else rows
    bc1 = 1.0 - ADAM_B1 ** ADAM_STEP
    bc2 = 1.0 - ADAM_B2 ** ADAM_STEP

    def body(w_ref, g_ref, m_ref, v_ref, d_ref, nm_ref, nv_ref):
        g_ = g_ref[...]
        nm = ADAM_B1 * m_ref[...] + (1.0 - ADAM_B1) * g_
        nv = ADAM_B2 * v_ref[...] + (1.0 - ADAM_B2) * (g_ * g_)
        d_ref[...] = -ADAM_LR * ((nm / bc1) / (jnp.sqrt(nv / bc2) + ADAM_EPS) + ADAM_WD * w_ref[...])
        nm_ref[...] = nm
        nv_ref[...] = nv

    spec = pl.BlockSpec((tr, cols), lambda i: (i, 0))
    shp = jax.ShapeDtypeStruct((rows, cols), F32)
    return pl.pallas_call(
        body, name=name, grid=(rows // tr,), in_specs=[spec] * 4, out_specs=(spec,) * 3, out_shape=(shp,) * 3,
        compiler_params=_params(),
    )(w, g, m, v)


def _to_slab(parts):
    flat = jnp.concatenate([q.reshape(-1) for q in parts])
    pad = (-flat.shape[0]) % (F32_SUBLANES * LANES)
    return jnp.pad(flat, (0, pad)).reshape(-1, LANES)


def _from_slab(slab, shapes):
    flat = slab.reshape(-1)
    out, off = [], 0
    for s in shapes:
        n = math.prod(s)
        out.append(flat[off:off + n].reshape(s))
        off += n
    return out


def kernel(x, w_in, b_gate, conv_w, v_norm_g, v_norm_b, w_s, b_s, w_pa, w_pb, w_o, ln1_g, ln1_b, w_ff1, w_ff2, ln2_g, ln2_b, loss_target, m_w_in, m_b_gate, m_conv_w, m_v_norm_g, m_v_norm_b, m_w_s, m_b_s, m_w_pa, m_w_pb, m_w_o, m_ln1_g, m_ln1_b, m_w_ff1, m_w_ff2, m_ln2_g, m_ln2_b, v_w_in, v_b_gate, v_conv_w, v_v_norm_g, v_v_norm_b, v_w_s, v_b_s, v_w_pa, v_w_pb, v_w_o, v_ln1_g, v_ln1_b, v_w_ff1, v_w_ff2, v_ln2_g, v_ln2_b):
    t, d = x.shape[1], x.shape[2]
    dm = _Dims(t, d)
    tm = 256 if t % 256 == 0 else CHUNK
    tm_big = 512 if t % 512 == 0 else tm
    me = 4 * lax.axis_index("x") + 2 * lax.axis_index("y") + lax.axis_index("c")
    x2, tgt = x[0], loss_target[0]

    conv_bits = lax.bitcast_convert_type(conv_w[0], BF16).reshape(-1)
    conv_blk = jnp.pad(conv_bits, (0, dm.conv_rows * d - conv_bits.shape[0])).reshape(dm.conv_rows, d)
    shard = jnp.concatenate([w_in[0].T.astype(BF16), w_pa[0].astype(BF16), w_pb[0].astype(BF16), w_o[0].astype(BF16),
                             w_ff1[0].T.astype(BF16), w_ff2[0].astype(BF16), conv_blk], axis=0)
    wall = _all_gather(shard)
    wa8 = dm.WA // N_DEV
    conv_all = lax.bitcast_convert_type(
        wall[:, dm.grad_rows:, :].reshape(N_DEV, -1)[:, :3 * wa8 * 2].reshape(N_DEV, 3, wa8, 2), F32)
    conv_full = jnp.transpose(conv_all, (1, 0, 2)).reshape(3, dm.WA)
    conv_w8 = jnp.pad(conv_full, ((0, F32_SUBLANES - 3), (0, 0)))
    ws_b = w_s[0].astype(BF16)
    wst_b = jnp.transpose(w_s[0], (0, 2, 1)).astype(BF16)
    bias_s = jnp.repeat(b_s[0].T, CHUNK, axis=1)
    head_sel = (jnp.arange(d)[:, None] // CHUNK == jnp.arange(CHUNK)[None, :]).astype(BF16)

    p = _proj_in(x2, wall, b_gate, dm, tm_big)
    ya, yb, r1 = _mixer_fwd(p, x2, wall, conv_w8, v_norm_g, v_norm_b, ws_b, bias_s, dm, tm)
    dr1, act, dh1, x1b, dr2b, sums = _ffn_fwd_bwd(r1, tgt, wall, ln1_g, ln1_b, ln2_g, ln2_b, dm, tm)
    fb = min(1024, dm.DFF)
    g_ff1t, g_ff2 = _wgrad(dh1, x1b, act, dr2b, tm_big, fb, "ffn_wgrad")
    dp, g_o, g_pa, g_pb, g_ws, g_bs_t, g_cw, g_bg, g_vn = _mixer_bwd(
        dr1, p, ya, yb, wall, conv_w8, v_norm_g, v_norm_b, ws_b, wst_b, bias_s, head_sel, dm, CHUNK)
    grad_x = _input_grad(dp, dr1, wall, dm, tm_big)
    g_int = _wgrad1(dp, x2, tm_big, 17 * LANES, "w_in_grad")

    blocks = [g.reshape(N_DEV, r, d) for g, r in zip((g_int, g_pa, g_pb, g_o, g_ff1t, g_ff2), dm.shard_rows)]
    packed = jnp.concatenate(blocks, axis=1).astype(BF16)
    small_parts = [g_bg[0], g_cw[0:3], g_vn[0], g_vn[1], g_ws, g_bs_t[:, :dm.H].T,
                   sums[3], sums[4], sums[1], sums[2], sums[0]]
    got_g, got_s = _exchange(packed, _to_slab(small_parts))
    gsum = _sum_slots(got_g, 256, "sum_grads")
    ssum = _sum_slots(got_s, 256, "sum_small")
    (s_bg, s_cw, s_vng, s_vnb, s_ws, s_bs, s_l1g, s_l1b, s_l2g, s_l2b, s_sq) = _from_slab(
        ssum, [(1, 2 * d), (3, dm.WA), (1, d), (1, d), (1, dm.H, CHUNK, CHUNK), (1, dm.H, CHUNK),
               (1, d), (1, d), (1, d), (1, d), (d,)])
    loss = 0.5 * jnp.sum(s_sq) / d
    s_cw = lax.dynamic_slice(s_cw, (0, me * wa8), (3, wa8))[None]
    offs = [dm.shard_offset(k) for k in range(6)]
    cut = [gsum[o:o + r] for o, r in zip(offs, dm.shard_rows)]
    big_g = [cut[0].T, cut[1], cut[2], cut[3], cut[4].T, cut[5]]

    big_w = [(w_in, m_w_in, v_w_in), (w_pa, m_w_pa, v_w_pa), (w_pb, m_w_pb, v_w_pb), (w_o, m_w_o, v_w_o),
             (w_ff1, m_w_ff1, v_w_ff1), (w_ff2, m_w_ff2, v_w_ff2)]
    big_out = [_adamw(w[0], g, m[0], v[0], "adamw_%d" % k) for k, ((w, m, v), g) in enumerate(zip(big_w, big_g))]
    small_w = [(b_gate, m_b_gate, v_b_gate), (conv_w, m_conv_w, v_conv_w), (v_norm_g, m_v_norm_g, v_v_norm_g),
               (v_norm_b, m_v_norm_b, v_v_norm_b), (w_s, m_w_s, v_w_s), (b_s, m_b_s, v_b_s), (ln1_g, m_ln1_g, v_ln1_g),
               (ln1_b, m_ln1_b, v_ln1_b), (ln2_g, m_ln2_g, v_ln2_g), (ln2_b, m_ln2_b, v_ln2_b)]
    small_g = [s_bg, s_cw, s_vng, s_vnb, s_ws, s_bs, s_l1g, s_l1b, s_l2g, s_l2b]
    small_shapes = [w.shape for w, _, _ in small_w]
    sd, sm, sv = _adamw(_to_slab([w for w, _, _ in small_w]), _to_slab(small_g), _to_slab([m for _, m, _ in small_w]),
                        _to_slab([v for _, _, v in small_w]), "adamw_small")
    small_out = list(zip(_from_slab(sd, small_shapes), _from_slab(sm, small_shapes), _from_slab(sv, small_shapes)))

    order = [("b", 0), ("s", 0), ("s", 1), ("s", 2), ("s", 3), ("s", 4), ("s", 5), ("b", 1), ("b", 2), ("b", 3),
             ("s", 6), ("s", 7), ("b", 4), ("b", 5), ("s", 8), ("s", 9)]
    grads, deltas, new_m, new_v = [], [], [], []
    for kind, k in order:
        if kind == "b":
            g, (dl, nm, nv) = big_g[k][None], big_out[k]
            dl, nm, nv = dl[None], nm[None], nv[None]
        else:
            g, (dl, nm, nv) = small_g[k], small_out[k]
        grads.append(g)
        deltas.append(dl)
        new_m.append(nm)
        new_v.append(nv)
    return (loss, grad_x[None], *grads, *deltas, *new_m, *new_v)
```

```python
import math

import jax
import jax.numpy as jnp
from jax import lax
from jax.experimental import pallas as pl
from jax.experimental.pallas import tpu as pltpu

F32 = jnp.float32
BF16 = jnp.bfloat16
N_DEV = 8
CHUNK = 128
LN_EPS = 1e-5
ALPHA = 2.0 ** 0.25
ADAM_LR, ADAM_B1, ADAM_B2, ADAM_EPS, ADAM_WD, ADAM_STEP = 0.001, 0.9, 0.999, 1e-08, 0.01, 10
F32_SUBLANES = 8
BF16_SUBLANES = 16
LANES = 128
VMEM_LIMIT = 56 * 1024 * 1024
MESH = pl.DeviceIdType.MESH
NT_DIMS = (((1,), (1,)), ((), ()))
TN_DIMS = (((0,), (0,)), ((), ()))
HBM_SPEC = pl.BlockSpec(memory_space=pltpu.HBM)


class _Dims:
    def __init__(self, t, d):
        self.T, self.D = t, d
        self.WA = 3 * d // 2
        self.NP = 3 * self.WA + 4 * d
        self.DFF = 4 * d
        self.H = d // CHUNK
        self.OFF_CA, self.OFF_HA = self.WA, 2 * self.WA
        self.OFF_UB = 3 * self.WA
        self.OFF_VB = self.OFF_UB + d
        self.OFF_GA = self.OFF_VB + d
        self.OFF_GB = self.OFF_GA + d
        self.shard_rows = (self.NP // N_DEV, self.WA // N_DEV, d // N_DEV, d // N_DEV, self.DFF // N_DEV, self.DFF // N_DEV)
        self.conv_rows = BF16_SUBLANES * max(1, -(-(3 * (self.WA // N_DEV) * 2) // (BF16_SUBLANES * d)))


def _params(sem=("arbitrary",), vmem=VMEM_LIMIT):
    return pltpu.CompilerParams(dimension_semantics=sem, vmem_limit_bytes=vmem)


def _mesh_pos():
    return lax.axis_index("x"), lax.axis_index("y"), lax.axis_index("c")


def _resident(shape):
    zeros = (0,) * len(shape)
    return pl.BlockSpec(shape, lambda *_: zeros, pipeline_mode=pl.Buffered(1))


def _all_gather(shards):
    n = len(shards)

    def body(*refs):
        x_refs, out_refs = refs[:n], refs[n:2 * n]
        send_sems, recv_sems, local_sems = refs[2 * n:]
        x, y, c = _mesh_pos()
        me, sibling = (x, y, c), (x, y, 1 - c)
        chips = [(1 - x, y), (x, 1 - y), (1 - x, 1 - y)]

        def slot(a, px, py, pc):
            rows = shards[a].shape[0]
            return out_refs[a].at[pl.ds((4 * px + 2 * py + pc) * rows, rows), :]

        def copy(a, k, block, to, src=None):
            return pltpu.make_async_remote_copy(
                src_ref=slot(a, *block) if src is None else src, dst_ref=slot(a, *block),
                send_sem=send_sems.at[7 * a + k], recv_sem=recv_sems.at[7 * a + k], device_id=to, device_id_type=MESH)

        mine = [pltpu.make_async_copy(x_refs[a], slot(a, *me), local_sems.at[a]) for a in range(n)]
        for cp in mine:
            cp.start()
        first = []
        for a in range(n):
            first.append(copy(a, 0, me, sibling, src=x_refs[a]))
            first += [copy(a, 1 + j, me, (*chip, c), src=x_refs[a]) for j, chip in enumerate(chips)]
        for cp in first:
            cp.start()
        passed = []
        for j, chip in enumerate(chips):
            for a in range(n):
                copy(a, 1 + j, (*chip, c), me).wait_recv()
                passed.append(copy(a, 4 + j, (*chip, c), sibling))
                passed[-1].start()
        for a in range(n):
            copy(a, 0, sibling, me).wait_recv()
            for j, chip in enumerate(chips):
                copy(a, 4 + j, (*chip, 1 - c), me).wait_recv()
        for cp in first + passed:
            cp.wait_send()
        for cp in mine:
            cp.wait()

    return pl.pallas_call(
        body, name="all_gather_weights",
        out_shape=tuple(jax.ShapeDtypeStruct((N_DEV * s.shape[0], s.shape[1]), s.dtype) for s in shards),
        in_specs=[HBM_SPEC] * n, out_specs=(HBM_SPEC,) * n,
        scratch_shapes=[pltpu.SemaphoreType.DMA((7 * n,)), pltpu.SemaphoreType.DMA((7 * n,)), pltpu.SemaphoreType.DMA((n,))],
    )(*shards)


def _exchange(grads, small):
    _, rows, cols = grads.shape
    srows, scols = small.shape

    def body(g_ref, s_ref, og_ref, os_ref, send_sems, recv_sems, local_sems):
        x, y, c = _mesh_pos()
        me = 4 * x + 2 * y + c
        own = [pltpu.make_async_copy(g_ref.at[me], og_ref.at[me], local_sems.at[0]),
               pltpu.make_async_copy(s_ref, os_ref.at[me], local_sems.at[1])]
        for cp in own:
            cp.start()
        sends, recvs = [], []
        for r in range(1, N_DEV):
            px = 1 - x if r & 4 else x
            py = 1 - y if r & 2 else y
            pc = 1 - c if r & 1 else c
            peer = 4 * px + 2 * py + pc
            k = r - 1
            sends.append(pltpu.make_async_remote_copy(
                src_ref=g_ref.at[peer], dst_ref=og_ref.at[me], send_sem=send_sems.at[k], recv_sem=recv_sems.at[k],
                device_id=(px, py, pc), device_id_type=MESH))
            sends.append(pltpu.make_async_remote_copy(
                src_ref=s_ref, dst_ref=os_ref.at[me], send_sem=send_sems.at[7 + k], recv_sem=recv_sems.at[7 + k],
                device_id=(px, py, pc), device_id_type=MESH))
            recvs.append(pltpu.make_async_remote_copy(
                src_ref=g_ref.at[me], dst_ref=og_ref.at[peer], send_sem=send_sems.at[k], recv_sem=recv_sems.at[k],
                device_id=(px, py, pc), device_id_type=MESH))
            recvs.append(pltpu.make_async_remote_copy(
                src_ref=s_ref, dst_ref=os_ref.at[peer], send_sem=send_sems.at[7 + k], recv_sem=recv_sems.at[7 + k],
                device_id=(px, py, pc), device_id_type=MESH))
        for cp in sends:
            cp.start()
        for cp in recvs:
            cp.wait_recv()
        for cp in sends:
            cp.wait_send()
        for cp in own:
            cp.wait()

    return pl.pallas_call(
        body, name="exchange_grads",
        out_shape=(jax.ShapeDtypeStruct((N_DEV, rows, cols), grads.dtype),
                   jax.ShapeDtypeStruct((N_DEV, srows, scols), small.dtype)),
        in_specs=[HBM_SPEC, HBM_SPEC], out_specs=(HBM_SPEC, HBM_SPEC),
        scratch_shapes=[pltpu.SemaphoreType.DMA((14,)), pltpu.SemaphoreType.DMA((14,)), pltpu.SemaphoreType.DMA((2,))],
    )(grads, small)


def _sum_slots(slots, tile_rows, name):
    _, rows, cols = slots.shape
    tr = tile_rows if rows % tile_rows == 0 else rows

    def body(s_ref, o_ref):
        acc = s_ref[0].astype(F32)
        for k in range(1, N_DEV):
            acc = acc + s_ref[k].astype(F32)
        o_ref[...] = acc

    return pl.pallas_call(
        body, name=name, grid=(rows // tr,),
        in_specs=[pl.BlockSpec((N_DEV, tr, cols), lambda i: (0, i, 0))],
        out_specs=pl.BlockSpec((tr, cols), lambda i: (i, 0)),
        out_shape=jax.ShapeDtypeStruct((rows, cols), F32),
        compiler_params=_params(),
    )(slots)


def _gelu_and_grad(x):
    k0 = math.sqrt(2.0 / math.pi)
    k1 = 0.044715
    x2 = x * x
    th = jnp.tanh(k0 * x * (1.0 + k1 * x2))
    half = 0.5 * (1.0 + th)
    return x * half, half + 0.5 * x * (1.0 - th * th) * (k0 * (1.0 + 3.0 * k1 * x2))


def _ln_stats(r):
    mu = jnp.mean(r, axis=-1, keepdims=True)
    rc = r - mu
    var = jnp.mean(rc * rc, axis=-1, keepdims=True)
    rstd = lax.rsqrt(var + LN_EPS)
    return rc * rstd, rstd


def _ln_bwd(dxh, xh, rstd):
    return rstd * (dxh - jnp.mean(dxh, axis=-1, keepdims=True) - xh * jnp.mean(dxh * xh, axis=-1, keepdims=True))


def _colsum(a):
    return jnp.sum(a, axis=0, keepdims=True)


def _mm(a, b):
    return jnp.dot(a, b, preferred_element_type=F32)


def _halo_maps(tm, t, unit):
    per, last = tm // unit, t // unit - 1
    return (lambda i: (jnp.maximum(i * per - 1, 0), 0)), (lambda i: (jnp.minimum((i + 1) * per, last), 0))


def _proj_in(x2, w_int, b_gate, dm, tm):
    t, d, npj = dm.T, dm.D, dm.NP
    cw = d // 2

    def body(x_ref, w_ref, bg_ref, p_ref):
        xb = x_ref[...].astype(BF16)
        for n in range(npj // cw):
            lo = n * cw
            acc = lax.dot_general(xb, w_ref[lo:lo + cw, :], NT_DIMS, preferred_element_type=F32)
            if lo >= dm.OFF_GA:
                acc = jax.nn.sigmoid(acc + bg_ref[:, lo - dm.OFF_GA:lo - dm.OFF_GA + cw])
            p_ref[:, lo:lo + cw] = acc.astype(BF16)

    return pl.pallas_call(
        body, name="proj_in", grid=(t // tm,),
        in_specs=[pl.BlockSpec((tm, d), lambda i: (i, 0)), _resident((npj, d)), _resident((1, 2 * d))],
        out_specs=pl.BlockSpec((tm, npj), lambda i: (i, 0)),
        out_shape=jax.ShapeDtypeStruct((t, npj), BF16),
        compiler_params=_params(),
    )(x2, w_int, b_gate)


def _conv_taps(ext_ref, center, prev_blk, next_blk, first, last, tm):
    h = F32_SUBLANES
    ext_ref[0:h, :] = jnp.where(first, 0.0, prev_blk)
    ext_ref[h:h + tm, :] = center
    ext_ref[h + tm:h + tm + h, :] = jnp.where(last, 0.0, next_blk)
    return ext_ref[pl.ds(h - 1, tm), :], ext_ref[pl.ds(h + 1, tm), :]


def _spatial_mix(vn, ws_ref, bias_ref, mixed_ref, dm, tm):
    vb = vn.astype(BF16)
    for cc in range(tm // CHUNK):
        r0 = cc * CHUNK
        for h in range(dm.H):
            c0 = h * CHUNK
            m = _mm(ws_ref[h], vb[r0:r0 + CHUNK, c0:c0 + CHUNK])
            mixed_ref[r0:r0 + CHUNK, c0:c0 + CHUNK] = m + bias_ref[:, c0:c0 + CHUNK]
    return mixed_ref[...]


def _mixer_fwd(p, x2, w_pa, w_pb, w_o, conv_w8, vng, vnb, ws_b, bias_s, dm, tm):
    t, d, wa, npj = dm.T, dm.D, dm.WA, dm.NP
    nt = t // tm
    hb = BF16_SUBLANES
    prev_map, next_map = _halo_maps(tm, t, hb)

    def body(p_ref, pp_ref, pn_ref, x_ref, wpa_ref, wpb_ref, wo_ref, cw_ref, vng_ref, vnb_ref, ws_ref, bias_ref,
             ya_ref, yb_ref, r1_ref, ext_ref, mixed_ref):
        i = pl.program_id(0)

        def col(ref, lo, width):
            return ref[:, lo:lo + width].astype(F32)

        ch = col(p_ref, dm.OFF_CA, wa) * col(p_ref, dm.OFF_HA, wa)
        chp = (col(pp_ref, dm.OFF_CA, wa) * col(pp_ref, dm.OFF_HA, wa))[hb - F32_SUBLANES:hb]
        chn = (col(pn_ref, dm.OFF_CA, wa) * col(pn_ref, dm.OFF_HA, wa))[0:F32_SUBLANES]
        up, dn = _conv_taps(ext_ref, ch, chp, chn, i == 0, i == nt - 1, tm)
        a = col(p_ref, 0, wa) * (cw_ref[0:1, :] * up + cw_ref[1:2, :] * ch + cw_ref[2:3, :] * dn)
        ya = _mm(a.astype(BF16), wpa_ref[...])
        gv, _ = _gelu_and_grad(col(p_ref, dm.OFF_VB, d))
        xhv, _ = _ln_stats(gv)
        mixed = _spatial_mix(xhv * vng_ref[...] + vnb_ref[...], ws_ref, bias_ref, mixed_ref, dm, tm)
        gu, _ = _gelu_and_grad(col(p_ref, dm.OFF_UB, d))
        yb = _mm((gu * mixed).astype(BF16), wpb_ref[...])
        s = col(p_ref, dm.OFF_GA, d) * ya + col(p_ref, dm.OFF_GB, d) * yb
        mix = _mm(s.astype(BF16), wo_ref[...])
        ya_ref[...] = ya.astype(BF16)
        yb_ref[...] = yb.astype(BF16)
        r1_ref[...] = ALPHA * x_ref[...] + mix

    tile = lambda i: (i, 0)
    return pl.pallas_call(
        body, name="mixer_fwd", grid=(nt,),
        in_specs=[pl.BlockSpec((tm, npj), tile), pl.BlockSpec((hb, npj), prev_map), pl.BlockSpec((hb, npj), next_map),
                  pl.BlockSpec((tm, d), tile), _resident((wa, d)), _resident((d, d)), _resident((d, d)),
                  _resident((F32_SUBLANES, wa)), _resident((1, d)), _resident((1, d)),
                  _resident((dm.H, CHUNK, CHUNK)), _resident((CHUNK, d))],
        out_specs=(pl.BlockSpec((tm, d), tile), pl.BlockSpec((tm, d), tile), pl.BlockSpec((tm, d), tile)),
        out_shape=(jax.ShapeDtypeStruct((t, d), BF16), jax.ShapeDtypeStruct((t, d), BF16), jax.ShapeDtypeStruct((t, d), F32)),
        scratch_shapes=[pltpu.VMEM((tm + 2 * F32_SUBLANES, wa), F32), pltpu.VMEM((tm, d), F32)],
        compiler_params=_params(),
    )(p, p, p, x2, w_pa, w_pb, w_o, conv_w8, vng, vnb, ws_b, bias_s)


def _ffn_fwd(r1, tgt, w1, w2, ln1g, ln1b, ln2g, ln2b, dm, tm):
    t, d, dff = dm.T, dm.D, dm.DFF
    fc = dff // N_DEV

    def body(r1_ref, tgt_ref, w1_ref, w2_ref, g1_ref, b1_ref, g2_ref, b2_ref, relu_ref, x1_ref, dr2_ref, dr2b_ref, sums_ref):
        @pl.when(pl.program_id(0) == 0)
        def _():
            sums_ref[...] = jnp.zeros_like(sums_ref)

        xh1, _ = _ln_stats(r1_ref[...])
        x1 = xh1 * g1_ref[...] + b1_ref[...]
        x1b = x1.astype(BF16)
        x1_ref[...] = x1b
        ffn = jnp.zeros((tm, d), F32)
        for k in range(dff // fc):
            ks = slice(k * fc, (k + 1) * fc)
            r = jnp.maximum(_mm(x1b, w1_ref[:, ks]), 0.0)
            relu_ref[:, ks] = r.astype(BF16)
            ffn = ffn + _mm((r * r).astype(BF16), w2_ref[ks, :])
        xh2, rstd2 = _ln_stats(ALPHA * x1 + ffn)
        diff = xh2 * g2_ref[...] + b2_ref[...] - tgt_ref[...]
        dy = diff * (1.0 / d)
        dr2 = _ln_bwd(dy * g2_ref[...], xh2, rstd2)
        dr2_ref[...] = dr2
        dr2b_ref[...] = dr2.astype(BF16)
        sums_ref[0:1, :] += _colsum(diff * diff)
        sums_ref[1:2, :] += _colsum(dy * xh2)
        sums_ref[2:3, :] += _colsum(dy)

    tile = lambda i: (i, 0)
    vec = _resident((1, d))
    return pl.pallas_call(
        body, name="ffn_fwd", grid=(t // tm,),
        in_specs=[pl.BlockSpec((tm, d), tile), pl.BlockSpec((tm, d), tile), _resident((d, dff)), _resident((dff, d)),
                  vec, vec, vec, vec],
        out_specs=(pl.BlockSpec((tm, dff), tile), pl.BlockSpec((tm, d), tile), pl.BlockSpec((tm, d), tile),
                   pl.BlockSpec((tm, d), tile), pl.BlockSpec((F32_SUBLANES, d), lambda i: (0, 0))),
        out_shape=(jax.ShapeDtypeStruct((t, dff), BF16), jax.ShapeDtypeStruct((t, d), BF16), jax.ShapeDtypeStruct((t, d), F32),
                   jax.ShapeDtypeStruct((t, d), BF16), jax.ShapeDtypeStruct((F32_SUBLANES, d), F32)),
        compiler_params=_params(),
    )(r1, tgt, w1, w2, ln1g, ln1b, ln2g, ln2b)


def _ffn_bwd(r1, relu, dr2, w2t, w1t, ln1g, dm, tm):
    t, d, dff = dm.T, dm.D, dm.DFF
    fc = dff // N_DEV

    def body(r1_ref, relu_ref, dr2_ref, w2t_ref, w1t_ref, g1_ref, dr1_ref, dh_ref, sums_ref):
        @pl.when(pl.program_id(0) == 0)
        def _():
            sums_ref[...] = jnp.zeros_like(sums_ref)

        xh1, rstd1 = _ln_stats(r1_ref[...])
        dr2 = dr2_ref[...]
        dr2b = dr2.astype(BF16)
        dx1 = ALPHA * dr2
        for k in range(dff // fc):
            ks = slice(k * fc, (k + 1) * fc)
            dhb = (_mm(dr2b, w2t_ref[:, ks]) * (2.0 * relu_ref[:, ks].astype(F32))).astype(BF16)
            dh_ref[:, ks] = dhb
            dx1 = dx1 + _mm(dhb, w1t_ref[ks, :])
        dr1_ref[...] = _ln_bwd(dx1 * g1_ref[...], xh1, rstd1)
        sums_ref[0:1, :] += _colsum(dx1 * xh1)
        sums_ref[1:2, :] += _colsum(dx1)

    tile = lambda i: (i, 0)
    return pl.pallas_call(
        body, name="ffn_bwd", grid=(t // tm,),
        in_specs=[pl.BlockSpec((tm, d), tile), pl.BlockSpec((tm, dff), tile), pl.BlockSpec((tm, d), tile),
                  _resident((d, dff)), _resident((dff, d)), _resident((1, d))],
        out_specs=(pl.BlockSpec((tm, d), tile), pl.BlockSpec((tm, dff), tile), pl.BlockSpec((F32_SUBLANES, d), lambda i: (0, 0))),
        out_shape=(jax.ShapeDtypeStruct((t, d), F32), jax.ShapeDtypeStruct((t, dff), BF16),
                   jax.ShapeDtypeStruct((F32_SUBLANES, d), F32)),
        compiler_params=_params(),
    )(r1, relu, dr2, w2t, w1t, ln1g)


def _wgrad(pairs, tt, fb, name):
    n = len(pairs)
    t, f = pairs[0][0].shape
    d = pairs[0][1].shape[1]
    squares = [sq for _, _, sq in pairs]

    def body(*refs):
        ins, outs = refs[:2 * n], refs[2 * n:]

        @pl.when(pl.program_id(1) == 0)
        def _():
            for o_ref in outs:
                o_ref[...] = jnp.zeros_like(o_ref)

        for q in range(n):
            lhs = ins[2 * q][...]
            if squares[q]:
                lf = lhs.astype(F32)
                lhs = (lf * lf).astype(BF16)
            outs[q][...] += lax.dot_general(lhs.astype(BF16), ins[2 * q + 1][...].astype(BF16), TN_DIMS, preferred_element_type=F32)

    lhs_spec = pl.BlockSpec((tt, fb), lambda j, i: (i, j))
    rhs_spec = pl.BlockSpec((tt, d), lambda j, i: (i, 0))
    out_spec = pl.BlockSpec((fb, d), lambda j, i: (j, 0))
    return pl.pallas_call(
        body, name=name, grid=(f // fb, t // tt),
        in_specs=[lhs_spec, rhs_spec] * n, out_specs=(out_spec,) * n,
        out_shape=(jax.ShapeDtypeStruct((f, d), F32),) * n,
        compiler_params=_params(("arbitrary", "arbitrary")),
    )(*[a for lhs, rhs, _ in pairs for a in (lhs, rhs)])


def _mixer_bwd(dr1, p, ya, yb, w_ot, w_pat, w_pbt, conv_w8, vng, vnb, ws_b, wst_b, bias_s, head_sel, dm, tm):
    t, d, wa, npj = dm.T, dm.D, dm.WA, dm.NP
    nt = t // tm
    h8, hb = F32_SUBLANES, BF16_SUBLANES
    ext = tm + 2 * h8
    prev_f, next_f = _halo_maps(tm, t, h8)
    prev_b, next_b = _halo_maps(tm, t, hb)

    def body(dr_ref, drp_ref, drn_ref, p_ref, pp_ref, pn_ref, ya_ref, yb_ref, wot_ref, wpat_ref, wpbt_ref,
             cw_ref, vng_ref, vnb_ref, ws_ref, wst_ref, bias_ref, sel_ref,
             dp_ref, a_ref, dya_ref, s_ref, bb_ref, dyb_ref, dws_ref, dbs_ref, dcw_ref, dbg_ref, dvn_ref,
             ext_ref, ext2_ref, mixed_ref, dvnm_ref):
        i = pl.program_id(0)

        @pl.when(i == 0)
        def _():
            for ref in (dws_ref, dbs_ref, dcw_ref, dbg_ref, dvn_ref):
                ref[...] = jnp.zeros_like(ref)

        def col(ref, lo, width):
            return ref[:, lo:lo + width].astype(F32)

        def ext_rows(prev_blk, center, next_blk):
            return jnp.concatenate([prev_blk, center, next_blk], axis=0)

        def ext_col(lo, width):
            return ext_rows(col(pp_ref, lo, width)[hb - h8:hb], col(p_ref, lo, width), col(pn_ref, lo, width)[0:h8])

        row = lax.broadcasted_iota(jnp.int32, (ext, 1), 0) + (i * tm - h8)
        inside = jnp.logical_and(row >= 0, row < t)
        dr_e = ext_rows(drp_ref[...], dr_ref[...], drn_ref[...])
        ds_e = _mm(dr_e.astype(BF16), wot_ref[...])
        dya_e = ds_e * ext_col(dm.OFF_GA, d)
        da_e = _mm(dya_e.astype(BF16), wpat_ref[...])
        dcv_e = jnp.where(inside, da_e * ext_col(0, wa), 0.0)
        ch_e = jnp.where(inside, ext_col(dm.OFF_CA, wa) * ext_col(dm.OFF_HA, wa), 0.0)
        ext_ref[...] = ch_e
        ext2_ref[...] = dcv_e
        ch, ch_up, ch_dn = ch_e[h8:h8 + tm], ext_ref[pl.ds(h8 - 1, tm), :], ext_ref[pl.ds(h8 + 1, tm), :]
        dcv, dcv_up, dcv_dn = dcv_e[h8:h8 + tm], ext2_ref[pl.ds(h8 - 1, tm), :], ext2_ref[pl.ds(h8 + 1, tm), :]
        w0, w1, w2 = cw_ref[0:1, :], cw_ref[1:2, :], cw_ref[2:3, :]
        cv = w0 * ch_up + w1 * ch + w2 * ch_dn
        dp_ref[:, 0:wa] = (da_e[h8:h8 + tm] * cv).astype(BF16)
        dch = w0 * dcv_dn + w1 * dcv + w2 * dcv_up
        dp_ref[:, dm.OFF_CA:dm.OFF_CA + wa] = (dch * col(p_ref, dm.OFF_HA, wa)).astype(BF16)
        dp_ref[:, dm.OFF_HA:dm.OFF_HA + wa] = (dch * col(p_ref, dm.OFF_CA, wa)).astype(BF16)
        dcw_ref[0:1, :] += _colsum(dcv * ch_up)
        dcw_ref[1:2, :] += _colsum(dcv * ch)
        dcw_ref[2:3, :] += _colsum(dcv * ch_dn)
        a_ref[...] = (col(p_ref, 0, wa) * cv).astype(BF16)
        dya_ref[...] = dya_e[h8:h8 + tm].astype(BF16)
        ds = ds_e[h8:h8 + tm]
        g_a, g_b = col(p_ref, dm.OFF_GA, d), col(p_ref, dm.OFF_GB, d)
        y_a, y_b = ya_ref[...].astype(F32), yb_ref[...].astype(F32)
        s_ref[...] = (g_a * y_a + g_b * y_b).astype(BF16)
        dzga = ds * y_a * g_a * (1.0 - g_a)
        dzgb = ds * y_b * g_b * (1.0 - g_b)
        dp_ref[:, dm.OFF_GA:dm.OFF_GA + d] = dzga.astype(BF16)
        dp_ref[:, dm.OFF_GB:dm.OFF_GB + d] = dzgb.astype(BF16)
        dbg_ref[0:1, 0:d] += _colsum(dzga)
        dbg_ref[0:1, d:2 * d] += _colsum(dzgb)
        dyb = (ds * g_b).astype(BF16)
        dyb_ref[...] = dyb
        gv, dgelu_v = _gelu_and_grad(col(p_ref, dm.OFF_VB, d))
        xhv, rstdv = _ln_stats(gv)
        vn = xhv * vng_ref[...] + vnb_ref[...]
        mixed = _spatial_mix(vn, ws_ref, bias_ref, mixed_ref, dm, tm)
        gu, dgelu_u = _gelu_and_grad(col(p_ref, dm.OFF_UB, d))
        bb_ref[...] = (gu * mixed).astype(BF16)
        dbb = _mm(dyb, wpbt_ref[...])
        dp_ref[:, dm.OFF_UB:dm.OFF_UB + d] = (dbb * mixed * dgelu_u).astype(BF16)
        dmb = (dbb * gu).astype(BF16)
        vb = vn.astype(BF16)
        dbs = jnp.zeros((CHUNK, CHUNK), F32)
        for cc in range(tm // CHUNK):
            r0 = cc * CHUNK
            dbs = dbs + _mm(dmb[r0:r0 + CHUNK, :], sel_ref[...])
            for h in range(dm.H):
                c0 = h * CHUNK
                blk = dmb[r0:r0 + CHUNK, c0:c0 + CHUNK]
                dvnm_ref[r0:r0 + CHUNK, c0:c0 + CHUNK] = _mm(wst_ref[h], blk)
                dws_ref[h] += lax.dot_general(blk, vb[r0:r0 + CHUNK, c0:c0 + CHUNK], NT_DIMS, preferred_element_type=F32)
        dbs_ref[...] += dbs
        dvn = dvnm_ref[...]
        dvn_ref[0:1, :] += _colsum(dvn * xhv)
        dvn_ref[1:2, :] += _colsum(dvn)
        dp_ref[:, dm.OFF_VB:dm.OFF_VB + d] = (_ln_bwd(dvn * vng_ref[...], xhv, rstdv) * dgelu_v).astype(BF16)

    full = lambda i: (0, 0)
    tile = lambda i: (i, 0)
    hcc = _resident((dm.H, CHUNK, CHUNK))
    tok = lambda w, dt: jax.ShapeDtypeStruct((t, w), dt)
    return pl.pallas_call(
        body, name="mixer_bwd", grid=(nt,),
        in_specs=[pl.BlockSpec((tm, d), tile), pl.BlockSpec((h8, d), prev_f), pl.BlockSpec((h8, d), next_f),
                  pl.BlockSpec((tm, npj), tile), pl.BlockSpec((hb, npj), prev_b), pl.BlockSpec((hb, npj), next_b),
                  pl.BlockSpec((tm, d), tile), pl.BlockSpec((tm, d), tile),
                  _resident((d, d)), _resident((d, wa)), _resident((d, d)),
                  _resident((h8, wa)), _resident((1, d)), _resident((1, d)), hcc, hcc, _resident((CHUNK, d)),
                  _resident((d, CHUNK))],
        out_specs=(pl.BlockSpec((tm, npj), tile), pl.BlockSpec((tm, wa), tile), pl.BlockSpec((tm, d), tile),
                   pl.BlockSpec((tm, d), tile), pl.BlockSpec((tm, d), tile), pl.BlockSpec((tm, d), tile),
                   pl.BlockSpec((dm.H, CHUNK, CHUNK), lambda i: (0, 0, 0)), pl.BlockSpec((CHUNK, CHUNK), full),
                   pl.BlockSpec((h8, wa), full), pl.BlockSpec((h8, 2 * d), full), pl.BlockSpec((h8, d), full)),
        out_shape=(tok(npj, BF16), tok(wa, BF16), tok(d, BF16), tok(d, BF16), tok(d, BF16), tok(d, BF16),
                   jax.ShapeDtypeStruct((dm.H, CHUNK, CHUNK), F32), jax.ShapeDtypeStruct((CHUNK, CHUNK), F32),
                   jax.ShapeDtypeStruct((h8, wa), F32), jax.ShapeDtypeStruct((h8, 2 * d), F32),
                   jax.ShapeDtypeStruct((h8, d), F32)),
        scratch_shapes=[pltpu.VMEM((ext, wa), F32), pltpu.VMEM((ext, wa), F32), pltpu.VMEM((tm, d), F32),
                        pltpu.VMEM((tm, d), F32)],
        compiler_params=_params(),
    )(dr1, dr1, dr1, p, p, p, ya, yb, w_ot, w_pat, w_pbt, conv_w8, vng, vnb, ws_b, wst_b, bias_s, head_sel)


def _input_grad(dp, dr1, w_int, dm, tm):
    t, d, npj = dm.T, dm.D, dm.NP

    def body(dp_ref, dr_ref, w_ref, dx_ref):
        dx_ref[...] = ALPHA * dr_ref[...] + _mm(dp_ref[...], w_ref[...])

    return pl.pallas_call(
        body, name="input_grad", grid=(t // tm,),
        in_specs=[pl.BlockSpec((tm, npj), lambda i: (i, 0)), pl.BlockSpec((tm, d), lambda i: (i, 0)), _resident((npj, d))],
        out_specs=pl.BlockSpec((tm, d), lambda i: (i, 0)),
        out_shape=jax.ShapeDtypeStruct((t, d), F32),
        compiler_params=_params(),
    )(dp, dr1, w_int)


def _adamw(w, g, m, v, name):
    rows, cols = w.shape
    tr = 256 if rows % 256 == 0 else rows
    bc1 = 1.0 - ADAM_B1 ** ADAM_STEP
    bc2 = 1.0 - ADAM_B2 ** ADAM_STEP

    def body(w_ref, g_ref, m_ref, v_ref, d_ref, nm_ref, nv_ref):
        g_ = g_ref[...]
        nm = ADAM_B1 * m_ref[...] + (1.0 - ADAM_B1) * g_
        nv = ADAM_B2 * v_ref[...] + (1.0 - ADAM_B2) * (g_ * g_)
        d_ref[...] = -ADAM_LR * ((nm / bc1) / (jnp.sqrt(nv / bc2) + ADAM_EPS) + ADAM_WD * w_ref[...])
        nm_ref[...] = nm
        nv_ref[...] = nv

    spec = pl.BlockSpec((tr, cols), lambda i: (i, 0))
    shp = jax.ShapeDtypeStruct((rows, cols), F32)
    return pl.pallas_call(
        body, name=name, grid=(rows // tr,), in_specs=[spec] * 4, out_specs=(spec,) * 3, out_shape=(shp,) * 3,
        compiler_params=_params(),
    )(w, g, m, v)


def _to_slab(parts):
    flat = jnp.concatenate([q.reshape(-1) for q in parts])
    pad = (-flat.shape[0]) % (F32_SUBLANES * LANES)
    return jnp.pad(flat, (0, pad)).reshape(-1, LANES)


def _from_slab(slab, shapes):
    flat = slab.reshape(-1)
    out, off = [], 0
    for s in shapes:
        n = math.prod(s)
        out.append(flat[off:off + n].reshape(s))
        off += n
    return out


def kernel(x, w_in, b_gate, conv_w, v_norm_g, v_norm_b, w_s, b_s, w_pa, w_pb, w_o, ln1_g, ln1_b, w_ff1, w_ff2, ln2_g, ln2_b, loss_target, m_w_in, m_b_gate, m_conv_w, m_v_norm_g, m_v_norm_b, m_w_s, m_b_s, m_w_pa, m_w_pb, m_w_o, m_ln1_g, m_ln1_b, m_w_ff1, m_w_ff2, m_ln2_g, m_ln2_b, v_w_in, v_b_gate, v_conv_w, v_v_norm_g, v_v_norm_b, v_w_s, v_b_s, v_w_pa, v_w_pb, v_w_o, v_ln1_g, v_ln1_b, v_w_ff1, v_w_ff2, v_ln2_g, v_ln2_b):
    t, d = x.shape[1], x.shape[2]
    dm = _Dims(t, d)
    tm = 256 if t % 256 == 0 else CHUNK
    tm_big = 512 if t % 512 == 0 else tm
    tt = 1024 if t % 1024 == 0 else tm
    me = 4 * lax.axis_index("x") + 2 * lax.axis_index("y") + lax.axis_index("c")
    x2, tgt = x[0], loss_target[0]

    conv_bits = lax.bitcast_convert_type(conv_w[0], BF16).reshape(-1)
    conv_blk = jnp.pad(conv_bits, (0, dm.conv_rows * d - conv_bits.shape[0])).reshape(dm.conv_rows, d)
    w_int, w_pa_f, w_pb_f, w_o_f, w_1t, w_2, conv_g = _all_gather(
        [w_in[0].T.astype(BF16), w_pa[0].astype(BF16), w_pb[0].astype(BF16), w_o[0].astype(BF16),
         w_ff1[0].T.astype(BF16), w_ff2[0].astype(BF16), conv_blk])
    wa8 = dm.WA // N_DEV
    conv_all = lax.bitcast_convert_type(conv_g.reshape(N_DEV, -1)[:, :3 * wa8 * 2].reshape(N_DEV, 3, wa8, 2), F32)
    conv_full = jnp.transpose(conv_all, (1, 0, 2)).reshape(3, dm.WA)
    conv_w8 = jnp.pad(conv_full, ((0, F32_SUBLANES - 3), (0, 0)))
    ws_b = w_s[0].astype(BF16)
    wst_b = jnp.transpose(w_s[0], (0, 2, 1)).astype(BF16)
    bias_s = jnp.repeat(b_s[0].T, CHUNK, axis=1)
    head_sel = (jnp.arange(d)[:, None] // CHUNK == jnp.arange(CHUNK)[None, :]).astype(BF16)

    p = _proj_in(x2, w_int, b_gate, dm, tm_big)
    ya, yb, r1 = _mixer_fwd(p, x2, w_pa_f, w_pb_f, w_o_f, conv_w8, v_norm_g, v_norm_b, ws_b, bias_s, dm, tm)
    relu, x1b, dr2, dr2b, sums2 = _ffn_fwd(r1, tgt, w_1t.T, w_2, ln1_g, ln1_b, ln2_g, ln2_b, dm, tm_big)
    dr1, dh1, sums1 = _ffn_bwd(r1, relu, dr2, w_2.T, w_1t, ln1_g, dm, tm_big)
    fb = min(1024, dm.DFF)
    g_ff1t, g_ff2 = _wgrad([(dh1, x1b, False), (relu, dr2b, True)], tt, fb, "ffn_wgrad")
    dp, a_m, dya, s_m, bb_m, dyb, g_ws, g_bs_t, g_cw, g_bg, g_vn = _mixer_bwd(
        dr1, p, ya, yb, w_o_f.T, w_pa_f.T, w_pb_f.T, conv_w8, v_norm_g, v_norm_b, ws_b, wst_b, bias_s, head_sel, dm, tm)
    (g_pa,) = _wgrad([(a_m, dya, False)], tt, dm.WA, "w_pa_grad")
    g_o, g_pb = _wgrad([(s_m, dr1, False), (bb_m, dyb, False)], tt, d, "w_o_pb_grad")
    grad_x = _input_grad(dp, dr1, w_int, dm, tm_big)
    (g_int,) = _wgrad([(dp, x2, False)], tt, 17 * LANES, "w_in_grad")

    blocks = [g.reshape(N_DEV, r, d) for g, r in zip((g_int, g_pa, g_pb, g_o, g_ff1t, g_ff2), dm.shard_rows)]
    packed = jnp.concatenate(blocks, axis=1).astype(BF16)
    small_parts = [g_bg[0], g_cw[0:3], g_vn[0], g_vn[1], g_ws, g_bs_t[:, :dm.H].T,
                   sums1[0], sums1[1], sums2[1], sums2[2], sums2[0]]
    got_g, got_s = _exchange(packed, _to_slab(small_parts))
    gsum = _sum_slots(got_g, 256, "sum_grads")
    ssum = _sum_slots(got_s, 256, "sum_small")
    (s_bg, s_cw, s_vng, s_vnb, s_ws, s_bs, s_l1g, s_l1b, s_l2g, s_l2b, s_sq) = _from_slab(
        ssum, [(1, 2 * d), (3, dm.WA), (1, d), (1, d), (1, dm.H, CHUNK, CHUNK), (1, dm.H, CHUNK),
               (1, d), (1, d), (1, d), (1, d), (d,)])
    loss = 0.5 * jnp.sum(s_sq) / d
    s_cw = lax.dynamic_slice(s_cw, (0, me * wa8), (3, wa8))[None]
    offs = [sum(dm.shard_rows[:k]) for k in range(6)]
    cut = [gsum[o:o + r] for o, r in zip(offs, dm.shard_rows)]
    big_g = [cut[0].T, cut[1], cut[2], cut[3], cut[4].T, cut[5]]

    big_w = [(w_in, m_w_in, v_w_in), (w_pa, m_w_pa, v_w_pa), (w_pb, m_w_pb, v_w_pb), (w_o, m_w_o, v_w_o),
             (w_ff1, m_w_ff1, v_w_ff1), (w_ff2, m_w_ff2, v_w_ff2)]
    big_out = [_adamw(w[0], g, m[0], v[0], "adamw_%d" % k) for k, ((w, m, v), g) in enumerate(zip(big_w, big_g))]
    small_w = [(b_gate, m_b_gate, v_b_gate), (conv_w, m_conv_w, v_conv_w), (v_norm_g, m_v_norm_g, v_v_norm_g),
               (v_norm_b, m_v_norm_b, v_v_norm_b), (w_s, m_w_s, v_w_s), (b_s, m_b_s, v_b_s), (ln1_g, m_ln1_g, v_ln1_g),
               (ln1_b, m_ln1_b, v_ln1_b), (ln2_g, m_ln2_g, v_ln2_g), (ln2_b, m_ln2_b, v_ln2_b)]
    small_g = [s_bg, s_cw, s_vng, s_vnb, s_ws, s_bs, s_l1g, s_l1b, s_l2g, s_l2b]
    small_shapes = [w.shape for w, _, _ in small_w]
    sd, sm, sv = _adamw(_to_slab([w for w, _, _ in small_w]), _to_slab(small_g), _to_slab([m for _, m, _ in small_w]),
                        _to_slab([v for _, _, v in small_w]), "adamw_small")
    small_out = list(zip(_from_slab(sd, small_shapes), _from_slab(sm, small_shapes), _from_slab(sv, small_shapes)))

    order = [("b", 0), ("s", 0), ("s", 1), ("s", 2), ("s", 3), ("s", 4), ("s", 5), ("b", 1), ("b", 2), ("b", 3),
             ("s", 6), ("s", 7), ("b", 4), ("b", 5), ("s", 8), ("s", 9)]
    grads, deltas, new_m, new_v = [], [], [], []
    for kind, k in order:
        if kind == "b":
            g, (dl, nm, nv) = big_g[k][None], big_out[k]
            dl, nm, nv = dl[None], nm[None], nv[None]
        else:
            g, (dl, nm, nv) = small_g[k], small_out[k]
        grads.append(g)
        deltas.append(dl)
        new_m.append(nm)
        new_v.append(nv)
    return (loss, grad_x[None], *grads, *deltas, *new_m, *new_v)
```

```python
import math

import jax
import jax.numpy as jnp
from jax import lax
from jax.experimental import pallas as pl
from jax.experimental.pallas import tpu as pltpu

F32 = jnp.float32
BF16 = jnp.bfloat16
N_DEV = 8
CHUNK = 128
LN_EPS = 1e-5
ALPHA = 2.0 ** 0.25
ADAM_LR, ADAM_B1, ADAM_B2, ADAM_EPS, ADAM_WD, ADAM_STEP = 0.001, 0.9, 0.999, 1e-08, 0.01, 10
F32_SUBLANES = 8
BF16_SUBLANES = 16
LANES = 128
VMEM_LIMIT = 56 * 1024 * 1024
MESH = pl.DeviceIdType.MESH
NT_DIMS = (((1,), (1,)), ((), ()))
TN_DIMS = (((0,), (0,)), ((), ()))
HBM_SPEC = pl.BlockSpec(memory_space=pltpu.HBM)


class _Dims:
    def __init__(self, t, d):
        self.T, self.D = t, d
        self.WA = 3 * d // 2
        self.NP = 3 * self.WA + 4 * d
        self.DFF = 4 * d
        self.H = d // CHUNK
        self.OFF_CA, self.OFF_HA = self.WA, 2 * self.WA
        self.OFF_UB = 3 * self.WA
        self.OFF_VB = self.OFF_UB + d
        self.OFF_GA = self.OFF_VB + d
        self.OFF_GB = self.OFF_GA + d
        self.shard_rows = (self.NP // N_DEV, self.WA // N_DEV, d // N_DEV, d // N_DEV, self.DFF // N_DEV, self.DFF // N_DEV)
        self.conv_rows = BF16_SUBLANES * max(1, -(-(3 * (self.WA // N_DEV) * 2) // (BF16_SUBLANES * d)))


def _params(sem=("arbitrary",), vmem=VMEM_LIMIT):
    return pltpu.CompilerParams(dimension_semantics=sem, vmem_limit_bytes=vmem)


def _mesh_pos():
    return lax.axis_index("x"), lax.axis_index("y"), lax.axis_index("c")


def _resident(shape):
    zeros = (0,) * len(shape)
    return pl.BlockSpec(shape, lambda *_: zeros, pipeline_mode=pl.Buffered(1))


class _TwoLevelGather:
    def __init__(self, shard_refs, out_refs, send_sems, recv_sems, local_sems):
        self.n = len(shard_refs)
        self.shard_refs, self.out_refs = shard_refs, out_refs
        self.send_sems, self.recv_sems, self.local_sems = send_sems, recv_sems, local_sems
        x, y, c = _mesh_pos()
        self.c = c
        self.me, self.sibling = (x, y, c), (x, y, 1 - c)
        self.chips = [(1 - x, y), (x, 1 - y), (1 - x, 1 - y)]

    def _slot(self, a, px, py, pc):
        rows = self.shard_refs[a].shape[0]
        return self.out_refs[a].at[pl.ds((4 * px + 2 * py + pc) * rows, rows), :]

    def _copy(self, a, k, block, to, src=None):
        return pltpu.make_async_remote_copy(
            src_ref=self._slot(a, *block) if src is None else src, dst_ref=self._slot(a, *block),
            send_sem=self.send_sems.at[7 * a + k], recv_sem=self.recv_sems.at[7 * a + k], device_id=to, device_id_type=MESH)

    def _mine(self):
        return [pltpu.make_async_copy(self.shard_refs[a], self._slot(a, *self.me), self.local_sems.at[a]) for a in range(self.n)]

    def _first(self):
        out = []
        for a in range(self.n):
            out.append(self._copy(a, 0, self.me, self.sibling, src=self.shard_refs[a]))
            out += [self._copy(a, 1 + j, self.me, (*chip, self.c), src=self.shard_refs[a]) for j, chip in enumerate(self.chips)]
        return out

    def _passed(self):
        return [self._copy(a, 4 + j, (*chip, self.c), self.sibling) for j, chip in enumerate(self.chips) for a in range(self.n)]

    def start(self):
        for cp in self._mine() + self._first():
            cp.start()

    def forward(self):
        passed = self._passed()
        for j, chip in enumerate(self.chips):
            for a in range(self.n):
                self._copy(a, 1 + j, (*chip, self.c), self.me).wait_recv()
                passed[j * self.n + a].start()

    def finish(self):
        for a in range(self.n):
            self._copy(a, 0, self.sibling, self.me).wait_recv()
            for j, chip in enumerate(self.chips):
                self._copy(a, 4 + j, (*chip, 1 - self.c), self.me).wait_recv()
        for cp in self._first() + self._passed():
            cp.wait_send()
        for cp in self._mine():
            cp.wait()

    @staticmethod
    def out_shapes(shards):
        return tuple(jax.ShapeDtypeStruct((N_DEV * s.shape[0], s.shape[1]), s.dtype) for s in shards)

    @staticmethod
    def semaphores(n):
        return [pltpu.SemaphoreType.DMA((7 * n,)), pltpu.SemaphoreType.DMA((7 * n,)), pltpu.SemaphoreType.DMA((n,))]


def _all_gather(shards):
    n = len(shards)

    def body(*refs):
        gather = _TwoLevelGather(refs[:n], refs[n:2 * n], *refs[2 * n:])
        gather.start()
        gather.forward()
        gather.finish()

    return pl.pallas_call(
        body, name="all_gather_w_in", out_shape=_TwoLevelGather.out_shapes(shards),
        in_specs=[HBM_SPEC] * n, out_specs=(HBM_SPEC,) * n, scratch_shapes=_TwoLevelGather.semaphores(n),
    )(*shards)


class _Exchange:
    def __init__(self, src_refs, recv_refs, rows, send_sems, recv_sems, local_sems):
        x, y, c = _mesh_pos()
        me = 4 * x + 2 * y + c
        self.own, self.sends, self.arrivals = [], [], []
        for a, (src, recv) in enumerate(zip(src_refs, recv_refs)):
            def blk(k, src=src, r=rows[a]):
                return src if r is None else src.at[pl.ds(k * r, r), :]

            self.own.append(pltpu.make_async_copy(blk(me), recv.at[me], local_sems.at[a]))
            for rel in range(1, N_DEV):
                px = 1 - x if rel & 4 else x
                py = 1 - y if rel & 2 else y
                pc = 1 - c if rel & 1 else c
                peer = 4 * px + 2 * py + pc
                sem = dict(send_sem=send_sems.at[7 * a + rel - 1], recv_sem=recv_sems.at[7 * a + rel - 1],
                           device_id=(px, py, pc), device_id_type=MESH)
                self.sends.append(pltpu.make_async_remote_copy(src_ref=blk(peer), dst_ref=recv.at[me], **sem))
                self.arrivals.append(pltpu.make_async_remote_copy(src_ref=blk(me), dst_ref=recv.at[peer], **sem))

    def start(self):
        for cp in self.own + self.sends:
            cp.start()

    def finish(self):
        for cp in self.arrivals:
            cp.wait_recv()
        for cp in self.sends:
            cp.wait_send()
        for cp in self.own:
            cp.wait()

    @staticmethod
    def out_shapes(srcs, rows):
        return tuple(jax.ShapeDtypeStruct((N_DEV, s.shape[0] if r is None else r, s.shape[1]), s.dtype) for s, r in zip(srcs, rows))

    @staticmethod
    def semaphores(n):
        return [pltpu.SemaphoreType.DMA((7 * n,)), pltpu.SemaphoreType.DMA((7 * n,)), pltpu.SemaphoreType.DMA((n,))]


def _sum_slots(slots, tile_rows, name):
    _, rows, cols = slots.shape
    tr = rows
    if N_DEV * rows * cols * slots.dtype.itemsize > 8 * 1024 * 1024:
        tr = next(c for c in (256, 192, 128, 64, 32, 16) if c <= tile_rows and rows % c == 0)

    def body(s_ref, o_ref):
        acc = s_ref[0].astype(F32)
        for k in range(1, N_DEV):
            acc = acc + s_ref[k].astype(F32)
        o_ref[...] = acc

    return pl.pallas_call(
        body, name=name, grid=(rows // tr,),
        in_specs=[pl.BlockSpec((N_DEV, tr, cols), lambda i: (0, i, 0))],
        out_specs=pl.BlockSpec((tr, cols), lambda i: (i, 0)),
        out_shape=jax.ShapeDtypeStruct((rows, cols), F32),
        compiler_params=_params(),
    )(slots)


def _gelu_and_grad(x):
    k0 = math.sqrt(2.0 / math.pi)
    k1 = 0.044715
    x2 = x * x
    th = jnp.tanh(k0 * x * (1.0 + k1 * x2))
    half = 0.5 * (1.0 + th)
    return x * half, half + 0.5 * x * (1.0 - th * th) * (k0 * (1.0 + 3.0 * k1 * x2))


def _ln_stats(r):
    mu = jnp.mean(r, axis=-1, keepdims=True)
    rc = r - mu
    var = jnp.mean(rc * rc, axis=-1, keepdims=True)
    rstd = lax.rsqrt(var + LN_EPS)
    return rc * rstd, rstd


def _ln_bwd(dxh, xh, rstd):
    return rstd * (dxh - jnp.mean(dxh, axis=-1, keepdims=True) - xh * jnp.mean(dxh * xh, axis=-1, keepdims=True))


def _colsum(a):
    return jnp.sum(a, axis=0, keepdims=True)


def _mm(a, b):
    return jnp.dot(a, b, preferred_element_type=F32)


def _halo_maps(tm, t, unit):
    per, last = tm // unit, t // unit - 1
    return (lambda i: (jnp.maximum(i * per - 1, 0), 0)), (lambda i: (jnp.minimum((i + 1) * per, last), 0))


def _proj_in(x2, w_int, b_gate, shards, dm, tm):
    t, d, npj = dm.T, dm.D, dm.NP
    cw = d // 2
    nt = t // tm
    n = len(shards)

    def body(x_ref, w_ref, bg_ref, *refs):
        p_ref = refs[n]
        gather = _TwoLevelGather(refs[:n], refs[n + 1:2 * n + 1], *refs[2 * n + 1:])
        i = pl.program_id(0)

        @pl.when(i == 0)
        def _():
            gather.start()

        @pl.when(i == nt // 2)
        def _():
            gather.forward()

        xb = x_ref[...].astype(BF16)
        for blk in range(npj // cw):
            lo = blk * cw
            acc = lax.dot_general(xb, w_ref[lo:lo + cw, :], NT_DIMS, preferred_element_type=F32)
            if lo >= dm.OFF_GA:
                acc = jax.nn.sigmoid(acc + bg_ref[:, lo - dm.OFF_GA:lo - dm.OFF_GA + cw])
            p_ref[:, lo:lo + cw] = acc.astype(BF16)

        @pl.when(i == nt - 1)
        def _():
            gather.finish()

    return pl.pallas_call(
        body, name="proj_in", grid=(nt,),
        in_specs=[pl.BlockSpec((tm, d), lambda i: (i, 0)), _resident((npj, d)), _resident((1, 2 * d))] + [HBM_SPEC] * n,
        out_specs=(pl.BlockSpec((tm, npj), lambda i: (i, 0)),) + (HBM_SPEC,) * n,
        out_shape=(jax.ShapeDtypeStruct((t, npj), BF16),) + _TwoLevelGather.out_shapes(shards),
        scratch_shapes=_TwoLevelGather.semaphores(n),
        compiler_params=_params(),
    )(x2, w_int, b_gate, *shards)


def _conv_taps(ext_ref, center, prev_blk, next_blk, first, last, tm):
    h = F32_SUBLANES
    ext_ref[0:h, :] = jnp.where(first, 0.0, prev_blk)
    ext_ref[h:h + tm, :] = center
    ext_ref[h + tm:h + tm + h, :] = jnp.where(last, 0.0, next_blk)
    return ext_ref[pl.ds(h - 1, tm), :], ext_ref[pl.ds(h + 1, tm), :]


def _spatial_mix(vn, ws_ref, bias_ref, mixed_ref, dm, tm):
    vb = vn.astype(BF16)
    for cc in range(tm // CHUNK):
        r0 = cc * CHUNK
        for h in range(dm.H):
            c0 = h * CHUNK
            m = _mm(ws_ref[h], vb[r0:r0 + CHUNK, c0:c0 + CHUNK])
            mixed_ref[r0:r0 + CHUNK, c0:c0 + CHUNK] = m + bias_ref[:, c0:c0 + CHUNK]
    return mixed_ref[...]


def _mixer_fwd(p, x2, w_pa, w_pb, w_o, conv_w8, vng, vnb, ws_b, bias_s, dm, tm):
    t, d, wa, npj = dm.T, dm.D, dm.WA, dm.NP
    nt = t // tm
    hb = BF16_SUBLANES
    prev_map, next_map = _halo_maps(tm, t, hb)

    def body(p_ref, pp_ref, pn_ref, x_ref, wpa_ref, wpb_ref, wo_ref, cw_ref, vng_ref, vnb_ref, ws_ref, bias_ref,
             ya_ref, yb_ref, r1_ref, ext_ref, mixed_ref):
        i = pl.program_id(0)

        def col(ref, lo, width):
            return ref[:, lo:lo + width].astype(F32)

        ch = col(p_ref, dm.OFF_CA, wa) * col(p_ref, dm.OFF_HA, wa)
        chp = (col(pp_ref, dm.OFF_CA, wa) * col(pp_ref, dm.OFF_HA, wa))[hb - F32_SUBLANES:hb]
        chn = (col(pn_ref, dm.OFF_CA, wa) * col(pn_ref, dm.OFF_HA, wa))[0:F32_SUBLANES]
        up, dn = _conv_taps(ext_ref, ch, chp, chn, i == 0, i == nt - 1, tm)
        a = col(p_ref, 0, wa) * (cw_ref[0:1, :] * up + cw_ref[1:2, :] * ch + cw_ref[2:3, :] * dn)
        ya = _mm(a.astype(BF16), wpa_ref[...])
        gv, _ = _gelu_and_grad(col(p_ref, dm.OFF_VB, d))
        xhv, _ = _ln_stats(gv)
        mixed = _spatial_mix(xhv * vng_ref[...] + vnb_ref[...], ws_ref, bias_ref, mixed_ref, dm, tm)
        gu, _ = _gelu_and_grad(col(p_ref, dm.OFF_UB, d))
        yb = _mm((gu * mixed).astype(BF16), wpb_ref[...])
        s = col(p_ref, dm.OFF_GA, d) * ya + col(p_ref, dm.OFF_GB, d) * yb
        mix = _mm(s.astype(BF16), wo_ref[...])
        ya_ref[...] = ya.astype(BF16)
        yb_ref[...] = yb.astype(BF16)
        r1_ref[...] = ALPHA * x_ref[...] + mix

    tile = lambda i: (i, 0)
    return pl.pallas_call(
        body, name="mixer_fwd", grid=(nt,),
        in_specs=[pl.BlockSpec((tm, npj), tile), pl.BlockSpec((hb, npj), prev_map), pl.BlockSpec((hb, npj), next_map),
                  pl.BlockSpec((tm, d), tile), _resident((wa, d)), _resident((d, d)), _resident((d, d)),
                  _resident((F32_SUBLANES, wa)), _resident((1, d)), _resident((1, d)),
                  _resident((dm.H, CHUNK, CHUNK)), _resident((CHUNK, d))],
        out_specs=(pl.BlockSpec((tm, d), tile), pl.BlockSpec((tm, d), tile), pl.BlockSpec((tm, d), tile)),
        out_shape=(jax.ShapeDtypeStruct((t, d), BF16), jax.ShapeDtypeStruct((t, d), BF16), jax.ShapeDtypeStruct((t, d), F32)),
        scratch_shapes=[pltpu.VMEM((tm + 2 * F32_SUBLANES, wa), F32), pltpu.VMEM((tm, d), F32)],
        compiler_params=_params(),
    )(p, p, p, x2, w_pa, w_pb, w_o, conv_w8, vng, vnb, ws_b, bias_s)


def _ffn_fwd(r1, tgt, w1, w2, ln1g, ln1b, ln2g, ln2b, dm, tm):
    t, d, dff = dm.T, dm.D, dm.DFF
    fc = dff // N_DEV

    def body(r1_ref, tgt_ref, w1_ref, w2_ref, g1_ref, b1_ref, g2_ref, b2_ref, relu_ref, x1_ref, dr2_ref, dr2b_ref, sums_ref):
        @pl.when(pl.program_id(0) == 0)
        def _():
            sums_ref[...] = jnp.zeros_like(sums_ref)

        xh1, _ = _ln_stats(r1_ref[...])
        x1 = xh1 * g1_ref[...] + b1_ref[...]
        x1b = x1.astype(BF16)
        x1_ref[...] = x1b
        ffn = jnp.zeros((tm, d), F32)
        for k in range(dff // fc):
            ks = slice(k * fc, (k + 1) * fc)
            r = jnp.maximum(_mm(x1b, w1_ref[:, ks]), 0.0)
            relu_ref[:, ks] = r.astype(BF16)
            ffn = ffn + _mm((r * r).astype(BF16), w2_ref[ks, :])
        xh2, rstd2 = _ln_stats(ALPHA * x1 + ffn)
        diff = xh2 * g2_ref[...] + b2_ref[...] - tgt_ref[...]
        dy = diff * (1.0 / d)
        dr2 = _ln_bwd(dy * g2_ref[...], xh2, rstd2)
        dr2_ref[...] = dr2
        dr2b_ref[...] = dr2.astype(BF16)
        sums_ref[0:1, :] += _colsum(diff * diff)
        sums_ref[1:2, :] += _colsum(dy * xh2)
        sums_ref[2:3, :] += _colsum(dy)

    tile = lambda i: (i, 0)
    vec = _resident((1, d))
    return pl.pallas_call(
        body, name="ffn_fwd", grid=(t // tm,),
        in_specs=[pl.BlockSpec((tm, d), tile), pl.BlockSpec((tm, d), tile), _resident((d, dff)), _resident((dff, d)),
                  vec, vec, vec, vec],
        out_specs=(pl.BlockSpec((tm, dff), tile), pl.BlockSpec((tm, d), tile), pl.BlockSpec((tm, d), tile),
                   pl.BlockSpec((tm, d), tile), pl.BlockSpec((F32_SUBLANES, d), lambda i: (0, 0))),
        out_shape=(jax.ShapeDtypeStruct((t, dff), BF16), jax.ShapeDtypeStruct((t, d), BF16), jax.ShapeDtypeStruct((t, d), F32),
                   jax.ShapeDtypeStruct((t, d), BF16), jax.ShapeDtypeStruct((F32_SUBLANES, d), F32)),
        compiler_params=_params(),
    )(r1, tgt, w1, w2, ln1g, ln1b, ln2g, ln2b)


def _ffn_bwd(r1, relu, dr2, w2t, w1t, ln1g, dm, tm):
    t, d, dff = dm.T, dm.D, dm.DFF
    fc = dff // N_DEV

    def body(r1_ref, relu_ref, dr2_ref, w2t_ref, w1t_ref, g1_ref, dr1_ref, dh_ref, sums_ref):
        @pl.when(pl.program_id(0) == 0)
        def _():
            sums_ref[...] = jnp.zeros_like(sums_ref)

        xh1, rstd1 = _ln_stats(r1_ref[...])
        dr2 = dr2_ref[...]
        dr2b = dr2.astype(BF16)
        dx1 = ALPHA * dr2
        for k in range(dff // fc):
            ks = slice(k * fc, (k + 1) * fc)
            dhb = (_mm(dr2b, w2t_ref[:, ks]) * (2.0 * relu_ref[:, ks].astype(F32))).astype(BF16)
            dh_ref[:, ks] = dhb
            dx1 = dx1 + _mm(dhb, w1t_ref[ks, :])
        dr1_ref[...] = _ln_bwd(dx1 * g1_ref[...], xh1, rstd1)
        sums_ref[0:1, :] += _colsum(dx1 * xh1)
        sums_ref[1:2, :] += _colsum(dx1)

    tile = lambda i: (i, 0)
    return pl.pallas_call(
        body, name="ffn_bwd", grid=(t // tm,),
        in_specs=[pl.BlockSpec((tm, d), tile), pl.BlockSpec((tm, dff), tile), pl.BlockSpec((tm, d), tile),
                  _resident((d, dff)), _resident((dff, d)), _resident((1, d))],
        out_specs=(pl.BlockSpec((tm, d), tile), pl.BlockSpec((tm, dff), tile), pl.BlockSpec((F32_SUBLANES, d), lambda i: (0, 0))),
        out_shape=(jax.ShapeDtypeStruct((t, d), F32), jax.ShapeDtypeStruct((t, dff), BF16),
                   jax.ShapeDtypeStruct((F32_SUBLANES, d), F32)),
        compiler_params=_params(),
    )(r1, relu, dr2, w2t, w1t, ln1g)


def _wgrad(pairs, tt, fb, name, xchg=()):
    n, m = len(pairs), len(xchg)
    t, f = pairs[0][0].shape
    d = pairs[0][1].shape[1]
    squares = [sq for _, _, sq in pairs]
    nj, ni = f // fb, t // tt
    xrows = [r for _, r in xchg]

    def body(*refs):
        ins, xsrc = refs[:2 * n], refs[2 * n:2 * n + m]
        outs, xrecv = refs[2 * n + m:3 * n + m], refs[3 * n + m:3 * n + 2 * m]
        accs, sems = refs[3 * n + 2 * m:4 * n + 2 * m], refs[4 * n + 2 * m:]
        j, i = pl.program_id(0), pl.program_id(1)
        exchange = _Exchange(xsrc, xrecv, xrows, *sems) if m else None

        if m:
            @pl.when(jnp.logical_and(j == 0, i == 0))
            def _():
                exchange.start()

        @pl.when(i == 0)
        def _():
            for acc in accs:
                acc[...] = jnp.zeros_like(acc)

        for q in range(n):
            lhs = ins[2 * q][...]
            if squares[q]:
                lf = lhs.astype(F32)
                lhs = (lf * lf).astype(BF16)
            accs[q][...] += lax.dot_general(lhs, ins[2 * q + 1][...].astype(BF16), TN_DIMS, preferred_element_type=F32)

        @pl.when(i == ni - 1)
        def _():
            for q in range(n):
                outs[q][...] = accs[q][...].astype(BF16)

        if m:
            @pl.when(jnp.logical_and(j == nj - 1, i == ni - 1))
            def _():
                exchange.finish()

    lhs_spec = pl.BlockSpec((tt, fb), lambda j, i: (i, j))
    rhs_spec = pl.BlockSpec((tt, d), lambda j, i: (i, 0))
    out_spec = pl.BlockSpec((fb, d), lambda j, i: (j, 0))
    xsrcs = [a_ for a_, _ in xchg]
    return pl.pallas_call(
        body, name=name, grid=(nj, ni),
        in_specs=[lhs_spec, rhs_spec] * n + [HBM_SPEC] * m, out_specs=(out_spec,) * n + (HBM_SPEC,) * m,
        out_shape=(jax.ShapeDtypeStruct((f, d), BF16),) * n + _Exchange.out_shapes(xsrcs, xrows),
        scratch_shapes=[pltpu.VMEM((fb, d), F32)] * n + (_Exchange.semaphores(m) if m else []),
        compiler_params=_params(("arbitrary", "arbitrary")),
    )(*[a_ for lhs, rhs, _ in pairs for a_ in (lhs, rhs)], *xsrcs)


def _mixer_bwd(dr1, p, ya, yb, w_ot, w_pat, w_pbt, conv_w8, vng, vnb, ws_b, wst_b, bias_s, head_sel, xchg, dm, tm):
    t, d, wa, npj = dm.T, dm.D, dm.WA, dm.NP
    nt = t // tm
    h8, hb = F32_SUBLANES, BF16_SUBLANES
    ext = tm + 2 * h8
    prev_f, next_f = _halo_maps(tm, t, h8)
    prev_b, next_b = _halo_maps(tm, t, hb)
    nx = len(xchg)
    xsrcs, xrows = [a_ for a_, _ in xchg], [r for _, r in xchg]

    def body(dr_ref, drp_ref, drn_ref, p_ref, pp_ref, pn_ref, ya_ref, yb_ref, wot_ref, wpat_ref, wpbt_ref,
             cw_ref, vng_ref, vnb_ref, ws_ref, wst_ref, bias_ref, sel_ref,
             *refs):
        xsrc = refs[:nx]
        dp_ref, a_ref, dya_ref, s_ref, bb_ref, dyb_ref, dws_ref, dbs_ref, dcw_ref, dbg_ref, dvn_ref = refs[nx:nx + 11]
        xrecv = refs[nx + 11:2 * nx + 11]
        ext_ref, ext2_ref, mixed_ref, dvnm_ref = refs[2 * nx + 11:2 * nx + 15]
        exchange = _Exchange(xsrc, xrecv, xrows, *refs[2 * nx + 15:])
        i = pl.program_id(0)

        @pl.when(i == 0)
        def _():
            exchange.start()
            for ref in (dws_ref, dbs_ref, dcw_ref, dbg_ref, dvn_ref):
                ref[...] = jnp.zeros_like(ref)

        def col(ref, lo, width):
            return ref[:, lo:lo + width].astype(F32)

        def ext_rows(prev_blk, center, next_blk):
            return jnp.concatenate([prev_blk, center, next_blk], axis=0)

        def ext_col(lo, width):
            return ext_rows(col(pp_ref, lo, width)[hb - h8:hb], col(p_ref, lo, width), col(pn_ref, lo, width)[0:h8])

        row = lax.broadcasted_iota(jnp.int32, (ext, 1), 0) + (i * tm - h8)
        inside = jnp.logical_and(row >= 0, row < t)
        dr_e = ext_rows(drp_ref[...], dr_ref[...], drn_ref[...])
        ds_e = _mm(dr_e.astype(BF16), wot_ref[...])
        dya_e = ds_e * ext_col(dm.OFF_GA, d)
        da_e = _mm(dya_e.astype(BF16), wpat_ref[...])
        dcv_e = jnp.where(inside, da_e * ext_col(0, wa), 0.0)
        ch_e = jnp.where(inside, ext_col(dm.OFF_CA, wa) * ext_col(dm.OFF_HA, wa), 0.0)
        ext_ref[...] = ch_e
        ext2_ref[...] = dcv_e
        ch, ch_up, ch_dn = ch_e[h8:h8 + tm], ext_ref[pl.ds(h8 - 1, tm), :], ext_ref[pl.ds(h8 + 1, tm), :]
        dcv, dcv_up, dcv_dn = dcv_e[h8:h8 + tm], ext2_ref[pl.ds(h8 - 1, tm), :], ext2_ref[pl.ds(h8 + 1, tm), :]
        w0, w1, w2 = cw_ref[0:1, :], cw_ref[1:2, :], cw_ref[2:3, :]
        cv = w0 * ch_up + w1 * ch + w2 * ch_dn
        dp_ref[:, 0:wa] = (da_e[h8:h8 + tm] * cv).astype(BF16)
        dch = w0 * dcv_dn + w1 * dcv + w2 * dcv_up
        dp_ref[:, dm.OFF_CA:dm.OFF_CA + wa] = (dch * col(p_ref, dm.OFF_HA, wa)).astype(BF16)
        dp_ref[:, dm.OFF_HA:dm.OFF_HA + wa] = (dch * col(p_ref, dm.OFF_CA, wa)).astype(BF16)
        dcw_ref[0:1, :] += _colsum(dcv * ch_up)
        dcw_ref[1:2, :] += _colsum(dcv * ch)
        dcw_ref[2:3, :] += _colsum(dcv * ch_dn)
        a_ref[...] = (col(p_ref, 0, wa) * cv).astype(BF16)
        dya_ref[...] = dya_e[h8:h8 + tm].astype(BF16)
        ds = ds_e[h8:h8 + tm]
        g_a, g_b = col(p_ref, dm.OFF_GA, d), col(p_ref, dm.OFF_GB, d)
        y_a, y_b = ya_ref[...].astype(F32), yb_ref[...].astype(F32)
        s_ref[...] = (g_a * y_a + g_b * y_b).astype(BF16)
        dzga = ds * y_a * g_a * (1.0 - g_a)
        dzgb = ds * y_b * g_b * (1.0 - g_b)
        dp_ref[:, dm.OFF_GA:dm.OFF_GA + d] = dzga.astype(BF16)
        dp_ref[:, dm.OFF_GB:dm.OFF_GB + d] = dzgb.astype(BF16)
        dbg_ref[0:1, 0:d] += _colsum(dzga)
        dbg_ref[0:1, d:2 * d] += _colsum(dzgb)
        dyb = (ds * g_b).astype(BF16)
        dyb_ref[...] = dyb
        gv, dgelu_v = _gelu_and_grad(col(p_ref, dm.OFF_VB, d))
        xhv, rstdv = _ln_stats(gv)
        vn = xhv * vng_ref[...] + vnb_ref[...]
        mixed = _spatial_mix(vn, ws_ref, bias_ref, mixed_ref, dm, tm)
        gu, dgelu_u = _gelu_and_grad(col(p_ref, dm.OFF_UB, d))
        bb_ref[...] = (gu * mixed).astype(BF16)
        dbb = _mm(dyb, wpbt_ref[...])
        dp_ref[:, dm.OFF_UB:dm.OFF_UB + d] = (dbb * mixed * dgelu_u).astype(BF16)
        dmb = (dbb * gu).astype(BF16)
        vb = vn.astype(BF16)
        dbs = jnp.zeros((CHUNK, CHUNK), F32)
        for cc in range(tm // CHUNK):
            r0 = cc * CHUNK
            dbs = dbs + _mm(dmb[r0:r0 + CHUNK, :], sel_ref[...])
            for h in range(dm.H):
                c0 = h * CHUNK
                blk = dmb[r0:r0 + CHUNK, c0:c0 + CHUNK]
                dvnm_ref[r0:r0 + CHUNK, c0:c0 + CHUNK] = _mm(wst_ref[h], blk)
                dws_ref[h] += lax.dot_general(blk, vb[r0:r0 + CHUNK, c0:c0 + CHUNK], NT_DIMS, preferred_element_type=F32)
        dbs_ref[...] += dbs
        dvn = dvnm_ref[...]
        dvn_ref[0:1, :] += _colsum(dvn * xhv)
        dvn_ref[1:2, :] += _colsum(dvn)
        dp_ref[:, dm.OFF_VB:dm.OFF_VB + d] = (_ln_bwd(dvn * vng_ref[...], xhv, rstdv) * dgelu_v).astype(BF16)

        @pl.when(i == nt - 1)
        def _():
            exchange.finish()

    full = lambda i: (0, 0)
    tile = lambda i: (i, 0)
    hcc = _resident((dm.H, CHUNK, CHUNK))
    tok = lambda w, dt: jax.ShapeDtypeStruct((t, w), dt)
    return pl.pallas_call(
        body, name="mixer_bwd", grid=(nt,),
        in_specs=[pl.BlockSpec((tm, d), tile), pl.BlockSpec((h8, d), prev_f), pl.BlockSpec((h8, d), next_f),
                  pl.BlockSpec((tm, npj), tile), pl.BlockSpec((hb, npj), prev_b), pl.BlockSpec((hb, npj), next_b),
                  pl.BlockSpec((tm, d), tile), pl.BlockSpec((tm, d), tile),
                  _resident((d, d)), _resident((d, wa)), _resident((d, d)),
                  _resident((h8, wa)), _resident((1, d)), _resident((1, d)), hcc, hcc, _resident((CHUNK, d)),
                  _resident((d, CHUNK))] + [HBM_SPEC] * nx,
        out_specs=(pl.BlockSpec((tm, npj), tile), pl.BlockSpec((tm, wa), tile), pl.BlockSpec((tm, d), tile),
                   pl.BlockSpec((tm, d), tile), pl.BlockSpec((tm, d), tile), pl.BlockSpec((tm, d), tile),
                   pl.BlockSpec((dm.H, CHUNK, CHUNK), lambda i: (0, 0, 0)), pl.BlockSpec((CHUNK, CHUNK), full),
                   pl.BlockSpec((h8, wa), full), pl.BlockSpec((h8, 2 * d), full), pl.BlockSpec((h8, d), full))
        + (HBM_SPEC,) * nx,
        out_shape=(tok(npj, BF16), tok(wa, BF16), tok(d, BF16), tok(d, BF16), tok(d, BF16), tok(d, BF16),
                   jax.ShapeDtypeStruct((dm.H, CHUNK, CHUNK), F32), jax.ShapeDtypeStruct((CHUNK, CHUNK), F32),
                   jax.ShapeDtypeStruct((h8, wa), F32), jax.ShapeDtypeStruct((h8, 2 * d), F32),
                   jax.ShapeDtypeStruct((h8, d), F32)) + _Exchange.out_shapes(xsrcs, xrows),
        scratch_shapes=[pltpu.VMEM((ext, wa), F32), pltpu.VMEM((ext, wa), F32), pltpu.VMEM((tm, d), F32),
                        pltpu.VMEM((tm, d), F32)] + _Exchange.semaphores(nx),
        compiler_params=_params(),
    )(dr1, dr1, dr1, p, p, p, ya, yb, w_ot, w_pat, w_pbt, conv_w8, vng, vnb, ws_b, wst_b, bias_s, head_sel, *xsrcs)


def _input_grad(dp, dr1, w_int, xchg, dm, tm):
    t, d, npj = dm.T, dm.D, dm.NP
    nt = t // tm
    nx = len(xchg)
    xsrcs, xrows = [a_ for a_, _ in xchg], [r for _, r in xchg]

    def body(dp_ref, dr_ref, w_ref, *refs):
        dx_ref = refs[nx]
        exchange = _Exchange(refs[:nx], refs[nx + 1:2 * nx + 1], xrows, *refs[2 * nx + 1:])
        i = pl.program_id(0)

        @pl.when(i == 0)
        def _():
            exchange.start()

        dx_ref[...] = ALPHA * dr_ref[...] + _mm(dp_ref[...], w_ref[...])

        @pl.when(i == nt - 1)
        def _():
            exchange.finish()

    return pl.pallas_call(
        body, name="input_grad", grid=(nt,),
        in_specs=[pl.BlockSpec((tm, npj), lambda i: (i, 0)), pl.BlockSpec((tm, d), lambda i: (i, 0)), _resident((npj, d))]
        + [HBM_SPEC] * nx,
        out_specs=(pl.BlockSpec((tm, d), lambda i: (i, 0)),) + (HBM_SPEC,) * nx,
        out_shape=(jax.ShapeDtypeStruct((t, d), F32),) + _Exchange.out_shapes(xsrcs, xrows),
        scratch_shapes=_Exchange.semaphores(nx),
        compiler_params=_params(),
    )(dp, dr1, w_int, *xsrcs)


def _adamw(w, g, m, v, name):
    rows, cols = w.shape
    tr = 256 if rows % 256 == 0 else rows
    bc1 = 1.0 - ADAM_B1 ** ADAM_STEP
    bc2 = 1.0 - ADAM_B2 ** ADAM_STEP

    def body(w_ref, g_ref, m_ref, v_ref, d_ref, nm_ref, nv_ref):
        g_ = g_ref[...]
        nm = ADAM_B1 * m_ref[...] + (1.0 - ADAM_B1) * g_
        nv = ADAM_B2 * v_ref[...] + (1.0 - ADAM_B2) * (g_ * g_)
        d_ref[...] = -ADAM_LR * ((nm / bc1) / (jnp.sqrt(nv / bc2) + ADAM_EPS) + ADAM_WD * w_ref[...])
        nm_ref[...] = nm
        nv_ref[...] = nv

    spec = pl.BlockSpec((tr, cols), lambda i: (i, 0))
    shp = jax.ShapeDtypeStruct((rows, cols), F32)
    return pl.pallas_call(
        body, name=name, grid=(rows // tr,), in_specs=[spec] * 4, out_specs=(spec,) * 3, out_shape=(shp,) * 3,
        compiler_params=_params(),
    )(w, g, m, v)


def _to_slab(parts):
    flat = jnp.concatenate([q.reshape(-1) for q in parts])
    pad = (-flat.shape[0]) % (F32_SUBLANES * LANES)
    return jnp.pad(flat, (0, pad)).reshape(-1, LANES)


def _from_slab(slab, shapes):
    flat = slab.reshape(-1)
    out, off = [], 0
    for s in shapes:
        n = math.prod(s)
        out.append(flat[off:off + n].reshape(s))
        off += n
    return out


def kernel(x, w_in, b_gate, conv_w, v_norm_g, v_norm_b, w_s, b_s, w_pa, w_pb, w_o, ln1_g, ln1_b, w_ff1, w_ff2, ln2_g, ln2_b, loss_target, m_w_in, m_b_gate, m_conv_w, m_v_norm_g, m_v_norm_b, m_w_s, m_b_s, m_w_pa, m_w_pb, m_w_o, m_ln1_g, m_ln1_b, m_w_ff1, m_w_ff2, m_ln2_g, m_ln2_b, v_w_in, v_b_gate, v_conv_w, v_v_norm_g, v_v_norm_b, v_w_s, v_b_s, v_w_pa, v_w_pb, v_w_o, v_ln1_g, v_ln1_b, v_w_ff1, v_w_ff2, v_ln2_g, v_ln2_b):
    t, d = x.shape[1], x.shape[2]
    dm = _Dims(t, d)
    tm = 256 if t % 256 == 0 else CHUNK
    tm_big = 512 if t % 512 == 0 else tm
    tt = 1024 if t % 1024 == 0 else tm
    me = 4 * lax.axis_index("x") + 2 * lax.axis_index("y") + lax.axis_index("c")
    x2, tgt = x[0], loss_target[0]

    conv_bits = lax.bitcast_convert_type(conv_w[0], BF16).reshape(-1)
    conv_blk = jnp.pad(conv_bits, (0, dm.conv_rows * d - conv_bits.shape[0])).reshape(dm.conv_rows, d)
    w_int, conv_g = _all_gather([w_in[0].T.astype(BF16), conv_blk])
    wa8 = dm.WA // N_DEV
    conv_all = lax.bitcast_convert_type(conv_g.reshape(N_DEV, -1)[:, :3 * wa8 * 2].reshape(N_DEV, 3, wa8, 2), F32)
    conv_full = jnp.transpose(conv_all, (1, 0, 2)).reshape(3, dm.WA)
    conv_w8 = jnp.pad(conv_full, ((0, F32_SUBLANES - 3), (0, 0)))
    ws_b = w_s[0].astype(BF16)
    wst_b = jnp.transpose(w_s[0], (0, 2, 1)).astype(BF16)
    bias_s = jnp.repeat(b_s[0].T, CHUNK, axis=1)
    head_sel = (jnp.arange(d)[:, None] // CHUNK == jnp.arange(CHUNK)[None, :]).astype(BF16)

    p, w_pa_f, w_pb_f, w_o_f, w_1t, w_2 = _proj_in(
        x2, w_int, b_gate, [w_pa[0].astype(BF16), w_pb[0].astype(BF16), w_o[0].astype(BF16), w_ff1[0].T.astype(BF16),
                            w_ff2[0].astype(BF16)], dm, tm_big)
    ya, yb, r1 = _mixer_fwd(p, x2, w_pa_f, w_pb_f, w_o_f, conv_w8, v_norm_g, v_norm_b, ws_b, bias_s, dm, tm)
    relu, x1b, dr2, dr2b, sums2 = _ffn_fwd(r1, tgt, w_1t.T, w_2, ln1_g, ln1_b, ln2_g, ln2_b, dm, tm_big)
    dr1, dh1, sums1 = _ffn_bwd(r1, relu, dr2, w_2.T, w_1t, ln1_g, dm, tm_big)
    fb = min(1024, dm.DFF)
    rows = dm.shard_rows
    g_ff1t, g_ff2 = _wgrad([(dh1, x1b, False), (relu, dr2b, True)], tt, fb, "ffn_wgrad")
    dp, a_m, dya, s_m, bb_m, dyb, g_ws, g_bs_t, g_cw, g_bg, g_vn, got_ff1t, got_ff2 = _mixer_bwd(
        dr1, p, ya, yb, w_o_f.T, w_pa_f.T, w_pb_f.T, conv_w8, v_norm_g, v_norm_b, ws_b, wst_b, bias_s, head_sel,
        [(g_ff1t, rows[4]), (g_ff2, rows[5])], dm, tm)
    (g_pa,) = _wgrad([(a_m, dya, False)], tt, dm.WA, "w_pa_grad")
    g_o, g_pb = _wgrad([(s_m, dr1, False), (bb_m, dyb, False)], tt, d, "w_o_pb_grad")
    g_int, got_pa, got_pb, got_o = _wgrad([(dp, x2, False)], tt, 17 * LANES, "w_in_grad",
                                          xchg=[(g_pa, rows[1]), (g_pb, rows[2]), (g_o, rows[3])])
    small_parts = [g_bg[0], g_cw[0:3], g_vn[0], g_vn[1], g_ws, g_bs_t[:, :dm.H].T,
                   sums1[0], sums1[1], sums2[1], sums2[2], sums2[0]]
    grad_x, got_int, got_s = _input_grad(dp, dr1, w_int, [(g_int, rows[0]), (_to_slab(small_parts), None)], dm, tm_big)

    cut = [_sum_slots(g, 256, "sum_grads_%d" % k) for k, g in enumerate((got_int, got_pa, got_pb, got_o, got_ff1t, got_ff2))]
    ssum = _sum_slots(got_s, 256, "sum_small")
    (s_bg, s_cw, s_vng, s_vnb, s_ws, s_bs, s_l1g, s_l1b, s_l2g, s_l2b, s_sq) = _from_slab(
        ssum, [(1, 2 * d), (3, dm.WA), (1, d), (1, d), (1, dm.H, CHUNK, CHUNK), (1, dm.H, CHUNK),
               (1, d), (1, d), (1, d), (1, d), (d,)])
    loss = 0.5 * jnp.sum(s_sq) / d
    s_cw = lax.dynamic_slice(s_cw, (0, me * wa8), (3, wa8))[None]
    big_g = [cut[0].T, cut[1], cut[2], cut[3], cut[4].T, cut[5]]

    big_w = [(w_in, m_w_in, v_w_in), (w_pa, m_w_pa, v_w_pa), (w_pb, m_w_pb, v_w_pb), (w_o, m_w_o, v_w_o),
             (w_ff1, m_w_ff1, v_w_ff1), (w_ff2, m_w_ff2, v_w_ff2)]
    big_out = [_adamw(w[0], g, m[0], v[0], "adamw_%d" % k) for k, ((w, m, v), g) in enumerate(zip(big_w, big_g))]
    small_w = [(b_gate, m_b_gate, v_b_gate), (conv_w, m_conv_w, v_conv_w), (v_norm_g, m_v_norm_g, v_v_norm_g),
               (v_norm_b, m_v_norm_b, v_v_norm_b), (w_s, m_w_s, v_w_s), (b_s, m_b_s, v_b_s), (ln1_g, m_ln1_g, v_ln1_g),
               (ln1_b, m_ln1_b, v_ln1_b), (ln2_g, m_ln2_g, v_ln2_g), (ln2_b, m_ln2_b, v_ln2_b)]
    small_g = [s_bg, s_cw, s_vng, s_vnb, s_ws, s_bs, s_l1g, s_l1b, s_l2g, s_l2b]
    small_shapes = [w.shape for w, _, _ in small_w]
    sd, sm, sv = _adamw(_to_slab([w for w, _, _ in small_w]), _to_slab(small_g), _to_slab([m for _, m, _ in small_w]),
                        _to_slab([v for _, _, v in small_w]), "adamw_small")
    small_out = list(zip(_from_slab(sd, small_shapes), _from_slab(sm, small_shapes), _from_slab(sv, small_shapes)))

    order = [("b", 0), ("s", 0), ("s", 1), ("s", 2), ("s", 3), ("s", 4), ("s", 5), ("b", 1), ("b", 2), ("b", 3),
             ("s", 6), ("s", 7), ("b", 4), ("b", 5), ("s", 8), ("s", 9)]
    grads, deltas, new_m, new_v = [], [], [], []
    for kind, k in order:
        if kind == "b":
            g, (dl, nm, nv) = big_g[k][None], big_out[k]
            dl, nm, nv = dl[None], nm[None], nv[None]
        else:
            g, (dl, nm, nv) = small_g[k], small_out[k]
        grads.append(g)
        deltas.append(dl)
        new_m.append(nm)
        new_v.append(nv)
    return (loss, grad_x[None], *grads, *deltas, *new_m, *new_v)
```

```python
import math

import jax
import jax.numpy as jnp
from jax import lax
from jax.experimental import pallas as pl
from jax.experimental.pallas import tpu as pltpu

F32 = jnp.float32
BF16 = jnp.bfloat16
N_DEV = 8
CHUNK = 128
LN_EPS = 1e-5
ALPHA = 2.0 ** 0.25
ADAM_LR, ADAM_B1, ADAM_B2, ADAM_EPS, ADAM_WD, ADAM_STEP = 0.001, 0.9, 0.999, 1e-08, 0.01, 10
F32_SUBLANES = 8
BF16_SUBLANES = 16
LANES = 128
VMEM_LIMIT = 56 * 1024 * 1024
MESH = pl.DeviceIdType.MESH
NT_DIMS = (((1,), (1,)), ((), ()))
TN_DIMS = (((0,), (0,)), ((), ()))
HBM_SPEC = pl.BlockSpec(memory_space=pltpu.HBM)


class _Dims:
    def __init__(self, t, d):
        self.T, self.D = t, d
        self.WA = 3 * d // 2
        self.NP = 3 * self.WA + 4 * d
        self.DFF = 4 * d
        self.H = d // CHUNK
        self.OFF_CA, self.OFF_HA = self.WA, 2 * self.WA
        self.OFF_UB = 3 * self.WA
        self.OFF_VB = self.OFF_UB + d
        self.OFF_GA = self.OFF_VB + d
        self.OFF_GB = self.OFF_GA + d
        self.shard_rows = (self.NP // N_DEV, self.WA // N_DEV, d // N_DEV, d // N_DEV, self.DFF // N_DEV, self.DFF // N_DEV)
        self.conv_rows = BF16_SUBLANES * max(1, -(-(3 * (self.WA // N_DEV) * 2) // (BF16_SUBLANES * d)))


def _params(sem=("arbitrary",), vmem=VMEM_LIMIT):
    return pltpu.CompilerParams(dimension_semantics=sem, vmem_limit_bytes=vmem)


def _mesh_pos():
    return lax.axis_index("x"), lax.axis_index("y"), lax.axis_index("c")


def _resident(shape):
    zeros = (0,) * len(shape)
    return pl.BlockSpec(shape, lambda *_: zeros, pipeline_mode=pl.Buffered(1))


class _TwoLevelGather:
    def __init__(self, shard_refs, out_refs, send_sems, recv_sems, local_sems):
        self.n = len(shard_refs)
        self.shard_refs, self.out_refs = shard_refs, out_refs
        self.send_sems, self.recv_sems, self.local_sems = send_sems, recv_sems, local_sems
        x, y, c = _mesh_pos()
        self.c = c
        self.me, self.sibling = (x, y, c), (x, y, 1 - c)
        self.chips = [(1 - x, y), (x, 1 - y), (1 - x, 1 - y)]

    def _slot(self, a, px, py, pc):
        rows = self.shard_refs[a].shape[0]
        return self.out_refs[a].at[pl.ds((4 * px + 2 * py + pc) * rows, rows), :]

    def _copy(self, a, k, block, to, src=None):
        return pltpu.make_async_remote_copy(
            src_ref=self._slot(a, *block) if src is None else src, dst_ref=self._slot(a, *block),
            send_sem=self.send_sems.at[7 * a + k], recv_sem=self.recv_sems.at[7 * a + k], device_id=to, device_id_type=MESH)

    def _mine(self):
        return [pltpu.make_async_copy(self.shard_refs[a], self._slot(a, *self.me), self.local_sems.at[a]) for a in range(self.n)]

    def _first(self):
        out = []
        for a in range(self.n):
            out.append(self._copy(a, 0, self.me, self.sibling, src=self.shard_refs[a]))
            out += [self._copy(a, 1 + j, self.me, (*chip, self.c), src=self.shard_refs[a]) for j, chip in enumerate(self.chips)]
        return out

    def _passed(self):
        return [self._copy(a, 4 + j, (*chip, self.c), self.sibling) for j, chip in enumerate(self.chips) for a in range(self.n)]

    def start(self):
        for cp in self._mine() + self._first():
            cp.start()

    def wait_own_chip(self, a):
        self._mine()[a].wait()
        self._copy(a, 0, self.sibling, self.me).wait_recv()

    def forward(self, a, j):
        chip = self.chips[j]
        self._copy(a, 1 + j, (*chip, self.c), self.me).wait_recv()
        self._copy(a, 4 + j, (*chip, self.c), self.sibling).start()

    def wait_forwarded(self, a, j):
        self._copy(a, 4 + j, (*self.chips[j], 1 - self.c), self.me).wait_recv()

    def wait_sends(self):
        for cp in self._first() + self._passed():
            cp.wait_send()

    @staticmethod
    def out_shapes(shards):
        return tuple(jax.ShapeDtypeStruct((N_DEV * s.shape[0], s.shape[1]), s.dtype) for s in shards)

    @staticmethod
    def semaphores(n):
        return [pltpu.SemaphoreType.DMA((7 * n,)), pltpu.SemaphoreType.DMA((7 * n,)), pltpu.SemaphoreType.DMA((n,))]


class _Exchange:
    def __init__(self, src_refs, recv_refs, rows, send_sems, recv_sems, local_sems):
        x, y, c = _mesh_pos()
        me = 4 * x + 2 * y + c
        self.own, self.sends, self.arrivals = [], [], []
        for a, (src, recv) in enumerate(zip(src_refs, recv_refs)):
            def blk(k, src=src, r=rows[a]):
                return src if r is None else src.at[pl.ds(k * r, r), :]

            self.own.append(pltpu.make_async_copy(blk(me), recv.at[me], local_sems.at[a]))
            for rel in range(1, N_DEV):
                px = 1 - x if rel & 4 else x
                py = 1 - y if rel & 2 else y
                pc = 1 - c if rel & 1 else c
                peer = 4 * px + 2 * py + pc
                sem = dict(send_sem=send_sems.at[7 * a + rel - 1], recv_sem=recv_sems.at[7 * a + rel - 1],
                           device_id=(px, py, pc), device_id_type=MESH)
                self.sends.append(pltpu.make_async_remote_copy(src_ref=blk(peer), dst_ref=recv.at[me], **sem))
                self.arrivals.append(pltpu.make_async_remote_copy(src_ref=blk(me), dst_ref=recv.at[peer], **sem))

    def start(self):
        for cp in self.own + self.sends:
            cp.start()

    def finish(self):
        for cp in self.arrivals:
            cp.wait_recv()
        for cp in self.sends:
            cp.wait_send()
        for cp in self.own:
            cp.wait()

    @staticmethod
    def out_shapes(srcs, rows):
        return tuple(jax.ShapeDtypeStruct((N_DEV, s.shape[0] if r is None else r, s.shape[1]), s.dtype) for s, r in zip(srcs, rows))

    @staticmethod
    def semaphores(n):
        return [pltpu.SemaphoreType.DMA((7 * n,)), pltpu.SemaphoreType.DMA((7 * n,)), pltpu.SemaphoreType.DMA((n,))]


def _sum_slots(slots, tile_rows, name):
    _, rows, cols = slots.shape
    tr = rows
    if N_DEV * rows * cols * slots.dtype.itemsize > 8 * 1024 * 1024:
        tr = next(c for c in (256, 192, 128, 64, 32, 16) if c <= tile_rows and rows % c == 0)

    def body(s_ref, o_ref):
        acc = s_ref[0].astype(F32)
        for k in range(1, N_DEV):
            acc = acc + s_ref[k].astype(F32)
        o_ref[...] = acc

    return pl.pallas_call(
        body, name=name, grid=(rows // tr,),
        in_specs=[pl.BlockSpec((N_DEV, tr, cols), lambda i: (0, i, 0))],
        out_specs=pl.BlockSpec((tr, cols), lambda i: (i, 0)),
        out_shape=jax.ShapeDtypeStruct((rows, cols), F32),
        compiler_params=_params(),
    )(slots)


def _gelu_and_grad(x):
    k0 = math.sqrt(2.0 / math.pi)
    k1 = 0.044715
    x2 = x * x
    th = jnp.tanh(k0 * x * (1.0 + k1 * x2))
    half = 0.5 * (1.0 + th)
    return x * half, half + 0.5 * x * (1.0 - th * th) * (k0 * (1.0 + 3.0 * k1 * x2))


def _ln_stats(r):
    mu = jnp.mean(r, axis=-1, keepdims=True)
    rc = r - mu
    var = jnp.mean(rc * rc, axis=-1, keepdims=True)
    rstd = lax.rsqrt(var + LN_EPS)
    return rc * rstd, rstd


def _ln_bwd(dxh, xh, rstd):
    return rstd * (dxh - jnp.mean(dxh, axis=-1, keepdims=True) - xh * jnp.mean(dxh * xh, axis=-1, keepdims=True))


def _colsum(a):
    return jnp.sum(a, axis=0, keepdims=True)


def _mm(a, b):
    return jnp.dot(a, b, preferred_element_type=F32)


def _halo_maps(tm, t, unit):
    per, last = tm // unit, t // unit - 1
    return (lambda i: (jnp.maximum(i * per - 1, 0), 0)), (lambda i: (jnp.minimum((i + 1) * per, last), 0))


def _proj_in(x2, b_gate, shards, chip_order, dm, tm):
    t, d, npj = dm.T, dm.D, dm.NP
    nt = t // tm
    n = len(shards)
    n_chips = N_DEV // 2
    gw = npj // n_chips
    pre = dm.OFF_GA - (n_chips - 1) * gw
    cw = 2 * LANES
    assert gw % LANES == 0 and 0 <= pre < gw and pre % LANES == 0 and (gw - pre) % cw == 0 and gw - pre == 2 * d

    def body(order_ref, x_ref, bg_ref, *refs):
        p_ref = refs[n]
        w_out = refs[n + 1]
        wbuf, load_sem = refs[2 * n + 1], refs[2 * n + 2]
        gather = _TwoLevelGather(refs[:n], refs[n + 1:2 * n + 1], *refs[2 * n + 3:])
        j, i = pl.program_id(0), pl.program_id(1)

        @pl.when(jnp.logical_and(j == 0, i == 0))
        def _():
            gather.start()

        for step in range(n_chips):
            @pl.when(jnp.logical_and(j == step, i == 0))
            def _(step=step):
                if step == 0:
                    gather.wait_own_chip(0)
                else:
                    gather.forward(0, step - 1)
                    gather.wait_forwarded(0, step - 1)
                load = pltpu.make_async_copy(w_out.at[pl.ds(order_ref[step] * gw, gw), :], wbuf, load_sem)
                load.start()
                load.wait()

        @pl.when(jnp.logical_and(j == n_chips - 1, i == nt // 2))
        def _():
            for a in range(1, n):
                for k in range(n_chips - 1):
                    gather.forward(a, k)

        def product(gated):
            xb = x_ref[...].astype(BF16)
            if pre:
                p_ref[:, 0:pre] = lax.dot_general(xb, wbuf[0:pre, :], NT_DIMS, preferred_element_type=F32).astype(BF16)
            for lo in range(pre, gw, cw):
                acc = lax.dot_general(xb, wbuf[lo:lo + cw, :], NT_DIMS, preferred_element_type=F32)
                if gated:
                    acc = jax.nn.sigmoid(acc + bg_ref[:, lo - pre:lo - pre + cw])
                p_ref[:, lo:lo + cw] = acc.astype(BF16)

        @pl.when(order_ref[j] == n_chips - 1)
        def _():
            product(True)

        @pl.when(order_ref[j] != n_chips - 1)
        def _():
            product(False)

        @pl.when(jnp.logical_and(j == n_chips - 1, i == nt - 1))
        def _():
            for a in range(1, n):
                gather.wait_own_chip(a)
                for k in range(n_chips - 1):
                    gather.wait_forwarded(a, k)
            gather.wait_sends()

    grid_spec = pltpu.PrefetchScalarGridSpec(
        num_scalar_prefetch=1, grid=(n_chips, nt),
        in_specs=[pl.BlockSpec((tm, d), lambda j, i, order: (i, 0)), _resident((1, 2 * d))] + [HBM_SPEC] * n,
        out_specs=(pl.BlockSpec((tm, gw), lambda j, i, order: (i, order[j])),) + (HBM_SPEC,) * n,
        scratch_shapes=[pltpu.VMEM((gw, d), BF16), pltpu.SemaphoreType.DMA] + _TwoLevelGather.semaphores(n))
    return pl.pallas_call(
        body, name="proj_in", grid_spec=grid_spec,
        out_shape=(jax.ShapeDtypeStruct((t, npj), BF16),) + _TwoLevelGather.out_shapes(shards),
        compiler_params=_params(("arbitrary", "arbitrary")),
    )(chip_order, x2, b_gate, *shards)


def _conv_taps(ext_ref, center, prev_blk, next_blk, first, last, tm):
    h = F32_SUBLANES
    ext_ref[0:h, :] = jnp.where(first, 0.0, prev_blk)
    ext_ref[h:h + tm, :] = center
    ext_ref[h + tm:h + tm + h, :] = jnp.where(last, 0.0, next_blk)
    return ext_ref[pl.ds(h - 1, tm), :], ext_ref[pl.ds(h + 1, tm), :]


def _spatial_mix(vn, ws_ref, bias_ref, mixed_ref, dm, tm):
    vb = vn.astype(BF16)
    for cc in range(tm // CHUNK):
        r0 = cc * CHUNK
        for h in range(dm.H):
            c0 = h * CHUNK
            m = _mm(ws_ref[h], vb[r0:r0 + CHUNK, c0:c0 + CHUNK])
            mixed_ref[r0:r0 + CHUNK, c0:c0 + CHUNK] = m + bias_ref[:, c0:c0 + CHUNK]
    return mixed_ref[...]


def _mixer_fwd(p, x2, w_pa, w_pb, w_o, conv_w8, vng, vnb, ws_b, bias_s, dm, tm):
    t, d, wa, npj = dm.T, dm.D, dm.WA, dm.NP
    nt = t // tm
    hb = BF16_SUBLANES
    prev_map, next_map = _halo_maps(tm, t, hb)

    def body(p_ref, pp_ref, pn_ref, x_ref, wpa_ref, wpb_ref, wo_ref, cw_ref, vng_ref, vnb_ref, ws_ref, bias_ref,
             ya_ref, yb_ref, r1_ref, ext_ref, mixed_ref):
        i = pl.program_id(0)

        def col(ref, lo, width):
            return ref[:, lo:lo + width].astype(F32)

        ch = col(p_ref, dm.OFF_CA, wa) * col(p_ref, dm.OFF_HA, wa)
        chp = (col(pp_ref, dm.OFF_CA, wa) * col(pp_ref, dm.OFF_HA, wa))[hb - F32_SUBLANES:hb]
        chn = (col(pn_ref, dm.OFF_CA, wa) * col(pn_ref, dm.OFF_HA, wa))[0:F32_SUBLANES]
        up, dn = _conv_taps(ext_ref, ch, chp, chn, i == 0, i == nt - 1, tm)
        a = col(p_ref, 0, wa) * (cw_ref[0:1, :] * up + cw_ref[1:2, :] * ch + cw_ref[2:3, :] * dn)
        ya = _mm(a.astype(BF16), wpa_ref[...])
        gv, _ = _gelu_and_grad(col(p_ref, dm.OFF_VB, d))
        xhv, _ = _ln_stats(gv)
        mixed = _spatial_mix(xhv * vng_ref[...] + vnb_ref[...], ws_ref, bias_ref, mixed_ref, dm, tm)
        gu, _ = _gelu_and_grad(col(p_ref, dm.OFF_UB, d))
        yb = _mm((gu * mixed).astype(BF16), wpb_ref[...])
        s = col(p_ref, dm.OFF_GA, d) * ya + col(p_ref, dm.OFF_GB, d) * yb
        mix = _mm(s.astype(BF16), wo_ref[...])
        ya_ref[...] = ya.astype(BF16)
        yb_ref[...] = yb.astype(BF16)
        r1_ref[...] = ALPHA * x_ref[...] + mix

    tile = lambda i: (i, 0)
    return pl.pallas_call(
        body, name="mixer_fwd", grid=(nt,),
        in_specs=[pl.BlockSpec((tm, npj), tile), pl.BlockSpec((hb, npj), prev_map), pl.BlockSpec((hb, npj), next_map),
                  pl.BlockSpec((tm, d), tile), _resident((wa, d)), _resident((d, d)), _resident((d, d)),
                  _resident((F32_SUBLANES, wa)), _resident((1, d)), _resident((1, d)),
                  _resident((dm.H, CHUNK, CHUNK)), _resident((CHUNK, d))],
        out_specs=(pl.BlockSpec((tm, d), tile), pl.BlockSpec((tm, d), tile), pl.BlockSpec((tm, d), tile)),
        out_shape=(jax.ShapeDtypeStruct((t, d), BF16), jax.ShapeDtypeStruct((t, d), BF16), jax.ShapeDtypeStruct((t, d), F32)),
        scratch_shapes=[pltpu.VMEM((tm + 2 * F32_SUBLANES, wa), F32), pltpu.VMEM((tm, d), F32)],
        compiler_params=_params(),
    )(p, p, p, x2, w_pa, w_pb, w_o, conv_w8, vng, vnb, ws_b, bias_s)


def _ffn_fwd(r1, tgt, w1, w2, ln1g, ln1b, ln2g, ln2b, dm, tm):
    t, d, dff = dm.T, dm.D, dm.DFF
    fc = dff // N_DEV

    def body(r1_ref, tgt_ref, w1_ref, w2_ref, g1_ref, b1_ref, g2_ref, b2_ref, relu_ref, x1_ref, dr2_ref, dr2b_ref, sums_ref):
        @pl.when(pl.program_id(0) == 0)
        def _():
            sums_ref[...] = jnp.zeros_like(sums_ref)

        xh1, _ = _ln_stats(r1_ref[...])
        x1 = xh1 * g1_ref[...] + b1_ref[...]
        x1b = x1.astype(BF16)
        x1_ref[...] = x1b
        ffn = jnp.zeros((tm, d), F32)
        for k in range(dff // fc):
            ks = slice(k * fc, (k + 1) * fc)
            r = jnp.maximum(_mm(x1b, w1_ref[:, ks]), 0.0)
            relu_ref[:, ks] = r.astype(BF16)
            ffn = ffn + _mm((r * r).astype(BF16), w2_ref[ks, :])
        xh2, rstd2 = _ln_stats(ALPHA * x1 + ffn)
        diff = xh2 * g2_ref[...] + b2_ref[...] - tgt_ref[...]
        dy = diff * (1.0 / d)
        dr2 = _ln_bwd(dy * g2_ref[...], xh2, rstd2)
        dr2_ref[...] = dr2
        dr2b_ref[...] = dr2.astype(BF16)
        sums_ref[0:1, :] += _colsum(diff * diff)
        sums_ref[1:2, :] += _colsum(dy * xh2)
        sums_ref[2:3, :] += _colsum(dy)

    tile = lambda i: (i, 0)
    vec = _resident((1, d))
    return pl.pallas_call(
        body, name="ffn_fwd", grid=(t // tm,),
        in_specs=[pl.BlockSpec((tm, d), tile), pl.BlockSpec((tm, d), tile), _resident((d, dff)), _resident((dff, d)),
                  vec, vec, vec, vec],
        out_specs=(pl.BlockSpec((tm, dff), tile), pl.BlockSpec((tm, d), tile), pl.BlockSpec((tm, d), tile),
                   pl.BlockSpec((tm, d), tile), pl.BlockSpec((F32_SUBLANES, d), lambda i: (0, 0))),
        out_shape=(jax.ShapeDtypeStruct((t, dff), BF16), jax.ShapeDtypeStruct((t, d), BF16), jax.ShapeDtypeStruct((t, d), F32),
                   jax.ShapeDtypeStruct((t, d), BF16), jax.ShapeDtypeStruct((F32_SUBLANES, d), F32)),
        compiler_params=_params(),
    )(r1, tgt, w1, w2, ln1g, ln1b, ln2g, ln2b)


def _ffn_bwd(r1, relu, dr2, w2t, w1t, ln1g, dm, tm):
    t, d, dff = dm.T, dm.D, dm.DFF
    fc = dff // N_DEV

    def body(r1_ref, relu_ref, dr2_ref, w2t_ref, w1t_ref, g1_ref, dr1_ref, dh_ref, sums_ref):
        @pl.when(pl.program_id(0) == 0)
        def _():
            sums_ref[...] = jnp.zeros_like(sums_ref)

        xh1, rstd1 = _ln_stats(r1_ref[...])
        dr2 = dr2_ref[...]
        dr2b = dr2.astype(BF16)
        dx1 = ALPHA * dr2
        for k in range(dff // fc):
            ks = slice(k * fc, (k + 1) * fc)
            dhb = (_mm(dr2b, w2t_ref[:, ks]) * (2.0 * relu_ref[:, ks].astype(F32))).astype(BF16)
            dh_ref[:, ks] = dhb
            dx1 = dx1 + _mm(dhb, w1t_ref[ks, :])
        dr1_ref[...] = _ln_bwd(dx1 * g1_ref[...], xh1, rstd1)
        sums_ref[0:1, :] += _colsum(dx1 * xh1)
        sums_ref[1:2, :] += _colsum(dx1)

    tile = lambda i: (i, 0)
    return pl.pallas_call(
        body, name="ffn_bwd", grid=(t // tm,),
        in_specs=[pl.BlockSpec((tm, d), tile), pl.BlockSpec((tm, dff), tile), pl.BlockSpec((tm, d), tile),
                  _resident((d, dff)), _resident((dff, d)), _resident((1, d))],
        out_specs=(pl.BlockSpec((tm, d), tile), pl.BlockSpec((tm, dff), tile), pl.BlockSpec((F32_SUBLANES, d), lambda i: (0, 0))),
        out_shape=(jax.ShapeDtypeStruct((t, d), F32), jax.ShapeDtypeStruct((t, dff), BF16),
                   jax.ShapeDtypeStruct((F32_SUBLANES, d), F32)),
        compiler_params=_params(),
    )(r1, relu, dr2, w2t, w1t, ln1g)


def _wgrad(pairs, tt, fb, name):
    n = len(pairs)
    t, f = pairs[0][0].shape
    d = pairs[0][1].shape[1]
    squares = [sq for _, _, sq in pairs]
    ni = t // tt

    def body(*refs):
        ins, outs, accs = refs[:2 * n], refs[2 * n:3 * n], refs[3 * n:]
        i = pl.program_id(1)

        @pl.when(i == 0)
        def _():
            for acc in accs:
                acc[...] = jnp.zeros_like(acc)

        for q in range(n):
            lhs = ins[2 * q][...]
            if squares[q]:
                lf = lhs.astype(F32)
                lhs = (lf * lf).astype(BF16)
            accs[q][...] += lax.dot_general(lhs, ins[2 * q + 1][...].astype(BF16), TN_DIMS, preferred_element_type=F32)

        @pl.when(i == ni - 1)
        def _():
            for q in range(n):
                outs[q][...] = accs[q][...].astype(BF16)

    lhs_spec = pl.BlockSpec((tt, fb), lambda j, i: (i, j))
    rhs_spec = pl.BlockSpec((tt, d), lambda j, i: (i, 0))
    out_spec = pl.BlockSpec((fb, d), lambda j, i: (j, 0))
    return pl.pallas_call(
        body, name=name, grid=(f // fb, ni),
        in_specs=[lhs_spec, rhs_spec] * n, out_specs=(out_spec,) * n,
        out_shape=(jax.ShapeDtypeStruct((f, d), BF16),) * n,
        scratch_shapes=[pltpu.VMEM((fb, d), F32)] * n,
        compiler_params=_params(("arbitrary", "arbitrary")),
    )(*[a_ for lhs, rhs, _ in pairs for a_ in (lhs, rhs)])


def _mixer_bwd(dr1, p, ya, yb, w_ot, w_pat, w_pbt, conv_w8, vng, vnb, ws_b, wst_b, bias_s, head_sel, xchg, dm, tm):
    t, d, wa, npj = dm.T, dm.D, dm.WA, dm.NP
    nt = t // tm
    h8, hb = F32_SUBLANES, BF16_SUBLANES
    ext = tm + 2 * h8
    prev_f, next_f = _halo_maps(tm, t, h8)
    prev_b, next_b = _halo_maps(tm, t, hb)
    nx = len(xchg)
    xsrcs, xrows = [a_ for a_, _ in xchg], [r for _, r in xchg]

    def body(dr_ref, drp_ref, drn_ref, p_ref, pp_ref, pn_ref, ya_ref, yb_ref, wot_ref, wpat_ref, wpbt_ref,
             cw_ref, vng_ref, vnb_ref, ws_ref, wst_ref, bias_ref, sel_ref,
             *refs):
        xsrc = refs[:nx]
        dp_ref, a_ref, dya_ref, s_ref, bb_ref, dyb_ref, dws_ref, dbs_ref, dcw_ref, dbg_ref, dvn_ref = refs[nx:nx + 11]
        xrecv = refs[nx + 11:2 * nx + 11]
        ext_ref, ext2_ref, mixed_ref, dvnm_ref = refs[2 * nx + 11:2 * nx + 15]
        exchange = _Exchange(xsrc, xrecv, xrows, *refs[2 * nx + 15:])
        i = pl.program_id(0)

        @pl.when(i == 0)
        def _():
            exchange.start()
            for ref in (dws_ref, dbs_ref, dcw_ref, dbg_ref, dvn_ref):
                ref[...] = jnp.zeros_like(ref)

        def col(ref, lo, width):
            return ref[:, lo:lo + width].astype(F32)

        def ext_rows(prev_blk, center, next_blk):
            return jnp.concatenate([prev_blk, center, next_blk], axis=0)

        def ext_col(lo, width):
            return ext_rows(col(pp_ref, lo, width)[hb - h8:hb], col(p_ref, lo, width), col(pn_ref, lo, width)[0:h8])

        row = lax.broadcasted_iota(jnp.int32, (ext, 1), 0) + (i * tm - h8)
        inside = jnp.logical_and(row >= 0, row < t)
        dr_e = ext_rows(drp_ref[...], dr_ref[...], drn_ref[...])
        ds_e = _mm(dr_e.astype(BF16), wot_ref[...])
        dya_e = ds_e * ext_col(dm.OFF_GA, d)
        da_e = _mm(dya_e.astype(BF16), wpat_ref[...])
        dcv_e = jnp.where(inside, da_e * ext_col(0, wa), 0.0)
        ch_e = jnp.where(inside, ext_col(dm.OFF_CA, wa) * ext_col(dm.OFF_HA, wa), 0.0)
        ext_ref[...] = ch_e
        ext2_ref[...] = dcv_e
        ch, ch_up, ch_dn = ch_e[h8:h8 + tm], ext_ref[pl.ds(h8 - 1, tm), :], ext_ref[pl.ds(h8 + 1, tm), :]
        dcv, dcv_up, dcv_dn = dcv_e[h8:h8 + tm], ext2_ref[pl.ds(h8 - 1, tm), :], ext2_ref[pl.ds(h8 + 1, tm), :]
        w0, w1, w2 = cw_ref[0:1, :], cw_ref[1:2, :], cw_ref[2:3, :]
        cv = w0 * ch_up + w1 * ch + w2 * ch_dn
        dp_ref[:, 0:wa] = (da_e[h8:h8 + tm] * cv).astype(BF16)
        dch = w0 * dcv_dn + w1 * dcv + w2 * dcv_up
        dp_ref[:, dm.OFF_CA:dm.OFF_CA + wa] = (dch * col(p_ref, dm.OFF_HA, wa)).astype(BF16)
        dp_ref[:, dm.OFF_HA:dm.OFF_HA + wa] = (dch * col(p_ref, dm.OFF_CA, wa)).astype(BF16)
        dcw_ref[0:1, :] += _colsum(dcv * ch_up)
        dcw_ref[1:2, :] += _colsum(dcv * ch)
        dcw_ref[2:3, :] += _colsum(dcv * ch_dn)
        a_ref[...] = (col(p_ref, 0, wa) * cv).astype(BF16)
        dya_ref[...] = dya_e[h8:h8 + tm].astype(BF16)
        ds = ds_e[h8:h8 + tm]
        g_a, g_b = col(p_ref, dm.OFF_GA, d), col(p_ref, dm.OFF_GB, d)
        y_a, y_b = ya_ref[...].astype(F32), yb_ref[...].astype(F32)
        s_ref[...] = (g_a * y_a + g_b * y_b).astype(BF16)
        dzga = ds * y_a * g_a * (1.0 - g_a)
        dzgb = ds * y_b * g_b * (1.0 - g_b)
        dp_ref[:, dm.OFF_GA:dm.OFF_GA + d] = dzga.astype(BF16)
        dp_ref[:, dm.OFF_GB:dm.OFF_GB + d] = dzgb.astype(BF16)
        dbg_ref[0:1, 0:d] += _colsum(dzga)
        dbg_ref[0:1, d:2 * d] += _colsum(dzgb)
        dyb = (ds * g_b).astype(BF16)
        dyb_ref[...] = dyb
        gv, dgelu_v = _gelu_and_grad(col(p_ref, dm.OFF_VB, d))
        xhv, rstdv = _ln_stats(gv)
        vn = xhv * vng_ref[...] + vnb_ref[...]
        mixed = _spatial_mix(vn, ws_ref, bias_ref, mixed_ref, dm, tm)
        gu, dgelu_u = _gelu_and_grad(col(p_ref, dm.OFF_UB, d))
        bb_ref[...] = (gu * mixed).astype(BF16)
        dbb = _mm(dyb, wpbt_ref[...])
        dp_ref[:, dm.OFF_UB:dm.OFF_UB + d] = (dbb * mixed * dgelu_u).astype(BF16)
        dmb = (dbb * gu).astype(BF16)
        vb = vn.astype(BF16)
        dbs = jnp.zeros((CHUNK, CHUNK), F32)
        for cc in range(tm // CHUNK):
            r0 = cc * CHUNK
            dbs = dbs + _mm(dmb[r0:r0 + CHUNK, :], sel_ref[...])
            for h in range(dm.H):
                c0 = h * CHUNK
                blk = dmb[r0:r0 + CHUNK, c0:c0 + CHUNK]
                dvnm_ref[r0:r0 + CHUNK, c0:c0 + CHUNK] = _mm(wst_ref[h], blk)
                dws_ref[h] += lax.dot_general(blk, vb[r0:r0 + CHUNK, c0:c0 + CHUNK], NT_DIMS, preferred_element_type=F32)
        dbs_ref[...] += dbs
        dvn = dvnm_ref[...]
        dvn_ref[0:1, :] += _colsum(dvn * xhv)
        dvn_ref[1:2, :] += _colsum(dvn)
        dp_ref[:, dm.OFF_VB:dm.OFF_VB + d] = (_ln_bwd(dvn * vng_ref[...], xhv, rstdv) * dgelu_v).astype(BF16)

        @pl.when(i == nt - 1)
        def _():
            exchange.finish()

    full = lambda i: (0, 0)
    tile = lambda i: (i, 0)
    hcc = _resident((dm.H, CHUNK, CHUNK))
    tok = lambda w, dt: jax.ShapeDtypeStruct((t, w), dt)
    return pl.pallas_call(
        body, name="mixer_bwd", grid=(nt,),
        in_specs=[pl.BlockSpec((tm, d), tile), pl.BlockSpec((h8, d), prev_f), pl.BlockSpec((h8, d), next_f),
                  pl.BlockSpec((tm, npj), tile), pl.BlockSpec((hb, npj), prev_b), pl.BlockSpec((hb, npj), next_b),
                  pl.BlockSpec((tm, d), tile), pl.BlockSpec((tm, d), tile),
                  _resident((d, d)), _resident((d, wa)), _resident((d, d)),
                  _resident((h8, wa)), _resident((1, d)), _resident((1, d)), hcc, hcc, _resident((CHUNK, d)),
                  _resident((d, CHUNK))] + [HBM_SPEC] * nx,
        out_specs=(pl.BlockSpec((tm, npj), tile), pl.BlockSpec((tm, wa), tile), pl.BlockSpec((tm, d), tile),
                   pl.BlockSpec((tm, d), tile), pl.BlockSpec((tm, d), tile), pl.BlockSpec((tm, d), tile),
                   pl.BlockSpec((dm.H, CHUNK, CHUNK), lambda i: (0, 0, 0)), pl.BlockSpec((CHUNK, CHUNK), full),
                   pl.BlockSpec((h8, wa), full), pl.BlockSpec((h8, 2 * d), full), pl.BlockSpec((h8, d), full))
        + (HBM_SPEC,) * nx,
        out_shape=(tok(npj, BF16), tok(wa, BF16), tok(d, BF16), tok(d, BF16), tok(d, BF16), tok(d, BF16),
                   jax.ShapeDtypeStruct((dm.H, CHUNK, CHUNK), F32), jax.ShapeDtypeStruct((CHUNK, CHUNK), F32),
                   jax.ShapeDtypeStruct((h8, wa), F32), jax.ShapeDtypeStruct((h8, 2 * d), F32),
                   jax.ShapeDtypeStruct((h8, d), F32)) + _Exchange.out_shapes(xsrcs, xrows),
        scratch_shapes=[pltpu.VMEM((ext, wa), F32), pltpu.VMEM((ext, wa), F32), pltpu.VMEM((tm, d), F32),
                        pltpu.VMEM((tm, d), F32)] + _Exchange.semaphores(nx),
        compiler_params=_params(),
    )(dr1, dr1, dr1, p, p, p, ya, yb, w_ot, w_pat, w_pbt, conv_w8, vng, vnb, ws_b, wst_b, bias_s, head_sel, *xsrcs)


def _input_grad(dp, dr1, w_int, dm, tm):
    t, d, npj = dm.T, dm.D, dm.NP

    def body(dp_ref, dr_ref, w_ref, dx_ref):
        dx_ref[...] = ALPHA * dr_ref[...] + _mm(dp_ref[...], w_ref[...])

    return pl.pallas_call(
        body, name="input_grad", grid=(t // tm,),
        in_specs=[pl.BlockSpec((tm, npj), lambda i: (i, 0)), pl.BlockSpec((tm, d), lambda i: (i, 0)), _resident((npj, d))],
        out_specs=pl.BlockSpec((tm, d), lambda i: (i, 0)),
        out_shape=jax.ShapeDtypeStruct((t, d), F32),
        compiler_params=_params(),
    )(dp, dr1, w_int)


def _w_in_grad(dp, x2, send_order, xchg, dm, tt):
    t, d, npj = dm.T, dm.D, dm.NP
    n_chips = N_DEV // 2
    gw = npj // n_chips
    rows = gw // 2
    ni = t // tt
    nx = len(xchg)
    xsrcs, xrows = [a_ for a_, _ in xchg], [r for _, r in xchg]

    def body(order_ref, dp_ref, x_ref, *refs):
        xsrc, got_ref, xrecv = refs[:nx], refs[nx], refs[nx + 1:2 * nx + 1]
        acc_ref, buf_ref, send_sems, recv_sems, local_sem = refs[2 * nx + 1:2 * nx + 6]
        exchange = _Exchange(xsrc, xrecv, xrows, *refs[2 * nx + 6:])
        j, i = pl.program_id(0), pl.program_id(1)
        x, y, c = _mesh_pos()
        me = 4 * x + 2 * y + c
        mine = buf_ref.at[pl.ds(pl.multiple_of(c * rows, BF16_SUBLANES), rows), :]
        other = buf_ref.at[pl.ds(pl.multiple_of((1 - c) * rows, BF16_SUBLANES), rows), :]
        chips = [(1 - x, y), (x, 1 - y), (1 - x, 1 - y), (x, y)]

        def send(step, same_core):
            px, py = chips[step]
            rel = (4 if step in (0, 2) else 0) + (2 if step in (1, 2) else 0) + (0 if same_core else 1)
            return pltpu.make_async_remote_copy(
                src_ref=mine if same_core else other, dst_ref=got_ref.at[me],
                send_sem=send_sems.at[rel - 1], recv_sem=recv_sems.at[rel - 1],
                device_id=(px, py, c if same_core else 1 - c), device_id_type=MESH)

        def sends(step):
            if step == n_chips - 1:
                return [pltpu.make_async_copy(mine, got_ref.at[me], local_sem), send(step, False)]
            return [send(step, True), send(step, False)]

        @pl.when(jnp.logical_and(j == 0, i == 0))
        def _():
            exchange.start()

        @pl.when(i == 0)
        def _():
            acc_ref[...] = jnp.zeros_like(acc_ref)

        acc_ref[...] += lax.dot_general(dp_ref[...], x_ref[...].astype(BF16), TN_DIMS, preferred_element_type=F32)

        for step in range(n_chips):
            @pl.when(jnp.logical_and(j == step, i == ni - 1))
            def _(step=step):
                if step:
                    for cp in sends(step - 1):
                        cp.wait_send()
                buf_ref[...] = acc_ref[...].astype(BF16)
                for cp in sends(step):
                    cp.start()

        @pl.when(jnp.logical_and(j == n_chips - 1, i == ni - 1))
        def _():
            local, to_sibling = sends(n_chips - 1)
            to_sibling.wait_send()
            local.wait()
            for rel in range(1, N_DEV):
                pltpu.make_async_remote_copy(
                    src_ref=mine, dst_ref=got_ref.at[me], send_sem=send_sems.at[rel - 1], recv_sem=recv_sems.at[rel - 1],
                    device_id=(x, y, c), device_id_type=MESH).wait_recv()
            exchange.finish()

    grid_spec = pltpu.PrefetchScalarGridSpec(
        num_scalar_prefetch=1, grid=(n_chips, ni),
        in_specs=[pl.BlockSpec((tt, gw), lambda j, i, order: (i, order[j])), pl.BlockSpec((tt, d), lambda j, i, order: (i, 0))]
        + [HBM_SPEC] * nx,
        out_specs=(HBM_SPEC,) * (nx + 1),
        scratch_shapes=[pltpu.VMEM((gw, d), F32), pltpu.VMEM((gw, d), BF16), pltpu.SemaphoreType.DMA((N_DEV - 1,)),
                        pltpu.SemaphoreType.DMA((N_DEV - 1,)), pltpu.SemaphoreType.DMA] + _Exchange.semaphores(nx))
    return pl.pallas_call(
        body, name="w_in_grad", grid_spec=grid_spec,
        out_shape=(jax.ShapeDtypeStruct((N_DEV, rows, d), BF16),) + _Exchange.out_shapes(xsrcs, xrows),
        compiler_params=_params(("arbitrary", "arbitrary")),
    )(send_order, dp, x2, *xsrcs)


def _adamw(w, g, m, v, name):
    rows, cols = w.shape
    tr = 256 if rows % 256 == 0 else rows
    bc1 = 1.0 - ADAM_B1 ** ADAM_STEP
    bc2 = 1.0 - ADAM_B2 ** ADAM_STEP

    def body(w_ref, g_ref, m_ref, v_ref, d_ref, nm_ref, nv_ref):
        g_ = g_ref[...]
        nm = ADAM_B1 * m_ref[...] + (1.0 - ADAM_B1) * g_
        nv = ADAM_B2 * v_ref[...] + (1.0 - ADAM_B2) * (g_ * g_)
        d_ref[...] = -ADAM_LR * ((nm / bc1) / (jnp.sqrt(nv / bc2) + ADAM_EPS) + ADAM_WD * w_ref[...])
        nm_ref[...] = nm
        nv_ref[...] = nv

    spec = pl.BlockSpec((tr, cols), lambda i: (i, 0))
    shp = jax.ShapeDtypeStruct((rows, cols), F32)
    return pl.pallas_call(
        body, name=name, grid=(rows // tr,), in_specs=[spec] * 4, out_specs=(spec,) * 3, out_shape=(shp,) * 3,
        compiler_params=_params(),
    )(w, g, m, v)


def _to_slab(parts):
    flat = jnp.concatenate([q.reshape(-1) for q in parts])
    pad = (-flat.shape[0]) % (F32_SUBLANES * LANES)
    return jnp.pad(flat, (0, pad)).reshape(-1, LANES)


def _from_slab(slab, shapes):
    flat = slab.reshape(-1)
    out, off = [], 0
    for s in shapes:
        n = math.prod(s)
        out.append(flat[off:off + n].reshape(s))
        off += n
    return out


def kernel(x, w_in, b_gate, conv_w, v_norm_g, v_norm_b, w_s, b_s, w_pa, w_pb, w_o, ln1_g, ln1_b, w_ff1, w_ff2, ln2_g, ln2_b, loss_target, m_w_in, m_b_gate, m_conv_w, m_v_norm_g, m_v_norm_b, m_w_s, m_b_s, m_w_pa, m_w_pb, m_w_o, m_ln1_g, m_ln1_b, m_w_ff1, m_w_ff2, m_ln2_g, m_ln2_b, v_w_in, v_b_gate, v_conv_w, v_v_norm_g, v_v_norm_b, v_w_s, v_b_s, v_w_pa, v_w_pb, v_w_o, v_ln1_g, v_ln1_b, v_w_ff1, v_w_ff2, v_ln2_g, v_ln2_b):
    t, d = x.shape[1], x.shape[2]
    dm = _Dims(t, d)
    tm = 256 if t % 256 == 0 else CHUNK
    tm_big = 512 if t % 512 == 0 else tm
    tt = 1024 if t % 1024 == 0 else tm
    tm_proj = tt
    me =4 * lax.axis_index("x") + 2 * lax.axis_index("y") + lax.axis_index("c")
    x2, tgt = x[0], loss_target[0]

    conv_bits = lax.bitcast_convert_type(conv_w[0], BF16).reshape(-1)
    conv_blk = jnp.pad(conv_bits, (0, dm.conv_rows * d - conv_bits.shape[0])).reshape(dm.conv_rows, d)
    ax, ay = lax.axis_index("x"), lax.axis_index("y")
    chip_order = jnp.stack([2 * ax + ay, 2 * (1 - ax) + ay, 2 * ax + 1 - ay, 2 * (1 - ax) + 1 - ay]).astype(jnp.int32)
    p, w_int, w_pa_f, w_pb_f, w_o_f, w_1t, w_2, conv_g = _proj_in(
        x2, b_gate, [w_in[0].T.astype(BF16), w_pa[0].astype(BF16), w_pb[0].astype(BF16), w_o[0].astype(BF16),
                     w_ff1[0].T.astype(BF16), w_ff2[0].astype(BF16), conv_blk], chip_order, dm, tm_proj)
    wa8 = dm.WA // N_DEV
    conv_all = lax.bitcast_convert_type(conv_g.reshape(N_DEV, -1)[:, :3 * wa8 * 2].reshape(N_DEV, 3, wa8, 2), F32)
    conv_full = jnp.transpose(conv_all, (1, 0, 2)).reshape(3, dm.WA)
    conv_w8 = jnp.pad(conv_full, ((0, F32_SUBLANES - 3), (0, 0)))
    ws_b = w_s[0].astype(BF16)
    wst_b = jnp.transpose(w_s[0], (0, 2, 1)).astype(BF16)
    bias_s = jnp.repeat(b_s[0].T, CHUNK, axis=1)
    head_sel = (jnp.arange(d)[:, None] // CHUNK == jnp.arange(CHUNK)[None, :]).astype(BF16)

    ya, yb, r1 = _mixer_fwd(p, x2, w_pa_f, w_pb_f, w_o_f, conv_w8, v_norm_g, v_norm_b, ws_b, bias_s, dm, tm)
    relu, x1b, dr2, dr2b, sums2 = _ffn_fwd(r1, tgt, w_1t.T, w_2, ln1_g, ln1_b, ln2_g, ln2_b, dm, tm_big)
    dr1, dh1, sums1 = _ffn_bwd(r1, relu, dr2, w_2.T, w_1t, ln1_g, dm, tm_big)
    fb = min(1024, dm.DFF)
    rows = dm.shard_rows
    g_ff1t, g_ff2 = _wgrad([(dh1, x1b, False), (relu, dr2b, True)], tt, fb, "ffn_wgrad")
    dp, a_m, dya, s_m, bb_m, dyb, g_ws, g_bs_t, g_cw, g_bg, g_vn, got_ff1t, got_ff2 = _mixer_bwd(
        dr1, p, ya, yb, w_o_f.T, w_pa_f.T, w_pb_f.T, conv_w8, v_norm_g, v_norm_b, ws_b, wst_b, bias_s, head_sel,
        [(g_ff1t, rows[4]), (g_ff2, rows[5])], dm, tm)
    (g_pa,) = _wgrad([(a_m, dya, False)], tt, dm.WA, "w_pa_grad")
    g_o, g_pb = _wgrad([(s_m, dr1, False), (bb_m, dyb, False)], tt, d, "w_o_pb_grad")
    grad_x = _input_grad(dp, dr1, w_int, dm, tm_big)
    small_parts = [g_bg[0], g_cw[0:3], g_vn[0], g_vn[1], g_ws, g_bs_t[:, :dm.H].T,
                   sums1[0], sums1[1], sums2[1], sums2[2], sums2[0]]
    got_int, got_pa, got_pb, got_o, got_s = _w_in_grad(
        dp, x2, jnp.roll(chip_order, -1),
        [(g_pa, rows[1]), (g_pb, rows[2]), (g_o, rows[3]), (_to_slab(small_parts), None)], dm, tt)

    cut = [_sum_slots(g, 256, "sum_grads_%d" % k) for k, g in enumerate((got_int, got_pa, got_pb, got_o, got_ff1t, got_ff2))]
    ssum = _sum_slots(got_s, 256, "sum_small")
    (s_bg, s_cw, s_vng, s_vnb, s_ws, s_bs, s_l1g, s_l1b, s_l2g, s_l2b, s_sq) = _from_slab(
        ssum, [(1, 2 * d), (3, dm.WA), (1, d), (1, d), (1, dm.H, CHUNK, CHUNK), (1, dm.H, CHUNK),
               (1, d), (1, d), (1, d), (1, d), (d,)])
    loss = 0.5 * jnp.sum(s_sq) / d
    s_cw = lax.dynamic_slice(s_cw, (0, me * wa8), (3, wa8))[None]
    big_g = [cut[0].T, cut[1], cut[2], cut[3], cut[4].T, cut[5]]

    big_w = [(w_in, m_w_in, v_w_in), (w_pa, m_w_pa, v_w_pa), (w_pb, m_w_pb, v_w_pb), (w_o, m_w_o, v_w_o),
             (w_ff1, m_w_ff1, v_w_ff1), (w_ff2, m_w_ff2, v_w_ff2)]
    big_out = [_adamw(w[0], g, m[0], v[0], "adamw_%d" % k) for k, ((w, m, v), g) in enumerate(zip(big_w, big_g))]
    small_w = [(b_gate, m_b_gate, v_b_gate), (conv_w, m_conv_w, v_conv_w), (v_norm_g, m_v_norm_g, v_v_norm_g),
               (v_norm_b, m_v_norm_b, v_v_norm_b), (w_s, m_w_s, v_w_s), (b_s, m_b_s, v_b_s), (ln1_g, m_ln1_g, v_ln1_g),
               (ln1_b, m_ln1_b, v_ln1_b), (ln2_g, m_ln2_g, v_ln2_g), (ln2_b, m_ln2_b, v_ln2_b)]
    small_g = [s_bg, s_cw, s_vng, s_vnb, s_ws, s_bs, s_l1g, s_l1b, s_l2g, s_l2b]
    small_shapes = [w.shape for w, _, _ in small_w]
    sd, sm, sv = _adamw(_to_slab([w for w, _, _ in small_w]), _to_slab(small_g), _to_slab([m for _, m, _ in small_w]),
                        _to_slab([v for _, _, v in small_w]), "adamw_small")
    small_out = list(zip(_from_slab(sd, small_shapes), _from_slab(sm, small_shapes), _from_slab(sv, small_shapes)))

    order = [("b", 0), ("s", 0), ("s", 1), ("s", 2), ("s", 3), ("s", 4), ("s", 5), ("b", 1), ("b", 2), ("b", 3),
             ("s", 6), ("s", 7), ("b", 4), ("b", 5), ("s", 8), ("s", 9)]
    grads, deltas, new_m, new_v = [], [], [], []
    for kind, k in order:
        if kind == "b":
            g, (dl, nm, nv) = big_g[k][None], big_out[k]
            dl, nm, nv = dl[None], nm[None], nv[None]
        else:
            g, (dl, nm, nv) = small_g[k], small_out[k]
        grads.append(g)
        deltas.append(dl)
        new_m.append(nm)
        new_v.append(nv)
    return (loss, grad_x[None], *grads, *deltas, *new_m, *new_v)
```

```python
import math

import jax
import jax.numpy as jnp
from jax import lax
from jax.experimental import pallas as pl
from jax.experimental.pallas import tpu as pltpu

F32 = jnp.float32
BF16 = jnp.bfloat16
N_DEV = 8
CHUNK = 128
LN_EPS = 1e-5
ALPHA = 2.0 ** 0.25
ADAM_LR, ADAM_B1, ADAM_B2, ADAM_EPS, ADAM_WD, ADAM_STEP = 0.001, 0.9, 0.999, 1e-08, 0.01, 10
F32_SUBLANES = 8
BF16_SUBLANES = 16
LANES = 128
VMEM_LIMIT = 56 * 1024 * 1024
MESH = pl.DeviceIdType.MESH
NT_DIMS = (((1,), (1,)), ((), ()))
TN_DIMS = (((0,), (0,)), ((), ()))
HBM_SPEC = pl.BlockSpec(memory_space=pltpu.HBM)


class _Dims:
    def __init__(self, t, d):
        self.T, self.D = t, d
        self.WA = 3 * d // 2
        self.NP = 3 * self.WA + 4 * d
        self.DFF = 4 * d
        self.H = d // CHUNK
        self.OFF_CA, self.OFF_HA = self.WA, 2 * self.WA
        self.OFF_UB = 3 * self.WA
        self.OFF_VB = self.OFF_UB + d
        self.OFF_GA = self.OFF_VB + d
        self.OFF_GB = self.OFF_GA + d
        self.shard_rows = (self.NP // N_DEV, self.WA // N_DEV, d // N_DEV, d // N_DEV, self.DFF // N_DEV, self.DFF // N_DEV)
        self.conv_rows = BF16_SUBLANES * max(1, -(-(3 * (self.WA // N_DEV) * 2) // (BF16_SUBLANES * d)))


def _params(sem=("arbitrary",), vmem=VMEM_LIMIT):
    return pltpu.CompilerParams(dimension_semantics=sem, vmem_limit_bytes=vmem)


def _mesh_pos():
    return lax.axis_index("x"), lax.axis_index("y"), lax.axis_index("c")


def _resident(shape):
    zeros = (0,) * len(shape)
    return pl.BlockSpec(shape, lambda *_: zeros, pipeline_mode=pl.Buffered(1))


class _TwoLevelGather:
    def __init__(self, shard_refs, out_refs, send_sems, recv_sems, local_sems):
        self.n = len(shard_refs)
        self.shard_refs, self.out_refs = shard_refs, out_refs
        self.send_sems, self.recv_sems, self.local_sems = send_sems, recv_sems, local_sems
        x, y, c = _mesh_pos()
        self.c = c
        self.me, self.sibling = (x, y, c), (x, y, 1 - c)
        self.chips = [(1 - x, y), (x, 1 - y), (1 - x, 1 - y)]

    def _slot(self, a, px, py, pc):
        rows = self.shard_refs[a].shape[0]
        return self.out_refs[a].at[pl.ds((4 * px + 2 * py + pc) * rows, rows), :]

    def _copy(self, a, k, block, to, src=None):
        return pltpu.make_async_remote_copy(
            src_ref=self._slot(a, *block) if src is None else src, dst_ref=self._slot(a, *block),
            send_sem=self.send_sems.at[7 * a + k], recv_sem=self.recv_sems.at[7 * a + k], device_id=to, device_id_type=MESH)

    def _mine(self):
        return [pltpu.make_async_copy(self.shard_refs[a], self._slot(a, *self.me), self.local_sems.at[a]) for a in range(self.n)]

    def _first(self):
        out = []
        for a in range(self.n):
            out.append(self._copy(a, 0, self.me, self.sibling, src=self.shard_refs[a]))
            out += [self._copy(a, 1 + j, self.me, (*chip, self.c), src=self.shard_refs[a]) for j, chip in enumerate(self.chips)]
        return out

    def _passed(self):
        return [self._copy(a, 4 + j, (*chip, self.c), self.sibling) for j, chip in enumerate(self.chips) for a in range(self.n)]

    def start(self, arrays):
        for cp in [cp for a in arrays for cp in [self._mine()[a]] + self._first()[4 * a:4 * a + 4]]:
            cp.start()

    def wait_own_chip(self, a):
        self._mine()[a].wait()
        self._copy(a, 0, self.sibling, self.me).wait_recv()

    def forward(self, a, j):
        chip = self.chips[j]
        self._copy(a, 1 + j, (*chip, self.c), self.me).wait_recv()
        self._copy(a, 4 + j, (*chip, self.c), self.sibling).start()

    def wait_forwarded(self, a, j):
        self._copy(a, 4 + j, (*self.chips[j], 1 - self.c), self.me).wait_recv()

    def wait_sends(self):
        for cp in self._first() + self._passed():
            cp.wait_send()

    @staticmethod
    def out_shapes(shards):
        return tuple(jax.ShapeDtypeStruct((N_DEV * s.shape[0], s.shape[1]), s.dtype) for s in shards)

    @staticmethod
    def semaphores(n):
        return [pltpu.SemaphoreType.DMA((7 * n,)), pltpu.SemaphoreType.DMA((7 * n,)), pltpu.SemaphoreType.DMA((n,))]


class _Exchange:
    def __init__(self, src_refs, recv_refs, rows, send_sems, recv_sems, local_sems):
        x, y, c = _mesh_pos()
        me = 4 * x + 2 * y + c
        self.own, self.sends, self.arrivals = [], [], []
        for a, (src, recv) in enumerate(zip(src_refs, recv_refs)):
            def blk(k, src=src, r=rows[a]):
                return src if r is None else src.at[pl.ds(k * r, r), :]

            self.own.append(pltpu.make_async_copy(blk(me), recv.at[me], local_sems.at[a]))
            for rel in range(1, N_DEV):
                px = 1 - x if rel & 4 else x
                py = 1 - y if rel & 2 else y
                pc = 1 - c if rel & 1 else c
                peer = 4 * px + 2 * py + pc
                sem = dict(send_sem=send_sems.at[7 * a + rel - 1], recv_sem=recv_sems.at[7 * a + rel - 1],
                           device_id=(px, py, pc), device_id_type=MESH)
                self.sends.append(pltpu.make_async_remote_copy(src_ref=blk(peer), dst_ref=recv.at[me], **sem))
                self.arrivals.append(pltpu.make_async_remote_copy(src_ref=blk(me), dst_ref=recv.at[peer], **sem))

    def start(self):
        for cp in self.own + self.sends:
            cp.start()

    def finish(self):
        for cp in self.arrivals:
            cp.wait_recv()
        for cp in self.sends:
            cp.wait_send()
        for cp in self.own:
            cp.wait()

    @staticmethod
    def out_shapes(srcs, rows):
        return tuple(jax.ShapeDtypeStruct((N_DEV, s.shape[0] if r is None else r, s.shape[1]), s.dtype) for s, r in zip(srcs, rows))

    @staticmethod
    def semaphores(n):
        return [pltpu.SemaphoreType.DMA((7 * n,)), pltpu.SemaphoreType.DMA((7 * n,)), pltpu.SemaphoreType.DMA((n,))]


def _sum_slots(slots, tile_rows, name):
    _, rows, cols = slots.shape
    tr = rows
    if N_DEV * rows * cols * slots.dtype.itemsize > 8 * 1024 * 1024:
        tr = next(c for c in (256, 192, 128, 64, 32, 16) if c <= tile_rows and rows % c == 0)

    def body(s_ref, o_ref):
        acc = s_ref[0].astype(F32)
        for k in range(1, N_DEV):
            acc = acc + s_ref[k].astype(F32)
        o_ref[...] = acc

    return pl.pallas_call(
        body, name=name, grid=(rows // tr,),
        in_specs=[pl.BlockSpec((N_DEV, tr, cols), lambda i: (0, i, 0))],
        out_specs=pl.BlockSpec((tr, cols), lambda i: (i, 0)),
        out_shape=jax.ShapeDtypeStruct((rows, cols), F32),
        compiler_params=_params(),
    )(slots)


def _gelu_and_grad(x):
    k0 = math.sqrt(2.0 / math.pi)
    k1 = 0.044715
    x2 = x * x
    th = jnp.tanh(k0 * x * (1.0 + k1 * x2))
    half = 0.5 * (1.0 + th)
    return x * half, half + 0.5 * x * (1.0 - th * th) * (k0 * (1.0 + 3.0 * k1 * x2))


def _ln_stats(r):
    mu = jnp.mean(r, axis=-1, keepdims=True)
    rc = r - mu
    var = jnp.mean(rc * rc, axis=-1, keepdims=True)
    rstd = lax.rsqrt(var + LN_EPS)
    return rc * rstd, rstd


def _ln_bwd(dxh, xh, rstd):
    return rstd * (dxh - jnp.mean(dxh, axis=-1, keepdims=True) - xh * jnp.mean(dxh * xh, axis=-1, keepdims=True))


def _colsum(a):
    return jnp.sum(a, axis=0, keepdims=True)


def _mm(a, b):
    return jnp.dot(a, b, preferred_element_type=F32)


def _halo_maps(tm, t, unit):
    per, last = tm // unit, t // unit - 1
    return (lambda i: (jnp.maximum(i * per - 1, 0), 0)), (lambda i: (jnp.minimum((i + 1) * per, last), 0))


def _proj_in(x2, b_gate, shards, chip_order, dm, tm):
    t, d, npj = dm.T, dm.D, dm.NP
    nt = t // tm
    n = len(shards)
    n_chips = N_DEV // 2
    gw = npj // n_chips
    pre = dm.OFF_GA - (n_chips - 1) * gw
    cw = 2 * LANES
    assert gw % LANES == 0 and 0 <= pre < gw and pre % LANES == 0 and (gw - pre) % cw == 0 and gw - pre == 2 * d

    def body(order_ref, x_ref, bg_ref, *refs):
        p_ref = refs[n]
        w_out = refs[n + 1]
        wbuf, load_sem = refs[2 * n + 1], refs[2 * n + 2]
        gather = _TwoLevelGather(refs[:n], refs[n + 1:2 * n + 1], *refs[2 * n + 3:])
        j, i = pl.program_id(0), pl.program_id(1)

        @pl.when(jnp.logical_and(j == 0, i == 0))
        def _():
            gather.start([0])

        for step in range(n_chips):
            @pl.when(jnp.logical_and(j == step, i == 0))
            def _(step=step):
                if step == 0:
                    gather.wait_own_chip(0)
                else:
                    gather.forward(0, step - 1)
                    gather.wait_forwarded(0, step - 1)
                if step == 1:
                    gather.start(range(1, n))
                load = pltpu.make_async_copy(w_out.at[pl.ds(order_ref[step] * gw, gw), :], wbuf, load_sem)
                load.start()
                load.wait()

        @pl.when(jnp.logical_and(j == n_chips - 1, i == nt // 2))
        def _():
            for a in range(1, n):
                for k in range(n_chips - 1):
                    gather.forward(a, k)

        def product(gated):
            xb = x_ref[...].astype(BF16)
            if pre:
                p_ref[:, 0:pre] = lax.dot_general(xb, wbuf[0:pre, :], NT_DIMS, preferred_element_type=F32).astype(BF16)
            for lo in range(pre, gw, cw):
                acc = lax.dot_general(xb, wbuf[lo:lo + cw, :], NT_DIMS, preferred_element_type=F32)
                if gated:
                    acc = jax.nn.sigmoid(acc + bg_ref[:, lo - pre:lo - pre + cw])
                p_ref[:, lo:lo + cw] = acc.astype(BF16)

        @pl.when(order_ref[j] == n_chips - 1)
        def _():
            product(True)

        @pl.when(order_ref[j] != n_chips - 1)
        def _():
            product(False)

        @pl.when(jnp.logical_and(j == n_chips - 1, i == nt - 1))
        def _():
            for a in range(1, n):
                gather.wait_own_chip(a)
                for k in range(n_chips - 1):
                    gather.wait_forwarded(a, k)
            gather.wait_sends()

    grid_spec = pltpu.PrefetchScalarGridSpec(
        num_scalar_prefetch=1, grid=(n_chips, nt),
        in_specs=[pl.BlockSpec((tm, d), lambda j, i, order: (i, 0)), _resident((1, 2 * d))] + [HBM_SPEC] * n,
        out_specs=(pl.BlockSpec((tm, gw), lambda j, i, order: (i, order[j])),) + (HBM_SPEC,) * n,
        scratch_shapes=[pltpu.VMEM((gw, d), BF16), pltpu.SemaphoreType.DMA] + _TwoLevelGather.semaphores(n))
    return pl.pallas_call(
        body, name="proj_in", grid_spec=grid_spec,
        out_shape=(jax.ShapeDtypeStruct((t, npj), BF16),) + _TwoLevelGather.out_shapes(shards),
        compiler_params=_params(("arbitrary", "arbitrary")),
    )(chip_order, x2, b_gate, *shards)


def _conv_taps(ext_ref, center, prev_blk, next_blk, first, last, tm):
    h = F32_SUBLANES
    ext_ref[0:h, :] = jnp.where(first, 0.0, prev_blk)
    ext_ref[h:h + tm, :] = center
    ext_ref[h + tm:h + tm + h, :] = jnp.where(last, 0.0, next_blk)
    return ext_ref[pl.ds(h - 1, tm), :], ext_ref[pl.ds(h + 1, tm), :]


def _spatial_mix(vn, ws_ref, bias_ref, mixed_ref, dm, tm):
    vb = vn.astype(BF16)
    for cc in range(tm // CHUNK):
        r0 = cc * CHUNK
        for h in range(dm.H):
            c0 = h * CHUNK
            m = _mm(ws_ref[h], vb[r0:r0 + CHUNK, c0:c0 + CHUNK])
            mixed_ref[r0:r0 + CHUNK, c0:c0 + CHUNK] = m + bias_ref[:, c0:c0 + CHUNK]
    return mixed_ref[...]


def _mixer_fwd(p, x2, w_pa, w_pb, w_o, conv_w8, vng, vnb, ws_b, bias_s, dm, tm):
    t, d, wa, npj = dm.T, dm.D, dm.WA, dm.NP
    nt = t // tm
    hb = BF16_SUBLANES
    prev_map, next_map = _halo_maps(tm, t, hb)

    def body(p_ref, pp_ref, pn_ref, x_ref, wpa_ref, wpb_ref, wo_ref, cw_ref, vng_ref, vnb_ref, ws_ref, bias_ref,
             ya_ref, yb_ref, r1_ref, ext_ref, mixed_ref):
        i = pl.program_id(0)

        def col(ref, lo, width):
            return ref[:, lo:lo + width].astype(F32)

        ch = col(p_ref, dm.OFF_CA, wa) * col(p_ref, dm.OFF_HA, wa)
        chp = (col(pp_ref, dm.OFF_CA, wa) * col(pp_ref, dm.OFF_HA, wa))[hb - F32_SUBLANES:hb]
        chn = (col(pn_ref, dm.OFF_CA, wa) * col(pn_ref, dm.OFF_HA, wa))[0:F32_SUBLANES]
        up, dn = _conv_taps(ext_ref, ch, chp, chn, i == 0, i == nt - 1, tm)
        a = col(p_ref, 0, wa) * (cw_ref[0:1, :] * up + cw_ref[1:2, :] * ch + cw_ref[2:3, :] * dn)
        ya = _mm(a.astype(BF16), wpa_ref[...])
        gv, _ = _gelu_and_grad(col(p_ref, dm.OFF_VB, d))
        xhv, _ = _ln_stats(gv)
        mixed = _spatial_mix(xhv * vng_ref[...] + vnb_ref[...], ws_ref, bias_ref, mixed_ref, dm, tm)
        gu, _ = _gelu_and_grad(col(p_ref, dm.OFF_UB, d))
        yb = _mm((gu * mixed).astype(BF16), wpb_ref[...])
        s = col(p_ref, dm.OFF_GA, d) * ya + col(p_ref, dm.OFF_GB, d) * yb
        mix = _mm(s.astype(BF16), wo_ref[...])
        ya_ref[...] = ya.astype(BF16)
        yb_ref[...] = yb.astype(BF16)
        r1_ref[...] = ALPHA * x_ref[...] + mix

    tile = lambda i: (i, 0)
    return pl.pallas_call(
        body, name="mixer_fwd", grid=(nt,),
        in_specs=[pl.BlockSpec((tm, npj), tile), pl.BlockSpec((hb, npj), prev_map), pl.BlockSpec((hb, npj), next_map),
                  pl.BlockSpec((tm, d), tile), _resident((wa, d)), _resident((d, d)), _resident((d, d)),
                  _resident((F32_SUBLANES, wa)), _resident((1, d)), _resident((1, d)),
                  _resident((dm.H, CHUNK, CHUNK)), _resident((CHUNK, d))],
        out_specs=(pl.BlockSpec((tm, d), tile), pl.BlockSpec((tm, d), tile), pl.BlockSpec((tm, d), tile)),
        out_shape=(jax.ShapeDtypeStruct((t, d), BF16), jax.ShapeDtypeStruct((t, d), BF16), jax.ShapeDtypeStruct((t, d), F32)),
        scratch_shapes=[pltpu.VMEM((tm + 2 * F32_SUBLANES, wa), F32), pltpu.VMEM((tm, d), F32)],
        compiler_params=_params(),
    )(p, p, p, x2, w_pa, w_pb, w_o, conv_w8, vng, vnb, ws_b, bias_s)


def _ffn_fwd(r1, tgt, w1, w2, ln1g, ln1b, ln2g, ln2b, dm, tm):
    t, d, dff = dm.T, dm.D, dm.DFF
    fc = dff // N_DEV

    def body(r1_ref, tgt_ref, w1_ref, w2_ref, g1_ref, b1_ref, g2_ref, b2_ref, relu_ref, x1_ref, dr2_ref, dr2b_ref, sums_ref):
        @pl.when(pl.program_id(0) == 0)
        def _():
            sums_ref[...] = jnp.zeros_like(sums_ref)

        xh1, _ = _ln_stats(r1_ref[...])
        x1 = xh1 * g1_ref[...] + b1_ref[...]
        x1b = x1.astype(BF16)
        x1_ref[...] = x1b
        ffn = jnp.zeros((tm, d), F32)
        for k in range(dff // fc):
            ks = slice(k * fc, (k + 1) * fc)
            r = jnp.maximum(_mm(x1b, w1_ref[:, ks]), 0.0)
            relu_ref[:, ks] = r.astype(BF16)
            ffn = ffn + _mm((r * r).astype(BF16), w2_ref[ks, :])
        xh2, rstd2 = _ln_stats(ALPHA * x1 + ffn)
        diff = xh2 * g2_ref[...] + b2_ref[...] - tgt_ref[...]
        dy = diff * (1.0 / d)
        dr2 = _ln_bwd(dy * g2_ref[...], xh2, rstd2)
        dr2_ref[...] = dr2
        dr2b_ref[...] = dr2.astype(BF16)
        sums_ref[0:1, :] += _colsum(diff * diff)
        sums_ref[1:2, :] += _colsum(dy * xh2)
        sums_ref[2:3, :] += _colsum(dy)

    tile = lambda i: (i, 0)
    vec = _resident((1, d))
    return pl.pallas_call(
        body, name="ffn_fwd", grid=(t // tm,),
        in_specs=[pl.BlockSpec((tm, d), tile), pl.BlockSpec((tm, d), tile), _resident((d, dff)), _resident((dff, d)),
                  vec, vec, vec, vec],
        out_specs=(pl.BlockSpec((tm, dff), tile), pl.BlockSpec((tm, d), tile), pl.BlockSpec((tm, d), tile),
                   pl.BlockSpec((tm, d), tile), pl.BlockSpec((F32_SUBLANES, d), lambda i: (0, 0))),
        out_shape=(jax.ShapeDtypeStruct((t, dff), BF16), jax.ShapeDtypeStruct((t, d), BF16), jax.ShapeDtypeStruct((t, d), F32),
                   jax.ShapeDtypeStruct((t, d), BF16), jax.ShapeDtypeStruct((F32_SUBLANES, d), F32)),
        compiler_params=_params(),
    )(r1, tgt, w1, w2, ln1g, ln1b, ln2g, ln2b)


def _ffn_bwd(r1, relu, dr2, w2t, w1t, ln1g, dm, tm):
    t, d, dff = dm.T, dm.D, dm.DFF
    fc = dff // N_DEV

    def body(r1_ref, relu_ref, dr2_ref, w2t_ref, w1t_ref, g1_ref, dr1_ref, dh_ref, sums_ref):
        @pl.when(pl.program_id(0) == 0)
        def _():
            sums_ref[...] = jnp.zeros_like(sums_ref)

        xh1, rstd1 = _ln_stats(r1_ref[...])
        dr2 = dr2_ref[...]
        dr2b = dr2.astype(BF16)
        dx1 = ALPHA * dr2
        for k in range(dff // fc):
            ks = slice(k * fc, (k + 1) * fc)
            dhb = (_mm(dr2b, w2t_ref[:, ks]) * (2.0 * relu_ref[:, ks].astype(F32))).astype(BF16)
            dh_ref[:, ks] = dhb
            dx1 = dx1 + _mm(dhb, w1t_ref[ks, :])
        dr1_ref[...] = _ln_bwd(dx1 * g1_ref[...], xh1, rstd1)
        sums_ref[0:1, :] += _colsum(dx1 * xh1)
        sums_ref[1:2, :] += _colsum(dx1)

    tile = lambda i: (i, 0)
    return pl.pallas_call(
        body, name="ffn_bwd", grid=(t // tm,),
        in_specs=[pl.BlockSpec((tm, d), tile), pl.BlockSpec((tm, dff), tile), pl.BlockSpec((tm, d), tile),
                  _resident((d, dff)), _resident((dff, d)), _resident((1, d))],
        out_specs=(pl.BlockSpec((tm, d), tile), pl.BlockSpec((tm, dff), tile), pl.BlockSpec((F32_SUBLANES, d), lambda i: (0, 0))),
        out_shape=(jax.ShapeDtypeStruct((t, d), F32), jax.ShapeDtypeStruct((t, dff), BF16),
                   jax.ShapeDtypeStruct((F32_SUBLANES, d), F32)),
        compiler_params=_params(),
    )(r1, relu, dr2, w2t, w1t, ln1g)


def _wgrad(pairs, tt, fb, name, xchg=()):
    n, m = len(pairs), len(xchg)
    t, f = pairs[0][0].shape
    d = pairs[0][1].shape[1]
    squares = [sq for _, _, sq in pairs]
    nj, ni = f // fb, t // tt
    xsrcs, xrows = [a_ for a_, _ in xchg], [r for _, r in xchg]

    def body(*refs):
        ins, xsrc = refs[:2 * n], refs[2 * n:2 * n + m]
        outs, xrecv = refs[2 * n + m:3 * n + m], refs[3 * n + m:3 * n + 2 * m]
        accs, sems = refs[3 * n + 2 * m:4 * n + 2 * m], refs[4 * n + 2 * m:]
        j, i = pl.program_id(0), pl.program_id(1)
        exchange = _Exchange(xsrc, xrecv, xrows, *sems) if m else None

        if m:
            @pl.when(jnp.logical_and(j == 0, i == 0))
            def _():
                exchange.start()

        @pl.when(i == 0)
        def _():
            for acc in accs:
                acc[...] = jnp.zeros_like(acc)

        for q in range(n):
            lhs = ins[2 * q][...]
            if squares[q]:
                lf = lhs.astype(F32)
                lhs = (lf * lf).astype(BF16)
            accs[q][...] += lax.dot_general(lhs, ins[2 * q + 1][...].astype(BF16), TN_DIMS, preferred_element_type=F32)

        @pl.when(i == ni - 1)
        def _():
            for q in range(n):
                outs[q][...] = accs[q][...].astype(BF16)

        if m:
            @pl.when(jnp.logical_and(j == nj - 1, i == ni - 1))
            def _():
                exchange.finish()

    lhs_spec = pl.BlockSpec((tt, fb), lambda j, i: (i, j))
    rhs_spec = pl.BlockSpec((tt, d), lambda j, i: (i, 0))
    out_spec = pl.BlockSpec((fb, d), lambda j, i: (j, 0))
    return pl.pallas_call(
        body, name=name, grid=(nj, ni),
        in_specs=[lhs_spec, rhs_spec] * n + [HBM_SPEC] * m, out_specs=(out_spec,) * n + (HBM_SPEC,) * m,
        out_shape=(jax.ShapeDtypeStruct((f, d), BF16),) * n + _Exchange.out_shapes(xsrcs, xrows),
        scratch_shapes=[pltpu.VMEM((fb, d), F32)] * n + (_Exchange.semaphores(m) if m else []),
        compiler_params=_params(("arbitrary", "arbitrary")),
    )(*[a_ for lhs, rhs, _ in pairs for a_ in (lhs, rhs)], *xsrcs)


def _mixer_bwd(dr1, p, ya, yb, w_ot, w_pat, w_pbt, conv_w8, vng, vnb, ws_b, wst_b, bias_s, head_sel, xchg, dm, tm):
    t, d, wa, npj = dm.T, dm.D, dm.WA, dm.NP
    nt = t // tm
    h8, hb = F32_SUBLANES, BF16_SUBLANES
    ext = tm + 2 * h8
    prev_f, next_f = _halo_maps(tm, t, h8)
    prev_b, next_b = _halo_maps(tm, t, hb)
    nx = len(xchg)
    xsrcs, xrows = [a_ for a_, _ in xchg], [r for _, r in xchg]

    def body(dr_ref, drp_ref, drn_ref, p_ref, pp_ref, pn_ref, ya_ref, yb_ref, wot_ref, wpat_ref, wpbt_ref,
             cw_ref, vng_ref, vnb_ref, ws_ref, wst_ref, bias_ref, sel_ref,
             *refs):
        xsrc = refs[:nx]
        dp_ref, a_ref, dya_ref, s_ref, bb_ref, dyb_ref, dws_ref, dbs_ref, dcw_ref, dbg_ref, dvn_ref = refs[nx:nx + 11]
        xrecv = refs[nx + 11:2 * nx + 11]
        ext_ref, ext2_ref, mixed_ref, dvnm_ref = refs[2 * nx + 11:2 * nx + 15]
        exchange = _Exchange(xsrc, xrecv, xrows, *refs[2 * nx + 15:])
        i = pl.program_id(0)

        @pl.when(i == 0)
        def _():
            exchange.start()
            for ref in (dws_ref, dbs_ref, dcw_ref, dbg_ref, dvn_ref):
                ref[...] = jnp.zeros_like(ref)

        def col(ref, lo, width):
            return ref[:, lo:lo + width].astype(F32)

        def ext_rows(prev_blk, center, next_blk):
            return jnp.concatenate([prev_blk, center, next_blk], axis=0)

        def ext_col(lo, width):
            return ext_rows(col(pp_ref, lo, width)[hb - h8:hb], col(p_ref, lo, width), col(pn_ref, lo, width)[0:h8])

        row = lax.broadcasted_iota(jnp.int32, (ext, 1), 0) + (i * tm - h8)
        inside = jnp.logical_and(row >= 0, row < t)
        dr_e = ext_rows(drp_ref[...], dr_ref[...], drn_ref[...])
        ds_e = _mm(dr_e.astype(BF16), wot_ref[...])
        dya_e = ds_e * ext_col(dm.OFF_GA, d)
        da_e = _mm(dya_e.astype(BF16), wpat_ref[...])
        dcv_e = jnp.where(inside, da_e * ext_col(0, wa), 0.0)
        ch_e = jnp.where(inside, ext_col(dm.OFF_CA, wa) * ext_col(dm.OFF_HA, wa), 0.0)
        ext_ref[...] = ch_e
        ext2_ref[...] = dcv_e
        ch, ch_up, ch_dn = ch_e[h8:h8 + tm], ext_ref[pl.ds(h8 - 1, tm), :], ext_ref[pl.ds(h8 + 1, tm), :]
        dcv, dcv_up, dcv_dn = dcv_e[h8:h8 + tm], ext2_ref[pl.ds(h8 - 1, tm), :], ext2_ref[pl.ds(h8 + 1, tm), :]
        w0, w1, w2 = cw_ref[0:1, :], cw_ref[1:2, :], cw_ref[2:3, :]
        cv = w0 * ch_up + w1 * ch + w2 * ch_dn
        dp_ref[:, 0:wa] = (da_e[h8:h8 + tm] * cv).astype(BF16)
        dch = w0 * dcv_dn + w1 * dcv + w2 * dcv_up
        dp_ref[:, dm.OFF_CA:dm.OFF_CA + wa] = (dch * col(p_ref, dm.OFF_HA, wa)).astype(BF16)
        dp_ref[:, dm.OFF_HA:dm.OFF_HA + wa] = (dch * col(p_ref, dm.OFF_CA, wa)).astype(BF16)
        dcw_ref[0:1, :] += _colsum(dcv * ch_up)
        dcw_ref[1:2, :] += _colsum(dcv * ch)
        dcw_ref[2:3, :] += _colsum(dcv * ch_dn)
        a_ref[...] = (col(p_ref, 0, wa) * cv).astype(BF16)
        dya_ref[...] = dya_e[h8:h8 + tm].astype(BF16)
        ds = ds_e[h8:h8 + tm]
        g_a, g_b = col(p_ref, dm.OFF_GA, d), col(p_ref, dm.OFF_GB, d)
        y_a, y_b = ya_ref[...].astype(F32), yb_ref[...].astype(F32)
        s_ref[...] = (g_a * y_a + g_b * y_b).astype(BF16)
        dzga = ds * y_a * g_a * (1.0 - g_a)
        dzgb = ds * y_b * g_b * (1.0 - g_b)
        dp_ref[:, dm.OFF_GA:dm.OFF_GA + d] = dzga.astype(BF16)
        dp_ref[:, dm.OFF_GB:dm.OFF_GB + d] = dzgb.astype(BF16)
        dbg_ref[0:1, 0:d] += _colsum(dzga)
        dbg_ref[0:1, d:2 * d] += _colsum(dzgb)
        dyb = (ds * g_b).astype(BF16)
        dyb_ref[...] = dyb
        gv, dgelu_v = _gelu_and_grad(col(p_ref, dm.OFF_VB, d))
        xhv, rstdv = _ln_stats(gv)
        vn = xhv * vng_ref[...] + vnb_ref[...]
        mixed = _spatial_mix(vn, ws_ref, bias_ref, mixed_ref, dm, tm)
        gu, dgelu_u = _gelu_and_grad(col(p_ref, dm.OFF_UB, d))
        bb_ref[...] = (gu * mixed).astype(BF16)
        dbb = _mm(dyb, wpbt_ref[...])
        dp_ref[:, dm.OFF_UB:dm.OFF_UB + d] = (dbb * mixed * dgelu_u).astype(BF16)
        dmb = (dbb * gu).astype(BF16)
        vb = vn.astype(BF16)
        dbs = jnp.zeros((CHUNK, CHUNK), F32)
        for cc in range(tm // CHUNK):
            r0 = cc * CHUNK
            dbs = dbs + _mm(dmb[r0:r0 + CHUNK, :], sel_ref[...])
            for h in range(dm.H):
                c0 = h * CHUNK
                blk = dmb[r0:r0 + CHUNK, c0:c0 + CHUNK]
                dvnm_ref[r0:r0 + CHUNK, c0:c0 + CHUNK] = _mm(wst_ref[h], blk)
                dws_ref[h] += lax.dot_general(blk, vb[r0:r0 + CHUNK, c0:c0 + CHUNK], NT_DIMS, preferred_element_type=F32)
        dbs_ref[...] += dbs
        dvn = dvnm_ref[...]
        dvn_ref[0:1, :] += _colsum(dvn * xhv)
        dvn_ref[1:2, :] += _colsum(dvn)
        dp_ref[:, dm.OFF_VB:dm.OFF_VB + d] = (_ln_bwd(dvn * vng_ref[...], xhv, rstdv) * dgelu_v).astype(BF16)

        @pl.when(i == nt - 1)
        def _():
            exchange.finish()

    full = lambda i: (0, 0)
    tile = lambda i: (i, 0)
    hcc = _resident((dm.H, CHUNK, CHUNK))
    tok = lambda w, dt: jax.ShapeDtypeStruct((t, w), dt)
    return pl.pallas_call(
        body, name="mixer_bwd", grid=(nt,),
        in_specs=[pl.BlockSpec((tm, d), tile), pl.BlockSpec((h8, d), prev_f), pl.BlockSpec((h8, d), next_f),
                  pl.BlockSpec((tm, npj), tile), pl.BlockSpec((hb, npj), prev_b), pl.BlockSpec((hb, npj), next_b),
                  pl.BlockSpec((tm, d), tile), pl.BlockSpec((tm, d), tile),
                  _resident((d, d)), _resident((d, wa)), _resident((d, d)),
                  _resident((h8, wa)), _resident((1, d)), _resident((1, d)), hcc, hcc, _resident((CHUNK, d)),
                  _resident((d, CHUNK))] + [HBM_SPEC] * nx,
        out_specs=(pl.BlockSpec((tm, npj), tile), pl.BlockSpec((tm, wa), tile), pl.BlockSpec((tm, d), tile),
                   pl.BlockSpec((tm, d), tile), pl.BlockSpec((tm, d), tile), pl.BlockSpec((tm, d), tile),
                   pl.BlockSpec((dm.H, CHUNK, CHUNK), lambda i: (0, 0, 0)), pl.BlockSpec((CHUNK, CHUNK), full),
                   pl.BlockSpec((h8, wa), full), pl.BlockSpec((h8, 2 * d), full), pl.BlockSpec((h8, d), full))
        + (HBM_SPEC,) * nx,
        out_shape=(tok(npj, BF16), tok(wa, BF16), tok(d, BF16), tok(d, BF16), tok(d, BF16), tok(d, BF16),
                   jax.ShapeDtypeStruct((dm.H, CHUNK, CHUNK), F32), jax.ShapeDtypeStruct((CHUNK, CHUNK), F32),
                   jax.ShapeDtypeStruct((h8, wa), F32), jax.ShapeDtypeStruct((h8, 2 * d), F32),
                   jax.ShapeDtypeStruct((h8, d), F32)) + _Exchange.out_shapes(xsrcs, xrows),
        scratch_shapes=[pltpu.VMEM((ext, wa), F32), pltpu.VMEM((ext, wa), F32), pltpu.VMEM((tm, d), F32),
                        pltpu.VMEM((tm, d), F32)] + _Exchange.semaphores(nx),
        compiler_params=_params(),
    )(dr1, dr1, dr1, p, p, p, ya, yb, w_ot, w_pat, w_pbt, conv_w8, vng, vnb, ws_b, wst_b, bias_s, head_sel, *xsrcs)


def _input_grad(dp, dr1, w_int, xchg, dm, tm):
    t, d, npj = dm.T, dm.D, dm.NP
    nt = t // tm
    nx = len(xchg)
    xsrcs, xrows = [a_ for a_, _ in xchg], [r for _, r in xchg]

    def body(dp_ref, dr_ref, w_ref, *refs):
        dx_ref = refs[nx]
        exchange = _Exchange(refs[:nx], refs[nx + 1:2 * nx + 1], xrows, *refs[2 * nx + 1:])
        i = pl.program_id(0)

        @pl.when(i == 0)
        def _():
            exchange.start()

        dx_ref[...] = ALPHA * dr_ref[...] + _mm(dp_ref[...], w_ref[...])

        @pl.when(i == nt - 1)
        def _():
            exchange.finish()

    return pl.pallas_call(
        body, name="input_grad", grid=(nt,),
        in_specs=[pl.BlockSpec((tm, npj), lambda i: (i, 0)), pl.BlockSpec((tm, d), lambda i: (i, 0)), _resident((npj, d))]
        + [HBM_SPEC] * nx,
        out_specs=(pl.BlockSpec((tm, d), lambda i: (i, 0)),) + (HBM_SPEC,) * nx,
        out_shape=(jax.ShapeDtypeStruct((t, d), F32),) + _Exchange.out_shapes(xsrcs, xrows),
        scratch_shapes=_Exchange.semaphores(nx),
        compiler_params=_params(),
    )(dp, dr1, w_int, *xsrcs)


def _adamw(w, g, m, v, name):
    rows, cols = w.shape
    tr = 256 if rows % 256 == 0 else rows
    bc1 = 1.0 - ADAM_B1 ** ADAM_STEP
    bc2 = 1.0 - ADAM_B2 ** ADAM_STEP

    def body(w_ref, g_ref, m_ref, v_ref, d_ref, nm_ref, nv_ref):
        g_ = g_ref[...]
        nm = ADAM_B1 * m_ref[...] + (1.0 - ADAM_B1) * g_
        nv = ADAM_B2 * v_ref[...] + (1.0 - ADAM_B2) * (g_ * g_)
        d_ref[...] = -ADAM_LR * ((nm / bc1) / (jnp.sqrt(nv / bc2) + ADAM_EPS) + ADAM_WD * w_ref[...])
        nm_ref[...] = nm
        nv_ref[...] = nv

    spec = pl.BlockSpec((tr, cols), lambda i: (i, 0))
    shp = jax.ShapeDtypeStruct((rows, cols), F32)
    return pl.pallas_call(
        body, name=name, grid=(rows // tr,), in_specs=[spec] * 4, out_specs=(spec,) * 3, out_shape=(shp,) * 3,
        compiler_params=_params(),
    )(w, g, m, v)


def _to_slab(parts):
    flat = jnp.concatenate([q.reshape(-1) for q in parts])
    pad = (-flat.shape[0]) % (F32_SUBLANES * LANES)
    return jnp.pad(flat, (0, pad)).reshape(-1, LANES)


def _from_slab(slab, shapes):
    flat = slab.reshape(-1)
    out, off = [], 0
    for s in shapes:
        n = math.prod(s)
        out.append(flat[off:off + n].reshape(s))
        off += n
    return out


def kernel(x, w_in, b_gate, conv_w, v_norm_g, v_norm_b, w_s, b_s, w_pa, w_pb, w_o, ln1_g, ln1_b, w_ff1, w_ff2, ln2_g, ln2_b, loss_target, m_w_in, m_b_gate, m_conv_w, m_v_norm_g, m_v_norm_b, m_w_s, m_b_s, m_w_pa, m_w_pb, m_w_o, m_ln1_g, m_ln1_b, m_w_ff1, m_w_ff2, m_ln2_g, m_ln2_b, v_w_in, v_b_gate, v_conv_w, v_v_norm_g, v_v_norm_b, v_w_s, v_b_s, v_w_pa, v_w_pb, v_w_o, v_ln1_g, v_ln1_b, v_w_ff1, v_w_ff2, v_ln2_g, v_ln2_b):
    t, d = x.shape[1], x.shape[2]
    dm = _Dims(t, d)
    tm = 256 if t % 256 == 0 else CHUNK
    tm_big = 512 if t % 512 == 0 else tm
    tt = 1024 if t % 1024 == 0 else tm
    tm_proj = tt
    me =4 * lax.axis_index("x") + 2 * lax.axis_index("y") + lax.axis_index("c")
    x2, tgt = x[0], loss_target[0]

    conv_bits = lax.bitcast_convert_type(conv_w[0], BF16).reshape(-1)
    conv_blk = jnp.pad(conv_bits, (0, dm.conv_rows * d - conv_bits.shape[0])).reshape(dm.conv_rows, d)
    ax, ay = lax.axis_index("x"), lax.axis_index("y")
    chip_order = jnp.stack([2 * ax + ay, 2 * (1 - ax) + ay, 2 * ax + 1 - ay, 2 * (1 - ax) + 1 - ay]).astype(jnp.int32)
    p, w_int, w_pa_f, w_pb_f, w_o_f, w_1t, w_2, conv_g = _proj_in(
        x2, b_gate, [w_in[0].T.astype(BF16), w_pa[0].astype(BF16), w_pb[0].astype(BF16), w_o[0].astype(BF16),
                     w_ff1[0].T.astype(BF16), w_ff2[0].astype(BF16), conv_blk], chip_order, dm, tm_proj)
    wa8 = dm.WA // N_DEV
    conv_all = lax.bitcast_convert_type(conv_g.reshape(N_DEV, -1)[:, :3 * wa8 * 2].reshape(N_DEV, 3, wa8, 2), F32)
    conv_full = jnp.transpose(conv_all, (1, 0, 2)).reshape(3, dm.WA)
    conv_w8 = jnp.pad(conv_full, ((0, F32_SUBLANES - 3), (0, 0)))
    ws_b = w_s[0].astype(BF16)
    wst_b = jnp.transpose(w_s[0], (0, 2, 1)).astype(BF16)
    bias_s = jnp.repeat(b_s[0].T, CHUNK, axis=1)
    head_sel = (jnp.arange(d)[:, None] // CHUNK == jnp.arange(CHUNK)[None, :]).astype(BF16)

    ya, yb, r1 = _mixer_fwd(p, x2, w_pa_f, w_pb_f, w_o_f, conv_w8, v_norm_g, v_norm_b, ws_b, bias_s, dm, tm)
    relu, x1b, dr2, dr2b, sums2 = _ffn_fwd(r1, tgt, w_1t.T, w_2, ln1_g, ln1_b, ln2_g, ln2_b, dm, tm_big)
    dr1, dh1, sums1 = _ffn_bwd(r1, relu, dr2, w_2.T, w_1t, ln1_g, dm, tm_big)
    fb = min(1024, dm.DFF)
    rows = dm.shard_rows
    g_ff1t, g_ff2 = _wgrad([(dh1, x1b, False), (relu, dr2b, True)], tt, fb, "ffn_wgrad")
    dp, a_m, dya, s_m, bb_m, dyb, g_ws, g_bs_t, g_cw, g_bg, g_vn, got_ff1t, got_ff2 = _mixer_bwd(
        dr1, p, ya, yb, w_o_f.T, w_pa_f.T, w_pb_f.T, conv_w8, v_norm_g, v_norm_b, ws_b, wst_b, bias_s, head_sel,
        [(g_ff1t, rows[4]), (g_ff2, rows[5])], dm, tm)
    (g_pa,) = _wgrad([(a_m, dya, False)], tt, dm.WA, "w_pa_grad")
    g_o, g_pb = _wgrad([(s_m, dr1, False), (bb_m, dyb, False)], tt, d, "w_o_pb_grad")
    g_int, got_pa, got_pb, got_o = _wgrad([(dp, x2, False)], tt, 17 * LANES, "w_in_grad",
                                          xchg=[(g_pa, rows[1]), (g_pb, rows[2]), (g_o, rows[3])])
    small_parts = [g_bg[0], g_cw[0:3], g_vn[0], g_vn[1], g_ws, g_bs_t[:, :dm.H].T,
                   sums1[0], sums1[1], sums2[1], sums2[2], sums2[0]]
    grad_x, got_int, got_s = _input_grad(dp, dr1, w_int, [(g_int, rows[0]), (_to_slab(small_parts), None)], dm, tm_big)

    cut = [_sum_slots(g, 256, "sum_grads_%d" % k) for k, g in enumerate((got_int, got_pa, got_pb, got_o, got_ff1t, got_ff2))]
    ssum = _sum_slots(got_s, 256, "sum_small")
    (s_bg, s_cw, s_vng, s_vnb, s_ws, s_bs, s_l1g, s_l1b, s_l2g, s_l2b, s_sq) = _from_slab(
        ssum, [(1, 2 * d), (3, dm.WA), (1, d), (1, d), (1, dm.H, CHUNK, CHUNK), (1, dm.H, CHUNK),
               (1, d), (1, d), (1, d), (1, d), (d,)])
    loss = 0.5 * jnp.sum(s_sq) / d
    s_cw = lax.dynamic_slice(s_cw, (0, me * wa8), (3, wa8))[None]
    big_g = [cut[0].T, cut[1], cut[2], cut[3], cut[4].T, cut[5]]

    big_w = [(w_in, m_w_in, v_w_in), (w_pa, m_w_pa, v_w_pa), (w_pb, m_w_pb, v_w_pb), (w_o, m_w_o, v_w_o),
             (w_ff1, m_w_ff1, v_w_ff1), (w_ff2, m_w_ff2, v_w_ff2)]
    big_out = [_adamw(w[0], g, m[0], v[0], "adamw_%d" % k) for k, ((w, m, v), g) in enumerate(zip(big_w, big_g))]
    small_w = [(b_gate, m_b_gate, v_b_gate), (conv_w, m_conv_w, v_conv_w), (v_norm_g, m_v_norm_g, v_v_norm_g),
               (v_norm_b, m_v_norm_b, v_v_norm_b), (w_s, m_w_s, v_w_s), (b_s, m_b_s, v_b_s), (ln1_g, m_ln1_g, v_ln1_g),
               (ln1_b, m_ln1_b, v_ln1_b), (ln2_g, m_ln2_g, v_ln2_g), (ln2_b, m_ln2_b, v_ln2_b)]
    small_g = [s_bg, s_cw, s_vng, s_vnb, s_ws, s_bs, s_l1g, s_l1b, s_l2g, s_l2b]
    small_shapes = [w.shape for w, _, _ in small_w]
    sd, sm, sv = _adamw(_to_slab([w for w, _, _ in small_w]), _to_slab(small_g), _to_slab([m for _, m, _ in small_w]),
                        _to_slab([v for _, _, v in small_w]), "adamw_small")
    small_out = list(zip(_from_slab(sd, small_shapes), _from_slab(sm, small_shapes), _from_slab(sv, small_shapes)))

    order = [("b", 0), ("s", 0), ("s", 1), ("s", 2), ("s", 3), ("s", 4), ("s", 5), ("b", 1), ("b", 2), ("b", 3),
             ("s", 6), ("s", 7), ("b", 4), ("b", 5), ("s", 8), ("s", 9)]
    grads, deltas, new_m, new_v = [], [], [], []
    for kind, k in order:
        if kind == "b":
            g, (dl, nm, nv) = big_g[k][None], big_out[k]
            dl, nm, nv = dl[None], nm[None], nv[None]
        else:
            g, (dl, nm, nv) = small_g[k], small_out[k]
        grads.append(g)
        deltas.append(dl)
        new_m.append(nm)
        new_v.append(nv)
    return (loss, grad_x[None], *grads, *deltas, *new_m, *new_v)
```

```python
import math

import jax
import jax.numpy as jnp
from jax import lax
from jax.experimental import pallas as pl
from jax.experimental.pallas import tpu as pltpu

F32 = jnp.float32
BF16 = jnp.bfloat16
N_DEV = 8
CHUNK = 128
LN_EPS = 1e-5
ALPHA = 2.0 ** 0.25
ADAM_LR, ADAM_B1, ADAM_B2, ADAM_EPS, ADAM_WD, ADAM_STEP = 0.001, 0.9, 0.999, 1e-08, 0.01, 10
F32_SUBLANES = 8
BF16_SUBLANES = 16
LANES = 128
VMEM_LIMIT = 56 * 1024 * 1024
MESH = pl.DeviceIdType.MESH
NT_DIMS = (((1,), (1,)), ((), ()))
TN_DIMS = (((0,), (0,)), ((), ()))
HBM_SPEC = pl.BlockSpec(memory_space=pltpu.HBM)


class _Dims:
    def __init__(self, t, d):
        self.T, self.D = t, d
        self.WA = 3 * d // 2
        self.NP = 3 * self.WA + 4 * d
        self.DFF = 4 * d
        self.H = d // CHUNK
        self.OFF_CA, self.OFF_HA = self.WA, 2 * self.WA
        self.OFF_UB = 3 * self.WA
        self.OFF_VB = self.OFF_UB + d
        self.OFF_GA = self.OFF_VB + d
        self.OFF_GB = self.OFF_GA + d
        self.shard_rows = (self.NP // N_DEV, self.WA // N_DEV, d // N_DEV, d // N_DEV, self.DFF // N_DEV, self.DFF // N_DEV)
        self.conv_rows = BF16_SUBLANES * max(1, -(-(3 * (self.WA // N_DEV) * 2) // (BF16_SUBLANES * d)))


def _params(sem=("arbitrary",), vmem=VMEM_LIMIT):
    return pltpu.CompilerParams(dimension_semantics=sem, vmem_limit_bytes=vmem)


def _mesh_pos():
    return lax.axis_index("x"), lax.axis_index("y"), lax.axis_index("c")


def _resident(shape):
    zeros = (0,) * len(shape)
    return pl.BlockSpec(shape, lambda *_: zeros, pipeline_mode=pl.Buffered(1))


class _TwoLevelGather:
    def __init__(self, shard_refs, out_refs, send_sems, recv_sems, local_sems):
        self.n = len(shard_refs)
        self.shard_refs, self.out_refs = shard_refs, out_refs
        self.send_sems, self.recv_sems, self.local_sems = send_sems, recv_sems, local_sems
        x, y, c = _mesh_pos()
        self.c = c
        self.me, self.sibling = (x, y, c), (x, y, 1 - c)
        self.chips = [(1 - x, y), (x, 1 - y), (1 - x, 1 - y)]

    def _slot(self, a, px, py, pc):
        rows = self.shard_refs[a].shape[0]
        return self.out_refs[a].at[pl.ds((4 * px + 2 * py + pc) * rows, rows), :]

    def _copy(self, a, k, block, to, src=None):
        return pltpu.make_async_remote_copy(
            src_ref=self._slot(a, *block) if src is None else src, dst_ref=self._slot(a, *block),
            send_sem=self.send_sems.at[7 * a + k], recv_sem=self.recv_sems.at[7 * a + k], device_id=to, device_id_type=MESH)

    def _mine(self):
        return [pltpu.make_async_copy(self.shard_refs[a], self._slot(a, *self.me), self.local_sems.at[a]) for a in range(self.n)]

    def _first(self):
        out = []
        for a in range(self.n):
            out.append(self._copy(a, 0, self.me, self.sibling, src=self.shard_refs[a]))
            out += [self._copy(a, 1 + j, self.me, (*chip, self.c), src=self.shard_refs[a]) for j, chip in enumerate(self.chips)]
        return out

    def _passed(self):
        return [self._copy(a, 4 + j, (*chip, self.c), self.sibling) for j, chip in enumerate(self.chips) for a in range(self.n)]

    def start(self):
        for cp in self._mine() + self._first():
            cp.start()

    def forward(self):
        passed = self._passed()
        for j, chip in enumerate(self.chips):
            for a in range(self.n):
                self._copy(a, 1 + j, (*chip, self.c), self.me).wait_recv()
                passed[j * self.n + a].start()

    def finish(self):
        for a in range(self.n):
            self._copy(a, 0, self.sibling, self.me).wait_recv()
            for j, chip in enumerate(self.chips):
                self._copy(a, 4 + j, (*chip, 1 - self.c), self.me).wait_recv()
        for cp in self._first() + self._passed():
            cp.wait_send()
        for cp in self._mine():
            cp.wait()

    @staticmethod
    def out_shapes(shards):
        return tuple(jax.ShapeDtypeStruct((N_DEV * s.shape[0], s.shape[1]), s.dtype) for s in shards)

    @staticmethod
    def semaphores(n):
        return [pltpu.SemaphoreType.DMA((7 * n,)), pltpu.SemaphoreType.DMA((7 * n,)), pltpu.SemaphoreType.DMA((n,))]


def _all_gather(shards):
    n = len(shards)

    def body(*refs):
        gather = _TwoLevelGather(refs[:n], refs[n:2 * n], *refs[2 * n:])
        gather.start()
        gather.forward()
        gather.finish()

    return pl.pallas_call(
        body, name="all_gather_w_in", out_shape=_TwoLevelGather.out_shapes(shards),
        in_specs=[HBM_SPEC] * n, out_specs=(HBM_SPEC,) * n, scratch_shapes=_TwoLevelGather.semaphores(n),
    )(*shards)


class _Exchange:
    def __init__(self, src_refs, recv_refs, rows, send_sems, recv_sems, local_sems):
        x, y, c = _mesh_pos()
        me = 4 * x + 2 * y + c
        self.own, self.sends, self.arrivals = [], [], []
        for a, (src, recv) in enumerate(zip(src_refs, recv_refs)):
            def blk(k, src=src, r=rows[a]):
                return src if r is None else src.at[pl.ds(k * r, r), :]

            self.own.append(pltpu.make_async_copy(blk(me), recv.at[me], local_sems.at[a]))
            for rel in range(1, N_DEV):
                px = 1 - x if rel & 4 else x
                py = 1 - y if rel & 2 else y
                pc = 1 - c if rel & 1 else c
                peer = 4 * px + 2 * py + pc
                sem = dict(send_sem=send_sems.at[7 * a + rel - 1], recv_sem=recv_sems.at[7 * a + rel - 1],
                           device_id=(px, py, pc), device_id_type=MESH)
                self.sends.append(pltpu.make_async_remote_copy(src_ref=blk(peer), dst_ref=recv.at[me], **sem))
                self.arrivals.append(pltpu.make_async_remote_copy(src_ref=blk(me), dst_ref=recv.at[peer], **sem))

    def start(self):
        for cp in self.own + self.sends:
            cp.start()

    def finish(self):
        for cp in self.arrivals:
            cp.wait_recv()
        for cp in self.sends:
            cp.wait_send()
        for cp in self.own:
            cp.wait()

    @staticmethod
    def out_shapes(srcs, rows):
        return tuple(jax.ShapeDtypeStruct((N_DEV, s.shape[0] if r is None else r, s.shape[1]), s.dtype) for s, r in zip(srcs, rows))

    @staticmethod
    def semaphores(n):
        return [pltpu.SemaphoreType.DMA((7 * n,)), pltpu.SemaphoreType.DMA((7 * n,)), pltpu.SemaphoreType.DMA((n,))]


def _sum_slots(slots, tile_rows, name):
    _, rows, cols = slots.shape
    tr = rows
    if N_DEV * rows * cols * slots.dtype.itemsize > 8 * 1024 * 1024:
        tr = next(c for c in (256, 192, 128, 64, 32, 16) if c <= tile_rows and rows % c == 0)

    def body(s_ref, o_ref):
        acc = s_ref[0].astype(F32)
        for k in range(1, N_DEV):
            acc = acc + s_ref[k].astype(F32)
        o_ref[...] = acc

    return pl.pallas_call(
        body, name=name, grid=(rows // tr,),
        in_specs=[pl.BlockSpec((N_DEV, tr, cols), lambda i: (0, i, 0))],
        out_specs=pl.BlockSpec((tr, cols), lambda i: (i, 0)),
        out_shape=jax.ShapeDtypeStruct((rows, cols), F32),
        compiler_params=_params(),
    )(slots)


def _gelu_and_grad(x):
    k0 = math.sqrt(2.0 / math.pi)
    k1 = 0.044715
    x2 = x * x
    th = jnp.tanh(k0 * x * (1.0 + k1 * x2))
    half = 0.5 * (1.0 + th)
    return x * half, half + 0.5 * x * (1.0 - th * th) * (k0 * (1.0 + 3.0 * k1 * x2))


def _ln_stats(r):
    mu = jnp.mean(r, axis=-1, keepdims=True)
    rc = r - mu
    var = jnp.mean(rc * rc, axis=-1, keepdims=True)
    rstd = lax.rsqrt(var + LN_EPS)
    return rc * rstd, rstd


def _ln_bwd(dxh, xh, rstd):
    return rstd * (dxh - jnp.mean(dxh, axis=-1, keepdims=True) - xh * jnp.mean(dxh * xh, axis=-1, keepdims=True))


def _colsum(a):
    return jnp.sum(a, axis=0, keepdims=True)


def _mm(a, b):
    return jnp.dot(a, b, preferred_element_type=F32)


def _halo_maps(tm, t, unit):
    per, last = tm // unit, t // unit - 1
    return (lambda i: (jnp.maximum(i * per - 1, 0), 0)), (lambda i: (jnp.minimum((i + 1) * per, last), 0))


def _proj_in(x2, w_int, b_gate, shards, dm, tm):
    t, d, npj = dm.T, dm.D, dm.NP
    cw = d // 2
    nt = t // tm
    n = len(shards)

    def body(x_ref, w_ref, bg_ref, *refs):
        p_ref = refs[n]
        gather = _TwoLevelGather(refs[:n], refs[n + 1:2 * n + 1], *refs[2 * n + 1:])
        i = pl.program_id(0)

        @pl.when(i == 0)
        def _():
            gather.start()

        @pl.when(i == nt // 2)
        def _():
            gather.forward()

        xb = x_ref[...].astype(BF16)
        for blk in range(npj // cw):
            lo = blk * cw
            acc = lax.dot_general(xb, w_ref[lo:lo + cw, :], NT_DIMS, preferred_element_type=F32)
            if lo >= dm.OFF_GA:
                acc = jax.nn.sigmoid(acc + bg_ref[:, lo - dm.OFF_GA:lo - dm.OFF_GA + cw])
            p_ref[:, lo:lo + cw] = acc.astype(BF16)

        @pl.when(i == nt - 1)
        def _():
            gather.finish()

    return pl.pallas_call(
        body, name="proj_in", grid=(nt,),
        in_specs=[pl.BlockSpec((tm, d), lambda i: (i, 0)), _resident((npj, d)), _resident((1, 2 * d))] + [HBM_SPEC] * n,
        out_specs=(pl.BlockSpec((tm, npj), lambda i: (i, 0)),) + (HBM_SPEC,) * n,
        out_shape=(jax.ShapeDtypeStruct((t, npj), BF16),) + _TwoLevelGather.out_shapes(shards),
        scratch_shapes=_TwoLevelGather.semaphores(n),
        compiler_params=_params(),
    )(x2, w_int, b_gate, *shards)


def _conv_taps(ext_ref, center, prev_blk, next_blk, first, last, tm):
    h = F32_SUBLANES
    ext_ref[0:h, :] = jnp.where(first, 0.0, prev_blk)
    ext_ref[h:h + tm, :] = center
    ext_ref[h + tm:h + tm + h, :] = jnp.where(last, 0.0, next_blk)
    return ext_ref[pl.ds(h - 1, tm), :], ext_ref[pl.ds(h + 1, tm), :]


def _spatial_mix(vn, ws_ref, bias_ref, mixed_ref, dm, tm):
    vb = vn.astype(BF16)
    for cc in range(tm // CHUNK):
        r0 = cc * CHUNK
        for h in range(dm.H):
            c0 = h * CHUNK
            m = _mm(ws_ref[h], vb[r0:r0 + CHUNK, c0:c0 + CHUNK])
            mixed_ref[r0:r0 + CHUNK, c0:c0 + CHUNK] = m + bias_ref[:, c0:c0 + CHUNK]
    return mixed_ref[...]


def _mixer_fwd(p, x2, w_pa, w_pb, w_o, conv_w8, vng, vnb, ws_b, bias_s, dm, tm):
    t, d, wa, npj = dm.T, dm.D, dm.WA, dm.NP
    nt = t // tm
    hb = BF16_SUBLANES
    prev_map, next_map = _halo_maps(tm, t, hb)

    def body(p_ref, pp_ref, pn_ref, x_ref, wpa_ref, wpb_ref, wo_ref, cw_ref, vng_ref, vnb_ref, ws_ref, bias_ref,
             ya_ref, yb_ref, r1_ref, ext_ref, mixed_ref):
        i = pl.program_id(0)

        def col(ref, lo, width):
            return ref[:, lo:lo + width].astype(F32)

        ch = col(p_ref, dm.OFF_CA, wa) * col(p_ref, dm.OFF_HA, wa)
        chp = (col(pp_ref, dm.OFF_CA, wa) * col(pp_ref, dm.OFF_HA, wa))[hb - F32_SUBLANES:hb]
        chn = (col(pn_ref, dm.OFF_CA, wa) * col(pn_ref, dm.OFF_HA, wa))[0:F32_SUBLANES]
        up, dn = _conv_taps(ext_ref, ch, chp, chn, i == 0, i == nt - 1, tm)
        a = col(p_ref, 0, wa) * (cw_ref[0:1, :] * up + cw_ref[1:2, :] * ch + cw_ref[2:3, :] * dn)
        ya = _mm(a.astype(BF16), wpa_ref[...])
        gv, _ = _gelu_and_grad(col(p_ref, dm.OFF_VB, d))
        xhv, _ = _ln_stats(gv)
        mixed = _spatial_mix(xhv * vng_ref[...] + vnb_ref[...], ws_ref, bias_ref, mixed_ref, dm, tm)
        gu, _ = _gelu_and_grad(col(p_ref, dm.OFF_UB, d))
        yb = _mm((gu * mixed).astype(BF16), wpb_ref[...])
        s = col(p_ref, dm.OFF_GA, d) * ya + col(p_ref, dm.OFF_GB, d) * yb
        mix = _mm(s.astype(BF16), wo_ref[...])
        ya_ref[...] = ya.astype(BF16)
        yb_ref[...] = yb.astype(BF16)
        r1_ref[...] = ALPHA * x_ref[...] + mix

    tile = lambda i: (i, 0)
    return pl.pallas_call(
        body, name="mixer_fwd", grid=(nt,),
        in_specs=[pl.BlockSpec((tm, npj), tile), pl.BlockSpec((hb, npj), prev_map), pl.BlockSpec((hb, npj), next_map),
                  pl.BlockSpec((tm, d), tile), _resident((wa, d)), _resident((d, d)), _resident((d, d)),
                  _resident((F32_SUBLANES, wa)), _resident((1, d)), _resident((1, d)),
                  _resident((dm.H, CHUNK, CHUNK)), _resident((CHUNK, d))],
        out_specs=(pl.BlockSpec((tm, d), tile), pl.BlockSpec((tm, d), tile), pl.BlockSpec((tm, d), tile)),
        out_shape=(jax.ShapeDtypeStruct((t, d), BF16), jax.ShapeDtypeStruct((t, d), BF16), jax.ShapeDtypeStruct((t, d), F32)),
        scratch_shapes=[pltpu.VMEM((tm + 2 * F32_SUBLANES, wa), F32), pltpu.VMEM((tm, d), F32)],
        compiler_params=_params(),
    )(p, p, p, x2, w_pa, w_pb, w_o, conv_w8, vng, vnb, ws_b, bias_s)


def _ffn_fwd(r1, tgt, w1t, w2, ln1g, ln1b, ln2g, ln2b, dm, tm):
    t, d, dff = dm.T, dm.D, dm.DFF
    fc = dff // N_DEV

    def body(r1_ref, tgt_ref, w1t_ref, w2_ref, g1_ref, b1_ref, g2_ref, b2_ref, relu_ref, x1_ref, dr2_ref, dr2b_ref, sums_ref):
        @pl.when(pl.program_id(0) == 0)
        def _():
            sums_ref[...] = jnp.zeros_like(sums_ref)

        xh1, _ = _ln_stats(r1_ref[...])
        x1 = xh1 * g1_ref[...] + b1_ref[...]
        x1b = x1.astype(BF16)
        x1_ref[...] = x1b
        ffn = jnp.zeros((tm, d), F32)
        for k in range(dff // fc):
            ks = slice(k * fc, (k + 1) * fc)
            r = jnp.maximum(lax.dot_general(x1b, w1t_ref[ks, :], NT_DIMS, preferred_element_type=F32), 0.0)
            relu_ref[:, ks] = r.astype(BF16)
            ffn = ffn + _mm((r * r).astype(BF16), w2_ref[ks, :])
        xh2, rstd2 = _ln_stats(ALPHA * x1 + ffn)
        diff = xh2 * g2_ref[...] + b2_ref[...] - tgt_ref[...]
        dy = diff * (1.0 / d)
        dr2 = _ln_bwd(dy * g2_ref[...], xh2, rstd2)
        dr2_ref[...] = dr2
        dr2b_ref[...] = dr2.astype(BF16)
        sums_ref[0:1, :] += _colsum(diff * diff)
        sums_ref[1:2, :] += _colsum(dy * xh2)
        sums_ref[2:3, :] += _colsum(dy)

    tile = lambda i: (i, 0)
    vec = _resident((1, d))
    return pl.pallas_call(
        body, name="ffn_fwd", grid=(t // tm,),
        in_specs=[pl.BlockSpec((tm, d), tile), pl.BlockSpec((tm, d), tile), _resident((dff, d)), _resident((dff, d)),
                  vec, vec, vec, vec],
        out_specs=(pl.BlockSpec((tm, dff), tile), pl.BlockSpec((tm, d), tile), pl.BlockSpec((tm, d), tile),
                   pl.BlockSpec((tm, d), tile), pl.BlockSpec((F32_SUBLANES, d), lambda i: (0, 0))),
        out_shape=(jax.ShapeDtypeStruct((t, dff), BF16), jax.ShapeDtypeStruct((t, d), BF16), jax.ShapeDtypeStruct((t, d), F32),
                   jax.ShapeDtypeStruct((t, d), BF16), jax.ShapeDtypeStruct((F32_SUBLANES, d), F32)),
        compiler_params=_params(),
    )(r1, tgt, w1t, w2, ln1g, ln1b, ln2g, ln2b)


def _ffn_bwd(r1, relu, dr2, w2, w1t, ln1g, dm, tm):
    t, d, dff = dm.T, dm.D, dm.DFF
    fc = dff // N_DEV

    def body(r1_ref, relu_ref, dr2_ref, w2_ref, w1t_ref, g1_ref, dr1_ref, dh_ref, sums_ref):
        @pl.when(pl.program_id(0) == 0)
        def _():
            sums_ref[...] = jnp.zeros_like(sums_ref)

        xh1, rstd1 = _ln_stats(r1_ref[...])
        dr2 = dr2_ref[...]
        dr2b = dr2.astype(BF16)
        dx1 = ALPHA * dr2
        for k in range(dff // fc):
            ks = slice(k * fc, (k + 1) * fc)
            dact = lax.dot_general(dr2b, w2_ref[ks, :], NT_DIMS, preferred_element_type=F32)
            dhb = (dact * (2.0 * relu_ref[:, ks].astype(F32))).astype(BF16)
            dh_ref[:, ks] = dhb
            dx1 = dx1 + _mm(dhb, w1t_ref[ks, :])
        dr1_ref[...] = _ln_bwd(dx1 * g1_ref[...], xh1, rstd1)
        sums_ref[0:1, :] += _colsum(dx1 * xh1)
        sums_ref[1:2, :] += _colsum(dx1)

    tile = lambda i: (i, 0)
    return pl.pallas_call(
        body, name="ffn_bwd", grid=(t // tm,),
        in_specs=[pl.BlockSpec((tm, d), tile), pl.BlockSpec((tm, dff), tile), pl.BlockSpec((tm, d), tile),
                  _resident((dff, d)), _resident((dff, d)), _resident((1, d))],
        out_specs=(pl.BlockSpec((tm, d), tile), pl.BlockSpec((tm, dff), tile), pl.BlockSpec((F32_SUBLANES, d), lambda i: (0, 0))),
        out_shape=(jax.ShapeDtypeStruct((t, d), F32), jax.ShapeDtypeStruct((t, dff), BF16),
                   jax.ShapeDtypeStruct((F32_SUBLANES, d), F32)),
        compiler_params=_params(),
    )(r1, relu, dr2, w2, w1t, ln1g)


def _wgrad(pairs, tt, fb, name, xchg=()):
    n, m = len(pairs), len(xchg)
    t, f = pairs[0][0].shape
    d = pairs[0][1].shape[1]
    squares = [sq for _, _, sq in pairs]
    nj, ni = f // fb, t // tt
    xrows = [r for _, r in xchg]

    def body(*refs):
        ins, xsrc = refs[:2 * n], refs[2 * n:2 * n + m]
        outs, xrecv = refs[2 * n + m:3 * n + m], refs[3 * n + m:3 * n + 2 * m]
        accs, sems = refs[3 * n + 2 * m:4 * n + 2 * m], refs[4 * n + 2 * m:]
        j, i = pl.program_id(0), pl.program_id(1)
        exchange = _Exchange(xsrc, xrecv, xrows, *sems) if m else None

        if m:
            @pl.when(jnp.logical_and(j == 0, i == 0))
            def _():
                exchange.start()

        @pl.when(i == 0)
        def _():
            for acc in accs:
                acc[...] = jnp.zeros_like(acc)

        for q in range(n):
            lhs = ins[2 * q][...]
            if squares[q]:
                lf = lhs.astype(F32)
                lhs = (lf * lf).astype(BF16)
            accs[q][...] += lax.dot_general(lhs, ins[2 * q + 1][...].astype(BF16), TN_DIMS, preferred_element_type=F32)

        @pl.when(i == ni - 1)
        def _():
            for q in range(n):
                outs[q][...] = accs[q][...].astype(BF16)

        if m:
            @pl.when(jnp.logical_and(j == nj - 1, i == ni - 1))
            def _():
                exchange.finish()

    lhs_spec = pl.BlockSpec((tt, fb), lambda j, i: (i, j))
    rhs_spec = pl.BlockSpec((tt, d), lambda j, i: (i, 0))
    out_spec = pl.BlockSpec((fb, d), lambda j, i: (j, 0))
    xsrcs = [a_ for a_, _ in xchg]
    return pl.pallas_call(
        body, name=name, grid=(nj, ni),
        in_specs=[lhs_spec, rhs_spec] * n + [HBM_SPEC] * m, out_specs=(out_spec,) * n + (HBM_SPEC,) * m,
        out_shape=(jax.ShapeDtypeStruct((f, d), BF16),) * n + _Exchange.out_shapes(xsrcs, xrows),
        scratch_shapes=[pltpu.VMEM((fb, d), F32)] * n + (_Exchange.semaphores(m) if m else []),
        compiler_params=_params(("arbitrary", "arbitrary")),
    )(*[a_ for lhs, rhs, _ in pairs for a_ in (lhs, rhs)], *xsrcs)


def _mixer_bwd(dr1, p, ya, yb, w_ot, w_pat, w_pbt, conv_w8, vng, vnb, ws_b, wst_b, bias_s, head_sel, xchg, dm, tm):
    t, d, wa, npj = dm.T, dm.D, dm.WA, dm.NP
    nt = t // tm
    h8, hb = F32_SUBLANES, BF16_SUBLANES
    ext = tm + 2 * h8
    prev_f, next_f = _halo_maps(tm, t, h8)
    prev_b, next_b = _halo_maps(tm, t, hb)
    nx = len(xchg)
    xsrcs, xrows = [a_ for a_, _ in xchg], [r for _, r in xchg]

    def body(dr_ref, drp_ref, drn_ref, p_ref, pp_ref, pn_ref, ya_ref, yb_ref, wot_ref, wpat_ref, wpbt_ref,
             cw_ref, vng_ref, vnb_ref, ws_ref, wst_ref, bias_ref, sel_ref,
             *refs):
        xsrc = refs[:nx]
        dp_ref, a_ref, dya_ref, s_ref, bb_ref, dyb_ref, dws_ref, dbs_ref, dcw_ref, dbg_ref, dvn_ref = refs[nx:nx + 11]
        xrecv = refs[nx + 11:2 * nx + 11]
        ext_ref, ext2_ref, mixed_ref, dvnm_ref = refs[2 * nx + 11:2 * nx + 15]
        exchange = _Exchange(xsrc, xrecv, xrows, *refs[2 * nx + 15:])
        i = pl.program_id(0)

        @pl.when(i == 0)
        def _():
            exchange.start()
            for ref in (dws_ref, dbs_ref, dcw_ref, dbg_ref, dvn_ref):
                ref[...] = jnp.zeros_like(ref)

        def col(ref, lo, width):
            return ref[:, lo:lo + width].astype(F32)

        def ext_rows(prev_blk, center, next_blk):
            return jnp.concatenate([prev_blk, center, next_blk], axis=0)

        def ext_col(lo, width):
            return ext_rows(col(pp_ref, lo, width)[hb - h8:hb], col(p_ref, lo, width), col(pn_ref, lo, width)[0:h8])

        row = lax.broadcasted_iota(jnp.int32, (ext, 1), 0) + (i * tm - h8)
        inside = jnp.logical_and(row >= 0, row < t)
        dr_e = ext_rows(drp_ref[...], dr_ref[...], drn_ref[...])
        ds_e = _mm(dr_e.astype(BF16), wot_ref[...])
        dya_e = ds_e * ext_col(dm.OFF_GA, d)
        da_e = _mm(dya_e.astype(BF16), wpat_ref[...])
        dcv_e = jnp.where(inside, da_e * ext_col(0, wa), 0.0)
        ch_e = jnp.where(inside, ext_col(dm.OFF_CA, wa) * ext_col(dm.OFF_HA, wa), 0.0)
        ext_ref[...] = ch_e
        ext2_ref[...] = dcv_e
        ch, ch_up, ch_dn = ch_e[h8:h8 + tm], ext_ref[pl.ds(h8 - 1, tm), :], ext_ref[pl.ds(h8 + 1, tm), :]
        dcv, dcv_up, dcv_dn = dcv_e[h8:h8 + tm], ext2_ref[pl.ds(h8 - 1, tm), :], ext2_ref[pl.ds(h8 + 1, tm), :]
        w0, w1, w2 = cw_ref[0:1, :], cw_ref[1:2, :], cw_ref[2:3, :]
        cv = w0 * ch_up + w1 * ch + w2 * ch_dn
        dp_ref[:, 0:wa] = (da_e[h8:h8 + tm] * cv).astype(BF16)
        dch = w0 * dcv_dn + w1 * dcv + w2 * dcv_up
        dp_ref[:, dm.OFF_CA:dm.OFF_CA + wa] = (dch * col(p_ref, dm.OFF_HA, wa)).astype(BF16)
        dp_ref[:, dm.OFF_HA:dm.OFF_HA + wa] = (dch * col(p_ref, dm.OFF_CA, wa)).astype(BF16)
        dcw_ref[0:1, :] += _colsum(dcv * ch_up)
        dcw_ref[1:2, :] += _colsum(dcv * ch)
        dcw_ref[2:3, :] += _colsum(dcv * ch_dn)
        a_ref[...] = (col(p_ref, 0, wa) * cv).astype(BF16)
        dya_ref[...] = dya_e[h8:h8 + tm].astype(BF16)
        ds = ds_e[h8:h8 + tm]
        g_a, g_b = col(p_ref, dm.OFF_GA, d), col(p_ref, dm.OFF_GB, d)
        y_a, y_b = ya_ref[...].astype(F32), yb_ref[...].astype(F32)
        s_ref[...] = (g_a * y_a + g_b * y_b).astype(BF16)
        dzga = ds * y_a * g_a * (1.0 - g_a)
        dzgb = ds * y_b * g_b * (1.0 - g_b)
        dp_ref[:, dm.OFF_GA:dm.OFF_GA + d] = dzga.astype(BF16)
        dp_ref[:, dm.OFF_GB:dm.OFF_GB + d] = dzgb.astype(BF16)
        dbg_ref[0:1, 0:d] += _colsum(dzga)
        dbg_ref[0:1, d:2 * d] += _colsum(dzgb)
        dyb = (ds * g_b).astype(BF16)
        dyb_ref[...] = dyb
        gv, dgelu_v = _gelu_and_grad(col(p_ref, dm.OFF_VB, d))
        xhv, rstdv = _ln_stats(gv)
        vn = xhv * vng_ref[...] + vnb_ref[...]
        mixed = _spatial_mix(vn, ws_ref, bias_ref, mixed_ref, dm, tm)
        gu, dgelu_u = _gelu_and_grad(col(p_ref, dm.OFF_UB, d))
        bb_ref[...] = (gu * mixed).astype(BF16)
        dbb = _mm(dyb, wpbt_ref[...])
        dp_ref[:, dm.OFF_UB:dm.OFF_UB + d] = (dbb * mixed * dgelu_u).astype(BF16)
        dmb = (dbb * gu).astype(BF16)
        vb = vn.astype(BF16)
        dbs = jnp.zeros((CHUNK, CHUNK), F32)
        for cc in range(tm // CHUNK):
            r0 = cc * CHUNK
            dbs = dbs + _mm(dmb[r0:r0 + CHUNK, :], sel_ref[...])
            for h in range(dm.H):
                c0 = h * CHUNK
                blk = dmb[r0:r0 + CHUNK, c0:c0 + CHUNK]
                dvnm_ref[r0:r0 + CHUNK, c0:c0 + CHUNK] = _mm(wst_ref[h], blk)
                dws_ref[h] += lax.dot_general(blk, vb[r0:r0 + CHUNK, c0:c0 + CHUNK], NT_DIMS, preferred_element_type=F32)
        dbs_ref[...] += dbs
        dvn = dvnm_ref[...]
        dvn_ref[0:1, :] += _colsum(dvn * xhv)
        dvn_ref[1:2, :] += _colsum(dvn)
        dp_ref[:, dm.OFF_VB:dm.OFF_VB + d] = (_ln_bwd(dvn * vng_ref[...], xhv, rstdv) * dgelu_v).astype(BF16)

        @pl.when(i == nt - 1)
        def _():
            exchange.finish()

    full = lambda i: (0, 0)
    tile = lambda i: (i, 0)
    hcc = _resident((dm.H, CHUNK, CHUNK))
    tok = lambda w, dt: jax.ShapeDtypeStruct((t, w), dt)
    return pl.pallas_call(
        body, name="mixer_bwd", grid=(nt,),
        in_specs=[pl.BlockSpec((tm, d), tile), pl.BlockSpec((h8, d), prev_f), pl.BlockSpec((h8, d), next_f),
                  pl.BlockSpec((tm, npj), tile), pl.BlockSpec((hb, npj), prev_b), pl.BlockSpec((hb, npj), next_b),
                  pl.BlockSpec((tm, d), tile), pl.BlockSpec((tm, d), tile),
                  _resident((d, d)), _resident((d, wa)), _resident((d, d)),
                  _resident((h8, wa)), _resident((1, d)), _resident((1, d)), hcc, hcc, _resident((CHUNK, d)),
                  _resident((d, CHUNK))] + [HBM_SPEC] * nx,
        out_specs=(pl.BlockSpec((tm, npj), tile), pl.BlockSpec((tm, wa), tile), pl.BlockSpec((tm, d), tile),
                   pl.BlockSpec((tm, d), tile), pl.BlockSpec((tm, d), tile), pl.BlockSpec((tm, d), tile),
                   pl.BlockSpec((dm.H, CHUNK, CHUNK), lambda i: (0, 0, 0)), pl.BlockSpec((CHUNK, CHUNK), full),
                   pl.BlockSpec((h8, wa), full), pl.BlockSpec((h8, 2 * d), full), pl.BlockSpec((h8, d), full))
        + (HBM_SPEC,) * nx,
        out_shape=(tok(npj, BF16), tok(wa, BF16), tok(d, BF16), tok(d, BF16), tok(d, BF16), tok(d, BF16),
                   jax.ShapeDtypeStruct((dm.H, CHUNK, CHUNK), F32), jax.ShapeDtypeStruct((CHUNK, CHUNK), F32),
                   jax.ShapeDtypeStruct((h8, wa), F32), jax.ShapeDtypeStruct((h8, 2 * d), F32),
                   jax.ShapeDtypeStruct((h8, d), F32)) + _Exchange.out_shapes(xsrcs, xrows),
        scratch_shapes=[pltpu.VMEM((ext, wa), F32), pltpu.VMEM((ext, wa), F32), pltpu.VMEM((tm, d), F32),
                        pltpu.VMEM((tm, d), F32)] + _Exchange.semaphores(nx),
        compiler_params=_params(),
    )(dr1, dr1, dr1, p, p, p, ya, yb, w_ot, w_pat, w_pbt, conv_w8, vng, vnb, ws_b, wst_b, bias_s, head_sel, *xsrcs)


def _input_grad(dp, dr1, w_int, xchg, dm, tm):
    t, d, npj = dm.T, dm.D, dm.NP
    nt = t // tm
    nx = len(xchg)
    xsrcs, xrows = [a_ for a_, _ in xchg], [r for _, r in xchg]

    def body(dp_ref, dr_ref, w_ref, *refs):
        dx_ref = refs[nx]
        exchange = _Exchange(refs[:nx], refs[nx + 1:2 * nx + 1], xrows, *refs[2 * nx + 1:])
        i = pl.program_id(0)

        @pl.when(i == 0)
        def _():
            exchange.start()

        dx_ref[...] = ALPHA * dr_ref[...] + _mm(dp_ref[...], w_ref[...])

        @pl.when(i == nt - 1)
        def _():
            exchange.finish()

    return pl.pallas_call(
        body, name="input_grad", grid=(nt,),
        in_specs=[pl.BlockSpec((tm, npj), lambda i: (i, 0)), pl.BlockSpec((tm, d), lambda i: (i, 0)), _resident((npj, d))]
        + [HBM_SPEC] * nx,
        out_specs=(pl.BlockSpec((tm, d), lambda i: (i, 0)),) + (HBM_SPEC,) * nx,
        out_shape=(jax.ShapeDtypeStruct((t, d), F32),) + _Exchange.out_shapes(xsrcs, xrows),
        scratch_shapes=_Exchange.semaphores(nx),
        compiler_params=_params(),
    )(dp, dr1, w_int, *xsrcs)


def _adamw_math(w, g, m, v):
    nm = ADAM_B1 * m + (1.0 - ADAM_B1) * g
    nv = ADAM_B2 * v + (1.0 - ADAM_B2) * (g * g)
    delta = -ADAM_LR * ((nm / (1.0 - ADAM_B1 ** ADAM_STEP)) / (jnp.sqrt(nv / (1.0 - ADAM_B2 ** ADAM_STEP)) + ADAM_EPS) + ADAM_WD * w)
    return delta, nm, nv


def _adamw(w, g, m, v, name):
    rows, cols = w.shape
    tr = 256 if rows % 256 == 0 else rows
    from_slots = g.ndim == 3

    def body(w_ref, g_ref, m_ref, v_ref, go_ref, d_ref, nm_ref, nv_ref):
        if from_slots:
            g_ = g_ref[0].astype(F32)
            for k in range(1, N_DEV):
                g_ = g_ + g_ref[k].astype(F32)
        else:
            g_ = g_ref[...]
        go_ref[...] = g_
        d_ref[...], nm_ref[...], nv_ref[...] = _adamw_math(w_ref[...], g_, m_ref[...], v_ref[...])

    spec = pl.BlockSpec((tr, cols), lambda i: (i, 0))
    g_spec = pl.BlockSpec((N_DEV, tr, cols), lambda i: (0, i, 0)) if from_slots else spec
    shp = jax.ShapeDtypeStruct((rows, cols), F32)
    return pl.pallas_call(
        body, name=name, grid=(rows // tr,), in_specs=[spec, g_spec, spec, spec], out_specs=(spec,) * 4, out_shape=(shp,) * 4,
        compiler_params=_params(),
    )(w, g, m, v)


def _adamw_small(ws, gs, ms, vs):
    n = len(ws)

    def body(*refs):
        ins, outs = refs[:4 * n], refs[4 * n:]
        for k in range(n):
            w_ref, g_ref, m_ref, v_ref = ins[k], ins[n + k], ins[2 * n + k], ins[3 * n + k]
            outs[k][...], outs[n + k][...], outs[2 * n + k][...] = _adamw_math(w_ref[...], g_ref[...], m_ref[...], v_ref[...])

    shapes = tuple(jax.ShapeDtypeStruct(w.shape, F32) for w in ws)
    out = pl.pallas_call(body, name="adamw_small", out_shape=shapes * 3, compiler_params=_params(()))(*ws, *gs, *ms, *vs)
    return out[:n], out[n:2 * n], out[2 * n:]


def _to_slab(parts):
    flat = jnp.concatenate([q.reshape(-1) for q in parts])
    pad = (-flat.shape[0]) % (F32_SUBLANES * LANES)
    return jnp.pad(flat, (0, pad)).reshape(-1, LANES)


def _from_slab(slab, shapes):
    flat = slab.reshape(-1)
    out, off = [], 0
    for s in shapes:
        n = math.prod(s)
        out.append(flat[off:off + n].reshape(s))
        off += n
    return out


def kernel(x, w_in, b_gate, conv_w, v_norm_g, v_norm_b, w_s, b_s, w_pa, w_pb, w_o, ln1_g, ln1_b, w_ff1, w_ff2, ln2_g, ln2_b, loss_target, m_w_in, m_b_gate, m_conv_w, m_v_norm_g, m_v_norm_b, m_w_s, m_b_s, m_w_pa, m_w_pb, m_w_o, m_ln1_g, m_ln1_b, m_w_ff1, m_w_ff2, m_ln2_g, m_ln2_b, v_w_in, v_b_gate, v_conv_w, v_v_norm_g, v_v_norm_b, v_w_s, v_b_s, v_w_pa, v_w_pb, v_w_o, v_ln1_g, v_ln1_b, v_w_ff1, v_w_ff2, v_ln2_g, v_ln2_b):
    t, d = x.shape[1], x.shape[2]
    dm = _Dims(t, d)
    tm = 256 if t % 256 == 0 else CHUNK
    tm_big = 512 if t % 512 == 0 else tm
    tt = 1024 if t % 1024 == 0 else tm
    me = 4 * lax.axis_index("x") + 2 * lax.axis_index("y") + lax.axis_index("c")
    x2, tgt = x[0], loss_target[0]

    conv_bits = lax.bitcast_convert_type(conv_w[0], BF16).reshape(-1)
    conv_blk = jnp.pad(conv_bits, (0, dm.conv_rows * d - conv_bits.shape[0])).reshape(dm.conv_rows, d)
    w_int, conv_g = _all_gather([w_in[0].T.astype(BF16), conv_blk])
    wa8 = dm.WA // N_DEV
    conv_all = lax.bitcast_convert_type(conv_g.reshape(N_DEV, -1)[:, :3 * wa8 * 2].reshape(N_DEV, 3, wa8, 2), F32)
    conv_full = jnp.transpose(conv_all, (1, 0, 2)).reshape(3, dm.WA)
    conv_w8 = jnp.pad(conv_full, ((0, F32_SUBLANES - 3), (0, 0)))
    ws_b = w_s[0].astype(BF16)
    wst_b = jnp.transpose(w_s[0], (0, 2, 1)).astype(BF16)
    bias_s = jnp.repeat(b_s[0].T, CHUNK, axis=1)
    head_sel = (jnp.arange(d)[:, None] // CHUNK == jnp.arange(CHUNK)[None, :]).astype(BF16)

    p, w_pa_f, w_pb_f, w_o_f, w_1t, w_2 = _proj_in(
        x2, w_int, b_gate, [w_pa[0].astype(BF16), w_pb[0].astype(BF16), w_o[0].astype(BF16), w_ff1[0].T.astype(BF16),
                            w_ff2[0].astype(BF16)], dm, tm_big)
    ya, yb, r1 = _mixer_fwd(p, x2, w_pa_f, w_pb_f, w_o_f, conv_w8, v_norm_g, v_norm_b, ws_b, bias_s, dm, tm)
    relu, x1b, dr2, dr2b, sums2 = _ffn_fwd(r1, tgt, w_1t, w_2, ln1_g, ln1_b, ln2_g, ln2_b, dm, tm_big)
    dr1, dh1, sums1 = _ffn_bwd(r1, relu, dr2, w_2, w_1t, ln1_g, dm, tm_big)
    fb = min(1024, dm.DFF)
    rows = dm.shard_rows
    g_ff1t, g_ff2 = _wgrad([(dh1, x1b, False), (relu, dr2b, True)], tt, fb, "ffn_wgrad")
    dp, a_m, dya, s_m, bb_m, dyb, g_ws, g_bs_t, g_cw, g_bg, g_vn, got_ff1t, got_ff2 = _mixer_bwd(
        dr1, p, ya, yb, w_o_f.T, w_pa_f.T, w_pb_f.T, conv_w8, v_norm_g, v_norm_b, ws_b, wst_b, bias_s, head_sel,
        [(g_ff1t, rows[4]), (g_ff2, rows[5])], dm, tm)
    (g_pa,) = _wgrad([(a_m, dya, False)], tt, dm.WA, "w_pa_grad")
    g_o, g_pb = _wgrad([(s_m, dr1, False), (bb_m, dyb, False)], tt, d, "w_o_pb_grad")
    g_int, got_pa, got_pb, got_o = _wgrad([(dp, x2, False)], tt, 17 * LANES, "w_in_grad",
                                          xchg=[(g_pa, rows[1]), (g_pb, rows[2]), (g_o, rows[3])])
    small_parts = [g_bg[0], g_cw[0:3], g_vn[0], g_vn[1], g_ws, g_bs_t[:, :dm.H].T,
                   sums1[0], sums1[1], sums2[1], sums2[2], sums2[0]]
    grad_x, got_int, got_s = _input_grad(dp, dr1, w_int, [(g_int, rows[0]), (_to_slab(small_parts), None)], dm, tm_big)

    ssum = _sum_slots(got_s, 256, "sum_small")
    (s_bg, s_cw, s_vng, s_vnb, s_ws, s_bs, s_l1g, s_l1b, s_l2g, s_l2b, s_sq) = _from_slab(
        ssum, [(1, 2 * d), (3, dm.WA), (1, d), (1, d), (1, dm.H, CHUNK, CHUNK), (1, dm.H, CHUNK),
               (1, d), (1, d), (1, d), (1, d), (d,)])
    loss = 0.5 * jnp.sum(s_sq) / d
    s_cw = lax.dynamic_slice(s_cw, (0, me * wa8), (3, wa8))[None]
    big_g = [_sum_slots(got_int, 256, "sum_grads_w_in").T, got_pa, got_pb, got_o,
             _sum_slots(got_ff1t, 256, "sum_grads_w_ff1").T, got_ff2]
    big_w = [(w_in, m_w_in, v_w_in), (w_pa, m_w_pa, v_w_pa), (w_pb, m_w_pb, v_w_pb), (w_o, m_w_o, v_w_o),
             (w_ff1, m_w_ff1, v_w_ff1), (w_ff2, m_w_ff2, v_w_ff2)]
    big_out = [_adamw(w[0], g, m[0], v[0], "adamw_%d" % k) for k, ((w, m, v), g) in enumerate(zip(big_w, big_g))]
    small_w = [(b_gate, m_b_gate, v_b_gate), (conv_w, m_conv_w, v_conv_w), (v_norm_g, m_v_norm_g, v_v_norm_g),
               (v_norm_b, m_v_norm_b, v_v_norm_b), (w_s, m_w_s, v_w_s), (b_s, m_b_s, v_b_s), (ln1_g, m_ln1_g, v_ln1_g),
               (ln1_b, m_ln1_b, v_ln1_b), (ln2_g, m_ln2_g, v_ln2_g), (ln2_b, m_ln2_b, v_ln2_b)]
    small_g = [s_bg, s_cw, s_vng, s_vnb, s_ws, s_bs, s_l1g, s_l1b, s_l2g, s_l2b]
    small_out = list(zip(*_adamw_small([w for w, _, _ in small_w], small_g, [m for _, m, _ in small_w],
                                       [v for _, _, v in small_w])))

    order = [("b", 0), ("s", 0), ("s", 1), ("s", 2), ("s", 3), ("s", 4), ("s", 5), ("b", 1), ("b", 2), ("b", 3),
             ("s", 6), ("s", 7), ("b", 4), ("b", 5), ("s", 8), ("s", 9)]
    grads, deltas, new_m, new_v = [], [], [], []
    for kind, k in order:
        if kind == "b":
            g, dl, nm, nv = (o[None] for o in big_out[k])
        else:
            g, (dl, nm, nv) = small_g[k], small_out[k]
        grads.append(g)
        deltas.append(dl)
        new_m.append(nm)
        new_v.append(nv)
    return (loss, grad_x[None], *grads, *deltas, *new_m, *new_v)
```

```python
import math

import jax
import jax.numpy as jnp
from jax import lax
from jax.experimental import pallas as pl
from jax.experimental.pallas import tpu as pltpu

F32 = jnp.float32
BF16 = jnp.bfloat16
N_DEV = 8
CHUNK = 128
LN_EPS = 1e-5
ALPHA = 2.0 ** 0.25
ADAM_LR, ADAM_B1, ADAM_B2, ADAM_EPS, ADAM_WD, ADAM_STEP = 0.001, 0.9, 0.999, 1e-08, 0.01, 10
F32_SUBLANES = 8
BF16_SUBLANES = 16
LANES = 128
VMEM_LIMIT = 56 * 1024 * 1024
MESH = pl.DeviceIdType.MESH
NT_DIMS = (((1,), (1,)), ((), ()))
TN_DIMS = (((0,), (0,)), ((), ()))
HBM_SPEC = pl.BlockSpec(memory_space=pltpu.HBM)


class _Dims:
    def __init__(self, t, d):
        self.T, self.D = t, d
        self.WA = 3 * d // 2
        self.NP = 3 * self.WA + 4 * d
        self.DFF = 4 * d
        self.H = d // CHUNK
        self.OFF_CA, self.OFF_HA = self.WA, 2 * self.WA
        self.OFF_UB = 3 * self.WA
        self.OFF_VB = self.OFF_UB + d
        self.OFF_GA = self.OFF_VB + d
        self.OFF_GB = self.OFF_GA + d
        self.shard_rows = (self.NP // N_DEV, self.WA // N_DEV, d // N_DEV, d // N_DEV, self.DFF // N_DEV, self.DFF // N_DEV)
        self.conv_rows = BF16_SUBLANES * max(1, -(-(3 * (self.WA // N_DEV) * 2) // (BF16_SUBLANES * d)))


def _params(sem=("arbitrary",), vmem=VMEM_LIMIT):
    return pltpu.CompilerParams(dimension_semantics=sem, vmem_limit_bytes=vmem)


def _mesh_pos():
    return lax.axis_index("x"), lax.axis_index("y"), lax.axis_index("c")


def _resident(shape):
    zeros = (0,) * len(shape)
    return pl.BlockSpec(shape, lambda *_: zeros, pipeline_mode=pl.Buffered(1))


class _TwoLevelGather:
    def __init__(self, shard_refs, out_refs, send_sems, recv_sems, local_sems):
        self.n = len(shard_refs)
        self.shard_refs, self.out_refs = shard_refs, out_refs
        self.send_sems, self.recv_sems, self.local_sems = send_sems, recv_sems, local_sems
        x, y, c = _mesh_pos()
        self.c = c
        self.me, self.sibling = (x, y, c), (x, y, 1 - c)
        self.chips = [(1 - x, y), (x, 1 - y), (1 - x, 1 - y)]

    def _slot(self, a, px, py, pc):
        rows = self.shard_refs[a].shape[0]
        return self.out_refs[a].at[pl.ds((4 * px + 2 * py + pc) * rows, rows), :]

    def _copy(self, a, k, block, to, src=None):
        return pltpu.make_async_remote_copy(
            src_ref=self._slot(a, *block) if src is None else src, dst_ref=self._slot(a, *block),
            send_sem=self.send_sems.at[7 * a + k], recv_sem=self.recv_sems.at[7 * a + k], device_id=to, device_id_type=MESH)

    def _mine(self):
        return [pltpu.make_async_copy(self.shard_refs[a], self._slot(a, *self.me), self.local_sems.at[a]) for a in range(self.n)]

    def _first(self):
        out = []
        for a in range(self.n):
            out.append(self._copy(a, 0, self.me, self.sibling, src=self.shard_refs[a]))
            out += [self._copy(a, 1 + j, self.me, (*chip, self.c), src=self.shard_refs[a]) for j, chip in enumerate(self.chips)]
        return out

    def _passed(self):
        return [self._copy(a, 4 + j, (*chip, self.c), self.sibling) for j, chip in enumerate(self.chips) for a in range(self.n)]

    def start(self):
        for cp in self._mine() + self._first():
            cp.start()

    def forward(self):
        passed = self._passed()
        for j, chip in enumerate(self.chips):
            for a in range(self.n):
                self._copy(a, 1 + j, (*chip, self.c), self.me).wait_recv()
                passed[j * self.n + a].start()

    def finish(self):
        for a in range(self.n):
            self._copy(a, 0, self.sibling, self.me).wait_recv()
            for j, chip in enumerate(self.chips):
                self._copy(a, 4 + j, (*chip, 1 - self.c), self.me).wait_recv()
        for cp in self._first() + self._passed():
            cp.wait_send()
        for cp in self._mine():
            cp.wait()

    @staticmethod
    def out_shapes(shards):
        return tuple(jax.ShapeDtypeStruct((N_DEV * s.shape[0], s.shape[1]), s.dtype) for s in shards)

    @staticmethod
    def semaphores(n):
        return [pltpu.SemaphoreType.DMA((7 * n,)), pltpu.SemaphoreType.DMA((7 * n,)), pltpu.SemaphoreType.DMA((n,))]


def _all_gather(shards):
    n = len(shards)

    def body(*refs):
        gather = _TwoLevelGather(refs[:n], refs[n:2 * n], *refs[2 * n:])
        gather.start()
        gather.forward()
        gather.finish()

    return pl.pallas_call(
        body, name="all_gather_w_in", out_shape=_TwoLevelGather.out_shapes(shards),
        in_specs=[HBM_SPEC] * n, out_specs=(HBM_SPEC,) * n, scratch_shapes=_TwoLevelGather.semaphores(n),
    )(*shards)


class _Exchange:
    def __init__(self, src_refs, recv_refs, rows, send_sems, recv_sems, local_sems):
        x, y, c = _mesh_pos()
        me = 4 * x + 2 * y + c
        self.own, self.sends, self.arrivals = [], [], []
        for a, (src, recv) in enumerate(zip(src_refs, recv_refs)):
            def blk(k, src=src, r=rows[a]):
                return src if r is None else src.at[pl.ds(k * r, r), :]

            self.own.append(pltpu.make_async_copy(blk(me), recv.at[me], local_sems.at[a]))
            for rel in range(1, N_DEV):
                px = 1 - x if rel & 4 else x
                py = 1 - y if rel & 2 else y
                pc = 1 - c if rel & 1 else c
                peer = 4 * px + 2 * py + pc
                sem = dict(send_sem=send_sems.at[7 * a + rel - 1], recv_sem=recv_sems.at[7 * a + rel - 1],
                           device_id=(px, py, pc), device_id_type=MESH)
                self.sends.append(pltpu.make_async_remote_copy(src_ref=blk(peer), dst_ref=recv.at[me], **sem))
                self.arrivals.append(pltpu.make_async_remote_copy(src_ref=blk(me), dst_ref=recv.at[peer], **sem))

    def start(self):
        for cp in self.own + self.sends:
            cp.start()

    def finish(self):
        for cp in self.arrivals:
            cp.wait_recv()
        for cp in self.sends:
            cp.wait_send()
        for cp in self.own:
            cp.wait()

    @staticmethod
    def out_shapes(srcs, rows):
        return tuple(jax.ShapeDtypeStruct((N_DEV, s.shape[0] if r is None else r, s.shape[1]), s.dtype) for s, r in zip(srcs, rows))

    @staticmethod
    def semaphores(n):
        return [pltpu.SemaphoreType.DMA((7 * n,)), pltpu.SemaphoreType.DMA((7 * n,)), pltpu.SemaphoreType.DMA((n,))]


def _sum_slots(slots, tile_rows, name):
    _, rows, cols = slots.shape
    tr = rows
    if N_DEV * rows * cols * slots.dtype.itemsize > 8 * 1024 * 1024:
        tr = next(c for c in (256, 192, 128, 64, 32, 16) if c <= tile_rows and rows % c == 0)

    def body(s_ref, o_ref):
        acc = s_ref[0].astype(F32)
        for k in range(1, N_DEV):
            acc = acc + s_ref[k].astype(F32)
        o_ref[...] = acc

    return pl.pallas_call(
        body, name=name, grid=(rows // tr,),
        in_specs=[pl.BlockSpec((N_DEV, tr, cols), lambda i: (0, i, 0))],
        out_specs=pl.BlockSpec((tr, cols), lambda i: (i, 0)),
        out_shape=jax.ShapeDtypeStruct((rows, cols), F32),
        compiler_params=_params(),
    )(slots)


def _gelu_and_grad(x):
    k0 = math.sqrt(2.0 / math.pi)
    k1 = 0.044715
    a = k1 * (x * x)
    half = 1.0 / (1.0 + jnp.exp((-2.0 * k0) * x * (1.0 + a)))
    g = x * half
    return g, half + g * (1.0 - half) * (2.0 * k0 + (6.0 * k0) * a)


def _ln_stats(r):
    mu = jnp.mean(r, axis=-1, keepdims=True)
    rc = r - mu
    var = jnp.mean(rc * rc, axis=-1, keepdims=True)
    rstd = lax.rsqrt(var + LN_EPS)
    return rc * rstd, rstd


def _ln_bwd(dxh, xh, rstd):
    return rstd * (dxh - jnp.mean(dxh, axis=-1, keepdims=True) - xh * jnp.mean(dxh * xh, axis=-1, keepdims=True))


def _colsum(a):
    return jnp.sum(a, axis=0, keepdims=True)


def _mm(a, b):
    return jnp.dot(a, b, preferred_element_type=F32)


def _halo_maps(tm, t, unit):
    per, last = tm // unit, t // unit - 1
    return (lambda i: (jnp.maximum(i * per - 1, 0), 0)), (lambda i: (jnp.minimum((i + 1) * per, last), 0))


def _proj_in(x2, w_int, b_gate, shards, dm, tm):
    t, d, npj = dm.T, dm.D, dm.NP
    cw = d // 2
    nt = t // tm
    n = len(shards)

    def body(x_ref, w_ref, bg_ref, *refs):
        p_ref = refs[n]
        gather = _TwoLevelGather(refs[:n], refs[n + 1:2 * n + 1], *refs[2 * n + 1:])
        i = pl.program_id(0)

        @pl.when(i == 0)
        def _():
            gather.start()

        @pl.when(i == nt // 2)
        def _():
            gather.forward()

        xb = x_ref[...].astype(BF16)
        for blk in range(npj // cw):
            lo = blk * cw
            acc = lax.dot_general(xb, w_ref[lo:lo + cw, :], NT_DIMS, preferred_element_type=F32)
            if lo >= dm.OFF_GA:
                acc = jax.nn.sigmoid(acc + bg_ref[:, lo - dm.OFF_GA:lo - dm.OFF_GA + cw])
            p_ref[:, lo:lo + cw] = acc.astype(BF16)

        @pl.when(i == nt - 1)
        def _():
            gather.finish()

    return pl.pallas_call(
        body, name="proj_in", grid=(nt,),
        in_specs=[pl.BlockSpec((tm, d), lambda i: (i, 0)), _resident((npj, d)), _resident((1, 2 * d))] + [HBM_SPEC] * n,
        out_specs=(pl.BlockSpec((tm, npj), lambda i: (i, 0)),) + (HBM_SPEC,) * n,
        out_shape=(jax.ShapeDtypeStruct((t, npj), BF16),) + _TwoLevelGather.out_shapes(shards),
        scratch_shapes=_TwoLevelGather.semaphores(n),
        compiler_params=_params(),
    )(x2, w_int, b_gate, *shards)


def _conv_taps(ext_ref, center, prev_blk, next_blk, first, last, tm):
    h = F32_SUBLANES
    ext_ref[0:h, :] = jnp.where(first, 0.0, prev_blk)
    ext_ref[h:h + tm, :] = center
    ext_ref[h + tm:h + tm + h, :] = jnp.where(last, 0.0, next_blk)
    return ext_ref[pl.ds(h - 1, tm), :], ext_ref[pl.ds(h + 1, tm), :]


def _spatial_mix(vn, ws_ref, bias_ref, mixed_ref, dm, tm):
    vb = vn.astype(BF16)
    for cc in range(tm // CHUNK):
        r0 = cc * CHUNK
        for h in range(dm.H):
            c0 = h * CHUNK
            m = _mm(ws_ref[h], vb[r0:r0 + CHUNK, c0:c0 + CHUNK])
            mixed_ref[r0:r0 + CHUNK, c0:c0 + CHUNK] = m + bias_ref[:, c0:c0 + CHUNK]
    return mixed_ref[...]


def _mixer_fwd(p, x2, w_pa, w_pb, w_o, conv_w8, vng, vnb, ws_b, bias_s, dm, tm):
    t, d, wa, npj = dm.T, dm.D, dm.WA, dm.NP
    nt = t // tm
    hb = BF16_SUBLANES
    prev_map, next_map = _halo_maps(tm, t, hb)

    def body(p_ref, pp_ref, pn_ref, x_ref, wpa_ref, wpb_ref, wo_ref, cw_ref, vng_ref, vnb_ref, ws_ref, bias_ref,
             ya_ref, yb_ref, r1_ref, gu_ref, dgu_ref, xhv_ref, rgv_ref, ext_ref, mixed_ref):
        i = pl.program_id(0)

        def col(ref, lo, width):
            return ref[:, lo:lo + width].astype(F32)

        ch = col(p_ref, dm.OFF_CA, wa) * col(p_ref, dm.OFF_HA, wa)
        chp = (col(pp_ref, dm.OFF_CA, wa) * col(pp_ref, dm.OFF_HA, wa))[hb - F32_SUBLANES:hb]
        chn = (col(pn_ref, dm.OFF_CA, wa) * col(pn_ref, dm.OFF_HA, wa))[0:F32_SUBLANES]
        up, dn = _conv_taps(ext_ref, ch, chp, chn, i == 0, i == nt - 1, tm)
        a = col(p_ref, 0, wa) * (cw_ref[0:1, :] * up + cw_ref[1:2, :] * ch + cw_ref[2:3, :] * dn)
        ya = _mm(a.astype(BF16), wpa_ref[...])
        gv, dgelu_v = _gelu_and_grad(col(p_ref, dm.OFF_VB, d))
        xhv, rstdv = _ln_stats(gv)
        xhv_ref[...] = xhv.astype(BF16)
        rgv_ref[...] = (rstdv * dgelu_v).astype(BF16)
        mixed = _spatial_mix(xhv * vng_ref[...] + vnb_ref[...], ws_ref, bias_ref, mixed_ref, dm, tm)
        gu, dgelu_u = _gelu_and_grad(col(p_ref, dm.OFF_UB, d))
        gu_ref[...] = gu.astype(BF16)
        dgu_ref[...] = dgelu_u.astype(BF16)
        yb = _mm((gu * mixed).astype(BF16), wpb_ref[...])
        s = col(p_ref, dm.OFF_GA, d) * ya + col(p_ref, dm.OFF_GB, d) * yb
        mix = _mm(s.astype(BF16), wo_ref[...])
        ya_ref[...] = ya.astype(BF16)
        yb_ref[...] = yb.astype(BF16)
        r1_ref[...] = ALPHA * x_ref[...] + mix

    tile = lambda i: (i, 0)
    return pl.pallas_call(
        body, name="mixer_fwd", grid=(nt,),
        in_specs=[pl.BlockSpec((tm, npj), tile), pl.BlockSpec((hb, npj), prev_map), pl.BlockSpec((hb, npj), next_map),
                  pl.BlockSpec((tm, d), tile), _resident((wa, d)), _resident((d, d)), _resident((d, d)),
                  _resident((F32_SUBLANES, wa)), _resident((1, d)), _resident((1, d)),
                  _resident((dm.H, CHUNK, CHUNK)), _resident((CHUNK, d))],
        out_specs=(pl.BlockSpec((tm, d), tile),) * 7,
        out_shape=(jax.ShapeDtypeStruct((t, d), BF16), jax.ShapeDtypeStruct((t, d), BF16), jax.ShapeDtypeStruct((t, d), F32))
        + (jax.ShapeDtypeStruct((t, d), BF16),) * 4,
        scratch_shapes=[pltpu.VMEM((tm + 2 * F32_SUBLANES, wa), F32), pltpu.VMEM((tm, d), F32)],
        compiler_params=_params(),
    )(p, p, p, x2, w_pa, w_pb, w_o, conv_w8, vng, vnb, ws_b, bias_s)


def _ffn_fwd(r1, tgt, w1t, w2, ln1g, ln1b, ln2g, ln2b, dm, tm):
    t, d, dff = dm.T, dm.D, dm.DFF
    fc = dff // N_DEV

    def body(r1_ref, tgt_ref, w1t_ref, w2_ref, g1_ref, b1_ref, g2_ref, b2_ref, relu_ref, x1_ref, dr2_ref, dr2b_ref, sums_ref):
        @pl.when(pl.program_id(0) == 0)
        def _():
            sums_ref[...] = jnp.zeros_like(sums_ref)

        xh1, _ = _ln_stats(r1_ref[...])
        x1 = xh1 * g1_ref[...] + b1_ref[...]
        x1b = x1.astype(BF16)
        x1_ref[...] = x1b
        ffn = jnp.zeros((tm, d), F32)
        for k in range(dff // fc):
            ks = slice(k * fc, (k + 1) * fc)
            r = jnp.maximum(lax.dot_general(x1b, w1t_ref[ks, :], NT_DIMS, preferred_element_type=F32), 0.0)
            relu_ref[:, ks] = r.astype(BF16)
            ffn = ffn + _mm((r * r).astype(BF16), w2_ref[ks, :])
        xh2, rstd2 = _ln_stats(ALPHA * x1 + ffn)
        diff = xh2 * g2_ref[...] + b2_ref[...] - tgt_ref[...]
        dy = diff * (1.0 / d)
        dr2 = _ln_bwd(dy * g2_ref[...], xh2, rstd2)
        dr2_ref[...] = dr2
        dr2b_ref[...] = dr2.astype(BF16)
        sums_ref[0:1, :] += _colsum(diff * diff)
        sums_ref[1:2, :] += _colsum(dy * xh2)
        sums_ref[2:3, :] += _colsum(dy)

    tile = lambda i: (i, 0)
    vec = _resident((1, d))
    return pl.pallas_call(
        body, name="ffn_fwd", grid=(t // tm,),
        in_specs=[pl.BlockSpec((tm, d), tile), pl.BlockSpec((tm, d), tile), _resident((dff, d)), _resident((dff, d)),
                  vec, vec, vec, vec],
        out_specs=(pl.BlockSpec((tm, dff), tile), pl.BlockSpec((tm, d), tile), pl.BlockSpec((tm, d), tile),
                   pl.BlockSpec((tm, d), tile), pl.BlockSpec((F32_SUBLANES, d), lambda i: (0, 0))),
        out_shape=(jax.ShapeDtypeStruct((t, dff), BF16), jax.ShapeDtypeStruct((t, d), BF16), jax.ShapeDtypeStruct((t, d), F32),
                   jax.ShapeDtypeStruct((t, d), BF16), jax.ShapeDtypeStruct((F32_SUBLANES, d), F32)),
        compiler_params=_params(),
    )(r1, tgt, w1t, w2, ln1g, ln1b, ln2g, ln2b)


def _ffn_bwd(r1, relu, dr2, w2, w1t, ln1g, dm, tm):
    t, d, dff = dm.T, dm.D, dm.DFF
    fc = dff // N_DEV

    def body(r1_ref, relu_ref, dr2_ref, w2_ref, w1t_ref, g1_ref, dr1_ref, dh_ref, sums_ref):
        @pl.when(pl.program_id(0) == 0)
        def _():
            sums_ref[...] = jnp.zeros_like(sums_ref)

        xh1, rstd1 = _ln_stats(r1_ref[...])
        dr2 = dr2_ref[...]
        dr2b = dr2.astype(BF16)
        dx1 = ALPHA * dr2
        for k in range(dff // fc):
            ks = slice(k * fc, (k + 1) * fc)
            dact = lax.dot_general(dr2b, w2_ref[ks, :], NT_DIMS, preferred_element_type=F32)
            dhb = (dact * (2.0 * relu_ref[:, ks].astype(F32))).astype(BF16)
            dh_ref[:, ks] = dhb
            dx1 = dx1 + _mm(dhb, w1t_ref[ks, :])
        dr1_ref[...] = _ln_bwd(dx1 * g1_ref[...], xh1, rstd1)
        sums_ref[0:1, :] += _colsum(dx1 * xh1)
        sums_ref[1:2, :] += _colsum(dx1)

    tile = lambda i: (i, 0)
    return pl.pallas_call(
        body, name="ffn_bwd", grid=(t // tm,),
        in_specs=[pl.BlockSpec((tm, d), tile), pl.BlockSpec((tm, dff), tile), pl.BlockSpec((tm, d), tile),
                  _resident((dff, d)), _resident((dff, d)), _resident((1, d))],
        out_specs=(pl.BlockSpec((tm, d), tile), pl.BlockSpec((tm, dff), tile), pl.BlockSpec((F32_SUBLANES, d), lambda i: (0, 0))),
        out_shape=(jax.ShapeDtypeStruct((t, d), F32), jax.ShapeDtypeStruct((t, dff), BF16),
                   jax.ShapeDtypeStruct((F32_SUBLANES, d), F32)),
        compiler_params=_params(),
    )(r1, relu, dr2, w2, w1t, ln1g)


def _wgrad(pairs, tt, fb, name, xchg=()):
    n, m = len(pairs), len(xchg)
    t, f = pairs[0][0].shape
    d = pairs[0][1].shape[1]
    squares = [sq for _, _, sq in pairs]
    nj, ni = f // fb, t // tt
    xrows = [r for _, r in xchg]

    def body(*refs):
        ins, xsrc = refs[:2 * n], refs[2 * n:2 * n + m]
        outs, xrecv = refs[2 * n + m:3 * n + m], refs[3 * n + m:3 * n + 2 * m]
        accs, sems = refs[3 * n + 2 * m:4 * n + 2 * m], refs[4 * n + 2 * m:]
        j, i = pl.program_id(0), pl.program_id(1)
        exchange = _Exchange(xsrc, xrecv, xrows, *sems) if m else None

        if m:
            @pl.when(jnp.logical_and(j == 0, i == 0))
            def _():
                exchange.start()

        @pl.when(i == 0)
        def _():
            for acc in accs:
                acc[...] = jnp.zeros_like(acc)

        for q in range(n):
            lhs = ins[2 * q][...]
            if squares[q]:
                lf = lhs.astype(F32)
                lhs = (lf * lf).astype(BF16)
            accs[q][...] += lax.dot_general(lhs, ins[2 * q + 1][...].astype(BF16), TN_DIMS, preferred_element_type=F32)

        @pl.when(i == ni - 1)
        def _():
            for q in range(n):
                outs[q][...] = accs[q][...].astype(BF16)

        if m:
            @pl.when(jnp.logical_and(j == nj - 1, i == ni - 1))
            def _():
                exchange.finish()

    lhs_spec = pl.BlockSpec((tt, fb), lambda j, i: (i, j))
    rhs_spec = pl.BlockSpec((tt, d), lambda j, i: (i, 0))
    out_spec = pl.BlockSpec((fb, d), lambda j, i: (j, 0))
    xsrcs = [a_ for a_, _ in xchg]
    return pl.pallas_call(
        body, name=name, grid=(nj, ni),
        in_specs=[lhs_spec, rhs_spec] * n + [HBM_SPEC] * m, out_specs=(out_spec,) * n + (HBM_SPEC,) * m,
        out_shape=(jax.ShapeDtypeStruct((f, d), BF16),) * n + _Exchange.out_shapes(xsrcs, xrows),
        scratch_shapes=[pltpu.VMEM((fb, d), F32)] * n + (_Exchange.semaphores(m) if m else []),
        compiler_params=_params(("arbitrary", "arbitrary")),
    )(*[a_ for lhs, rhs, _ in pairs for a_ in (lhs, rhs)], *xsrcs)


def _mixer_bwd(dr1, p, ya, yb, gu_s, dgu_s, xhv_s, rgv_s, w_ot, w_pat, w_pbt, conv_w8, vng, vnb, ws_b, wst_b, bias_s, head_sel, xchg, dm, tm):
    t, d, wa, npj = dm.T, dm.D, dm.WA, dm.NP
    nt = t // tm
    h8, hb = F32_SUBLANES, BF16_SUBLANES
    ext = tm + 2 * h8
    prev_f, next_f = _halo_maps(tm, t, h8)
    prev_b, next_b = _halo_maps(tm, t, hb)
    nx = len(xchg)
    xsrcs, xrows = [a_ for a_, _ in xchg], [r for _, r in xchg]

    def body(dr_ref, drp_ref, drn_ref, p_ref, pp_ref, pn_ref, ya_ref, yb_ref, gu_ref, dgu_ref, xhv_ref, rgv_ref,
             wot_ref, wpat_ref, wpbt_ref,
             cw_ref, vng_ref, vnb_ref, ws_ref, wst_ref, bias_ref, sel_ref,
             *refs):
        xsrc = refs[:nx]
        dp_ref, a_ref, dya_ref, s_ref, bb_ref, dyb_ref, dws_ref, dbs_ref, dcw_ref, dbg_ref, dvn_ref = refs[nx:nx + 11]
        xrecv = refs[nx + 11:2 * nx + 11]
        ext_ref, ext2_ref, mixed_ref, dvnm_ref = refs[2 * nx + 11:2 * nx + 15]
        exchange = _Exchange(xsrc, xrecv, xrows, *refs[2 * nx + 15:])
        i = pl.program_id(0)

        @pl.when(i == 0)
        def _():
            exchange.start()
            for ref in (dws_ref, dbs_ref, dcw_ref, dbg_ref, dvn_ref):
                ref[...] = jnp.zeros_like(ref)

        def col(ref, lo, width):
            return ref[:, lo:lo + width].astype(F32)

        def ext_rows(prev_blk, center, next_blk):
            return jnp.concatenate([prev_blk, center, next_blk], axis=0)

        def ext_col(lo, width):
            return ext_rows(col(pp_ref, lo, width)[hb - h8:hb], col(p_ref, lo, width), col(pn_ref, lo, width)[0:h8])

        row = lax.broadcasted_iota(jnp.int32, (ext, 1), 0) + (i * tm - h8)
        inside = jnp.logical_and(row >= 0, row < t)
        dr_e = ext_rows(drp_ref[...], dr_ref[...], drn_ref[...])
        ds_e = _mm(dr_e.astype(BF16), wot_ref[...])
        dya_e = ds_e * ext_col(dm.OFF_GA, d)
        da_e = _mm(dya_e.astype(BF16), wpat_ref[...])
        dcv_e = jnp.where(inside, da_e * ext_col(0, wa), 0.0)
        ch_e = jnp.where(inside, ext_col(dm.OFF_CA, wa) * ext_col(dm.OFF_HA, wa), 0.0)
        ext_ref[...] = ch_e
        ext2_ref[...] = dcv_e
        ch, ch_up, ch_dn = ch_e[h8:h8 + tm], ext_ref[pl.ds(h8 - 1, tm), :], ext_ref[pl.ds(h8 + 1, tm), :]
        dcv, dcv_up, dcv_dn = dcv_e[h8:h8 + tm], ext2_ref[pl.ds(h8 - 1, tm), :], ext2_ref[pl.ds(h8 + 1, tm), :]
        w0, w1, w2 = cw_ref[0:1, :], cw_ref[1:2, :], cw_ref[2:3, :]
        cv = w0 * ch_up + w1 * ch + w2 * ch_dn
        dp_ref[:, 0:wa] = (da_e[h8:h8 + tm] * cv).astype(BF16)
        dch = w0 * dcv_dn + w1 * dcv + w2 * dcv_up
        dp_ref[:, dm.OFF_CA:dm.OFF_CA + wa] = (dch * col(p_ref, dm.OFF_HA, wa)).astype(BF16)
        dp_ref[:, dm.OFF_HA:dm.OFF_HA + wa] = (dch * col(p_ref, dm.OFF_CA, wa)).astype(BF16)
        dcw_ref[0:1, :] += _colsum(dcv * ch_up)
        dcw_ref[1:2, :] += _colsum(dcv * ch)
        dcw_ref[2:3, :] += _colsum(dcv * ch_dn)
        a_ref[...] = (col(p_ref, 0, wa) * cv).astype(BF16)
        dya_ref[...] = dya_e[h8:h8 + tm].astype(BF16)
        ds = ds_e[h8:h8 + tm]
        g_a, g_b = col(p_ref, dm.OFF_GA, d), col(p_ref, dm.OFF_GB, d)
        y_a, y_b = ya_ref[...].astype(F32), yb_ref[...].astype(F32)
        s_ref[...] = (g_a * y_a + g_b * y_b).astype(BF16)
        dzga = ds * y_a * g_a * (1.0 - g_a)
        dzgb = ds * y_b * g_b * (1.0 - g_b)
        dp_ref[:, dm.OFF_GA:dm.OFF_GA + d] = dzga.astype(BF16)
        dp_ref[:, dm.OFF_GB:dm.OFF_GB + d] = dzgb.astype(BF16)
        dbg_ref[0:1, 0:d] += _colsum(dzga)
        dbg_ref[0:1, d:2 * d] += _colsum(dzgb)
        dyb = (ds * g_b).astype(BF16)
        dyb_ref[...] = dyb
        xhv = xhv_ref[...].astype(F32)
        vn = xhv * vng_ref[...] + vnb_ref[...]
        mixed = _spatial_mix(vn, ws_ref, bias_ref, mixed_ref, dm, tm)
        gu = gu_ref[...].astype(F32)
        bb_ref[...] = (gu * mixed).astype(BF16)
        dbb = _mm(dyb, wpbt_ref[...])
        dp_ref[:, dm.OFF_UB:dm.OFF_UB + d] = (dbb * mixed * dgu_ref[...].astype(F32)).astype(BF16)
        dmb = (dbb * gu).astype(BF16)
        vb = vn.astype(BF16)
        dbs = jnp.zeros((CHUNK, CHUNK), F32)
        for cc in range(tm // CHUNK):
            r0 = cc * CHUNK
            dbs = dbs + _mm(dmb[r0:r0 + CHUNK, :], sel_ref[...])
            for h in range(dm.H):
                c0 = h * CHUNK
                blk = dmb[r0:r0 + CHUNK, c0:c0 + CHUNK]
                dvnm_ref[r0:r0 + CHUNK, c0:c0 + CHUNK] = _mm(wst_ref[h], blk)
                dws_ref[h] += lax.dot_general(blk, vb[r0:r0 + CHUNK, c0:c0 + CHUNK], NT_DIMS, preferred_element_type=F32)
        dbs_ref[...] += dbs
        dvn = dvnm_ref[...]
        dvn_ref[0:1, :] += _colsum(dvn * xhv)
        dvn_ref[1:2, :] += _colsum(dvn)
        dp_ref[:, dm.OFF_VB:dm.OFF_VB + d] = (_ln_bwd(dvn * vng_ref[...], xhv, rgv_ref[...].astype(F32))).astype(BF16)

        @pl.when(i == nt - 1)
        def _():
            exchange.finish()

    full = lambda i: (0, 0)
    tile = lambda i: (i, 0)
    hcc = _resident((dm.H, CHUNK, CHUNK))
    tok = lambda w, dt: jax.ShapeDtypeStruct((t, w), dt)
    return pl.pallas_call(
        body, name="mixer_bwd", grid=(nt,),
        in_specs=[pl.BlockSpec((tm, d), tile), pl.BlockSpec((h8, d), prev_f), pl.BlockSpec((h8, d), next_f),
                  pl.BlockSpec((tm, npj), tile), pl.BlockSpec((hb, npj), prev_b), pl.BlockSpec((hb, npj), next_b),
                  pl.BlockSpec((tm, d), tile), pl.BlockSpec((tm, d), tile), pl.BlockSpec((tm, d), tile),
                  pl.BlockSpec((tm, d), tile), pl.BlockSpec((tm, d), tile), pl.BlockSpec((tm, d), tile),
                  _resident((d, d)), _resident((d, wa)), _resident((d, d)),
                  _resident((h8, wa)), _resident((1, d)), _resident((1, d)), hcc, hcc, _resident((CHUNK, d)),
                  _resident((d, CHUNK))] + [HBM_SPEC] * nx,
        out_specs=(pl.BlockSpec((tm, npj), tile), pl.BlockSpec((tm, wa), tile), pl.BlockSpec((tm, d), tile),
                   pl.BlockSpec((tm, d), tile), pl.BlockSpec((tm, d), tile), pl.BlockSpec((tm, d), tile),
                   pl.BlockSpec((dm.H, CHUNK, CHUNK), lambda i: (0, 0, 0)), pl.BlockSpec((CHUNK, CHUNK), full),
                   pl.BlockSpec((h8, wa), full), pl.BlockSpec((h8, 2 * d), full), pl.BlockSpec((h8, d), full))
        + (HBM_SPEC,) * nx,
        out_shape=(tok(npj, BF16), tok(wa, BF16), tok(d, BF16), tok(d, BF16), tok(d, BF16), tok(d, BF16),
                   jax.ShapeDtypeStruct((dm.H, CHUNK, CHUNK), F32), jax.ShapeDtypeStruct((CHUNK, CHUNK), F32),
                   jax.ShapeDtypeStruct((h8, wa), F32), jax.ShapeDtypeStruct((h8, 2 * d), F32),
                   jax.ShapeDtypeStruct((h8, d), F32)) + _Exchange.out_shapes(xsrcs, xrows),
        scratch_shapes=[pltpu.VMEM((ext, wa), F32), pltpu.VMEM((ext, wa), F32), pltpu.VMEM((tm, d), F32),
                        pltpu.VMEM((tm, d), F32)] + _Exchange.semaphores(nx),
        compiler_params=_params(),
    )(dr1, dr1, dr1, p, p, p, ya, yb, gu_s, dgu_s, xhv_s, rgv_s, w_ot, w_pat, w_pbt, conv_w8, vng, vnb, ws_b, wst_b, bias_s, head_sel, *xsrcs)


def _input_grad(dp, dr1, w_int, xchg, dm, tm):
    t, d, npj = dm.T, dm.D, dm.NP
    nt = t // tm
    nx = len(xchg)
    xsrcs, xrows = [a_ for a_, _ in xchg], [r for _, r in xchg]

    def body(dp_ref, dr_ref, w_ref, *refs):
        dx_ref = refs[nx]
        exchange = _Exchange(refs[:nx], refs[nx + 1:2 * nx + 1], xrows, *refs[2 * nx + 1:])
        i = pl.program_id(0)

        @pl.when(i == 0)
        def _():
            exchange.start()

        dx_ref[...] = ALPHA * dr_ref[...] + _mm(dp_ref[...], w_ref[...])

        @pl.when(i == nt - 1)
        def _():
            exchange.finish()

    return pl.pallas_call(
        body, name="input_grad", grid=(nt,),
        in_specs=[pl.BlockSpec((tm, npj), lambda i: (i, 0)), pl.BlockSpec((tm, d), lambda i: (i, 0)), _resident((npj, d))]
        + [HBM_SPEC] * nx,
        out_specs=(pl.BlockSpec((tm, d), lambda i: (i, 0)),) + (HBM_SPEC,) * nx,
        out_shape=(jax.ShapeDtypeStruct((t, d), F32),) + _Exchange.out_shapes(xsrcs, xrows),
        scratch_shapes=_Exchange.semaphores(nx),
        compiler_params=_params(),
    )(dp, dr1, w_int, *xsrcs)


def _adamw_math(w, g, m, v):
    nm = ADAM_B1 * m + (1.0 - ADAM_B1) * g
    nv = ADAM_B2 * v + (1.0 - ADAM_B2) * (g * g)
    delta = -ADAM_LR * ((nm / (1.0 - ADAM_B1 ** ADAM_STEP)) / (jnp.sqrt(nv / (1.0 - ADAM_B2 ** ADAM_STEP)) + ADAM_EPS) + ADAM_WD * w)
    return delta, nm, nv


def _adamw(w, g, m, v, name):
    _, rows, cols = w.shape
    tr = 256 if rows % 256 == 0 else rows
    from_slots = g.ndim == 3

    def body(w_ref, g_ref, m_ref, v_ref, go_ref, d_ref, nm_ref, nv_ref):
        if from_slots:
            g_ = g_ref[0].astype(F32)
            for k in range(1, N_DEV):
                g_ = g_ + g_ref[k].astype(F32)
        else:
            g_ = g_ref[...]
        go_ref[0] = g_
        d_ref[0], nm_ref[0], nv_ref[0] = _adamw_math(w_ref[0], g_, m_ref[0], v_ref[0])

    spec = pl.BlockSpec((1, tr, cols), lambda i: (0, i, 0))
    g_spec = pl.BlockSpec((N_DEV, tr, cols), lambda i: (0, i, 0)) if from_slots else pl.BlockSpec((tr, cols), lambda i: (i, 0))
    shp = jax.ShapeDtypeStruct((1, rows, cols), F32)
    return pl.pallas_call(
        body, name=name, grid=(rows // tr,), in_specs=[spec, g_spec, spec, spec], out_specs=(spec,) * 4, out_shape=(shp,) * 4,
        compiler_params=_params(),
    )(w, g, m, v)


def _adamw_small(ws, gs, ms, vs):
    n = len(ws)

    def body(*refs):
        ins, outs = refs[:4 * n], refs[4 * n:]
        for k in range(n):
            w_ref, g_ref, m_ref, v_ref = ins[k], ins[n + k], ins[2 * n + k], ins[3 * n + k]
            outs[k][...], outs[n + k][...], outs[2 * n + k][...] = _adamw_math(w_ref[...], g_ref[...], m_ref[...], v_ref[...])

    shapes = tuple(jax.ShapeDtypeStruct(w.shape, F32) for w in ws)
    out = pl.pallas_call(body, name="adamw_small", out_shape=shapes * 3, compiler_params=_params(()))(*ws, *gs, *ms, *vs)
    return out[:n], out[n:2 * n], out[2 * n:]


def _to_slab(parts):
    flat = jnp.concatenate([q.reshape(-1) for q in parts])
    pad = (-flat.shape[0]) % (F32_SUBLANES * LANES)
    return jnp.pad(flat, (0, pad)).reshape(-1, LANES)


def _from_slab(slab, shapes):
    flat = slab.reshape(-1)
    out, off = [], 0
    for s in shapes:
        n = math.prod(s)
        out.append(flat[off:off + n].reshape(s))
        off += n
    return out


def kernel(x, w_in, b_gate, conv_w, v_norm_g, v_norm_b, w_s, b_s, w_pa, w_pb, w_o, ln1_g, ln1_b, w_ff1, w_ff2, ln2_g, ln2_b, loss_target, m_w_in, m_b_gate, m_conv_w, m_v_norm_g, m_v_norm_b, m_w_s, m_b_s, m_w_pa, m_w_pb, m_w_o, m_ln1_g, m_ln1_b, m_w_ff1, m_w_ff2, m_ln2_g, m_ln2_b, v_w_in, v_b_gate, v_conv_w, v_v_norm_g, v_v_norm_b, v_w_s, v_b_s, v_w_pa, v_w_pb, v_w_o, v_ln1_g, v_ln1_b, v_w_ff1, v_w_ff2, v_ln2_g, v_ln2_b):
    t, d = x.shape[1], x.shape[2]
    dm = _Dims(t, d)
    tm = 256 if t % 256 == 0 else CHUNK
    tm_big = 512 if t % 512 == 0 else tm
    tt = 1024 if t % 1024 == 0 else tm
    me = 4 * lax.axis_index("x") + 2 * lax.axis_index("y") + lax.axis_index("c")
    x2, tgt = x[0], loss_target[0]

    conv_bits = lax.bitcast_convert_type(conv_w[0], BF16).reshape(-1)
    conv_blk = jnp.pad(conv_bits, (0, dm.conv_rows * d - conv_bits.shape[0])).reshape(dm.conv_rows, d)
    w_int, conv_g = _all_gather([w_in[0].T.astype(BF16), conv_blk])
    wa8 = dm.WA // N_DEV
    conv_all = lax.bitcast_convert_type(conv_g.reshape(N_DEV, -1)[:, :3 * wa8 * 2].reshape(N_DEV, 3, wa8, 2), F32)
    conv_full = jnp.transpose(conv_all, (1, 0, 2)).reshape(3, dm.WA)
    conv_w8 = jnp.pad(conv_full, ((0, F32_SUBLANES - 3), (0, 0)))
    ws_b = w_s[0].astype(BF16)
    wst_b = jnp.transpose(w_s[0], (0, 2, 1)).astype(BF16)
    bias_s = jnp.repeat(b_s[0].T, CHUNK, axis=1)
    head_sel = (jnp.arange(d)[:, None] // CHUNK == jnp.arange(CHUNK)[None, :]).astype(BF16)

    p, w_pa_f, w_pb_f, w_o_f, w_1t, w_2 = _proj_in(
        x2, w_int, b_gate, [w_pa[0].astype(BF16), w_pb[0].astype(BF16), w_o[0].astype(BF16), w_ff1[0].T.astype(BF16),
                            w_ff2[0].astype(BF16)], dm, tm_big)
    ya, yb, r1, gu_s, dgu_s, xhv_s, rgv_s = _mixer_fwd(p, x2, w_pa_f, w_pb_f, w_o_f, conv_w8, v_norm_g, v_norm_b, ws_b, bias_s, dm, tm_big)
    relu, x1b, dr2, dr2b, sums2 = _ffn_fwd(r1, tgt, w_1t, w_2, ln1_g, ln1_b, ln2_g, ln2_b, dm, tm_big)
    dr1, dh1, sums1 = _ffn_bwd(r1, relu, dr2, w_2, w_1t, ln1_g, dm, tm_big)
    fb = min(1024, dm.DFF)
    rows = dm.shard_rows
    g_ff1t, g_ff2 = _wgrad([(dh1, x1b, False), (relu, dr2b, True)], tt, fb, "ffn_wgrad")
    dp, a_m, dya, s_m, bb_m, dyb, g_ws, g_bs_t, g_cw, g_bg, g_vn, got_ff1t, got_ff2 = _mixer_bwd(
        dr1, p, ya, yb, gu_s, dgu_s, xhv_s, rgv_s, w_o_f.T, w_pa_f.T, w_pb_f.T, conv_w8, v_norm_g, v_norm_b, ws_b, wst_b, bias_s, head_sel,
        [(g_ff1t, rows[4]), (g_ff2, rows[5])], dm, tm)
    (g_pa,) = _wgrad([(a_m, dya, False)], tt, dm.WA, "w_pa_grad")
    g_o, g_pb = _wgrad([(s_m, dr1, False), (bb_m, dyb, False)], tt, d, "w_o_pb_grad")
    g_int, got_pa, got_pb, got_o = _wgrad([(dp, x2, False)], tt, 17 * LANES, "w_in_grad",
                                          xchg=[(g_pa, rows[1]), (g_pb, rows[2]), (g_o, rows[3])])
    small_parts = [g_bg[0], g_cw[0:3], g_vn[0], g_vn[1], g_ws, g_bs_t[:, :dm.H].T,
                   sums1[0], sums1[1], sums2[1], sums2[2], sums2[0]]
    grad_x, got_int, got_s = _input_grad(dp, dr1, w_int, [(g_int, rows[0]), (_to_slab(small_parts), None)], dm, tm_big)

    ssum = _sum_slots(got_s, 256, "sum_small")
    (s_bg, s_cw, s_vng, s_vnb, s_ws, s_bs, s_l1g, s_l1b, s_l2g, s_l2b, s_sq) = _from_slab(
        ssum, [(1, 2 * d), (3, dm.WA), (1, d), (1, d), (1, dm.H, CHUNK, CHUNK), (1, dm.H, CHUNK),
               (1, d), (1, d), (1, d), (1, d), (d,)])
    loss = 0.5 * jnp.sum(s_sq) / d
    s_cw = lax.dynamic_slice(s_cw, (0, me * wa8), (3, wa8))[None]
    big_g = [_sum_slots(got_int, 256, "sum_grads_w_in").T, got_pa, got_pb, got_o,
             _sum_slots(got_ff1t, 256, "sum_grads_w_ff1").T, got_ff2]
    big_w = [(w_in, m_w_in, v_w_in), (w_pa, m_w_pa, v_w_pa), (w_pb, m_w_pb, v_w_pb), (w_o, m_w_o, v_w_o),
             (w_ff1, m_w_ff1, v_w_ff1), (w_ff2, m_w_ff2, v_w_ff2)]
    big_out = [_adamw(w, g, m, v, "adamw_%d" % k) for k, ((w, m, v), g) in enumerate(zip(big_w, big_g))]
    small_w = [(b_gate, m_b_gate, v_b_gate), (conv_w, m_conv_w, v_conv_w), (v_norm_g, m_v_norm_g, v_v_norm_g),
               (v_norm_b, m_v_norm_b, v_v_norm_b), (w_s, m_w_s, v_w_s), (b_s, m_b_s, v_b_s), (ln1_g, m_ln1_g, v_ln1_g),
               (ln1_b, m_ln1_b, v_ln1_b), (ln2_g, m_ln2_g, v_ln2_g), (ln2_b, m_ln2_b, v_ln2_b)]
    small_g = [s_bg, s_cw, s_vng, s_vnb, s_ws, s_bs, s_l1g, s_l1b, s_l2g, s_l2b]
    small_out = list(zip(*_adamw_small([w for w, _, _ in small_w], small_g, [m for _, m, _ in small_w],
                                       [v for _, _, v in small_w])))

    order = [("b", 0), ("s", 0), ("s", 1), ("s", 2), ("s", 3), ("s", 4), ("s", 5), ("b", 1), ("b", 2), ("b", 3),
             ("s", 6), ("s", 7), ("b", 4), ("b", 5), ("s", 8), ("s", 9)]
    grads, deltas, new_m, new_v = [], [], [], []
    for kind, k in order:
        if kind == "b":
            g, dl, nm, nv = big_out[k]
        else:
            g, (dl, nm, nv) = small_g[k], small_out[k]
        grads.append(g)
        deltas.append(dl)
        new_m.append(nm)
        new_v.append(nv)
    return (loss, grad_x[None], *grads, *deltas, *new_m, *new_v)
```

```python
import math

import jax
import jax.numpy as jnp
from jax import lax
from jax.experimental import pallas as pl
from jax.experimental.pallas import tpu as pltpu

F32 = jnp.float32
BF16 = jnp.bfloat16
N_DEV = 8
CHUNK = 128
LN_EPS = 1e-5
ALPHA = 2.0 ** 0.25
ADAM_LR, ADAM_B1, ADAM_B2, ADAM_EPS, ADAM_WD, ADAM_STEP = 0.001, 0.9, 0.999, 1e-08, 0.01, 10
F32_SUBLANES = 8
BF16_SUBLANES = 16
LANES = 128
VMEM_LIMIT = 56 * 1024 * 1024
MESH = pl.DeviceIdType.MESH
NT_DIMS = (((1,), (1,)), ((), ()))
TN_DIMS = (((0,), (0,)), ((), ()))
HBM_SPEC = pl.BlockSpec(memory_space=pltpu.HBM)


class _Dims:
    def __init__(self, t, d):
        self.T, self.D = t, d
        self.WA = 3 * d // 2
        self.NP = 3 * self.WA + 4 * d
        self.DFF = 4 * d
        self.H = d // CHUNK
        self.OFF_CA, self.OFF_HA = self.WA, 2 * self.WA
        self.OFF_UB = 3 * self.WA
        self.OFF_VB = self.OFF_UB + d
        self.OFF_GA = self.OFF_VB + d
        self.OFF_GB = self.OFF_GA + d
        self.shard_rows = (self.NP // N_DEV, self.WA // N_DEV, d // N_DEV, d // N_DEV, self.DFF // N_DEV, self.DFF // N_DEV)
        self.conv_rows = BF16_SUBLANES * max(1, -(-(3 * (self.WA // N_DEV) * 2) // (BF16_SUBLANES * d)))


def _params(sem=("arbitrary",), vmem=VMEM_LIMIT):
    return pltpu.CompilerParams(dimension_semantics=sem, vmem_limit_bytes=vmem)


def _mesh_pos():
    return lax.axis_index("x"), lax.axis_index("y"), lax.axis_index("c")


def _resident(shape):
    zeros = (0,) * len(shape)
    return pl.BlockSpec(shape, lambda *_: zeros, pipeline_mode=pl.Buffered(1))


class _TwoLevelGather:
    def __init__(self, shard_refs, out_refs, send_sems, recv_sems, local_sems):
        self.n = len(shard_refs)
        self.shard_refs, self.out_refs = shard_refs, out_refs
        self.send_sems, self.recv_sems, self.local_sems = send_sems, recv_sems, local_sems
        x, y, c = _mesh_pos()
        self.c = c
        self.me, self.sibling = (x, y, c), (x, y, 1 - c)
        self.chips = [(1 - x, y), (x, 1 - y), (1 - x, 1 - y)]

    def _slot(self, a, px, py, pc):
        rows = self.shard_refs[a].shape[0]
        return self.out_refs[a].at[pl.ds((4 * px + 2 * py + pc) * rows, rows), :]

    def _copy(self, a, k, block, to, src=None):
        return pltpu.make_async_remote_copy(
            src_ref=self._slot(a, *block) if src is None else src, dst_ref=self._slot(a, *block),
            send_sem=self.send_sems.at[7 * a + k], recv_sem=self.recv_sems.at[7 * a + k], device_id=to, device_id_type=MESH)

    def _mine(self):
        return [pltpu.make_async_copy(self.shard_refs[a], self._slot(a, *self.me), self.local_sems.at[a]) for a in range(self.n)]

    def _first(self):
        out = []
        for a in range(self.n):
            out.append(self._copy(a, 0, self.me, self.sibling, src=self.shard_refs[a]))
            out += [self._copy(a, 1 + j, self.me, (*chip, self.c), src=self.shard_refs[a]) for j, chip in enumerate(self.chips)]
        return out

    def _passed(self):
        return [self._copy(a, 4 + j, (*chip, self.c), self.sibling) for j, chip in enumerate(self.chips) for a in range(self.n)]

    def start(self):
        for cp in self._mine() + self._first():
            cp.start()

    def forward(self):
        passed = self._passed()
        for j, chip in enumerate(self.chips):
            for a in range(self.n):
                self._copy(a, 1 + j, (*chip, self.c), self.me).wait_recv()
                passed[j * self.n + a].start()

    def finish(self):
        for a in range(self.n):
            self._copy(a, 0, self.sibling, self.me).wait_recv()
            for j, chip in enumerate(self.chips):
                self._copy(a, 4 + j, (*chip, 1 - self.c), self.me).wait_recv()
        for cp in self._first() + self._passed():
            cp.wait_send()
        for cp in self._mine():
            cp.wait()

    @staticmethod
    def out_shapes(shards):
        return tuple(jax.ShapeDtypeStruct((N_DEV * s.shape[0], s.shape[1]), s.dtype) for s in shards)

    @staticmethod
    def semaphores(n):
        return [pltpu.SemaphoreType.DMA((7 * n,)), pltpu.SemaphoreType.DMA((7 * n,)), pltpu.SemaphoreType.DMA((n,))]


def _all_gather(shards):
    n = len(shards)

    def body(*refs):
        gather = _TwoLevelGather(refs[:n], refs[n:2 * n], *refs[2 * n:])
        gather.start()
        gather.forward()
        gather.finish()

    return pl.pallas_call(
        body, name="all_gather_w_in", out_shape=_TwoLevelGather.out_shapes(shards),
        in_specs=[HBM_SPEC] * n, out_specs=(HBM_SPEC,) * n, scratch_shapes=_TwoLevelGather.semaphores(n),
    )(*shards)


class _Exchange:
    def __init__(self, src_refs, recv_refs, rows, send_sems, recv_sems, local_sems):
        x, y, c = _mesh_pos()
        me = 4 * x + 2 * y + c
        self.own, self.sends, self.arrivals = [], [], []
        for a, (src, recv) in enumerate(zip(src_refs, recv_refs)):
            def blk(k, src=src, r=rows[a]):
                return src if r is None else src.at[pl.ds(k * r, r), :]

            self.own.append(pltpu.make_async_copy(blk(me), recv.at[me], local_sems.at[a]))
            for rel in range(1, N_DEV):
                px = 1 - x if rel & 4 else x
                py = 1 - y if rel & 2 else y
                pc = 1 - c if rel & 1 else c
                peer = 4 * px + 2 * py + pc
                sem = dict(send_sem=send_sems.at[7 * a + rel - 1], recv_sem=recv_sems.at[7 * a + rel - 1],
                           device_id=(px, py, pc), device_id_type=MESH)
                self.sends.append(pltpu.make_async_remote_copy(src_ref=blk(peer), dst_ref=recv.at[me], **sem))
                self.arrivals.append(pltpu.make_async_remote_copy(src_ref=blk(me), dst_ref=recv.at[peer], **sem))

    def start(self):
        for cp in self.own + self.sends:
            cp.start()

    def finish(self):
        for cp in self.arrivals:
            cp.wait_recv()
        for cp in self.sends:
            cp.wait_send()
        for cp in self.own:
            cp.wait()

    @staticmethod
    def out_shapes(srcs, rows):
        return tuple(jax.ShapeDtypeStruct((N_DEV, s.shape[0] if r is None else r, s.shape[1]), s.dtype) for s, r in zip(srcs, rows))

    @staticmethod
    def semaphores(n):
        return [pltpu.SemaphoreType.DMA((7 * n,)), pltpu.SemaphoreType.DMA((7 * n,)), pltpu.SemaphoreType.DMA((n,))]


def _sum_slots(slots, tile_rows, name):
    _, rows, cols = slots.shape
    tr = rows
    if N_DEV * rows * cols * slots.dtype.itemsize > 8 * 1024 * 1024:
        tr = next(c for c in (256, 192, 128, 64, 32, 16) if c <= tile_rows and rows % c == 0)

    def body(s_ref, o_ref):
        acc = s_ref[0].astype(F32)
        for k in range(1, N_DEV):
            acc = acc + s_ref[k].astype(F32)
        o_ref[...] = acc

    return pl.pallas_call(
        body, name=name, grid=(rows // tr,),
        in_specs=[pl.BlockSpec((N_DEV, tr, cols), lambda i: (0, i, 0))],
        out_specs=pl.BlockSpec((tr, cols), lambda i: (i, 0)),
        out_shape=jax.ShapeDtypeStruct((rows, cols), F32),
        compiler_params=_params(),
    )(slots)


def _gelu_and_grad(x):
    k0 = math.sqrt(2.0 / math.pi)
    k1 = 0.044715
    a = k1 * (x * x)
    half = 1.0 / (1.0 + jnp.exp((-2.0 * k0) * x * (1.0 + a)))
    g = x * half
    return g, half + g * (1.0 - half) * (2.0 * k0 + (6.0 * k0) * a)


def _ln_stats(r):
    mu = jnp.mean(r, axis=-1, keepdims=True)
    rc = r - mu
    var = jnp.mean(rc * rc, axis=-1, keepdims=True)
    rstd = lax.rsqrt(var + LN_EPS)
    return rc * rstd, rstd


def _ln_bwd(dxh, xh, rstd):
    return rstd * (dxh - jnp.mean(dxh, axis=-1, keepdims=True) - xh * jnp.mean(dxh * xh, axis=-1, keepdims=True))


def _colsum(a):
    return jnp.sum(a, axis=0, keepdims=True)


def _mm(a, b):
    return jnp.dot(a, b, preferred_element_type=F32)


def _halo_maps(tm, t, unit):
    per, last = tm // unit, t // unit - 1
    return (lambda i: (jnp.maximum(i * per - 1, 0), 0)), (lambda i: (jnp.minimum((i + 1) * per, last), 0))


def _proj_in(x2, w_int, b_gate, shards, dm, tm):
    t, d, npj = dm.T, dm.D, dm.NP
    cw = d // 2
    nt = t // tm
    n = len(shards)

    def body(x_ref, w_ref, bg_ref, *refs):
        p_ref = refs[n]
        gather = _TwoLevelGather(refs[:n], refs[n + 1:2 * n + 1], *refs[2 * n + 1:])
        i = pl.program_id(0)

        @pl.when(i == 0)
        def _():
            gather.start()

        @pl.when(i == nt // 2)
        def _():
            gather.forward()

        xb = x_ref[...].astype(BF16)
        for blk in range(npj // cw):
            lo = blk * cw
            acc = lax.dot_general(xb, w_ref[lo:lo + cw, :], NT_DIMS, preferred_element_type=F32)
            if lo >= dm.OFF_GA:
                acc = jax.nn.sigmoid(acc + bg_ref[:, lo - dm.OFF_GA:lo - dm.OFF_GA + cw])
            p_ref[:, lo:lo + cw] = acc.astype(BF16)

        @pl.when(i == nt - 1)
        def _():
            gather.finish()

    return pl.pallas_call(
        body, name="proj_in", grid=(nt,),
        in_specs=[pl.BlockSpec((tm, d), lambda i: (i, 0)), _resident((npj, d)), _resident((1, 2 * d))] + [HBM_SPEC] * n,
        out_specs=(pl.BlockSpec((tm, npj), lambda i: (i, 0)),) + (HBM_SPEC,) * n,
        out_shape=(jax.ShapeDtypeStruct((t, npj), BF16),) + _TwoLevelGather.out_shapes(shards),
        scratch_shapes=_TwoLevelGather.semaphores(n),
        compiler_params=_params(),
    )(x2, w_int, b_gate, *shards)


def _conv_taps(ext_ref, center, prev_blk, next_blk, first, last, tm):
    h = F32_SUBLANES
    ext_ref[0:h, :] = jnp.where(first, 0.0, prev_blk)
    ext_ref[h:h + tm, :] = center
    ext_ref[h + tm:h + tm + h, :] = jnp.where(last, 0.0, next_blk)
    return ext_ref[pl.ds(h - 1, tm), :], ext_ref[pl.ds(h + 1, tm), :]


def _spatial_mix(vn, ws_ref, bias_ref, mixed_ref, dm, tm):
    vb = vn.astype(BF16)
    for cc in range(tm // CHUNK):
        r0 = cc * CHUNK
        for h in range(dm.H):
            c0 = h * CHUNK
            m = _mm(ws_ref[h], vb[r0:r0 + CHUNK, c0:c0 + CHUNK])
            mixed_ref[r0:r0 + CHUNK, c0:c0 + CHUNK] = m + bias_ref[:, c0:c0 + CHUNK]
    return mixed_ref[...]


def _mixer_fwd(p, x2, w_pa, w_pb, w_o, conv_w8, vng, vnb, ws_b, bias_s, dm, tm):
    t, d, wa, npj = dm.T, dm.D, dm.WA, dm.NP
    nt = t // tm
    hb = BF16_SUBLANES
    prev_map, next_map = _halo_maps(tm, t, hb)

    def body(p_ref, pp_ref, pn_ref, x_ref, wpa_ref, wpb_ref, wo_ref, cw_ref, vng_ref, vnb_ref, ws_ref, bias_ref,
             ya_ref, yb_ref, r1_ref, gu_ref, dgu_ref, xhv_ref, rgv_ref, cv_ref, s_ref, ext_ref, mixed_ref):
        i = pl.program_id(0)

        def col(ref, lo, width):
            return ref[:, lo:lo + width].astype(F32)

        ch = col(p_ref, dm.OFF_CA, wa) * col(p_ref, dm.OFF_HA, wa)
        chp = (col(pp_ref, dm.OFF_CA, wa) * col(pp_ref, dm.OFF_HA, wa))[hb - F32_SUBLANES:hb]
        chn = (col(pn_ref, dm.OFF_CA, wa) * col(pn_ref, dm.OFF_HA, wa))[0:F32_SUBLANES]
        up, dn = _conv_taps(ext_ref, ch, chp, chn, i == 0, i == nt - 1, tm)
        cv = cw_ref[0:1, :] * up + cw_ref[1:2, :] * ch + cw_ref[2:3, :] * dn
        cv_ref[...] = cv.astype(BF16)
        ya = _mm((col(p_ref, 0, wa) * cv).astype(BF16), wpa_ref[...])
        gv, dgelu_v = _gelu_and_grad(col(p_ref, dm.OFF_VB, d))
        xhv, rstdv = _ln_stats(gv)
        xhv_ref[...] = xhv.astype(BF16)
        rgv_ref[...] = (rstdv * dgelu_v).astype(BF16)
        mixed = _spatial_mix(xhv * vng_ref[...] + vnb_ref[...], ws_ref, bias_ref, mixed_ref, dm, tm)
        gu, dgelu_u = _gelu_and_grad(col(p_ref, dm.OFF_UB, d))
        gu_ref[...] = gu.astype(BF16)
        dgu_ref[...] = dgelu_u.astype(BF16)
        yb = _mm((gu * mixed).astype(BF16), wpb_ref[...])
        sb = (col(p_ref, dm.OFF_GA, d) * ya + col(p_ref, dm.OFF_GB, d) * yb).astype(BF16)
        s_ref[...] = sb
        mix = _mm(sb, wo_ref[...])
        ya_ref[...] = ya.astype(BF16)
        yb_ref[...] = yb.astype(BF16)
        r1_ref[...] = ALPHA * x_ref[...] + mix

    tile = lambda i: (i, 0)
    return pl.pallas_call(
        body, name="mixer_fwd", grid=(nt,),
        in_specs=[pl.BlockSpec((tm, npj), tile), pl.BlockSpec((hb, npj), prev_map), pl.BlockSpec((hb, npj), next_map),
                  pl.BlockSpec((tm, d), tile), _resident((wa, d)), _resident((d, d)), _resident((d, d)),
                  _resident((F32_SUBLANES, wa)), _resident((1, d)), _resident((1, d)),
                  _resident((dm.H, CHUNK, CHUNK)), _resident((CHUNK, d))],
        out_specs=(pl.BlockSpec((tm, d), tile),) * 7 + (pl.BlockSpec((tm, wa), tile), pl.BlockSpec((tm, d), tile)),
        out_shape=(jax.ShapeDtypeStruct((t, d), BF16), jax.ShapeDtypeStruct((t, d), BF16), jax.ShapeDtypeStruct((t, d), F32))
        + (jax.ShapeDtypeStruct((t, d), BF16),) * 4 + (jax.ShapeDtypeStruct((t, wa), BF16), jax.ShapeDtypeStruct((t, d), BF16)),
        scratch_shapes=[pltpu.VMEM((tm + 2 * F32_SUBLANES, wa), F32), pltpu.VMEM((tm, d), F32)],
        compiler_params=_params(),
    )(p, p, p, x2, w_pa, w_pb, w_o, conv_w8, vng, vnb, ws_b, bias_s)


def _ffn_fwd(r1, tgt, w1t, w2, ln1g, ln1b, ln2g, ln2b, dm, tm):
    t, d, dff = dm.T, dm.D, dm.DFF
    fc = dff // N_DEV

    def body(r1_ref, tgt_ref, w1t_ref, w2_ref, g1_ref, b1_ref, g2_ref, b2_ref, relu_ref, x1_ref, dr2_ref, dr2b_ref, sums_ref):
        @pl.when(pl.program_id(0) == 0)
        def _():
            sums_ref[...] = jnp.zeros_like(sums_ref)

        xh1, _ = _ln_stats(r1_ref[...])
        x1 = xh1 * g1_ref[...] + b1_ref[...]
        x1b = x1.astype(BF16)
        x1_ref[...] = x1b
        ffn = jnp.zeros((tm, d), F32)
        for k in range(dff // fc):
            ks = slice(k * fc, (k + 1) * fc)
            r = jnp.maximum(lax.dot_general(x1b, w1t_ref[ks, :], NT_DIMS, preferred_element_type=F32), 0.0)
            relu_ref[:, ks] = r.astype(BF16)
            ffn = ffn + _mm((r * r).astype(BF16), w2_ref[ks, :])
        xh2, rstd2 = _ln_stats(ALPHA * x1 + ffn)
        diff = xh2 * g2_ref[...] + b2_ref[...] - tgt_ref[...]
        dy = diff * (1.0 / d)
        dr2 = _ln_bwd(dy * g2_ref[...], xh2, rstd2)
        dr2_ref[...] = dr2
        dr2b_ref[...] = dr2.astype(BF16)
        sums_ref[0:1, :] += _colsum(diff * diff)
        sums_ref[1:2, :] += _colsum(dy * xh2)
        sums_ref[2:3, :] += _colsum(dy)

    tile = lambda i: (i, 0)
    vec = _resident((1, d))
    return pl.pallas_call(
        body, name="ffn_fwd", grid=(t // tm,),
        in_specs=[pl.BlockSpec((tm, d), tile), pl.BlockSpec((tm, d), tile), _resident((dff, d)), _resident((dff, d)),
                  vec, vec, vec, vec],
        out_specs=(pl.BlockSpec((tm, dff), tile), pl.BlockSpec((tm, d), tile), pl.BlockSpec((tm, d), tile),
                   pl.BlockSpec((tm, d), tile), pl.BlockSpec((F32_SUBLANES, d), lambda i: (0, 0))),
        out_shape=(jax.ShapeDtypeStruct((t, dff), BF16), jax.ShapeDtypeStruct((t, d), BF16), jax.ShapeDtypeStruct((t, d), F32),
                   jax.ShapeDtypeStruct((t, d), BF16), jax.ShapeDtypeStruct((F32_SUBLANES, d), F32)),
        compiler_params=_params(),
    )(r1, tgt, w1t, w2, ln1g, ln1b, ln2g, ln2b)


def _ffn_bwd(r1, relu, dr2, w2, w1t, ln1g, dm, tm):
    t, d, dff = dm.T, dm.D, dm.DFF
    fc = dff // N_DEV

    def body(r1_ref, relu_ref, dr2_ref, w2_ref, w1t_ref, g1_ref, dr1_ref, dh_ref, sums_ref):
        @pl.when(pl.program_id(0) == 0)
        def _():
            sums_ref[...] = jnp.zeros_like(sums_ref)

        xh1, rstd1 = _ln_stats(r1_ref[...])
        dr2 = dr2_ref[...]
        dr2b = dr2.astype(BF16)
        dx1 = ALPHA * dr2
        for k in range(dff // fc):
            ks = slice(k * fc, (k + 1) * fc)
            dact = lax.dot_general(dr2b, w2_ref[ks, :], NT_DIMS, preferred_element_type=F32)
            dhb = (dact * (2.0 * relu_ref[:, ks].astype(F32))).astype(BF16)
            dh_ref[:, ks] = dhb
            dx1 = dx1 + _mm(dhb, w1t_ref[ks, :])
        dr1_ref[...] = _ln_bwd(dx1 * g1_ref[...], xh1, rstd1)
        sums_ref[0:1, :] += _colsum(dx1 * xh1)
        sums_ref[1:2, :] += _colsum(dx1)

    tile = lambda i: (i, 0)
    return pl.pallas_call(
        body, name="ffn_bwd", grid=(t // tm,),
        in_specs=[pl.BlockSpec((tm, d), tile), pl.BlockSpec((tm, dff), tile), pl.BlockSpec((tm, d), tile),
                  _resident((dff, d)), _resident((dff, d)), _resident((1, d))],
        out_specs=(pl.BlockSpec((tm, d), tile), pl.BlockSpec((tm, dff), tile), pl.BlockSpec((F32_SUBLANES, d), lambda i: (0, 0))),
        out_shape=(jax.ShapeDtypeStruct((t, d), F32), jax.ShapeDtypeStruct((t, dff), BF16),
                   jax.ShapeDtypeStruct((F32_SUBLANES, d), F32)),
        compiler_params=_params(),
    )(r1, relu, dr2, w2, w1t, ln1g)


def _wgrad(pairs, tt, fb, name, xchg=()):
    n, m = len(pairs), len(xchg)
    t, f = pairs[0][0].shape
    d = pairs[0][1].shape[1]
    squares = [sq for _, _, sq in pairs]
    nj, ni = f // fb, t // tt
    xrows = [r for _, r in xchg]

    def body(*refs):
        ins, xsrc = refs[:2 * n], refs[2 * n:2 * n + m]
        outs, xrecv = refs[2 * n + m:3 * n + m], refs[3 * n + m:3 * n + 2 * m]
        accs, sems = refs[3 * n + 2 * m:4 * n + 2 * m], refs[4 * n + 2 * m:]
        j, i = pl.program_id(0), pl.program_id(1)
        exchange = _Exchange(xsrc, xrecv, xrows, *sems) if m else None

        if m:
            @pl.when(jnp.logical_and(j == 0, i == 0))
            def _():
                exchange.start()

        @pl.when(i == 0)
        def _():
            for acc in accs:
                acc[...] = jnp.zeros_like(acc)

        for q in range(n):
            lhs = ins[2 * q][...]
            if squares[q]:
                lf = lhs.astype(F32)
                lhs = (lf * lf).astype(BF16)
            accs[q][...] += lax.dot_general(lhs, ins[2 * q + 1][...].astype(BF16), TN_DIMS, preferred_element_type=F32)

        @pl.when(i == ni - 1)
        def _():
            for q in range(n):
                outs[q][...] = accs[q][...].astype(BF16)

        if m:
            @pl.when(jnp.logical_and(j == nj - 1, i == ni - 1))
            def _():
                exchange.finish()

    lhs_spec = pl.BlockSpec((tt, fb), lambda j, i: (i, j))
    rhs_spec = pl.BlockSpec((tt, d), lambda j, i: (i, 0))
    out_spec = pl.BlockSpec((fb, d), lambda j, i: (j, 0))
    xsrcs = [a_ for a_, _ in xchg]
    return pl.pallas_call(
        body, name=name, grid=(nj, ni),
        in_specs=[lhs_spec, rhs_spec] * n + [HBM_SPEC] * m, out_specs=(out_spec,) * n + (HBM_SPEC,) * m,
        out_shape=(jax.ShapeDtypeStruct((f, d), BF16),) * n + _Exchange.out_shapes(xsrcs, xrows),
        scratch_shapes=[pltpu.VMEM((fb, d), F32)] * n + (_Exchange.semaphores(m) if m else []),
        compiler_params=_params(("arbitrary", "arbitrary")),
    )(*[a_ for lhs, rhs, _ in pairs for a_ in (lhs, rhs)], *xsrcs)


def _mixer_bwd(dr1, p, ya, yb, gu_s, dgu_s, xhv_s, rgv_s, cv_s, w_ot, w_pat, w_pbt, conv_w8, vng, vnb, ws_b, wst_b, bias_s, head_sel, xchg, dm, tm):
    t, d, wa, npj = dm.T, dm.D, dm.WA, dm.NP
    nt = t // tm
    h8, hb = F32_SUBLANES, BF16_SUBLANES
    ext = tm + 2 * h8
    prev_f, next_f = _halo_maps(tm, t, h8)
    prev_b, next_b = _halo_maps(tm, t, hb)
    nx = len(xchg)
    xsrcs, xrows = [a_ for a_, _ in xchg], [r for _, r in xchg]

    def body(dr_ref, drp_ref, drn_ref, p_ref, pp_ref, pn_ref, ya_ref, yb_ref, gu_ref, dgu_ref, xhv_ref, rgv_ref, cv_ref,
             wot_ref, wpat_ref, wpbt_ref,
             cw_ref, vng_ref, vnb_ref, ws_ref, wst_ref, bias_ref, sel_ref,
             *refs):
        xsrc = refs[:nx]
        dp_ref, a_ref, dya_ref, bb_ref, dyb_ref, dws_ref, dbs_ref, dcw_ref, dbg_ref, dvn_ref = refs[nx:nx + 10]
        xrecv = refs[nx + 10:2 * nx + 10]
        ext2_ref, mixed_ref, dvnm_ref = refs[2 * nx + 10:2 * nx + 13]
        exchange = _Exchange(xsrc, xrecv, xrows, *refs[2 * nx + 13:])
        i = pl.program_id(0)

        @pl.when(i == 0)
        def _():
            exchange.start()
            for ref in (dws_ref, dbs_ref, dcw_ref, dbg_ref, dvn_ref):
                ref[...] = jnp.zeros_like(ref)

        def col(ref, lo, width):
            return ref[:, lo:lo + width].astype(F32)

        def ext_rows(prev_blk, center, next_blk):
            return jnp.concatenate([prev_blk, center, next_blk], axis=0)

        def ext_col(lo, width):
            return ext_rows(col(pp_ref, lo, width)[hb - h8:hb], col(p_ref, lo, width), col(pn_ref, lo, width)[0:h8])

        row = lax.broadcasted_iota(jnp.int32, (ext, 1), 0) + (i * tm - h8)
        inside = jnp.logical_and(row >= 0, row < t)
        dr_e = ext_rows(drp_ref[...], dr_ref[...], drn_ref[...])
        ds_e = _mm(dr_e.astype(BF16), wot_ref[...])
        dya_e = ds_e * ext_col(dm.OFF_GA, d)
        da_e = _mm(dya_e.astype(BF16), wpat_ref[...])
        dcv_e = jnp.where(inside, da_e * ext_col(0, wa), 0.0)
        ext2_ref[...] = dcv_e
        dcv, dcv_up, dcv_dn = dcv_e[h8:h8 + tm], ext2_ref[pl.ds(h8 - 1, tm), :], ext2_ref[pl.ds(h8 + 1, tm), :]
        w0, w1, w2 = cw_ref[0:1, :], cw_ref[1:2, :], cw_ref[2:3, :]
        cv = cv_ref[...].astype(F32)
        dp_ref[:, 0:wa] = (da_e[h8:h8 + tm] * cv).astype(BF16)
        a_ref[...] = (col(p_ref, 0, wa) * cv).astype(BF16)
        dch = w0 * dcv_dn + w1 * dcv + w2 * dcv_up
        c_a, h_a = col(p_ref, dm.OFF_CA, wa), col(p_ref, dm.OFF_HA, wa)
        dp_ref[:, dm.OFF_CA:dm.OFF_CA + wa] = (dch * h_a).astype(BF16)
        dp_ref[:, dm.OFF_HA:dm.OFF_HA + wa] = (dch * c_a).astype(BF16)
        ch = c_a * h_a
        dcw_ref[0:1, :] += _colsum(dcv_dn * ch)
        dcw_ref[1:2, :] += _colsum(dcv * ch)
        dcw_ref[2:3, :] += _colsum(dcv_up * ch)
        dya_ref[...] = dya_e[h8:h8 + tm].astype(BF16)
        ds = ds_e[h8:h8 + tm]
        g_a, g_b = col(p_ref, dm.OFF_GA, d), col(p_ref, dm.OFF_GB, d)
        y_a, y_b = ya_ref[...].astype(F32), yb_ref[...].astype(F32)
        dzga = ds * y_a * g_a * (1.0 - g_a)
        dzgb = ds * y_b * g_b * (1.0 - g_b)
        dp_ref[:, dm.OFF_GA:dm.OFF_GA + d] = dzga.astype(BF16)
        dp_ref[:, dm.OFF_GB:dm.OFF_GB + d] = dzgb.astype(BF16)
        dbg_ref[0:1, 0:d] += _colsum(dzga)
        dbg_ref[0:1, d:2 * d] += _colsum(dzgb)
        dyb = (ds * g_b).astype(BF16)
        dyb_ref[...] = dyb
        xhv = xhv_ref[...].astype(F32)
        vn = xhv * vng_ref[...] + vnb_ref[...]
        mixed = _spatial_mix(vn, ws_ref, bias_ref, mixed_ref, dm, tm)
        gu = gu_ref[...].astype(F32)
        bb_ref[...] = (gu * mixed).astype(BF16)
        dbb = _mm(dyb, wpbt_ref[...])
        dp_ref[:, dm.OFF_UB:dm.OFF_UB + d] = (dbb * mixed * dgu_ref[...].astype(F32)).astype(BF16)
        dmb = (dbb * gu).astype(BF16)
        vb = vn.astype(BF16)
        dbs = jnp.zeros((CHUNK, CHUNK), F32)
        for cc in range(tm // CHUNK):
            r0 = cc * CHUNK
            dbs = dbs + _mm(dmb[r0:r0 + CHUNK, :], sel_ref[...])
            for h in range(dm.H):
                c0 = h * CHUNK
                blk = dmb[r0:r0 + CHUNK, c0:c0 + CHUNK]
                dvnm_ref[r0:r0 + CHUNK, c0:c0 + CHUNK] = _mm(wst_ref[h], blk)
                dws_ref[h] += lax.dot_general(blk, vb[r0:r0 + CHUNK, c0:c0 + CHUNK], NT_DIMS, preferred_element_type=F32)
        dbs_ref[...] += dbs
        dvn = dvnm_ref[...]
        dvn_ref[0:1, :] += _colsum(dvn * xhv)
        dvn_ref[1:2, :] += _colsum(dvn)
        dp_ref[:, dm.OFF_VB:dm.OFF_VB + d] = (_ln_bwd(dvn * vng_ref[...], xhv, rgv_ref[...].astype(F32))).astype(BF16)

        @pl.when(i == nt - 1)
        def _():
            exchange.finish()

    full = lambda i: (0, 0)
    tile = lambda i: (i, 0)
    hcc = _resident((dm.H, CHUNK, CHUNK))
    tok = lambda w, dt: jax.ShapeDtypeStruct((t, w), dt)
    return pl.pallas_call(
        body, name="mixer_bwd", grid=(nt,),
        in_specs=[pl.BlockSpec((tm, d), tile), pl.BlockSpec((h8, d), prev_f), pl.BlockSpec((h8, d), next_f),
                  pl.BlockSpec((tm, npj), tile), pl.BlockSpec((hb, npj), prev_b), pl.BlockSpec((hb, npj), next_b),
                  pl.BlockSpec((tm, d), tile), pl.BlockSpec((tm, d), tile), pl.BlockSpec((tm, d), tile),
                  pl.BlockSpec((tm, d), tile), pl.BlockSpec((tm, d), tile), pl.BlockSpec((tm, d), tile),
                  pl.BlockSpec((tm, wa), tile), _resident((d, d)), _resident((d, wa)), _resident((d, d)),
                  _resident((h8, wa)), _resident((1, d)), _resident((1, d)), hcc, hcc, _resident((CHUNK, d)),
                  _resident((d, CHUNK))] + [HBM_SPEC] * nx,
        out_specs=(pl.BlockSpec((tm, npj), tile), pl.BlockSpec((tm, wa), tile), pl.BlockSpec((tm, d), tile),
                   pl.BlockSpec((tm, d), tile), pl.BlockSpec((tm, d), tile),
                   pl.BlockSpec((dm.H, CHUNK, CHUNK), lambda i: (0, 0, 0)), pl.BlockSpec((CHUNK, CHUNK), full),
                   pl.BlockSpec((h8, wa), full), pl.BlockSpec((h8, 2 * d), full), pl.BlockSpec((h8, d), full))
        + (HBM_SPEC,) * nx,
        out_shape=(tok(npj, BF16), tok(wa, BF16), tok(d, BF16), tok(d, BF16), tok(d, BF16),
                   jax.ShapeDtypeStruct((dm.H, CHUNK, CHUNK), F32), jax.ShapeDtypeStruct((CHUNK, CHUNK), F32),
                   jax.ShapeDtypeStruct((h8, wa), F32), jax.ShapeDtypeStruct((h8, 2 * d), F32),
                   jax.ShapeDtypeStruct((h8, d), F32)) + _Exchange.out_shapes(xsrcs, xrows),
        scratch_shapes=[pltpu.VMEM((ext, wa), F32), pltpu.VMEM((tm, d), F32), pltpu.VMEM((tm, d), F32)]
        + _Exchange.semaphores(nx),
        compiler_params=_params(),
    )(dr1, dr1, dr1, p, p, p, ya, yb, gu_s, dgu_s, xhv_s, rgv_s, cv_s, w_ot, w_pat, w_pbt, conv_w8, vng, vnb, ws_b, wst_b, bias_s, head_sel, *xsrcs)


def _input_grad(dp, dr1, w_int, xchg, dm, tm):
    t, d, npj = dm.T, dm.D, dm.NP
    nt = t // tm
    nx = len(xchg)
    xsrcs, xrows = [a_ for a_, _ in xchg], [r for _, r in xchg]

    def body(dp_ref, dr_ref, w_ref, *refs):
        dx_ref = refs[nx]
        exchange = _Exchange(refs[:nx], refs[nx + 1:2 * nx + 1], xrows, *refs[2 * nx + 1:])
        i = pl.program_id(0)

        @pl.when(i == 0)
        def _():
            exchange.start()

        dx_ref[...] = ALPHA * dr_ref[...] + _mm(dp_ref[...], w_ref[...])

        @pl.when(i == nt - 1)
        def _():
            exchange.finish()

    return pl.pallas_call(
        body, name="input_grad", grid=(nt,),
        in_specs=[pl.BlockSpec((tm, npj), lambda i: (i, 0)), pl.BlockSpec((tm, d), lambda i: (i, 0)), _resident((npj, d))]
        + [HBM_SPEC] * nx,
        out_specs=(pl.BlockSpec((tm, d), lambda i: (i, 0)),) + (HBM_SPEC,) * nx,
        out_shape=(jax.ShapeDtypeStruct((t, d), F32),) + _Exchange.out_shapes(xsrcs, xrows),
        scratch_shapes=_Exchange.semaphores(nx),
        compiler_params=_params(),
    )(dp, dr1, w_int, *xsrcs)


def _adamw_math(w, g, m, v):
    nm = ADAM_B1 * m + (1.0 - ADAM_B1) * g
    nv = ADAM_B2 * v + (1.0 - ADAM_B2) * (g * g)
    delta = -ADAM_LR * ((nm / (1.0 - ADAM_B1 ** ADAM_STEP)) / (jnp.sqrt(nv / (1.0 - ADAM_B2 ** ADAM_STEP)) + ADAM_EPS) + ADAM_WD * w)
    return delta, nm, nv


def _adamw(w, g, m, v, name):
    _, rows, cols = w.shape
    tr = 256 if rows % 256 == 0 else rows
    from_slots = g.ndim == 3

    def body(w_ref, g_ref, m_ref, v_ref, go_ref, d_ref, nm_ref, nv_ref):
        if from_slots:
            g_ = g_ref[0].astype(F32)
            for k in range(1, N_DEV):
                g_ = g_ + g_ref[k].astype(F32)
        else:
            g_ = g_ref[...]
        go_ref[0] = g_
        d_ref[0], nm_ref[0], nv_ref[0] = _adamw_math(w_ref[0], g_, m_ref[0], v_ref[0])

    spec = pl.BlockSpec((1, tr, cols), lambda i: (0, i, 0))
    g_spec = pl.BlockSpec((N_DEV, tr, cols), lambda i: (0, i, 0)) if from_slots else pl.BlockSpec((tr, cols), lambda i: (i, 0))
    shp = jax.ShapeDtypeStruct((1, rows, cols), F32)
    return pl.pallas_call(
        body, name=name, grid=(rows // tr,), in_specs=[spec, g_spec, spec, spec], out_specs=(spec,) * 4, out_shape=(shp,) * 4,
        compiler_params=_params(),
    )(w, g, m, v)


def _adamw_small(ws, gs, ms, vs):
    n = len(ws)

    def body(*refs):
        ins, outs = refs[:4 * n], refs[4 * n:]
        for k in range(n):
            w_ref, g_ref, m_ref, v_ref = ins[k], ins[n + k], ins[2 * n + k], ins[3 * n + k]
            outs[k][...], outs[n + k][...], outs[2 * n + k][...] = _adamw_math(w_ref[...], g_ref[...], m_ref[...], v_ref[...])

    shapes = tuple(jax.ShapeDtypeStruct(w.shape, F32) for w in ws)
    out = pl.pallas_call(body, name="adamw_small", out_shape=shapes * 3, compiler_params=_params(()))(*ws, *gs, *ms, *vs)
    return out[:n], out[n:2 * n], out[2 * n:]


def _to_slab(parts):
    flat = jnp.concatenate([q.reshape(-1) for q in parts])
    pad = (-flat.shape[0]) % (F32_SUBLANES * LANES)
    return jnp.pad(flat, (0, pad)).reshape(-1, LANES)


def _from_slab(slab, shapes):
    flat = slab.reshape(-1)
    out, off = [], 0
    for s in shapes:
        n = math.prod(s)
        out.append(flat[off:off + n].reshape(s))
        off += n
    return out


def kernel(x, w_in, b_gate, conv_w, v_norm_g, v_norm_b, w_s, b_s, w_pa, w_pb, w_o, ln1_g, ln1_b, w_ff1, w_ff2, ln2_g, ln2_b, loss_target, m_w_in, m_b_gate, m_conv_w, m_v_norm_g, m_v_norm_b, m_w_s, m_b_s, m_w_pa, m_w_pb, m_w_o, m_ln1_g, m_ln1_b, m_w_ff1, m_w_ff2, m_ln2_g, m_ln2_b, v_w_in, v_b_gate, v_conv_w, v_v_norm_g, v_v_norm_b, v_w_s, v_b_s, v_w_pa, v_w_pb, v_w_o, v_ln1_g, v_ln1_b, v_w_ff1, v_w_ff2, v_ln2_g, v_ln2_b):
    t, d = x.shape[1], x.shape[2]
    dm = _Dims(t, d)
    tm = 256 if t % 256 == 0 else CHUNK
    tm_big = 512 if t % 512 == 0 else tm
    tt = 1024 if t % 1024 == 0 else tm
    me = 4 * lax.axis_index("x") + 2 * lax.axis_index("y") + lax.axis_index("c")
    x2, tgt = x[0], loss_target[0]

    conv_bits = lax.bitcast_convert_type(conv_w[0], BF16).reshape(-1)
    conv_blk = jnp.pad(conv_bits, (0, dm.conv_rows * d - conv_bits.shape[0])).reshape(dm.conv_rows, d)
    w_int, conv_g = _all_gather([w_in[0].T.astype(BF16), conv_blk])
    wa8 = dm.WA // N_DEV
    conv_all = lax.bitcast_convert_type(conv_g.reshape(N_DEV, -1)[:, :3 * wa8 * 2].reshape(N_DEV, 3, wa8, 2), F32)
    conv_full = jnp.transpose(conv_all, (1, 0, 2)).reshape(3, dm.WA)
    conv_w8 = jnp.pad(conv_full, ((0, F32_SUBLANES - 3), (0, 0)))
    ws_b = w_s[0].astype(BF16)
    wst_b = jnp.transpose(w_s[0], (0, 2, 1)).astype(BF16)
    bias_s = jnp.repeat(b_s[0].T, CHUNK, axis=1)
    head_sel = (jnp.arange(d)[:, None] // CHUNK == jnp.arange(CHUNK)[None, :]).astype(BF16)

    p, w_pa_f, w_pb_f, w_o_f, w_1t, w_2 = _proj_in(
        x2, w_int, b_gate, [w_pa[0].astype(BF16), w_pb[0].astype(BF16), w_o[0].astype(BF16), w_ff1[0].T.astype(BF16),
                            w_ff2[0].astype(BF16)], dm, tm_big)
    ya, yb, r1, gu_s, dgu_s, xhv_s, rgv_s, cv_s, s_m = _mixer_fwd(
        p, x2, w_pa_f, w_pb_f, w_o_f, conv_w8, v_norm_g, v_norm_b, ws_b, bias_s, dm, tm)
    relu, x1b, dr2, dr2b, sums2 = _ffn_fwd(r1, tgt, w_1t, w_2, ln1_g, ln1_b, ln2_g, ln2_b, dm, tm_big)
    dr1, dh1, sums1 = _ffn_bwd(r1, relu, dr2, w_2, w_1t, ln1_g, dm, tm_big)
    fb = min(1024, dm.DFF)
    rows = dm.shard_rows
    g_ff1t, g_ff2 = _wgrad([(dh1, x1b, False), (relu, dr2b, True)], tt, fb, "ffn_wgrad")
    dp, a_m, dya, bb_m, dyb, g_ws, g_bs_t, g_cw, g_bg, g_vn, got_ff1t, got_ff2 = _mixer_bwd(
        dr1, p, ya, yb, gu_s, dgu_s, xhv_s, rgv_s, cv_s, w_o_f.T, w_pa_f.T, w_pb_f.T, conv_w8, v_norm_g, v_norm_b, ws_b, wst_b, bias_s, head_sel,
        [(g_ff1t, rows[4]), (g_ff2, rows[5])], dm, tm)
    (g_pa,) = _wgrad([(a_m, dya, False)], tt, dm.WA, "w_pa_grad")
    g_o, g_pb = _wgrad([(s_m, dr1, False), (bb_m, dyb, False)], tt, d, "w_o_pb_grad")
    g_int, got_pa, got_pb, got_o = _wgrad([(dp, x2, False)], tt, 17 * LANES, "w_in_grad",
                                          xchg=[(g_pa, rows[1]), (g_pb, rows[2]), (g_o, rows[3])])
    small_parts = [g_bg[0], g_cw[0:3], g_vn[0], g_vn[1], g_ws, g_bs_t[:, :dm.H].T,
                   sums1[0], sums1[1], sums2[1], sums2[2], sums2[0]]
    grad_x, got_int, got_s = _input_grad(dp, dr1, w_int, [(g_int, rows[0]), (_to_slab(small_parts), None)], dm, tm_big)

    ssum = _sum_slots(got_s, 256, "sum_small")
    (s_bg, s_cw, s_vng, s_vnb, s_ws, s_bs, s_l1g, s_l1b, s_l2g, s_l2b, s_sq) = _from_slab(
        ssum, [(1, 2 * d), (3, dm.WA), (1, d), (1, d), (1, dm.H, CHUNK, CHUNK), (1, dm.H, CHUNK),
               (1, d), (1, d), (1, d), (1, d), (d,)])
    loss = 0.5 * jnp.sum(s_sq) / d
    s_cw = lax.dynamic_slice(s_cw, (0, me * wa8), (3, wa8))[None]
    big_g = [_sum_slots(got_int, 256, "sum_grads_w_in").T, got_pa, got_pb, got_o,
             _sum_slots(got_ff1t, 256, "sum_grads_w_ff1").T, got_ff2]
    big_w = [(w_in, m_w_in, v_w_in), (w_pa, m_w_pa, v_w_pa), (w_pb, m_w_pb, v_w_pb), (w_o, m_w_o, v_w_o),
             (w_ff1, m_w_ff1, v_w_ff1), (w_ff2, m_w_ff2, v_w_ff2)]
    big_out = [_adamw(w, g, m, v, "adamw_%d" % k) for k, ((w, m, v), g) in enumerate(zip(big_w, big_g))]
    small_w = [(b_gate, m_b_gate, v_b_gate), (conv_w, m_conv_w, v_conv_w), (v_norm_g, m_v_norm_g, v_v_norm_g),
               (v_norm_b, m_v_norm_b, v_v_norm_b), (w_s, m_w_s, v_w_s), (b_s, m_b_s, v_b_s), (ln1_g, m_ln1_g, v_ln1_g),
               (ln1_b, m_ln1_b, v_ln1_b), (ln2_g, m_ln2_g, v_ln2_g), (ln2_b, m_ln2_b, v_ln2_b)]
    small_g = [s_bg, s_cw, s_vng, s_vnb, s_ws, s_bs, s_l1g, s_l1b, s_l2g, s_l2b]
    small_out = list(zip(*_adamw_small([w for w, _, _ in small_w], small_g, [m for _, m, _ in small_w],
                                       [v for _, _, v in small_w])))

    order = [("b", 0), ("s", 0), ("s", 1), ("s", 2), ("s", 3), ("s", 4), ("s", 5), ("b", 1), ("b", 2), ("b", 3),
             ("s", 6), ("s", 7), ("b", 4), ("b", 5), ("s", 8), ("s", 9)]
    grads, deltas, new_m, new_v = [], [], [], []
    for kind, k in order:
        if kind == "b":
            g, dl, nm, nv = big_out[k]
        else:
            g, (dl, nm, nv) = small_g[k], small_out[k]
        grads.append(g)
        deltas.append(dl)
        new_m.append(nm)
        new_v.append(nv)
    return (loss, grad_x[None], *grads, *deltas, *new_m, *new_v)
```

```python
import math

import jax
import jax.numpy as jnp
from jax import lax
from jax.experimental import pallas as pl
from jax.experimental.pallas import tpu as pltpu

F32 = jnp.float32
BF16 = jnp.bfloat16
N_DEV = 8
CHUNK = 128
LN_EPS = 1e-5
ALPHA = 2.0 ** 0.25
ADAM_LR, ADAM_B1, ADAM_B2, ADAM_EPS, ADAM_WD, ADAM_STEP = 0.001, 0.9, 0.999, 1e-08, 0.01, 10
F32_SUBLANES = 8
BF16_SUBLANES = 16
LANES = 128
VMEM_LIMIT = 56 * 1024 * 1024
MESH = pl.DeviceIdType.MESH
NT_DIMS = (((1,), (1,)), ((), ()))
TN_DIMS = (((0,), (0,)), ((), ()))
HBM_SPEC = pl.BlockSpec(memory_space=pltpu.HBM)


class _Dims:
    def __init__(self, t, d):
        self.T, self.D = t, d
        self.WA = 3 * d // 2
        self.NP = 3 * self.WA + 4 * d
        self.DFF = 4 * d
        self.H = d // CHUNK
        self.OFF_CA, self.OFF_HA = self.WA, 2 * self.WA
        self.OFF_UB = 3 * self.WA
        self.OFF_VB = self.OFF_UB + d
        self.OFF_GA = self.OFF_VB + d
        self.OFF_GB = self.OFF_GA + d
        self.shard_rows = (self.NP // N_DEV, self.WA // N_DEV, d // N_DEV, d // N_DEV, self.DFF // N_DEV, self.DFF // N_DEV)
        self.conv_rows = BF16_SUBLANES * max(1, -(-(3 * (self.WA // N_DEV) * 2) // (BF16_SUBLANES * d)))


def _params(sem=("arbitrary",), vmem=VMEM_LIMIT):
    return pltpu.CompilerParams(dimension_semantics=sem, vmem_limit_bytes=vmem)


def _mesh_pos():
    return lax.axis_index("x"), lax.axis_index("y"), lax.axis_index("c")


def _resident(shape):
    zeros = (0,) * len(shape)
    return pl.BlockSpec(shape, lambda *_: zeros, pipeline_mode=pl.Buffered(1))


class _TwoLevelGather:
    def __init__(self, shard_refs, out_refs, send_sems, recv_sems, local_sems):
        self.n = len(shard_refs)
        self.shard_refs, self.out_refs = shard_refs, out_refs
        self.send_sems, self.recv_sems, self.local_sems = send_sems, recv_sems, local_sems
        x, y, c = _mesh_pos()
        self.c = c
        self.me, self.sibling = (x, y, c), (x, y, 1 - c)
        self.chips = [(1 - x, y), (x, 1 - y), (1 - x, 1 - y)]

    def _slot(self, a, px, py, pc):
        rows = self.shard_refs[a].shape[0]
        return self.out_refs[a].at[pl.ds((4 * px + 2 * py + pc) * rows, rows), :]

    def _copy(self, a, k, block, to, src=None):
        return pltpu.make_async_remote_copy(
            src_ref=self._slot(a, *block) if src is None else src, dst_ref=self._slot(a, *block),
            send_sem=self.send_sems.at[7 * a + k], recv_sem=self.recv_sems.at[7 * a + k], device_id=to, device_id_type=MESH)

    def _mine(self):
        return [pltpu.make_async_copy(self.shard_refs[a], self._slot(a, *self.me), self.local_sems.at[a]) for a in range(self.n)]

    def _first(self):
        out = []
        for a in range(self.n):
            out.append(self._copy(a, 0, self.me, self.sibling, src=self.shard_refs[a]))
            out += [self._copy(a, 1 + j, self.me, (*chip, self.c), src=self.shard_refs[a]) for j, chip in enumerate(self.chips)]
        return out

    def _passed(self):
        return [self._copy(a, 4 + j, (*chip, self.c), self.sibling) for j, chip in enumerate(self.chips) for a in range(self.n)]

    def start(self):
        for cp in self._mine() + self._first():
            cp.start()

    def forward(self):
        passed = self._passed()
        for j, chip in enumerate(self.chips):
            for a in range(self.n):
                self._copy(a, 1 + j, (*chip, self.c), self.me).wait_recv()
                passed[j * self.n + a].start()

    def finish(self):
        for a in range(self.n):
            self._copy(a, 0, self.sibling, self.me).wait_recv()
            for j, chip in enumerate(self.chips):
                self._copy(a, 4 + j, (*chip, 1 - self.c), self.me).wait_recv()
        for cp in self._first() + self._passed():
            cp.wait_send()
        for cp in self._mine():
            cp.wait()

    @staticmethod
    def out_shapes(shards):
        return tuple(jax.ShapeDtypeStruct((N_DEV * s.shape[0], s.shape[1]), s.dtype) for s in shards)

    @staticmethod
    def semaphores(n):
        return [pltpu.SemaphoreType.DMA((7 * n,)), pltpu.SemaphoreType.DMA((7 * n,)), pltpu.SemaphoreType.DMA((n,))]


def _all_gather(shards):
    n = len(shards)

    def body(*refs):
        gather = _TwoLevelGather(refs[:n], refs[n:2 * n], *refs[2 * n:])
        gather.start()
        gather.forward()
        gather.finish()

    return pl.pallas_call(
        body, name="all_gather_w_in", out_shape=_TwoLevelGather.out_shapes(shards),
        in_specs=[HBM_SPEC] * n, out_specs=(HBM_SPEC,) * n, scratch_shapes=_TwoLevelGather.semaphores(n),
    )(*shards)


class _Exchange:
    def __init__(self, src_refs, recv_refs, rows, send_sems, recv_sems, local_sems):
        x, y, c = _mesh_pos()
        me = 4 * x + 2 * y + c
        self.own, self.sends, self.arrivals = [], [], []
        for a, (src, recv) in enumerate(zip(src_refs, recv_refs)):
            def blk(k, src=src, r=rows[a]):
                return src if r is None else src.at[pl.ds(k * r, r), :]

            self.own.append(pltpu.make_async_copy(blk(me), recv.at[me], local_sems.at[a]))
            for rel in range(1, N_DEV):
                px = 1 - x if rel & 4 else x
                py = 1 - y if rel & 2 else y
                pc = 1 - c if rel & 1 else c
                peer = 4 * px + 2 * py + pc
                sem = dict(send_sem=send_sems.at[7 * a + rel - 1], recv_sem=recv_sems.at[7 * a + rel - 1],
                           device_id=(px, py, pc), device_id_type=MESH)
                self.sends.append(pltpu.make_async_remote_copy(src_ref=blk(peer), dst_ref=recv.at[me], **sem))
                self.arrivals.append(pltpu.make_async_remote_copy(src_ref=blk(me), dst_ref=recv.at[peer], **sem))

    def start(self):
        for cp in self.own + self.sends:
            cp.start()

    def finish(self):
        for cp in self.arrivals:
            cp.wait_recv()
        for cp in self.sends:
            cp.wait_send()
        for cp in self.own:
            cp.wait()

    @staticmethod
    def out_shapes(srcs, rows):
        return tuple(jax.ShapeDtypeStruct((N_DEV, s.shape[0] if r is None else r, s.shape[1]), s.dtype) for s, r in zip(srcs, rows))

    @staticmethod
    def semaphores(n):
        return [pltpu.SemaphoreType.DMA((7 * n,)), pltpu.SemaphoreType.DMA((7 * n,)), pltpu.SemaphoreType.DMA((n,))]


def _sum_slots(slots, tile_rows, name):
    _, rows, cols = slots.shape
    tr = rows
    if N_DEV * rows * cols * slots.dtype.itemsize > 8 * 1024 * 1024:
        tr = next(c for c in (256, 192, 128, 64, 32, 16) if c <= tile_rows and rows % c == 0)

    def body(s_ref, o_ref):
        acc = s_ref[0].astype(F32)
        for k in range(1, N_DEV):
            acc = acc + s_ref[k].astype(F32)
        o_ref[...] = acc

    return pl.pallas_call(
        body, name=name, grid=(rows // tr,),
        in_specs=[pl.BlockSpec((N_DEV, tr, cols), lambda i: (0, i, 0))],
        out_specs=pl.BlockSpec((tr, cols), lambda i: (i, 0)),
        out_shape=jax.ShapeDtypeStruct((rows, cols), F32),
        compiler_params=_params(),
    )(slots)


def _gelu_and_grad(x):
    k0 = math.sqrt(2.0 / math.pi)
    k1 = 0.044715
    a = k1 * (x * x)
    half = 1.0 / (1.0 + jnp.exp((-2.0 * k0) * x * (1.0 + a)))
    g = x * half
    return g, half + g * (1.0 - half) * (2.0 * k0 + (6.0 * k0) * a)


def _ln_stats(r):
    mu = jnp.mean(r, axis=-1, keepdims=True)
    rc = r - mu
    var = jnp.mean(rc * rc, axis=-1, keepdims=True)
    rstd = lax.rsqrt(var + LN_EPS)
    return rc * rstd, rstd


def _ln_bwd(dxh, xh, rstd):
    return rstd * (dxh - jnp.mean(dxh, axis=-1, keepdims=True) - xh * jnp.mean(dxh * xh, axis=-1, keepdims=True))


def _colsum(a):
    return jnp.sum(a, axis=0, keepdims=True)


def _mm(a, b):
    return jnp.dot(a, b, preferred_element_type=F32)


def _halo_maps(tm, t, unit):
    per, last = tm // unit, t // unit - 1
    return (lambda i: (jnp.maximum(i * per - 1, 0), 0)), (lambda i: (jnp.minimum((i + 1) * per, last), 0))


def _proj_in(x2, w_int, b_gate, shards, dm, tm):
    t, d, npj = dm.T, dm.D, dm.NP
    cw = d // 2
    nt = t // tm
    n = len(shards)

    def body(x_ref, w_ref, bg_ref, *refs):
        p_ref = refs[n]
        gather = _TwoLevelGather(refs[:n], refs[n + 1:2 * n + 1], *refs[2 * n + 1:])
        i = pl.program_id(0)

        @pl.when(i == 0)
        def _():
            gather.start()

        @pl.when(i == nt // 2)
        def _():
            gather.forward()

        xb = x_ref[...].astype(BF16)
        for blk in range(npj // cw):
            lo = blk * cw
            acc = lax.dot_general(xb, w_ref[lo:lo + cw, :], NT_DIMS, preferred_element_type=F32)
            if lo >= dm.OFF_GA:
                acc = jax.nn.sigmoid(acc + bg_ref[:, lo - dm.OFF_GA:lo - dm.OFF_GA + cw])
            p_ref[:, lo:lo + cw] = acc.astype(BF16)

        @pl.when(i == nt - 1)
        def _():
            gather.finish()

    return pl.pallas_call(
        body, name="proj_in", grid=(nt,),
        in_specs=[pl.BlockSpec((tm, d), lambda i: (i, 0)), _resident((npj, d)), _resident((1, 2 * d))] + [HBM_SPEC] * n,
        out_specs=(pl.BlockSpec((tm, npj), lambda i: (i, 0)),) + (HBM_SPEC,) * n,
        out_shape=(jax.ShapeDtypeStruct((t, npj), BF16),) + _TwoLevelGather.out_shapes(shards),
        scratch_shapes=_TwoLevelGather.semaphores(n),
        compiler_params=_params(),
    )(x2, w_int, b_gate, *shards)


def _row_neighbours(ext, tm):
    h, n = F32_SUBLANES, ext.shape[0]
    return pltpu.roll(ext, 1, 0)[h:h + tm], pltpu.roll(ext, n - 1, 0)[h:h + tm]


def _spatial_mix(vn, ws_ref, bias_ref, mixed_ref, dm, tm):
    vb = vn.astype(BF16)
    for cc in range(tm // CHUNK):
        r0 = cc * CHUNK
        for h in range(dm.H):
            c0 = h * CHUNK
            m = _mm(ws_ref[h], vb[r0:r0 + CHUNK, c0:c0 + CHUNK])
            mixed_ref[r0:r0 + CHUNK, c0:c0 + CHUNK] = m + bias_ref[:, c0:c0 + CHUNK]
    return mixed_ref[...]


def _mixer_fwd(p, x2, w_pa, w_pb, w_o, conv_w8, vng, vnb, ws_b, bias_s, dm, tm):
    t, d, wa, npj = dm.T, dm.D, dm.WA, dm.NP
    nt = t // tm
    hb = BF16_SUBLANES
    prev_map, next_map = _halo_maps(tm, t, hb)

    def body(p_ref, pp_ref, pn_ref, x_ref, wpa_ref, wpb_ref, wo_ref, cw_ref, vng_ref, vnb_ref, ws_ref, bias_ref,
             ya_ref, yb_ref, r1_ref, gu_ref, dgu_ref, xhv_ref, rgv_ref, cv_ref, s_ref, mixed_ref):
        i = pl.program_id(0)

        def col(ref, lo, width):
            return ref[:, lo:lo + width].astype(F32)

        ch = col(p_ref, dm.OFF_CA, wa) * col(p_ref, dm.OFF_HA, wa)
        chp = (col(pp_ref, dm.OFF_CA, wa) * col(pp_ref, dm.OFF_HA, wa))[hb - F32_SUBLANES:hb]
        chn = (col(pn_ref, dm.OFF_CA, wa) * col(pn_ref, dm.OFF_HA, wa))[0:F32_SUBLANES]
        ch_e = jnp.concatenate([jnp.where(i == 0, 0.0, chp), ch, jnp.where(i == nt - 1, 0.0, chn)], axis=0)
        up, dn = _row_neighbours(ch_e, tm)
        cv = cw_ref[0:1, :] * up + cw_ref[1:2, :] * ch + cw_ref[2:3, :] * dn
        cv_ref[...] = cv.astype(BF16)
        ya = _mm((col(p_ref, 0, wa) * cv).astype(BF16), wpa_ref[...])
        gv, dgelu_v = _gelu_and_grad(col(p_ref, dm.OFF_VB, d))
        xhv, rstdv = _ln_stats(gv)
        xhv_ref[...] = xhv.astype(BF16)
        rgv_ref[...] = (rstdv * dgelu_v).astype(BF16)
        mixed = _spatial_mix(xhv * vng_ref[...] + vnb_ref[...], ws_ref, bias_ref, mixed_ref, dm, tm)
        gu, dgelu_u = _gelu_and_grad(col(p_ref, dm.OFF_UB, d))
        gu_ref[...] = gu.astype(BF16)
        dgu_ref[...] = dgelu_u.astype(BF16)
        yb = _mm((gu * mixed).astype(BF16), wpb_ref[...])
        sb = (col(p_ref, dm.OFF_GA, d) * ya + col(p_ref, dm.OFF_GB, d) * yb).astype(BF16)
        s_ref[...] = sb
        mix = _mm(sb, wo_ref[...])
        ya_ref[...] = ya.astype(BF16)
        yb_ref[...] = yb.astype(BF16)
        r1_ref[...] = ALPHA * x_ref[...] + mix

    tile = lambda i: (i, 0)
    return pl.pallas_call(
        body, name="mixer_fwd", grid=(nt,),
        in_specs=[pl.BlockSpec((tm, npj), tile), pl.BlockSpec((hb, npj), prev_map), pl.BlockSpec((hb, npj), next_map),
                  pl.BlockSpec((tm, d), tile), _resident((wa, d)), _resident((d, d)), _resident((d, d)),
                  _resident((F32_SUBLANES, wa)), _resident((1, d)), _resident((1, d)),
                  _resident((dm.H, CHUNK, CHUNK)), _resident((CHUNK, d))],
        out_specs=(pl.BlockSpec((tm, d), tile),) * 7 + (pl.BlockSpec((tm, wa), tile), pl.BlockSpec((tm, d), tile)),
        out_shape=(jax.ShapeDtypeStruct((t, d), BF16), jax.ShapeDtypeStruct((t, d), BF16), jax.ShapeDtypeStruct((t, d), F32))
        + (jax.ShapeDtypeStruct((t, d), BF16),) * 4 + (jax.ShapeDtypeStruct((t, wa), BF16), jax.ShapeDtypeStruct((t, d), BF16)),
        scratch_shapes=[pltpu.VMEM((tm, d), F32)],
        compiler_params=_params(),
    )(p, p, p, x2, w_pa, w_pb, w_o, conv_w8, vng, vnb, ws_b, bias_s)


def _ffn_fwd(r1, tgt, w1t, w2, ln1g, ln1b, ln2g, ln2b, dm, tm):
    t, d, dff = dm.T, dm.D, dm.DFF
    fc = dff // N_DEV

    def body(r1_ref, tgt_ref, w1t_ref, w2_ref, g1_ref, b1_ref, g2_ref, b2_ref, relu_ref, x1_ref, dr2_ref, dr2b_ref, sums_ref):
        @pl.when(pl.program_id(0) == 0)
        def _():
            sums_ref[...] = jnp.zeros_like(sums_ref)

        xh1, _ = _ln_stats(r1_ref[...])
        x1 = xh1 * g1_ref[...] + b1_ref[...]
        x1b = x1.astype(BF16)
        x1_ref[...] = x1b
        ffn = jnp.zeros((tm, d), F32)
        for k in range(dff // fc):
            ks = slice(k * fc, (k + 1) * fc)
            r = jnp.maximum(lax.dot_general(x1b, w1t_ref[ks, :], NT_DIMS, preferred_element_type=F32), 0.0)
            relu_ref[:, ks] = r.astype(BF16)
            ffn = ffn + _mm((r * r).astype(BF16), w2_ref[ks, :])
        xh2, rstd2 = _ln_stats(ALPHA * x1 + ffn)
        diff = xh2 * g2_ref[...] + b2_ref[...] - tgt_ref[...]
        dy = diff * (1.0 / d)
        dr2 = _ln_bwd(dy * g2_ref[...], xh2, rstd2)
        dr2_ref[...] = dr2
        dr2b_ref[...] = dr2.astype(BF16)
        sums_ref[0:1, :] += _colsum(diff * diff)
        sums_ref[1:2, :] += _colsum(dy * xh2)
        sums_ref[2:3, :] += _colsum(dy)

    tile = lambda i: (i, 0)
    vec = _resident((1, d))
    return pl.pallas_call(
        body, name="ffn_fwd", grid=(t // tm,),
        in_specs=[pl.BlockSpec((tm, d), tile), pl.BlockSpec((tm, d), tile), _resident((dff, d)), _resident((dff, d)),
                  vec, vec, vec, vec],
        out_specs=(pl.BlockSpec((tm, dff), tile), pl.BlockSpec((tm, d), tile), pl.BlockSpec((tm, d), tile),
                   pl.BlockSpec((tm, d), tile), pl.BlockSpec((F32_SUBLANES, d), lambda i: (0, 0))),
        out_shape=(jax.ShapeDtypeStruct((t, dff), BF16), jax.ShapeDtypeStruct((t, d), BF16), jax.ShapeDtypeStruct((t, d), F32),
                   jax.ShapeDtypeStruct((t, d), BF16), jax.ShapeDtypeStruct((F32_SUBLANES, d), F32)),
        compiler_params=_params(),
    )(r1, tgt, w1t, w2, ln1g, ln1b, ln2g, ln2b)


def _ffn_bwd(r1, relu, dr2, w2, w1t, ln1g, dm, tm):
    t, d, dff = dm.T, dm.D, dm.DFF
    fc = dff // N_DEV

    def body(r1_ref, relu_ref, dr2_ref, w2_ref, w1t_ref, g1_ref, dr1_ref, dh_ref, sums_ref):
        @pl.when(pl.program_id(0) == 0)
        def _():
            sums_ref[...] = jnp.zeros_like(sums_ref)

        xh1, rstd1 = _ln_stats(r1_ref[...])
        dr2 = dr2_ref[...]
        dr2b = dr2.astype(BF16)
        dx1 = ALPHA * dr2
        for k in range(dff // fc):
            ks = slice(k * fc, (k + 1) * fc)
            dact = lax.dot_general(dr2b, w2_ref[ks, :], NT_DIMS, preferred_element_type=F32)
            dhb = (dact * (2.0 * relu_ref[:, ks].astype(F32))).astype(BF16)
            dh_ref[:, ks] = dhb
            dx1 = dx1 + _mm(dhb, w1t_ref[ks, :])
        dr1_ref[...] = _ln_bwd(dx1 * g1_ref[...], xh1, rstd1)
        sums_ref[0:1, :] += _colsum(dx1 * xh1)
        sums_ref[1:2, :] += _colsum(dx1)

    tile = lambda i: (i, 0)
    return pl.pallas_call(
        body, name="ffn_bwd", grid=(t // tm,),
        in_specs=[pl.BlockSpec((tm, d), tile), pl.BlockSpec((tm, dff), tile), pl.BlockSpec((tm, d), tile),
                  _resident((dff, d)), _resident((dff, d)), _resident((1, d))],
        out_specs=(pl.BlockSpec((tm, d), tile), pl.BlockSpec((tm, dff), tile), pl.BlockSpec((F32_SUBLANES, d), lambda i: (0, 0))),
        out_shape=(jax.ShapeDtypeStruct((t, d), F32), jax.ShapeDtypeStruct((t, dff), BF16),
                   jax.ShapeDtypeStruct((F32_SUBLANES, d), F32)),
        compiler_params=_params(),
    )(r1, relu, dr2, w2, w1t, ln1g)


def _wgrad(pairs, tt, fb, name, xchg=()):
    n, m = len(pairs), len(xchg)
    t, f = pairs[0][0].shape
    d = pairs[0][1].shape[1]
    squares = [sq for _, _, sq in pairs]
    nj, ni = f // fb, t // tt
    xrows = [r for _, r in xchg]

    def body(*refs):
        ins, xsrc = refs[:2 * n], refs[2 * n:2 * n + m]
        outs, xrecv = refs[2 * n + m:3 * n + m], refs[3 * n + m:3 * n + 2 * m]
        accs, sems = refs[3 * n + 2 * m:4 * n + 2 * m], refs[4 * n + 2 * m:]
        j, i = pl.program_id(0), pl.program_id(1)
        exchange = _Exchange(xsrc, xrecv, xrows, *sems) if m else None

        if m:
            @pl.when(jnp.logical_and(j == 0, i == 0))
            def _():
                exchange.start()

        @pl.when(i == 0)
        def _():
            for acc in accs:
                acc[...] = jnp.zeros_like(acc)

        for q in range(n):
            lhs = ins[2 * q][...]
            if squares[q]:
                lf = lhs.astype(F32)
                lhs = (lf * lf).astype(BF16)
            accs[q][...] += lax.dot_general(lhs, ins[2 * q + 1][...].astype(BF16), TN_DIMS, preferred_element_type=F32)

        @pl.when(i == ni - 1)
        def _():
            for q in range(n):
                outs[q][...] = accs[q][...].astype(BF16)

        if m:
            @pl.when(jnp.logical_and(j == nj - 1, i == ni - 1))
            def _():
                exchange.finish()

    lhs_spec = pl.BlockSpec((tt, fb), lambda j, i: (i, j))
    rhs_spec = pl.BlockSpec((tt, d), lambda j, i: (i, 0))
    out_spec = pl.BlockSpec((fb, d), lambda j, i: (j, 0))
    xsrcs = [a_ for a_, _ in xchg]
    return pl.pallas_call(
        body, name=name, grid=(nj, ni),
        in_specs=[lhs_spec, rhs_spec] * n + [HBM_SPEC] * m, out_specs=(out_spec,) * n + (HBM_SPEC,) * m,
        out_shape=(jax.ShapeDtypeStruct((f, d), BF16),) * n + _Exchange.out_shapes(xsrcs, xrows),
        scratch_shapes=[pltpu.VMEM((fb, d), F32)] * n + (_Exchange.semaphores(m) if m else []),
        compiler_params=_params(("arbitrary", "arbitrary")),
    )(*[a_ for lhs, rhs, _ in pairs for a_ in (lhs, rhs)], *xsrcs)


def _mixer_bwd(dr1, p, ya, yb, gu_s, dgu_s, xhv_s, rgv_s, cv_s, w_ot, w_pat, w_pbt, conv_w8, vng, vnb, ws_b, wst_b, bias_s, head_sel, xchg, dm, tm):
    t, d, wa, npj = dm.T, dm.D, dm.WA, dm.NP
    nt = t // tm
    h8, hb = F32_SUBLANES, BF16_SUBLANES
    ext = tm + 2 * h8
    prev_f, next_f = _halo_maps(tm, t, h8)
    prev_b, next_b = _halo_maps(tm, t, hb)
    nx = len(xchg)
    xsrcs, xrows = [a_ for a_, _ in xchg], [r for _, r in xchg]

    def body(dr_ref, drp_ref, drn_ref, p_ref, pp_ref, pn_ref, ya_ref, yb_ref, gu_ref, dgu_ref, xhv_ref, rgv_ref, cv_ref,
             wot_ref, wpat_ref, wpbt_ref,
             cw_ref, vng_ref, vnb_ref, ws_ref, wst_ref, bias_ref, sel_ref,
             *refs):
        xsrc = refs[:nx]
        dp_ref, a_ref, dya_ref, bb_ref, dyb_ref, dws_ref, dbs_ref, dcw_ref, dbg_ref, dvn_ref = refs[nx:nx + 10]
        xrecv = refs[nx + 10:2 * nx + 10]
        mixed_ref, dvnm_ref = refs[2 * nx + 10:2 * nx + 12]
        exchange = _Exchange(xsrc, xrecv, xrows, *refs[2 * nx + 12:])
        i = pl.program_id(0)

        @pl.when(i == 0)
        def _():
            exchange.start()
            for ref in (dws_ref, dbs_ref, dcw_ref, dbg_ref, dvn_ref):
                ref[...] = jnp.zeros_like(ref)

        def col(ref, lo, width):
            return ref[:, lo:lo + width].astype(F32)

        def ext_rows(prev_blk, center, next_blk):
            return jnp.concatenate([prev_blk, center, next_blk], axis=0)

        def ext_col(lo, width):
            return ext_rows(col(pp_ref, lo, width)[hb - h8:hb], col(p_ref, lo, width), col(pn_ref, lo, width)[0:h8])

        row = lax.broadcasted_iota(jnp.int32, (ext, 1), 0) + (i * tm - h8)
        inside = jnp.logical_and(row >= 0, row < t)
        dr_e = ext_rows(drp_ref[...], dr_ref[...], drn_ref[...])
        ds_e = _mm(dr_e.astype(BF16), wot_ref[...])
        dya_e = ds_e * ext_col(dm.OFF_GA, d)
        da_e = _mm(dya_e.astype(BF16), wpat_ref[...])
        dcv_e = jnp.where(inside, da_e * ext_col(0, wa), 0.0)
        dcv, (dcv_up, dcv_dn) = dcv_e[h8:h8 + tm], _row_neighbours(dcv_e, tm)
        w0, w1, w2 = cw_ref[0:1, :], cw_ref[1:2, :], cw_ref[2:3, :]
        cv = cv_ref[...].astype(F32)
        dp_ref[:, 0:wa] = (da_e[h8:h8 + tm] * cv).astype(BF16)
        a_ref[...] = (col(p_ref, 0, wa) * cv).astype(BF16)
        dch = w0 * dcv_dn + w1 * dcv + w2 * dcv_up
        c_a, h_a = col(p_ref, dm.OFF_CA, wa), col(p_ref, dm.OFF_HA, wa)
        dp_ref[:, dm.OFF_CA:dm.OFF_CA + wa] = (dch * h_a).astype(BF16)
        dp_ref[:, dm.OFF_HA:dm.OFF_HA + wa] = (dch * c_a).astype(BF16)
        ch = c_a * h_a
        dcw_ref[0:1, :] += _colsum(dcv_dn * ch)
        dcw_ref[1:2, :] += _colsum(dcv * ch)
        dcw_ref[2:3, :] += _colsum(dcv_up * ch)
        dya_ref[...] = dya_e[h8:h8 + tm].astype(BF16)
        ds = ds_e[h8:h8 + tm]
        g_a, g_b = col(p_ref, dm.OFF_GA, d), col(p_ref, dm.OFF_GB, d)
        y_a, y_b = ya_ref[...].astype(F32), yb_ref[...].astype(F32)
        dzga = ds * y_a * g_a * (1.0 - g_a)
        dzgb = ds * y_b * g_b * (1.0 - g_b)
        dp_ref[:, dm.OFF_GA:dm.OFF_GA + d] = dzga.astype(BF16)
        dp_ref[:, dm.OFF_GB:dm.OFF_GB + d] = dzgb.astype(BF16)
        dbg_ref[0:1, 0:d] += _colsum(dzga)
        dbg_ref[0:1, d:2 * d] += _colsum(dzgb)
        dyb = (ds * g_b).astype(BF16)
        dyb_ref[...] = dyb
        xhv = xhv_ref[...].astype(F32)
        vn = xhv * vng_ref[...] + vnb_ref[...]
        mixed = _spatial_mix(vn, ws_ref, bias_ref, mixed_ref, dm, tm)
        gu = gu_ref[...].astype(F32)
        bb_ref[...] = (gu * mixed).astype(BF16)
        dbb = _mm(dyb, wpbt_ref[...])
        dp_ref[:, dm.OFF_UB:dm.OFF_UB + d] = (dbb * mixed * dgu_ref[...].astype(F32)).astype(BF16)
        dmb = (dbb * gu).astype(BF16)
        vb = vn.astype(BF16)
        dbs = jnp.zeros((CHUNK, CHUNK), F32)
        for cc in range(tm // CHUNK):
            r0 = cc * CHUNK
            dbs = dbs + _mm(dmb[r0:r0 + CHUNK, :], sel_ref[...])
            for h in range(dm.H):
                c0 = h * CHUNK
                blk = dmb[r0:r0 + CHUNK, c0:c0 + CHUNK]
                dvnm_ref[r0:r0 + CHUNK, c0:c0 + CHUNK] = _mm(wst_ref[h], blk)
                dws_ref[h] += lax.dot_general(blk, vb[r0:r0 + CHUNK, c0:c0 + CHUNK], NT_DIMS, preferred_element_type=F32)
        dbs_ref[...] += dbs
        dvn = dvnm_ref[...]
        dvn_ref[0:1, :] += _colsum(dvn * xhv)
        dvn_ref[1:2, :] += _colsum(dvn)
        dp_ref[:, dm.OFF_VB:dm.OFF_VB + d] = (_ln_bwd(dvn * vng_ref[...], xhv, rgv_ref[...].astype(F32))).astype(BF16)

        @pl.when(i == nt - 1)
        def _():
            exchange.finish()

    full = lambda i: (0, 0)
    tile = lambda i: (i, 0)
    hcc = _resident((dm.H, CHUNK, CHUNK))
    tok = lambda w, dt: jax.ShapeDtypeStruct((t, w), dt)
    return pl.pallas_call(
        body, name="mixer_bwd", grid=(nt,),
        in_specs=[pl.BlockSpec((tm, d), tile), pl.BlockSpec((h8, d), prev_f), pl.BlockSpec((h8, d), next_f),
                  pl.BlockSpec((tm, npj), tile), pl.BlockSpec((hb, npj), prev_b), pl.BlockSpec((hb, npj), next_b),
                  pl.BlockSpec((tm, d), tile), pl.BlockSpec((tm, d), tile), pl.BlockSpec((tm, d), tile),
                  pl.BlockSpec((tm, d), tile), pl.BlockSpec((tm, d), tile), pl.BlockSpec((tm, d), tile),
                  pl.BlockSpec((tm, wa), tile), _resident((d, d)), _resident((d, wa)), _resident((d, d)),
                  _resident((h8, wa)), _resident((1, d)), _resident((1, d)), hcc, hcc, _resident((CHUNK, d)),
                  _resident((d, CHUNK))] + [HBM_SPEC] * nx,
        out_specs=(pl.BlockSpec((tm, npj), tile), pl.BlockSpec((tm, wa), tile), pl.BlockSpec((tm, d), tile),
                   pl.BlockSpec((tm, d), tile), pl.BlockSpec((tm, d), tile),
                   pl.BlockSpec((dm.H, CHUNK, CHUNK), lambda i: (0, 0, 0)), pl.BlockSpec((CHUNK, CHUNK), full),
                   pl.BlockSpec((h8, wa), full), pl.BlockSpec((h8, 2 * d), full), pl.BlockSpec((h8, d), full))
        + (HBM_SPEC,) * nx,
        out_shape=(tok(npj, BF16), tok(wa, BF16), tok(d, BF16), tok(d, BF16), tok(d, BF16),
                   jax.ShapeDtypeStruct((dm.H, CHUNK, CHUNK), F32), jax.ShapeDtypeStruct((CHUNK, CHUNK), F32),
                   jax.ShapeDtypeStruct((h8, wa), F32), jax.ShapeDtypeStruct((h8, 2 * d), F32),
                   jax.ShapeDtypeStruct((h8, d), F32)) + _Exchange.out_shapes(xsrcs, xrows),
        scratch_shapes=[pltpu.VMEM((tm, d), F32), pltpu.VMEM((tm, d), F32)] + _Exchange.semaphores(nx),
        compiler_params=_params(),
    )(dr1, dr1, dr1, p, p, p, ya, yb, gu_s, dgu_s, xhv_s, rgv_s, cv_s, w_ot, w_pat, w_pbt, conv_w8, vng, vnb, ws_b, wst_b, bias_s, head_sel, *xsrcs)


def _input_grad(dp, dr1, w_int, xchg, dm, tm):
    t, d, npj = dm.T, dm.D, dm.NP
    nt = t // tm
    nx = len(xchg)
    xsrcs, xrows = [a_ for a_, _ in xchg], [r for _, r in xchg]

    def body(dp_ref, dr_ref, w_ref, *refs):
        dx_ref = refs[nx]
        exchange = _Exchange(refs[:nx], refs[nx + 1:2 * nx + 1], xrows, *refs[2 * nx + 1:])
        i = pl.program_id(0)

        @pl.when(i == 0)
        def _():
            exchange.start()

        dx_ref[...] = ALPHA * dr_ref[...] + _mm(dp_ref[...], w_ref[...])

        @pl.when(i == nt - 1)
        def _():
            exchange.finish()

    return pl.pallas_call(
        body, name="input_grad", grid=(nt,),
        in_specs=[pl.BlockSpec((tm, npj), lambda i: (i, 0)), pl.BlockSpec((tm, d), lambda i: (i, 0)), _resident((npj, d))]
        + [HBM_SPEC] * nx,
        out_specs=(pl.BlockSpec((tm, d), lambda i: (i, 0)),) + (HBM_SPEC,) * nx,
        out_shape=(jax.ShapeDtypeStruct((t, d), F32),) + _Exchange.out_shapes(xsrcs, xrows),
        scratch_shapes=_Exchange.semaphores(nx),
        compiler_params=_params(),
    )(dp, dr1, w_int, *xsrcs)


def _adamw_math(w, g, m, v):
    nm = ADAM_B1 * m + (1.0 - ADAM_B1) * g
    nv = ADAM_B2 * v + (1.0 - ADAM_B2) * (g * g)
    delta = -ADAM_LR * ((nm / (1.0 - ADAM_B1 ** ADAM_STEP)) / (jnp.sqrt(nv / (1.0 - ADAM_B2 ** ADAM_STEP)) + ADAM_EPS) + ADAM_WD * w)
    return delta, nm, nv


def _adamw(w, g, m, v, name):
    _, rows, cols = w.shape
    tr = 256 if rows % 256 == 0 else rows
    from_slots = g.ndim == 3

    def body(w_ref, g_ref, m_ref, v_ref, go_ref, d_ref, nm_ref, nv_ref):
        if from_slots:
            g_ = g_ref[0].astype(F32)
            for k in range(1, N_DEV):
                g_ = g_ + g_ref[k].astype(F32)
        else:
            g_ = g_ref[...]
        go_ref[0] = g_
        d_ref[0], nm_ref[0], nv_ref[0] = _adamw_math(w_ref[0], g_, m_ref[0], v_ref[0])

    spec = pl.BlockSpec((1, tr, cols), lambda i: (0, i, 0))
    g_spec = pl.BlockSpec((N_DEV, tr, cols), lambda i: (0, i, 0)) if from_slots else pl.BlockSpec((tr, cols), lambda i: (i, 0))
    shp = jax.ShapeDtypeStruct((1, rows, cols), F32)
    return pl.pallas_call(
        body, name=name, grid=(rows // tr,), in_specs=[spec, g_spec, spec, spec], out_specs=(spec,) * 4, out_shape=(shp,) * 4,
        compiler_params=_params(),
    )(w, g, m, v)


def _adamw_small(ws, gs, ms, vs):
    n = len(ws)

    def body(*refs):
        ins, outs = refs[:4 * n], refs[4 * n:]
        for k in range(n):
            w_ref, g_ref, m_ref, v_ref = ins[k], ins[n + k], ins[2 * n + k], ins[3 * n + k]
            outs[k][...], outs[n + k][...], outs[2 * n + k][...] = _adamw_math(w_ref[...], g_ref[...], m_ref[...], v_ref[...])

    shapes = tuple(jax.ShapeDtypeStruct(w.shape, F32) for w in ws)
    out = pl.pallas_call(body, name="adamw_small", out_shape=shapes * 3, compiler_params=_params(()))(*ws, *gs, *ms, *vs)
    return out[:n], out[n:2 * n], out[2 * n:]


def _to_slab(parts):
    flat = jnp.concatenate([q.reshape(-1) for q in parts])
    pad = (-flat.shape[0]) % (F32_SUBLANES * LANES)
    return jnp.pad(flat, (0, pad)).reshape(-1, LANES)


def _from_slab(slab, shapes):
    flat = slab.reshape(-1)
    out, off = [], 0
    for s in shapes:
        n = math.prod(s)
        out.append(flat[off:off + n].reshape(s))
        off += n
    return out


def kernel(x, w_in, b_gate, conv_w, v_norm_g, v_norm_b, w_s, b_s, w_pa, w_pb, w_o, ln1_g, ln1_b, w_ff1, w_ff2, ln2_g, ln2_b, loss_target, m_w_in, m_b_gate, m_conv_w, m_v_norm_g, m_v_norm_b, m_w_s, m_b_s, m_w_pa, m_w_pb, m_w_o, m_ln1_g, m_ln1_b, m_w_ff1, m_w_ff2, m_ln2_g, m_ln2_b, v_w_in, v_b_gate, v_conv_w, v_v_norm_g, v_v_norm_b, v_w_s, v_b_s, v_w_pa, v_w_pb, v_w_o, v_ln1_g, v_ln1_b, v_w_ff1, v_w_ff2, v_ln2_g, v_ln2_b):
    t, d = x.shape[1], x.shape[2]
    dm = _Dims(t, d)
    tm = 256 if t % 256 == 0 else CHUNK
    tm_big = 512 if t % 512 == 0 else tm
    tt = 1024 if t % 1024 == 0 else tm
    me = 4 * lax.axis_index("x") + 2 * lax.axis_index("y") + lax.axis_index("c")
    x2, tgt = x[0], loss_target[0]

    conv_bits = lax.bitcast_convert_type(conv_w[0], BF16).reshape(-1)
    conv_blk = jnp.pad(conv_bits, (0, dm.conv_rows * d - conv_bits.shape[0])).reshape(dm.conv_rows, d)
    w_int, conv_g = _all_gather([w_in[0].T.astype(BF16), conv_blk])
    wa8 = dm.WA // N_DEV
    conv_all = lax.bitcast_convert_type(conv_g.reshape(N_DEV, -1)[:, :3 * wa8 * 2].reshape(N_DEV, 3, wa8, 2), F32)
    conv_full = jnp.transpose(conv_all, (1, 0, 2)).reshape(3, dm.WA)
    conv_w8 = jnp.pad(conv_full, ((0, F32_SUBLANES - 3), (0, 0)))
    ws_b = w_s[0].astype(BF16)
    wst_b = jnp.transpose(w_s[0], (0, 2, 1)).astype(BF16)
    bias_s = jnp.repeat(b_s[0].T, CHUNK, axis=1)
    head_sel = (jnp.arange(d)[:, None] // CHUNK == jnp.arange(CHUNK)[None, :]).astype(BF16)

    p, w_pa_f, w_pb_f, w_o_f, w_1t, w_2 = _proj_in(
        x2, w_int, b_gate, [w_pa[0].astype(BF16), w_pb[0].astype(BF16), w_o[0].astype(BF16), w_ff1[0].T.astype(BF16),
                            w_ff2[0].astype(BF16)], dm, tm_big)
    ya, yb, r1, gu_s, dgu_s, xhv_s, rgv_s, cv_s, s_m = _mixer_fwd(
        p, x2, w_pa_f, w_pb_f, w_o_f, conv_w8, v_norm_g, v_norm_b, ws_b, bias_s, dm, tm)
    relu, x1b, dr2, dr2b, sums2 = _ffn_fwd(r1, tgt, w_1t, w_2, ln1_g, ln1_b, ln2_g, ln2_b, dm, tm_big)
    dr1, dh1, sums1 = _ffn_bwd(r1, relu, dr2, w_2, w_1t, ln1_g, dm, tm_big)
    fb = min(1024, dm.DFF)
    rows = dm.shard_rows
    g_ff1t, g_ff2 = _wgrad([(dh1, x1b, False), (relu, dr2b, True)], tt, fb, "ffn_wgrad")
    dp, a_m, dya, bb_m, dyb, g_ws, g_bs_t, g_cw, g_bg, g_vn, got_ff1t, got_ff2 = _mixer_bwd(
        dr1, p, ya, yb, gu_s, dgu_s, xhv_s, rgv_s, cv_s, w_o_f.T, w_pa_f.T, w_pb_f.T, conv_w8, v_norm_g, v_norm_b, ws_b, wst_b, bias_s, head_sel,
        [(g_ff1t, rows[4]), (g_ff2, rows[5])], dm, tm)
    (g_pa,) = _wgrad([(a_m, dya, False)], tt, dm.WA, "w_pa_grad")
    g_o, g_pb = _wgrad([(s_m, dr1, False), (bb_m, dyb, False)], tt, d, "w_o_pb_grad")
    g_int, got_pa, got_pb, got_o = _wgrad([(dp, x2, False)], tt, 17 * LANES, "w_in_grad",
                                          xchg=[(g_pa, rows[1]), (g_pb, rows[2]), (g_o, rows[3])])
    small_parts = [g_bg[0], g_cw[0:3], g_vn[0], g_vn[1], g_ws, g_bs_t[:, :dm.H].T,
                   sums1[0], sums1[1], sums2[1], sums2[2], sums2[0]]
    grad_x, got_int, got_s = _input_grad(dp, dr1, w_int, [(g_int, rows[0]), (_to_slab(small_parts), None)], dm, tm_big)

    ssum = _sum_slots(got_s, 256, "sum_small")
    (s_bg, s_cw, s_vng, s_vnb, s_ws, s_bs, s_l1g, s_l1b, s_l2g, s_l2b, s_sq) = _from_slab(
        ssum, [(1, 2 * d), (3, dm.WA), (1, d), (1, d), (1, dm.H, CHUNK, CHUNK), (1, dm.H, CHUNK),
               (1, d), (1, d), (1, d), (1, d), (d,)])
    loss = 0.5 * jnp.sum(s_sq) / d
    s_cw = lax.dynamic_slice(s_cw, (0, me * wa8), (3, wa8))[None]
    big_g = [_sum_slots(got_int, 256, "sum_grads_w_in").T, got_pa, got_pb, got_o,
             _sum_slots(got_ff1t, 256, "sum_grads_w_ff1").T, got_ff2]
    big_w = [(w_in, m_w_in, v_w_in), (w_pa, m_w_pa, v_w_pa), (w_pb, m_w_pb, v_w_pb), (w_o, m_w_o, v_w_o),
             (w_ff1, m_w_ff1, v_w_ff1), (w_ff2, m_w_ff2, v_w_ff2)]
    big_out = [_adamw(w, g, m, v, "adamw_%d" % k) for k, ((w, m, v), g) in enumerate(zip(big_w, big_g))]
    small_w = [(b_gate, m_b_gate, v_b_gate), (conv_w, m_conv_w, v_conv_w), (v_norm_g, m_v_norm_g, v_v_norm_g),
               (v_norm_b, m_v_norm_b, v_v_norm_b), (w_s, m_w_s, v_w_s), (b_s, m_b_s, v_b_s), (ln1_g, m_ln1_g, v_ln1_g),
               (ln1_b, m_ln1_b, v_ln1_b), (ln2_g, m_ln2_g, v_ln2_g), (ln2_b, m_ln2_b, v_ln2_b)]
    small_g = [s_bg, s_cw, s_vng, s_vnb, s_ws, s_bs, s_l1g, s_l1b, s_l2g, s_l2b]
    small_out = list(zip(*_adamw_small([w for w, _, _ in small_w], small_g, [m for _, m, _ in small_w],
                                       [v for _, _, v in small_w])))

    order = [("b", 0), ("s", 0), ("s", 1), ("s", 2), ("s", 3), ("s", 4), ("s", 5), ("b", 1), ("b", 2), ("b", 3),
             ("s", 6), ("s", 7), ("b", 4), ("b", 5), ("s", 8), ("s", 9)]
    grads, deltas, new_m, new_v = [], [], [], []
    for kind, k in order:
        if kind == "b":
            g, dl, nm, nv = big_out[k]
        else:
            g, (dl, nm, nv) = small_g[k], small_out[k]
        grads.append(g)
        deltas.append(dl)
        new_m.append(nm)
        new_v.append(nv)
    return (loss, grad_x[None], *grads, *deltas, *new_m, *new_v)
```

```python
import math

import jax
import jax.numpy as jnp
from jax import lax
from jax.experimental import pallas as pl
from jax.experimental.pallas import tpu as pltpu

F32 = jnp.float32
BF16 = jnp.bfloat16
N_DEV = 8
CHUNK = 128
LN_EPS = 1e-5
ALPHA = 2.0 ** 0.25
ADAM_LR, ADAM_B1, ADAM_B2, ADAM_EPS, ADAM_WD, ADAM_STEP = 0.001, 0.9, 0.999, 1e-08, 0.01, 10
F32_SUBLANES = 8
BF16_SUBLANES = 16
LANES = 128
VMEM_LIMIT = 56 * 1024 * 1024
MESH = pl.DeviceIdType.MESH
NT_DIMS = (((1,), (1,)), ((), ()))
TN_DIMS = (((0,), (0,)), ((), ()))
HBM_SPEC = pl.BlockSpec(memory_space=pltpu.HBM)


class _Dims:
    def __init__(self, t, d):
        self.T, self.D = t, d
        self.WA = 3 * d // 2
        self.NP = 3 * self.WA + 4 * d
        self.DFF = 4 * d
        self.H = d // CHUNK
        self.OFF_CA, self.OFF_HA = self.WA, 2 * self.WA
        self.OFF_UB = 3 * self.WA
        self.OFF_VB = self.OFF_UB + d
        self.OFF_GA = self.OFF_VB + d
        self.OFF_GB = self.OFF_GA + d
        self.shard_rows = (self.NP // N_DEV, self.WA // N_DEV, d // N_DEV, d // N_DEV, self.DFF // N_DEV, self.DFF // N_DEV)
        self.conv_rows = BF16_SUBLANES * max(1, -(-(3 * (self.WA // N_DEV) * 2) // (BF16_SUBLANES * d)))


def _params(sem=("arbitrary",), vmem=VMEM_LIMIT):
    return pltpu.CompilerParams(dimension_semantics=sem, vmem_limit_bytes=vmem)


def _mesh_pos():
    return lax.axis_index("x"), lax.axis_index("y"), lax.axis_index("c")


def _resident(shape):
    zeros = (0,) * len(shape)
    return pl.BlockSpec(shape, lambda *_: zeros, pipeline_mode=pl.Buffered(1))


class _TwoLevelGather:
    def __init__(self, shard_refs, out_refs, send_sems, recv_sems, local_sems):
        self.n = len(shard_refs)
        self.shard_refs, self.out_refs = shard_refs, out_refs
        self.send_sems, self.recv_sems, self.local_sems = send_sems, recv_sems, local_sems
        x, y, c = _mesh_pos()
        self.c = c
        self.me, self.sibling = (x, y, c), (x, y, 1 - c)
        self.chips = [(1 - x, y), (x, 1 - y), (1 - x, 1 - y)]

    def _slot(self, a, px, py, pc):
        rows = self.shard_refs[a].shape[0]
        return self.out_refs[a].at[pl.ds((4 * px + 2 * py + pc) * rows, rows), :]

    def _copy(self, a, k, block, to, src=None):
        return pltpu.make_async_remote_copy(
            src_ref=self._slot(a, *block) if src is None else src, dst_ref=self._slot(a, *block),
            send_sem=self.send_sems.at[7 * a + k], recv_sem=self.recv_sems.at[7 * a + k], device_id=to, device_id_type=MESH)

    def _mine(self):
        return [pltpu.make_async_copy(self.shard_refs[a], self._slot(a, *self.me), self.local_sems.at[a]) for a in range(self.n)]

    def _first(self):
        out = []
        for a in range(self.n):
            out.append(self._copy(a, 0, self.me, self.sibling, src=self.shard_refs[a]))
            out += [self._copy(a, 1 + j, self.me, (*chip, self.c), src=self.shard_refs[a]) for j, chip in enumerate(self.chips)]
        return out

    def _passed(self):
        return [self._copy(a, 4 + j, (*chip, self.c), self.sibling) for j, chip in enumerate(self.chips) for a in range(self.n)]

    def start(self):
        for cp in self._mine() + self._first():
            cp.start()

    def forward(self):
        passed = self._passed()
        for j, chip in enumerate(self.chips):
            for a in range(self.n):
                self._copy(a, 1 + j, (*chip, self.c), self.me).wait_recv()
                passed[j * self.n + a].start()

    def finish(self):
        for a in range(self.n):
            self._copy(a, 0, self.sibling, self.me).wait_recv()
            for j, chip in enumerate(self.chips):
                self._copy(a, 4 + j, (*chip, 1 - self.c), self.me).wait_recv()
        for cp in self._first() + self._passed():
            cp.wait_send()
        for cp in self._mine():
            cp.wait()

    @staticmethod
    def out_shapes(shards):
        return tuple(jax.ShapeDtypeStruct((N_DEV * s.shape[0], s.shape[1]), s.dtype) for s in shards)

    @staticmethod
    def semaphores(n):
        return [pltpu.SemaphoreType.DMA((7 * n,)), pltpu.SemaphoreType.DMA((7 * n,)), pltpu.SemaphoreType.DMA((n,))]


def _all_gather(shards):
    n = len(shards)

    def body(*refs):
        gather = _TwoLevelGather(refs[:n], refs[n:2 * n], *refs[2 * n:])
        gather.start()
        gather.forward()
        gather.finish()

    return pl.pallas_call(
        body, name="all_gather_w_in", out_shape=_TwoLevelGather.out_shapes(shards),
        in_specs=[HBM_SPEC] * n, out_specs=(HBM_SPEC,) * n, scratch_shapes=_TwoLevelGather.semaphores(n),
    )(*shards)


class _Exchange:
    def __init__(self, src_refs, recv_refs, rows, send_sems, recv_sems, local_sems):
        x, y, c = _mesh_pos()
        me = 4 * x + 2 * y + c
        self.own, self.sends, self.arrivals = [], [], []
        for a, (src, recv) in enumerate(zip(src_refs, recv_refs)):
            def blk(k, src=src, r=rows[a]):
                return src if r is None else src.at[pl.ds(k * r, r), :]

            self.own.append(pltpu.make_async_copy(blk(me), recv.at[me], local_sems.at[a]))
            for rel in range(1, N_DEV):
                px = 1 - x if rel & 4 else x
                py = 1 - y if rel & 2 else y
                pc = 1 - c if rel & 1 else c
                peer = 4 * px + 2 * py + pc
                sem = dict(send_sem=send_sems.at[7 * a + rel - 1], recv_sem=recv_sems.at[7 * a + rel - 1],
                           device_id=(px, py, pc), device_id_type=MESH)
                self.sends.append(pltpu.make_async_remote_copy(src_ref=blk(peer), dst_ref=recv.at[me], **sem))
                self.arrivals.append(pltpu.make_async_remote_copy(src_ref=blk(me), dst_ref=recv.at[peer], **sem))

    def start(self):
        for cp in self.own + self.sends:
            cp.start()

    def finish(self):
        for cp in self.arrivals:
            cp.wait_recv()
        for cp in self.sends:
            cp.wait_send()
        for cp in self.own:
            cp.wait()

    @staticmethod
    def out_shapes(srcs, rows):
        return tuple(jax.ShapeDtypeStruct((N_DEV, s.shape[0] if r is None else r, s.shape[1]), s.dtype) for s, r in zip(srcs, rows))

    @staticmethod
    def semaphores(n):
        return [pltpu.SemaphoreType.DMA((7 * n,)), pltpu.SemaphoreType.DMA((7 * n,)), pltpu.SemaphoreType.DMA((n,))]


def _sum_slots(slots, tile_rows, name):
    _, rows, cols = slots.shape
    tr = rows
    if N_DEV * rows * cols * slots.dtype.itemsize > 8 * 1024 * 1024:
        tr = next(c for c in (256, 192, 128, 64, 32, 16) if c <= tile_rows and rows % c == 0)

    def body(s_ref, o_ref):
        acc = s_ref[0].astype(F32)
        for k in range(1, N_DEV):
            acc = acc + s_ref[k].astype(F32)
        o_ref[...] = acc

    return pl.pallas_call(
        body, name=name, grid=(rows // tr,),
        in_specs=[pl.BlockSpec((N_DEV, tr, cols), lambda i: (0, i, 0))],
        out_specs=pl.BlockSpec((tr, cols), lambda i: (i, 0)),
        out_shape=jax.ShapeDtypeStruct((rows, cols), F32),
        compiler_params=_params(),
    )(slots)


def _gelu_and_grad(x):
    k0 = math.sqrt(2.0 / math.pi)
    k1 = 0.044715
    a = k1 * (x * x)
    half = 1.0 / (1.0 + jnp.exp((-2.0 * k0) * x * (1.0 + a)))
    g = x * half
    return g, half + g * (1.0 - half) * (2.0 * k0 + (6.0 * k0) * a)


def _ln_stats(r):
    mu = jnp.mean(r, axis=-1, keepdims=True)
    rc = r - mu
    var = jnp.mean(rc * rc, axis=-1, keepdims=True)
    rstd = lax.rsqrt(var + LN_EPS)
    return rc * rstd, rstd


def _ln_bwd(dxh, xh, rstd):
    return rstd * (dxh - jnp.mean(dxh, axis=-1, keepdims=True) - xh * jnp.mean(dxh * xh, axis=-1, keepdims=True))


def _colsum(a):
    return jnp.sum(a, axis=0, keepdims=True)


def _mm(a, b):
    return jnp.dot(a, b, preferred_element_type=F32)


def _halo_maps(tm, t, unit):
    per, last = tm // unit, t // unit - 1
    return (lambda i: (jnp.maximum(i * per - 1, 0), 0)), (lambda i: (jnp.minimum((i + 1) * per, last), 0))


def _proj_in(x2, w_int, b_gate, shards, dm, tm):
    t, d, npj = dm.T, dm.D, dm.NP
    cw = d // 2
    nt = t // tm
    n = len(shards)

    def body(x_ref, w_ref, bg_ref, *refs):
        p_ref = refs[n]
        gather = _TwoLevelGather(refs[:n], refs[n + 1:2 * n + 1], *refs[2 * n + 1:])
        i = pl.program_id(0)

        @pl.when(i == 0)
        def _():
            gather.start()

        @pl.when(i == nt // 2)
        def _():
            gather.forward()

        xb = x_ref[...].astype(BF16)
        for blk in range(npj // cw):
            lo = blk * cw
            acc = lax.dot_general(xb, w_ref[lo:lo + cw, :], NT_DIMS, preferred_element_type=F32)
            if lo >= dm.OFF_GA:
                acc = jax.nn.sigmoid(acc + bg_ref[:, lo - dm.OFF_GA:lo - dm.OFF_GA + cw])
            p_ref[:, lo:lo + cw] = acc.astype(BF16)

        @pl.when(i == nt - 1)
        def _():
            gather.finish()

    return pl.pallas_call(
        body, name="proj_in", grid=(nt,),
        in_specs=[pl.BlockSpec((tm, d), lambda i: (i, 0)), _resident((npj, d)), _resident((1, 2 * d))] + [HBM_SPEC] * n,
        out_specs=(pl.BlockSpec((tm, npj), lambda i: (i, 0)),) + (HBM_SPEC,) * n,
        out_shape=(jax.ShapeDtypeStruct((t, npj), BF16),) + _TwoLevelGather.out_shapes(shards),
        scratch_shapes=_TwoLevelGather.semaphores(n),
        compiler_params=_params(),
    )(x2, w_int, b_gate, *shards)


def _row_neighbours(ext, tm):
    h, n = F32_SUBLANES, ext.shape[0]
    return pltpu.roll(ext, 1, 0)[h:h + tm], pltpu.roll(ext, n - 1, 0)[h:h + tm]


def _spatial_mix(vn, ws_ref, bias_ref, mixed_ref, dm, tm):
    vb = vn.astype(BF16)
    for cc in range(tm // CHUNK):
        r0 = cc * CHUNK
        for h in range(dm.H):
            c0 = h * CHUNK
            m = _mm(ws_ref[h], vb[r0:r0 + CHUNK, c0:c0 + CHUNK])
            mixed_ref[r0:r0 + CHUNK, c0:c0 + CHUNK] = m + bias_ref[:, c0:c0 + CHUNK]
    return mixed_ref[...]


def _mixer_fwd(p, x2, w_pa, w_pb, w_o, conv_w8, vng, vnb, ws_b, bias_s, dm, tm):
    t, d, wa, npj = dm.T, dm.D, dm.WA, dm.NP
    nt = t // tm
    hb = BF16_SUBLANES
    prev_map, next_map = _halo_maps(tm, t, hb)

    def body(p_ref, pp_ref, pn_ref, x_ref, wpa_ref, wpb_ref, wo_ref, cw_ref, vng_ref, vnb_ref, ws_ref, bias_ref,
             ya_ref, yb_ref, r1_ref, gu_ref, dgu_ref, xhv_ref, rgv_ref, cv_ref, s_ref, mixed_ref):
        i = pl.program_id(0)

        def col(ref, lo, width):
            return ref[:, lo:lo + width].astype(F32)

        ch = col(p_ref, dm.OFF_CA, wa) * col(p_ref, dm.OFF_HA, wa)
        chp = (col(pp_ref, dm.OFF_CA, wa) * col(pp_ref, dm.OFF_HA, wa))[hb - F32_SUBLANES:hb]
        chn = (col(pn_ref, dm.OFF_CA, wa) * col(pn_ref, dm.OFF_HA, wa))[0:F32_SUBLANES]
        ch_e = jnp.concatenate([jnp.where(i == 0, 0.0, chp), ch, jnp.where(i == nt - 1, 0.0, chn)], axis=0)
        up, dn = _row_neighbours(ch_e, tm)
        cv = cw_ref[0:1, :] * up + cw_ref[1:2, :] * ch + cw_ref[2:3, :] * dn
        cv_ref[...] = cv.astype(BF16)
        ya = _mm((col(p_ref, 0, wa) * cv).astype(BF16), wpa_ref[...])
        gv, dgelu_v = _gelu_and_grad(col(p_ref, dm.OFF_VB, d))
        xhv, rstdv = _ln_stats(gv)
        xhv_ref[...] = xhv.astype(BF16)
        rgv_ref[...] = (rstdv * dgelu_v).astype(BF16)
        mixed = _spatial_mix(xhv * vng_ref[...] + vnb_ref[...], ws_ref, bias_ref, mixed_ref, dm, tm)
        gu, dgelu_u = _gelu_and_grad(col(p_ref, dm.OFF_UB, d))
        gu_ref[...] = gu.astype(BF16)
        dgu_ref[...] = dgelu_u.astype(BF16)
        yb = _mm((gu * mixed).astype(BF16), wpb_ref[...])
        sb = (col(p_ref, dm.OFF_GA, d) * ya + col(p_ref, dm.OFF_GB, d) * yb).astype(BF16)
        s_ref[...] = sb
        mix = _mm(sb, wo_ref[...])
        ya_ref[...] = ya.astype(BF16)
        yb_ref[...] = yb.astype(BF16)
        r1_ref[...] = ALPHA * x_ref[...] + mix

    tile = lambda i: (i, 0)
    return pl.pallas_call(
        body, name="mixer_fwd", grid=(nt,),
        in_specs=[pl.BlockSpec((tm, npj), tile), pl.BlockSpec((hb, npj), prev_map), pl.BlockSpec((hb, npj), next_map),
                  pl.BlockSpec((tm, d), tile), _resident((wa, d)), _resident((d, d)), _resident((d, d)),
                  _resident((F32_SUBLANES, wa)), _resident((1, d)), _resident((1, d)),
                  _resident((dm.H, CHUNK, CHUNK)), _resident((CHUNK, d))],
        out_specs=(pl.BlockSpec((tm, d), tile),) * 7 + (pl.BlockSpec((tm, wa), tile), pl.BlockSpec((tm, d), tile)),
        out_shape=(jax.ShapeDtypeStruct((t, d), BF16), jax.ShapeDtypeStruct((t, d), BF16), jax.ShapeDtypeStruct((t, d), F32))
        + (jax.ShapeDtypeStruct((t, d), BF16),) * 4 + (jax.ShapeDtypeStruct((t, wa), BF16), jax.ShapeDtypeStruct((t, d), BF16)),
        scratch_shapes=[pltpu.VMEM((tm, d), F32)],
        compiler_params=_params(),
    )(p, p, p, x2, w_pa, w_pb, w_o, conv_w8, vng, vnb, ws_b, bias_s)


def _ffn_fwd(r1, tgt, w1t, w2, ln1g, ln1b, ln2g, ln2b, dm, tm):
    t, d, dff = dm.T, dm.D, dm.DFF
    fc = dff // N_DEV

    def body(r1_ref, tgt_ref, w1t_ref, w2_ref, g1_ref, b1_ref, g2_ref, b2_ref, relu_ref, x1_ref, dr2_ref, dr2b_ref, sums_ref):
        @pl.when(pl.program_id(0) == 0)
        def _():
            sums_ref[...] = jnp.zeros_like(sums_ref)

        xh1, _ = _ln_stats(r1_ref[...])
        x1 = xh1 * g1_ref[...] + b1_ref[...]
        x1b = x1.astype(BF16)
        x1_ref[...] = x1b
        ffn = jnp.zeros((tm, d), F32)
        for k in range(dff // fc):
            ks = slice(k * fc, (k + 1) * fc)
            r = jnp.maximum(lax.dot_general(x1b, w1t_ref[ks, :], NT_DIMS, preferred_element_type=F32), 0.0)
            relu_ref[:, ks] = r.astype(BF16)
            ffn = ffn + _mm((r * r).astype(BF16), w2_ref[ks, :])
        xh2, rstd2 = _ln_stats(ALPHA * x1 + ffn)
        diff = xh2 * g2_ref[...] + b2_ref[...] - tgt_ref[...]
        dy = diff * (1.0 / d)
        dr2 = _ln_bwd(dy * g2_ref[...], xh2, rstd2)
        dr2_ref[...] = dr2
        dr2b_ref[...] = dr2.astype(BF16)
        sums_ref[0:1, :] += _colsum(diff * diff)
        sums_ref[1:2, :] += _colsum(dy * xh2)
        sums_ref[2:3, :] += _colsum(dy)

    tile = lambda i: (i, 0)
    vec = _resident((1, d))
    return pl.pallas_call(
        body, name="ffn_fwd", grid=(t // tm,),
        in_specs=[pl.BlockSpec((tm, d), tile), pl.BlockSpec((tm, d), tile), _resident((dff, d)), _resident((dff, d)),
                  vec, vec, vec, vec],
        out_specs=(pl.BlockSpec((tm, dff), tile), pl.BlockSpec((tm, d), tile), pl.BlockSpec((tm, d), tile),
                   pl.BlockSpec((tm, d), tile), pl.BlockSpec((F32_SUBLANES, d), lambda i: (0, 0))),
        out_shape=(jax.ShapeDtypeStruct((t, dff), BF16), jax.ShapeDtypeStruct((t, d), BF16), jax.ShapeDtypeStruct((t, d), F32),
                   jax.ShapeDtypeStruct((t, d), BF16), jax.ShapeDtypeStruct((F32_SUBLANES, d), F32)),
        compiler_params=_params(),
    )(r1, tgt, w1t, w2, ln1g, ln1b, ln2g, ln2b)


def _ffn_bwd(r1, relu, dr2, w2, w1t, ln1g, dm, tm):
    t, d, dff = dm.T, dm.D, dm.DFF
    fc = dff // N_DEV

    def body(r1_ref, relu_ref, dr2_ref, w2_ref, w1t_ref, g1_ref, dr1_ref, dh_ref, sums_ref):
        @pl.when(pl.program_id(0) == 0)
        def _():
            sums_ref[...] = jnp.zeros_like(sums_ref)

        xh1, rstd1 = _ln_stats(r1_ref[...])
        dr2 = dr2_ref[...]
        dr2b = dr2.astype(BF16)
        dx1 = ALPHA * dr2
        for k in range(dff // fc):
            ks = slice(k * fc, (k + 1) * fc)
            dact = lax.dot_general(dr2b, w2_ref[ks, :], NT_DIMS, preferred_element_type=F32)
            dhb = (dact * (2.0 * relu_ref[:, ks].astype(F32))).astype(BF16)
            dh_ref[:, ks] = dhb
            dx1 = dx1 + _mm(dhb, w1t_ref[ks, :])
        dr1_ref[...] = _ln_bwd(dx1 * g1_ref[...], xh1, rstd1)
        sums_ref[0:1, :] += _colsum(dx1 * xh1)
        sums_ref[1:2, :] += _colsum(dx1)

    tile = lambda i: (i, 0)
    return pl.pallas_call(
        body, name="ffn_bwd", grid=(t // tm,),
        in_specs=[pl.BlockSpec((tm, d), tile), pl.BlockSpec((tm, dff), tile), pl.BlockSpec((tm, d), tile),
                  _resident((dff, d)), _resident((dff, d)), _resident((1, d))],
        out_specs=(pl.BlockSpec((tm, d), tile), pl.BlockSpec((tm, dff), tile), pl.BlockSpec((F32_SUBLANES, d), lambda i: (0, 0))),
        out_shape=(jax.ShapeDtypeStruct((t, d), F32), jax.ShapeDtypeStruct((t, dff), BF16),
                   jax.ShapeDtypeStruct((F32_SUBLANES, d), F32)),
        compiler_params=_params(),
    )(r1, relu, dr2, w2, w1t, ln1g)


def _wgrad(pairs, tt, fb, name, xchg=()):
    n, m = len(pairs), len(xchg)
    t, f = pairs[0][0].shape
    d = pairs[0][1].shape[1]
    squares = [sq for _, _, sq in pairs]
    nj, ni = f // fb, t // tt
    xrows = [r for _, r in xchg]

    def body(*refs):
        ins, xsrc = refs[:2 * n], refs[2 * n:2 * n + m]
        outs, xrecv = refs[2 * n + m:3 * n + m], refs[3 * n + m:3 * n + 2 * m]
        accs, sems = refs[3 * n + 2 * m:4 * n + 2 * m], refs[4 * n + 2 * m:]
        j, i = pl.program_id(0), pl.program_id(1)
        exchange = _Exchange(xsrc, xrecv, xrows, *sems) if m else None

        if m:
            @pl.when(jnp.logical_and(j == 0, i == 0))
            def _():
                exchange.start()

        @pl.when(i == 0)
        def _():
            for acc in accs:
                acc[...] = jnp.zeros_like(acc)

        for q in range(n):
            lhs = ins[2 * q][...]
            if squares[q]:
                lf = lhs.astype(F32)
                lhs = (lf * lf).astype(BF16)
            accs[q][...] += lax.dot_general(lhs, ins[2 * q + 1][...].astype(BF16), TN_DIMS, preferred_element_type=F32)

        @pl.when(i == ni - 1)
        def _():
            for q in range(n):
                outs[q][...] = accs[q][...].astype(BF16)

        if m:
            @pl.when(jnp.logical_and(j == nj - 1, i == ni - 1))
            def _():
                exchange.finish()

    lhs_spec = pl.BlockSpec((tt, fb), lambda j, i: (i, j))
    rhs_spec = pl.BlockSpec((tt, d), lambda j, i: (i, 0))
    out_spec = pl.BlockSpec((fb, d), lambda j, i: (j, 0))
    xsrcs = [a_ for a_, _ in xchg]
    return pl.pallas_call(
        body, name=name, grid=(nj, ni),
        in_specs=[lhs_spec, rhs_spec] * n + [HBM_SPEC] * m, out_specs=(out_spec,) * n + (HBM_SPEC,) * m,
        out_shape=(jax.ShapeDtypeStruct((f, d), BF16),) * n + _Exchange.out_shapes(xsrcs, xrows),
        scratch_shapes=[pltpu.VMEM((fb, d), F32)] * n + (_Exchange.semaphores(m) if m else []),
        compiler_params=_params(("arbitrary", "arbitrary")),
    )(*[a_ for lhs, rhs, _ in pairs for a_ in (lhs, rhs)], *xsrcs)


def _mixer_bwd(dr1, p, ya, yb, gu_s, dgu_s, xhv_s, rgv_s, cv_s, w_ot, w_pat, w_pbt, conv_w8, vng, vnb, ws_b, wst_b, bias_s, head_sel, xchg, dm, tm):
    t, d, wa, npj = dm.T, dm.D, dm.WA, dm.NP
    nt = t // tm
    h8, hb = F32_SUBLANES, BF16_SUBLANES
    ext = tm + 2 * h8
    prev_f, next_f = _halo_maps(tm, t, h8)
    prev_b, next_b = _halo_maps(tm, t, hb)
    nx = len(xchg)
    xsrcs, xrows = [a_ for a_, _ in xchg], [r for _, r in xchg]

    def body(dr_ref, drp_ref, drn_ref, p_ref, pp_ref, pn_ref, ya_ref, yb_ref, gu_ref, dgu_ref, xhv_ref, rgv_ref, cv_ref,
             wot_ref, wpat_ref, wpbt_ref,
             cw_ref, vng_ref, vnb_ref, ws_ref, wst_ref, bias_ref, sel_ref,
             *refs):
        xsrc = refs[:nx]
        dp_ref, a_ref, dya_ref, bb_ref, dyb_ref, dws_ref, dbs_ref, dcw_ref, dbg_ref, dvn_ref = refs[nx:nx + 10]
        xrecv = refs[nx + 10:2 * nx + 10]
        mixed_ref, dvnm_ref = refs[2 * nx + 10:2 * nx + 12]
        exchange = _Exchange(xsrc, xrecv, xrows, *refs[2 * nx + 12:])
        i = pl.program_id(0)

        @pl.when(i == 0)
        def _():
            exchange.start()
            for ref in (dws_ref, dbs_ref, dcw_ref, dbg_ref, dvn_ref):
                ref[...] = jnp.zeros_like(ref)

        def col(ref, lo, width):
            return ref[:, lo:lo + width].astype(F32)

        def ext_rows(prev_blk, center, next_blk):
            return jnp.concatenate([prev_blk, center, next_blk], axis=0)

        def ext_col(lo, width):
            return ext_rows(col(pp_ref, lo, width)[hb - h8:hb], col(p_ref, lo, width), col(pn_ref, lo, width)[0:h8])

        xhv = xhv_ref[...].astype(F32)
        vn = xhv * vng_ref[...] + vnb_ref[...]
        mixed = _spatial_mix(vn, ws_ref, bias_ref, mixed_ref, dm, tm)
        gu = gu_ref[...].astype(F32)
        bb_ref[...] = (gu * mixed).astype(BF16)
        g_a, g_b = col(p_ref, dm.OFF_GA, d), col(p_ref, dm.OFF_GB, d)
        y_a, y_b = ya_ref[...].astype(F32), yb_ref[...].astype(F32)
        qa = y_a * g_a * (1.0 - g_a)
        qb = y_b * g_b * (1.0 - g_b)
        cv = cv_ref[...].astype(F32)
        a_ref[...] = (col(p_ref, 0, wa) * cv).astype(BF16)
        c_a, h_a = col(p_ref, dm.OFF_CA, wa), col(p_ref, dm.OFF_HA, wa)
        ch = c_a * h_a
        row = lax.broadcasted_iota(jnp.int32, (ext, 1), 0) + (i * tm - h8)
        inside = jnp.logical_and(row >= 0, row < t)
        dr_e = ext_rows(drp_ref[...], dr_ref[...], drn_ref[...])
        ds_e = _mm(dr_e.astype(BF16), wot_ref[...])
        dya_e = ds_e * ext_col(dm.OFF_GA, d)
        da_e = _mm(dya_e.astype(BF16), wpat_ref[...])
        dcv_e = jnp.where(inside, da_e * ext_col(0, wa), 0.0)
        dcv, (dcv_up, dcv_dn) = dcv_e[h8:h8 + tm], _row_neighbours(dcv_e, tm)
        w0, w1, w2 = cw_ref[0:1, :], cw_ref[1:2, :], cw_ref[2:3, :]
        dp_ref[:, 0:wa] = (da_e[h8:h8 + tm] * cv).astype(BF16)
        dch = w0 * dcv_dn + w1 * dcv + w2 * dcv_up
        dp_ref[:, dm.OFF_CA:dm.OFF_CA + wa] = (dch * h_a).astype(BF16)
        dp_ref[:, dm.OFF_HA:dm.OFF_HA + wa] = (dch * c_a).astype(BF16)
        dcw_ref[0:1, :] += _colsum(dcv_dn * ch)
        dcw_ref[1:2, :] += _colsum(dcv * ch)
        dcw_ref[2:3, :] += _colsum(dcv_up * ch)
        dya_ref[...] = dya_e[h8:h8 + tm].astype(BF16)
        ds = ds_e[h8:h8 + tm]
        dzga = ds * qa
        dzgb = ds * qb
        dp_ref[:, dm.OFF_GA:dm.OFF_GA + d] = dzga.astype(BF16)
        dp_ref[:, dm.OFF_GB:dm.OFF_GB + d] = dzgb.astype(BF16)
        dbg_ref[0:1, 0:d] += _colsum(dzga)
        dbg_ref[0:1, d:2 * d] += _colsum(dzgb)
        dyb = (ds * g_b).astype(BF16)
        dyb_ref[...] = dyb
        dbb = _mm(dyb, wpbt_ref[...])
        dp_ref[:, dm.OFF_UB:dm.OFF_UB + d] = (dbb * mixed * dgu_ref[...].astype(F32)).astype(BF16)
        dmb = (dbb * gu).astype(BF16)
        vb = vn.astype(BF16)
        dbs = jnp.zeros((CHUNK, CHUNK), F32)
        for cc in range(tm // CHUNK):
            r0 = cc * CHUNK
            dbs = dbs + _mm(dmb[r0:r0 + CHUNK, :], sel_ref[...])
            for h in range(dm.H):
                c0 = h * CHUNK
                blk = dmb[r0:r0 + CHUNK, c0:c0 + CHUNK]
                dvnm_ref[r0:r0 + CHUNK, c0:c0 + CHUNK] = _mm(wst_ref[h], blk)
                dws_ref[h] += lax.dot_general(blk, vb[r0:r0 + CHUNK, c0:c0 + CHUNK], NT_DIMS, preferred_element_type=F32)
        dbs_ref[...] += dbs
        dvn = dvnm_ref[...]
        dvn_ref[0:1, :] += _colsum(dvn * xhv)
        dvn_ref[1:2, :] += _colsum(dvn)
        dp_ref[:, dm.OFF_VB:dm.OFF_VB + d] = (_ln_bwd(dvn * vng_ref[...], xhv, rgv_ref[...].astype(F32))).astype(BF16)

        @pl.when(i == nt - 1)
        def _():
            exchange.finish()

    full = lambda i: (0, 0)
    tile = lambda i: (i, 0)
    hcc = _resident((dm.H, CHUNK, CHUNK))
    tok = lambda w, dt: jax.ShapeDtypeStruct((t, w), dt)
    return pl.pallas_call(
        body, name="mixer_bwd", grid=(nt,),
        in_specs=[pl.BlockSpec((tm, d), tile), pl.BlockSpec((h8, d), prev_f), pl.BlockSpec((h8, d), next_f),
                  pl.BlockSpec((tm, npj), tile), pl.BlockSpec((hb, npj), prev_b), pl.BlockSpec((hb, npj), next_b),
                  pl.BlockSpec((tm, d), tile), pl.BlockSpec((tm, d), tile), pl.BlockSpec((tm, d), tile),
                  pl.BlockSpec((tm, d), tile), pl.BlockSpec((tm, d), tile), pl.BlockSpec((tm, d), tile),
                  pl.BlockSpec((tm, wa), tile), _resident((d, d)), _resident((d, wa)), _resident((d, d)),
                  _resident((h8, wa)), _resident((1, d)), _resident((1, d)), hcc, hcc, _resident((CHUNK, d)),
                  _resident((d, CHUNK))] + [HBM_SPEC] * nx,
        out_specs=(pl.BlockSpec((tm, npj), tile), pl.BlockSpec((tm, wa), tile), pl.BlockSpec((tm, d), tile),
                   pl.BlockSpec((tm, d), tile), pl.BlockSpec((tm, d), tile),
                   pl.BlockSpec((dm.H, CHUNK, CHUNK), lambda i: (0, 0, 0)), pl.BlockSpec((CHUNK, CHUNK), full),
                   pl.BlockSpec((h8, wa), full), pl.BlockSpec((h8, 2 * d), full), pl.BlockSpec((h8, d), full))
        + (HBM_SPEC,) * nx,
        out_shape=(tok(npj, BF16), tok(wa, BF16), tok(d, BF16), tok(d, BF16), tok(d, BF16),
                   jax.ShapeDtypeStruct((dm.H, CHUNK, CHUNK), F32), jax.ShapeDtypeStruct((CHUNK, CHUNK), F32),
                   jax.ShapeDtypeStruct((h8, wa), F32), jax.ShapeDtypeStruct((h8, 2 * d), F32),
                   jax.ShapeDtypeStruct((h8, d), F32)) + _Exchange.out_shapes(xsrcs, xrows),
        scratch_shapes=[pltpu.VMEM((tm, d), F32), pltpu.VMEM((tm, d), F32)] + _Exchange.semaphores(nx),
        compiler_params=_params(),
    )(dr1, dr1, dr1, p, p, p, ya, yb, gu_s, dgu_s, xhv_s, rgv_s, cv_s, w_ot, w_pat, w_pbt, conv_w8, vng, vnb, ws_b, wst_b, bias_s, head_sel, *xsrcs)


def _input_grad(dp, dr1, w_int, xchg, dm, tm):
    t, d, npj = dm.T, dm.D, dm.NP
    nt = t // tm
    nx = len(xchg)
    xsrcs, xrows = [a_ for a_, _ in xchg], [r for _, r in xchg]

    def body(dp_ref, dr_ref, w_ref, *refs):
        dx_ref = refs[nx]
        exchange = _Exchange(refs[:nx], refs[nx + 1:2 * nx + 1], xrows, *refs[2 * nx + 1:])
        i = pl.program_id(0)

        @pl.when(i == 0)
        def _():
            exchange.start()

        dx_ref[...] = ALPHA * dr_ref[...] + _mm(dp_ref[...], w_ref[...])

        @pl.when(i == nt - 1)
        def _():
            exchange.finish()

    return pl.pallas_call(
        body, name="input_grad", grid=(nt,),
        in_specs=[pl.BlockSpec((tm, npj), lambda i: (i, 0)), pl.BlockSpec((tm, d), lambda i: (i, 0)), _resident((npj, d))]
        + [HBM_SPEC] * nx,
        out_specs=(pl.BlockSpec((tm, d), lambda i: (i, 0)),) + (HBM_SPEC,) * nx,
        out_shape=(jax.ShapeDtypeStruct((t, d), F32),) + _Exchange.out_shapes(xsrcs, xrows),
        scratch_shapes=_Exchange.semaphores(nx),
        compiler_params=_params(),
    )(dp, dr1, w_int, *xsrcs)


def _adamw_math(w, g, m, v):
    nm = ADAM_B1 * m + (1.0 - ADAM_B1) * g
    nv = ADAM_B2 * v + (1.0 - ADAM_B2) * (g * g)
    delta = -ADAM_LR * ((nm / (1.0 - ADAM_B1 ** ADAM_STEP)) / (jnp.sqrt(nv / (1.0 - ADAM_B2 ** ADAM_STEP)) + ADAM_EPS) + ADAM_WD * w)
    return delta, nm, nv


def _adamw(w, g, m, v, name):
    _, rows, cols = w.shape
    tr = 256 if rows % 256 == 0 else rows
    from_slots = g.ndim == 3

    def body(w_ref, g_ref, m_ref, v_ref, go_ref, d_ref, nm_ref, nv_ref):
        if from_slots:
            g_ = g_ref[0].astype(F32)
            for k in range(1, N_DEV):
                g_ = g_ + g_ref[k].astype(F32)
        else:
            g_ = g_ref[...]
        go_ref[0] = g_
        d_ref[0], nm_ref[0], nv_ref[0] = _adamw_math(w_ref[0], g_, m_ref[0], v_ref[0])

    spec = pl.BlockSpec((1, tr, cols), lambda i: (0, i, 0))
    g_spec = pl.BlockSpec((N_DEV, tr, cols), lambda i: (0, i, 0)) if from_slots else pl.BlockSpec((tr, cols), lambda i: (i, 0))
    shp = jax.ShapeDtypeStruct((1, rows, cols), F32)
    return pl.pallas_call(
        body, name=name, grid=(rows // tr,), in_specs=[spec, g_spec, spec, spec], out_specs=(spec,) * 4, out_shape=(shp,) * 4,
        compiler_params=_params(),
    )(w, g, m, v)


def _adamw_small(ws, gs, ms, vs):
    n = len(ws)

    def body(*refs):
        ins, outs = refs[:4 * n], refs[4 * n:]
        for k in range(n):
            w_ref, g_ref, m_ref, v_ref = ins[k], ins[n + k], ins[2 * n + k], ins[3 * n + k]
            outs[k][...], outs[n + k][...], outs[2 * n + k][...] = _adamw_math(w_ref[...], g_ref[...], m_ref[...], v_ref[...])

    shapes = tuple(jax.ShapeDtypeStruct(w.shape, F32) for w in ws)
    out = pl.pallas_call(body, name="adamw_small", out_shape=shapes * 3, compiler_params=_params(()))(*ws, *gs, *ms, *vs)
    return out[:n], out[n:2 * n], out[2 * n:]


def _to_slab(parts):
    flat = jnp.concatenate([q.reshape(-1) for q in parts])
    pad = (-flat.shape[0]) % (F32_SUBLANES * LANES)
    return jnp.pad(flat, (0, pad)).reshape(-1, LANES)


def _from_slab(slab, shapes):
    flat = slab.reshape(-1)
    out, off = [], 0
    for s in shapes:
        n = math.prod(s)
        out.append(flat[off:off + n].reshape(s))
        off += n
    return out


def kernel(x, w_in, b_gate, conv_w, v_norm_g, v_norm_b, w_s, b_s, w_pa, w_pb, w_o, ln1_g, ln1_b, w_ff1, w_ff2, ln2_g, ln2_b, loss_target, m_w_in, m_b_gate, m_conv_w, m_v_norm_g, m_v_norm_b, m_w_s, m_b_s, m_w_pa, m_w_pb, m_w_o, m_ln1_g, m_ln1_b, m_w_ff1, m_w_ff2, m_ln2_g, m_ln2_b, v_w_in, v_b_gate, v_conv_w, v_v_norm_g, v_v_norm_b, v_w_s, v_b_s, v_w_pa, v_w_pb, v_w_o, v_ln1_g, v_ln1_b, v_w_ff1, v_w_ff2, v_ln2_g, v_ln2_b):
    t, d = x.shape[1], x.shape[2]
    dm = _Dims(t, d)
    tm = 256 if t % 256 == 0 else CHUNK
    tm_big = 512 if t % 512 == 0 else tm
    tt = 1024 if t % 1024 == 0 else tm
    me = 4 * lax.axis_index("x") + 2 * lax.axis_index("y") + lax.axis_index("c")
    x2, tgt = x[0], loss_target[0]

    conv_bits = lax.bitcast_convert_type(conv_w[0], BF16).reshape(-1)
    conv_blk = jnp.pad(conv_bits, (0, dm.conv_rows * d - conv_bits.shape[0])).reshape(dm.conv_rows, d)
    w_int, conv_g = _all_gather([w_in[0].T.astype(BF16), conv_blk])
    wa8 = dm.WA // N_DEV
    conv_all = lax.bitcast_convert_type(conv_g.reshape(N_DEV, -1)[:, :3 * wa8 * 2].reshape(N_DEV, 3, wa8, 2), F32)
    conv_full = jnp.transpose(conv_all, (1, 0, 2)).reshape(3, dm.WA)
    conv_w8 = jnp.pad(conv_full, ((0, F32_SUBLANES - 3), (0, 0)))
    ws_b = w_s[0].astype(BF16)
    wst_b = jnp.transpose(w_s[0], (0, 2, 1)).astype(BF16)
    bias_s = jnp.repeat(b_s[0].T, CHUNK, axis=1)
    head_sel = (jnp.arange(d)[:, None] // CHUNK == jnp.arange(CHUNK)[None, :]).astype(BF16)

    p, w_pa_f, w_pb_f, w_o_f, w_1t, w_2 = _proj_in(
        x2, w_int, b_gate, [w_pa[0].astype(BF16), w_pb[0].astype(BF16), w_o[0].astype(BF16), w_ff1[0].T.astype(BF16),
                            w_ff2[0].astype(BF16)], dm, tm_big)
    ya, yb, r1, gu_s, dgu_s, xhv_s, rgv_s, cv_s, s_m = _mixer_fwd(
        p, x2, w_pa_f, w_pb_f, w_o_f, conv_w8, v_norm_g, v_norm_b, ws_b, bias_s, dm, tm)
    relu, x1b, dr2, dr2b, sums2 = _ffn_fwd(r1, tgt, w_1t, w_2, ln1_g, ln1_b, ln2_g, ln2_b, dm, tm_big)
    dr1, dh1, sums1 = _ffn_bwd(r1, relu, dr2, w_2, w_1t, ln1_g, dm, tm_big)
    fb = min(1024, dm.DFF)
    rows = dm.shard_rows
    g_ff1t, g_ff2 = _wgrad([(dh1, x1b, False), (relu, dr2b, True)], tt, fb, "ffn_wgrad")
    dp, a_m, dya, bb_m, dyb, g_ws, g_bs_t, g_cw, g_bg, g_vn, got_ff1t, got_ff2 = _mixer_bwd(
        dr1, p, ya, yb, gu_s, dgu_s, xhv_s, rgv_s, cv_s, w_o_f.T, w_pa_f.T, w_pb_f.T, conv_w8, v_norm_g, v_norm_b, ws_b, wst_b, bias_s, head_sel,
        [(g_ff1t, rows[4]), (g_ff2, rows[5])], dm, tm)
    (g_pa,) = _wgrad([(a_m, dya, False)], tt, dm.WA, "w_pa_grad")
    g_o, g_pb = _wgrad([(s_m, dr1, False), (bb_m, dyb, False)], tt, d, "w_o_pb_grad")
    g_int, got_pa, got_pb, got_o = _wgrad([(dp, x2, False)], tt, 17 * LANES, "w_in_grad",
                                          xchg=[(g_pa, rows[1]), (g_pb, rows[2]), (g_o, rows[3])])
    small_parts = [g_bg[0], g_cw[0:3], g_vn[0], g_vn[1], g_ws, g_bs_t[:, :dm.H].T,
                   sums1[0], sums1[1], sums2[1], sums2[2], sums2[0]]
    grad_x, got_int, got_s = _input_grad(dp, dr1, w_int, [(g_int, rows[0]), (_to_slab(small_parts), None)], dm, tm_big)

    ssum = _sum_slots(got_s, 256, "sum_small")
    (s_bg, s_cw, s_vng, s_vnb, s_ws, s_bs, s_l1g, s_l1b, s_l2g, s_l2b, s_sq) = _from_slab(
        ssum, [(1, 2 * d), (3, dm.WA), (1, d), (1, d), (1, dm.H, CHUNK, CHUNK), (1, dm.H, CHUNK),
               (1, d), (1, d), (1, d), (1, d), (d,)])
    loss = 0.5 * jnp.sum(s_sq) / d
    s_cw = lax.dynamic_slice(s_cw, (0, me * wa8), (3, wa8))[None]
    big_g = [_sum_slots(got_int, 256, "sum_grads_w_in").T, got_pa, got_pb, got_o,
             _sum_slots(got_ff1t, 256, "sum_grads_w_ff1").T, got_ff2]
    big_w = [(w_in, m_w_in, v_w_in), (w_pa, m_w_pa, v_w_pa), (w_pb, m_w_pb, v_w_pb), (w_o, m_w_o, v_w_o),
             (w_ff1, m_w_ff1, v_w_ff1), (w_ff2, m_w_ff2, v_w_ff2)]
    big_out = [_adamw(w, g, m, v, "adamw_%d" % k) for k, ((w, m, v), g) in enumerate(zip(big_w, big_g))]
    small_w = [(b_gate, m_b_gate, v_b_gate), (conv_w, m_conv_w, v_conv_w), (v_norm_g, m_v_norm_g, v_v_norm_g),
               (v_norm_b, m_v_norm_b, v_v_norm_b), (w_s, m_w_s, v_w_s), (b_s, m_b_s, v_b_s), (ln1_g, m_ln1_g, v_ln1_g),
               (ln1_b, m_ln1_b, v_ln1_b), (ln2_g, m_ln2_g, v_ln2_g), (ln2_b, m_ln2_b, v_ln2_b)]
    small_g = [s_bg, s_cw, s_vng, s_vnb, s_ws, s_bs, s_l1g, s_l1b, s_l2g, s_l2b]
    small_out = list(zip(*_adamw_small([w for w, _, _ in small_w], small_g, [m for _, m, _ in small_w],
                                       [v for _, _, v in small_w])))

    order = [("b", 0), ("s", 0), ("s", 1), ("s", 2), ("s", 3), ("s", 4), ("s", 5), ("b", 1), ("b", 2), ("b", 3),
             ("s", 6), ("s", 7), ("b", 4), ("b", 5), ("s", 8), ("s", 9)]
    grads, deltas, new_m, new_v = [], [], [], []
    for kind, k in order:
        if kind == "b":
            g, dl, nm, nv = big_out[k]
        else:
            g, (dl, nm, nv) = small_g[k], small_out[k]
        grads.append(g)
        deltas.append(dl)
        new_m.append(nm)
        new_v.append(nv)
    return (loss, grad_x[None], *grads, *deltas, *new_m, *new_v)
```

```python
import math

import jax
import jax.numpy as jnp
from jax import lax
from jax.experimental import pallas as pl
from jax.experimental.pallas import tpu as pltpu

F32 = jnp.float32
BF16 = jnp.bfloat16
N_DEV = 8
CHUNK = 128
LN_EPS = 1e-5
ALPHA = 2.0 ** 0.25
ADAM_LR, ADAM_B1, ADAM_B2, ADAM_EPS, ADAM_WD, ADAM_STEP = 0.001, 0.9, 0.999, 1e-08, 0.01, 10
F32_SUBLANES = 8
BF16_SUBLANES = 16
LANES = 128
VMEM_LIMIT = 56 * 1024 * 1024
MESH = pl.DeviceIdType.MESH
NT_DIMS = (((1,), (1,)), ((), ()))
TN_DIMS = (((0,), (0,)), ((), ()))
HBM_SPEC = pl.BlockSpec(memory_space=pltpu.HBM)


class _Dims:
    def __init__(self, t, d):
        self.T, self.D = t, d
        self.WA = 3 * d // 2
        self.NP = 3 * self.WA + 4 * d
        self.DFF = 4 * d
        self.H = d // CHUNK
        self.OFF_CA, self.OFF_HA = self.WA, 2 * self.WA
        self.OFF_UB = 3 * self.WA
        self.OFF_VB = self.OFF_UB + d
        self.OFF_GA = self.OFF_VB + d
        self.OFF_GB = self.OFF_GA + d
        self.shard_rows = (self.NP // N_DEV, self.WA // N_DEV, d // N_DEV, d // N_DEV, self.DFF // N_DEV, self.DFF // N_DEV)
        self.conv_rows = BF16_SUBLANES * max(1, -(-(3 * (self.WA // N_DEV) * 2) // (BF16_SUBLANES * d)))


def _params(sem=("arbitrary",), vmem=VMEM_LIMIT):
    return pltpu.CompilerParams(dimension_semantics=sem, vmem_limit_bytes=vmem)


def _mesh_pos():
    return lax.axis_index("x"), lax.axis_index("y"), lax.axis_index("c")


def _resident(shape):
    zeros = (0,) * len(shape)
    return pl.BlockSpec(shape, lambda *_: zeros, pipeline_mode=pl.Buffered(1))


class _TwoLevelGather:
    def __init__(self, shard_refs, out_refs, send_sems, recv_sems, local_sems):
        self.n = len(shard_refs)
        self.shard_refs, self.out_refs = shard_refs, out_refs
        self.send_sems, self.recv_sems, self.local_sems = send_sems, recv_sems, local_sems
        x, y, c = _mesh_pos()
        self.c = c
        self.me, self.sibling = (x, y, c), (x, y, 1 - c)
        self.chips = [(1 - x, y), (x, 1 - y), (1 - x, 1 - y)]

    def _slot(self, a, px, py, pc):
        rows = self.shard_refs[a].shape[0]
        return self.out_refs[a].at[pl.ds((4 * px + 2 * py + pc) * rows, rows), :]

    def _copy(self, a, k, block, to, src=None):
        return pltpu.make_async_remote_copy(
            src_ref=self._slot(a, *block) if src is None else src, dst_ref=self._slot(a, *block),
            send_sem=self.send_sems.at[7 * a + k], recv_sem=self.recv_sems.at[7 * a + k], device_id=to, device_id_type=MESH)

    def _mine(self):
        return [pltpu.make_async_copy(self.shard_refs[a], self._slot(a, *self.me), self.local_sems.at[a]) for a in range(self.n)]

    def _first(self):
        out = []
        for a in range(self.n):
            out.append(self._copy(a, 0, self.me, self.sibling, src=self.shard_refs[a]))
            out += [self._copy(a, 1 + j, self.me, (*chip, self.c), src=self.shard_refs[a]) for j, chip in enumerate(self.chips)]
        return out

    def _passed(self):
        return [self._copy(a, 4 + j, (*chip, self.c), self.sibling) for j, chip in enumerate(self.chips) for a in range(self.n)]

    def start(self):
        for cp in self._mine() + self._first():
            cp.start()

    def forward(self):
        passed = self._passed()
        for j, chip in enumerate(self.chips):
            for a in range(self.n):
                self._copy(a, 1 + j, (*chip, self.c), self.me).wait_recv()
                passed[j * self.n + a].start()

    def finish(self):
        for a in range(self.n):
            self._copy(a, 0, self.sibling, self.me).wait_recv()
            for j, chip in enumerate(self.chips):
                self._copy(a, 4 + j, (*chip, 1 - self.c), self.me).wait_recv()
        for cp in self._first() + self._passed():
            cp.wait_send()
        for cp in self._mine():
            cp.wait()

    @staticmethod
    def out_shapes(shards):
        return tuple(jax.ShapeDtypeStruct((N_DEV * s.shape[0], s.shape[1]), s.dtype) for s in shards)

    @staticmethod
    def semaphores(n):
        return [pltpu.SemaphoreType.DMA((7 * n,)), pltpu.SemaphoreType.DMA((7 * n,)), pltpu.SemaphoreType.DMA((n,))]


def _all_gather(shards):
    n = len(shards)

    def body(*refs):
        gather = _TwoLevelGather(refs[:n], refs[n:2 * n], *refs[2 * n:])
        gather.start()
        gather.forward()
        gather.finish()

    return pl.pallas_call(
        body, name="all_gather_w_in", out_shape=_TwoLevelGather.out_shapes(shards),
        in_specs=[HBM_SPEC] * n, out_specs=(HBM_SPEC,) * n, scratch_shapes=_TwoLevelGather.semaphores(n),
    )(*shards)


class _Exchange:
    def __init__(self, src_refs, recv_refs, rows, send_sems, recv_sems, local_sems):
        x, y, c = _mesh_pos()
        me = 4 * x + 2 * y + c
        self.own, self.sends, self.arrivals = [], [], []
        for a, (src, recv) in enumerate(zip(src_refs, recv_refs)):
            def blk(k, src=src, r=rows[a]):
                return src if r is None else src.at[pl.ds(k * r, r), :]

            self.own.append(pltpu.make_async_copy(blk(me), recv.at[me], local_sems.at[a]))
            for rel in range(1, N_DEV):
                px = 1 - x if rel & 4 else x
                py = 1 - y if rel & 2 else y
                pc = 1 - c if rel & 1 else c
                peer = 4 * px + 2 * py + pc
                sem = dict(send_sem=send_sems.at[7 * a + rel - 1], recv_sem=recv_sems.at[7 * a + rel - 1],
                           device_id=(px, py, pc), device_id_type=MESH)
                self.sends.append(pltpu.make_async_remote_copy(src_ref=blk(peer), dst_ref=recv.at[me], **sem))
                self.arrivals.append(pltpu.make_async_remote_copy(src_ref=blk(me), dst_ref=recv.at[peer], **sem))

    def start(self):
        for cp in self.own + self.sends:
            cp.start()

    def finish(self):
        for cp in self.arrivals:
            cp.wait_recv()
        for cp in self.sends:
            cp.wait_send()
        for cp in self.own:
            cp.wait()

    @staticmethod
    def out_shapes(srcs, rows):
        return tuple(jax.ShapeDtypeStruct((N_DEV, s.shape[0] if r is None else r, s.shape[1]), s.dtype) for s, r in zip(srcs, rows))

    @staticmethod
    def semaphores(n):
        return [pltpu.SemaphoreType.DMA((7 * n,)), pltpu.SemaphoreType.DMA((7 * n,)), pltpu.SemaphoreType.DMA((n,))]


def _sum_slots(slots, tile_rows, name):
    _, rows, cols = slots.shape
    tr = rows
    if N_DEV * rows * cols * slots.dtype.itemsize > 8 * 1024 * 1024:
        tr = next(c for c in (256, 192, 128, 64, 32, 16) if c <= tile_rows and rows % c == 0)

    def body(s_ref, o_ref):
        acc = s_ref[0].astype(F32)
        for k in range(1, N_DEV):
            acc = acc + s_ref[k].astype(F32)
        o_ref[...] = acc

    return pl.pallas_call(
        body, name=name, grid=(rows // tr,),
        in_specs=[pl.BlockSpec((N_DEV, tr, cols), lambda i: (0, i, 0))],
        out_specs=pl.BlockSpec((tr, cols), lambda i: (i, 0)),
        out_shape=jax.ShapeDtypeStruct((rows, cols), F32),
        compiler_params=_params(),
    )(slots)


def _gelu_and_grad(x):
    k0 = math.sqrt(2.0 / math.pi)
    k1 = 0.044715
    a = k1 * (x * x)
    half = 1.0 / (1.0 + jnp.exp((-2.0 * k0) * x * (1.0 + a)))
    g = x * half
    return g, half + g * (1.0 - half) * (2.0 * k0 + (6.0 * k0) * a)


def _ln_stats(r):
    mu = jnp.mean(r, axis=-1, keepdims=True)
    rc = r - mu
    var = jnp.mean(rc * rc, axis=-1, keepdims=True)
    rstd = lax.rsqrt(var + LN_EPS)
    return rc * rstd, rstd


def _ln_bwd(dxh, xh, rstd):
    return rstd * (dxh - jnp.mean(dxh, axis=-1, keepdims=True) - xh * jnp.mean(dxh * xh, axis=-1, keepdims=True))


def _colsum(a):
    return jnp.sum(a, axis=0, keepdims=True)


def _mm(a, b):
    return jnp.dot(a, b, preferred_element_type=F32)


def _halo_maps(tm, t, unit):
    per, last = tm // unit, t // unit - 1
    return (lambda i: (jnp.maximum(i * per - 1, 0), 0)), (lambda i: (jnp.minimum((i + 1) * per, last), 0))


def _proj_in(x2, w_int, b_gate, shards, dm, tm):
    t, d, npj = dm.T, dm.D, dm.NP
    cw = d // 2
    nt = t // tm
    n = len(shards)

    def body(x_ref, w_ref, bg_ref, *refs):
        p_ref = refs[n]
        gather = _TwoLevelGather(refs[:n], refs[n + 1:2 * n + 1], *refs[2 * n + 1:])
        i = pl.program_id(0)

        @pl.when(i == 0)
        def _():
            gather.start()

        @pl.when(i == nt // 2)
        def _():
            gather.forward()

        xb = x_ref[...].astype(BF16)
        for blk in range(npj // cw):
            lo = blk * cw
            acc = lax.dot_general(xb, w_ref[lo:lo + cw, :], NT_DIMS, preferred_element_type=F32)
            if lo >= dm.OFF_GA:
                acc = jax.nn.sigmoid(acc + bg_ref[:, lo - dm.OFF_GA:lo - dm.OFF_GA + cw])
            p_ref[:, lo:lo + cw] = acc.astype(BF16)

        @pl.when(i == nt - 1)
        def _():
            gather.finish()

    return pl.pallas_call(
        body, name="proj_in", grid=(nt,),
        in_specs=[pl.BlockSpec((tm, d), lambda i: (i, 0)), _resident((npj, d)), _resident((1, 2 * d))] + [HBM_SPEC] * n,
        out_specs=(pl.BlockSpec((tm, npj), lambda i: (i, 0)),) + (HBM_SPEC,) * n,
        out_shape=(jax.ShapeDtypeStruct((t, npj), BF16),) + _TwoLevelGather.out_shapes(shards),
        scratch_shapes=_TwoLevelGather.semaphores(n),
        compiler_params=_params(),
    )(x2, w_int, b_gate, *shards)


def _row_neighbours(ext, tm):
    h, n = F32_SUBLANES, ext.shape[0]
    return pltpu.roll(ext, 1, 0)[h:h + tm], pltpu.roll(ext, n - 1, 0)[h:h + tm]


def _spatial_mix(vn, ws_ref, bias_ref, mixed_ref, dm, tm):
    vb = vn.astype(BF16)
    for cc in range(tm // CHUNK):
        r0 = cc * CHUNK
        for h in range(dm.H):
            c0 = h * CHUNK
            m = _mm(ws_ref[h], vb[r0:r0 + CHUNK, c0:c0 + CHUNK])
            mixed_ref[r0:r0 + CHUNK, c0:c0 + CHUNK] = m + bias_ref[:, c0:c0 + CHUNK]
    return mixed_ref[...]


def _mixer_fwd(p, x2, w_pa, w_pb, w_o, conv_w8, vng, vnb, ws_b, bias_s, dm, tm):
    t, d, wa, npj = dm.T, dm.D, dm.WA, dm.NP
    nt = t // tm
    hb = BF16_SUBLANES
    prev_map, next_map = _halo_maps(tm, t, hb)

    def body(p_ref, pp_ref, pn_ref, x_ref, wpa_ref, wpb_ref, wo_ref, cw_ref, vng_ref, vnb_ref, ws_ref, bias_ref,
             ya_ref, yb_ref, r1_ref, gu_ref, dgu_ref, xhv_ref, rgv_ref, cv_ref, s_ref, mixed_ref):
        i = pl.program_id(0)

        def col(ref, lo, width):
            return ref[:, lo:lo + width].astype(F32)

        ch = col(p_ref, dm.OFF_CA, wa) * col(p_ref, dm.OFF_HA, wa)
        chp = (col(pp_ref, dm.OFF_CA, wa) * col(pp_ref, dm.OFF_HA, wa))[hb - F32_SUBLANES:hb]
        chn = (col(pn_ref, dm.OFF_CA, wa) * col(pn_ref, dm.OFF_HA, wa))[0:F32_SUBLANES]
        ch_e = jnp.concatenate([jnp.where(i == 0, 0.0, chp), ch, jnp.where(i == nt - 1, 0.0, chn)], axis=0)
        up, dn = _row_neighbours(ch_e, tm)
        cv = cw_ref[0:1, :] * up + cw_ref[1:2, :] * ch + cw_ref[2:3, :] * dn
        cv_ref[...] = cv.astype(BF16)
        ya = _mm((col(p_ref, 0, wa) * cv).astype(BF16), wpa_ref[...])
        gv, dgelu_v = _gelu_and_grad(col(p_ref, dm.OFF_VB, d))
        xhv, rstdv = _ln_stats(gv)
        xhv_ref[...] = xhv.astype(BF16)
        rgv_ref[...] = (rstdv * dgelu_v).astype(BF16)
        mixed = _spatial_mix(xhv * vng_ref[...] + vnb_ref[...], ws_ref, bias_ref, mixed_ref, dm, tm)
        gu, dgelu_u = _gelu_and_grad(col(p_ref, dm.OFF_UB, d))
        gu_ref[...] = gu.astype(BF16)
        dgu_ref[...] = dgelu_u.astype(BF16)
        yb = _mm((gu * mixed).astype(BF16), wpb_ref[...])
        sb = (col(p_ref, dm.OFF_GA, d) * ya + col(p_ref, dm.OFF_GB, d) * yb).astype(BF16)
        s_ref[...] = sb
        mix = _mm(sb, wo_ref[...])
        ya_ref[...] = ya.astype(BF16)
        yb_ref[...] = yb.astype(BF16)
        r1_ref[...] = ALPHA * x_ref[...] + mix

    tile = lambda i: (i, 0)
    return pl.pallas_call(
        body, name="mixer_fwd", grid=(nt,),
        in_specs=[pl.BlockSpec((tm, npj), tile), pl.BlockSpec((hb, npj), prev_map), pl.BlockSpec((hb, npj), next_map),
                  pl.BlockSpec((tm, d), tile), _resident((wa, d)), _resident((d, d)), _resident((d, d)),
                  _resident((F32_SUBLANES, wa)), _resident((1, d)), _resident((1, d)),
                  _resident((dm.H, CHUNK, CHUNK)), _resident((CHUNK, d))],
        out_specs=(pl.BlockSpec((tm, d), tile),) * 7 + (pl.BlockSpec((tm, wa), tile), pl.BlockSpec((tm, d), tile)),
        out_shape=(jax.ShapeDtypeStruct((t, d), BF16), jax.ShapeDtypeStruct((t, d), BF16), jax.ShapeDtypeStruct((t, d), F32))
        + (jax.ShapeDtypeStruct((t, d), BF16),) * 4 + (jax.ShapeDtypeStruct((t, wa), BF16), jax.ShapeDtypeStruct((t, d), BF16)),
        scratch_shapes=[pltpu.VMEM((tm, d), F32)],
        compiler_params=_params(),
    )(p, p, p, x2, w_pa, w_pb, w_o, conv_w8, vng, vnb, ws_b, bias_s)


def _ffn_fwd(r1, tgt, w1t, w2, ln1g, ln1b, ln2g, ln2b, dm, tm):
    t, d, dff = dm.T, dm.D, dm.DFF
    fc = dff // N_DEV

    def body(r1_ref, tgt_ref, w1t_ref, w2_ref, g1_ref, b1_ref, g2_ref, b2_ref, relu_ref, x1_ref, dr2_ref, dr2b_ref, sums_ref):
        @pl.when(pl.program_id(0) == 0)
        def _():
            sums_ref[...] = jnp.zeros_like(sums_ref)

        xh1, _ = _ln_stats(r1_ref[...])
        x1 = xh1 * g1_ref[...] + b1_ref[...]
        x1b = x1.astype(BF16)
        x1_ref[...] = x1b
        ffn = jnp.zeros((tm, d), F32)
        for k in range(dff // fc):
            ks = slice(k * fc, (k + 1) * fc)
            r = jnp.maximum(lax.dot_general(x1b, w1t_ref[ks, :], NT_DIMS, preferred_element_type=F32), 0.0)
            relu_ref[:, ks] = r.astype(BF16)
            ffn = ffn + _mm((r * r).astype(BF16), w2_ref[ks, :])
        xh2, rstd2 = _ln_stats(ALPHA * x1 + ffn)
        diff = xh2 * g2_ref[...] + b2_ref[...] - tgt_ref[...]
        dy = diff * (1.0 / d)
        dr2 = _ln_bwd(dy * g2_ref[...], xh2, rstd2)
        dr2_ref[...] = dr2
        dr2b_ref[...] = dr2.astype(BF16)
        sums_ref[0:1, :] += _colsum(diff * diff)
        sums_ref[1:2, :] += _colsum(dy * xh2)
        sums_ref[2:3, :] += _colsum(dy)

    tile = lambda i: (i, 0)
    vec = _resident((1, d))
    return pl.pallas_call(
        body, name="ffn_fwd", grid=(t // tm,),
        in_specs=[pl.BlockSpec((tm, d), tile), pl.BlockSpec((tm, d), tile), _resident((dff, d)), _resident((dff, d)),
                  vec, vec, vec, vec],
        out_specs=(pl.BlockSpec((tm, dff), tile), pl.BlockSpec((tm, d), tile), pl.BlockSpec((tm, d), tile),
                   pl.BlockSpec((tm, d), tile), pl.BlockSpec((F32_SUBLANES, d), lambda i: (0, 0))),
        out_shape=(jax.ShapeDtypeStruct((t, dff), BF16), jax.ShapeDtypeStruct((t, d), BF16), jax.ShapeDtypeStruct((t, d), F32),
                   jax.ShapeDtypeStruct((t, d), BF16), jax.ShapeDtypeStruct((F32_SUBLANES, d), F32)),
        compiler_params=_params(),
    )(r1, tgt, w1t, w2, ln1g, ln1b, ln2g, ln2b)


def _ffn_bwd(r1, relu, dr2, w2, w1t, ln1g, dm, tm):
    t, d, dff = dm.T, dm.D, dm.DFF
    fc = dff // N_DEV

    def body(r1_ref, relu_ref, dr2_ref, w2_ref, w1t_ref, g1_ref, dr1_ref, dh_ref, sums_ref):
        @pl.when(pl.program_id(0) == 0)
        def _():
            sums_ref[...] = jnp.zeros_like(sums_ref)

        xh1, rstd1 = _ln_stats(r1_ref[...])
        dr2 = dr2_ref[...]
        dr2b = dr2.astype(BF16)
        dx1 = ALPHA * dr2
        for k in range(dff // fc):
            ks = slice(k * fc, (k + 1) * fc)
            dact = lax.dot_general(dr2b, w2_ref[ks, :], NT_DIMS, preferred_element_type=F32)
            dhb = (dact * (2.0 * relu_ref[:, ks].astype(F32))).astype(BF16)
            dh_ref[:, ks] = dhb
            dx1 = dx1 + _mm(dhb, w1t_ref[ks, :])
        dr1_ref[...] = _ln_bwd(dx1 * g1_ref[...], xh1, rstd1)
        sums_ref[0:1, :] += _colsum(dx1 * xh1)
        sums_ref[1:2, :] += _colsum(dx1)

    tile = lambda i: (i, 0)
    return pl.pallas_call(
        body, name="ffn_bwd", grid=(t // tm,),
        in_specs=[pl.BlockSpec((tm, d), tile), pl.BlockSpec((tm, dff), tile), pl.BlockSpec((tm, d), tile),
                  _resident((dff, d)), _resident((dff, d)), _resident((1, d))],
        out_specs=(pl.BlockSpec((tm, d), tile), pl.BlockSpec((tm, dff), tile), pl.BlockSpec((F32_SUBLANES, d), lambda i: (0, 0))),
        out_shape=(jax.ShapeDtypeStruct((t, d), F32), jax.ShapeDtypeStruct((t, dff), BF16),
                   jax.ShapeDtypeStruct((F32_SUBLANES, d), F32)),
        compiler_params=_params(),
    )(r1, relu, dr2, w2, w1t, ln1g)


def _wgrad(pairs, tt, fb, name, xchg=()):
    n, m = len(pairs), len(xchg)
    t, f = pairs[0][0].shape
    d = pairs[0][1].shape[1]
    squares = [sq for _, _, sq in pairs]
    nj, ni = f // fb, t // tt
    xrows = [r for _, r in xchg]

    def body(*refs):
        ins, xsrc = refs[:2 * n], refs[2 * n:2 * n + m]
        outs, xrecv = refs[2 * n + m:3 * n + m], refs[3 * n + m:3 * n + 2 * m]
        accs, sems = refs[3 * n + 2 * m:4 * n + 2 * m], refs[4 * n + 2 * m:]
        j, i = pl.program_id(0), pl.program_id(1)
        exchange = _Exchange(xsrc, xrecv, xrows, *sems) if m else None

        if m:
            @pl.when(jnp.logical_and(j == 0, i == 0))
            def _():
                exchange.start()

        @pl.when(i == 0)
        def _():
            for acc in accs:
                acc[...] = jnp.zeros_like(acc)

        for q in range(n):
            lhs = ins[2 * q][...]
            if squares[q]:
                lf = lhs.astype(F32)
                lhs = (lf * lf).astype(BF16)
            accs[q][...] += lax.dot_general(lhs, ins[2 * q + 1][...].astype(BF16), TN_DIMS, preferred_element_type=F32)

        @pl.when(i == ni - 1)
        def _():
            for q in range(n):
                outs[q][...] = accs[q][...].astype(BF16)

        if m:
            @pl.when(jnp.logical_and(j == nj - 1, i == ni - 1))
            def _():
                exchange.finish()

    lhs_spec = pl.BlockSpec((tt, fb), lambda j, i: (i, j))
    rhs_spec = pl.BlockSpec((tt, d), lambda j, i: (i, 0))
    out_spec = pl.BlockSpec((fb, d), lambda j, i: (j, 0))
    xsrcs = [a_ for a_, _ in xchg]
    return pl.pallas_call(
        body, name=name, grid=(nj, ni),
        in_specs=[lhs_spec, rhs_spec] * n + [HBM_SPEC] * m, out_specs=(out_spec,) * n + (HBM_SPEC,) * m,
        out_shape=(jax.ShapeDtypeStruct((f, d), BF16),) * n + _Exchange.out_shapes(xsrcs, xrows),
        scratch_shapes=[pltpu.VMEM((fb, d), F32)] * n + (_Exchange.semaphores(m) if m else []),
        compiler_params=_params(("arbitrary", "arbitrary")),
    )(*[a_ for lhs, rhs, _ in pairs for a_ in (lhs, rhs)], *xsrcs)


def _mixer_bwd(dr1, p, ya, yb, gu_s, dgu_s, xhv_s, rgv_s, cv_s, w_ot, w_pat, w_pbt, conv_w8, vng, vnb, ws_b, wst_b, bias_s, head_sel, xchg, dm, tm):
    t, d, wa, npj = dm.T, dm.D, dm.WA, dm.NP
    nt = t // tm
    h8, hb = F32_SUBLANES, BF16_SUBLANES
    ext = tm + 2 * h8
    prev_f, next_f = _halo_maps(tm, t, h8)
    prev_b, next_b = _halo_maps(tm, t, hb)
    nx = len(xchg)
    xsrcs, xrows = [a_ for a_, _ in xchg], [r for _, r in xchg]

    gw = d // 2

    def gate_map(rows, k):
        return lambda i: (rows(i)[0], dm.OFF_GA // gw + k)

    def body(dr_ref, drp_ref, drn_ref, pc_ref, pg0_ref, pg1_ref, pg2_ref, pg3_ref, ppc_ref, ppg0_ref, ppg1_ref,
             pnc_ref, png0_ref, png1_ref, ya_ref, yb_ref, gu_ref, dgu_ref, xhv_ref, rgv_ref, cv_ref,
             wot_ref, wpat_ref, wpbt_ref,
             cw_ref, vng_ref, vnb_ref, ws_ref, wst_ref, bias_ref, sel_ref,
             *refs):
        xsrc = refs[:nx]
        dp_ref, a_ref, dya_ref, bb_ref, dyb_ref, dws_ref, dbs_ref, dcw_ref, dbg_ref, dvn_ref = refs[nx:nx + 10]
        xrecv = refs[nx + 10:2 * nx + 10]
        mixed_ref, dvnm_ref = refs[2 * nx + 10:2 * nx + 12]
        exchange = _Exchange(xsrc, xrecv, xrows, *refs[2 * nx + 12:])
        i = pl.program_id(0)

        @pl.when(i == 0)
        def _():
            exchange.start()
            for ref in (dws_ref, dbs_ref, dcw_ref, dbg_ref, dvn_ref):
                ref[...] = jnp.zeros_like(ref)

        def col(ref, lo, width):
            return ref[:, lo:lo + width].astype(F32)

        def ext_rows(prev_blk, center, next_blk):
            return jnp.concatenate([prev_blk, center, next_blk], axis=0)

        def lanes(*refs):
            return jnp.concatenate([r[...].astype(F32) for r in refs], axis=1)

        xhv = xhv_ref[...].astype(F32)
        vn = xhv * vng_ref[...] + vnb_ref[...]
        mixed = _spatial_mix(vn, ws_ref, bias_ref, mixed_ref, dm, tm)
        gu = gu_ref[...].astype(F32)
        bb_ref[...] = (gu * mixed).astype(BF16)
        g_a, g_b = lanes(pg0_ref, pg1_ref), lanes(pg2_ref, pg3_ref)
        y_a, y_b = ya_ref[...].astype(F32), yb_ref[...].astype(F32)
        qa = y_a * g_a * (1.0 - g_a)
        qb = y_b * g_b * (1.0 - g_b)
        cv = cv_ref[...].astype(F32)
        b_a, c_a, h_a = col(pc_ref, 0, wa), col(pc_ref, dm.OFF_CA, wa), col(pc_ref, dm.OFF_HA, wa)
        a_ref[...] = (b_a * cv).astype(BF16)
        ch = c_a * h_a
        row = lax.broadcasted_iota(jnp.int32, (ext, 1), 0) + (i * tm - h8)
        inside = jnp.logical_and(row >= 0, row < t)
        dr_e = ext_rows(drp_ref[...], dr_ref[...], drn_ref[...])
        ds_e = _mm(dr_e.astype(BF16), wot_ref[...])
        dya_e = ds_e * ext_rows(lanes(ppg0_ref, ppg1_ref)[hb - h8:hb], g_a, lanes(png0_ref, png1_ref)[0:h8])
        da_e = _mm(dya_e.astype(BF16), wpat_ref[...])
        dcv_e = jnp.where(inside, da_e * ext_rows(col(ppc_ref, 0, wa)[hb - h8:hb], b_a, col(pnc_ref, 0, wa)[0:h8]), 0.0)
        dcv, (dcv_up, dcv_dn) = dcv_e[h8:h8 + tm], _row_neighbours(dcv_e, tm)
        w0, w1, w2 = cw_ref[0:1, :], cw_ref[1:2, :], cw_ref[2:3, :]
        dp_ref[:, 0:wa] = (da_e[h8:h8 + tm] * cv).astype(BF16)
        dch = w0 * dcv_dn + w1 * dcv + w2 * dcv_up
        dp_ref[:, dm.OFF_CA:dm.OFF_CA + wa] = (dch * h_a).astype(BF16)
        dp_ref[:, dm.OFF_HA:dm.OFF_HA + wa] = (dch * c_a).astype(BF16)
        dcw_ref[0:1, :] += _colsum(dcv_dn * ch)
        dcw_ref[1:2, :] += _colsum(dcv * ch)
        dcw_ref[2:3, :] += _colsum(dcv_up * ch)
        dya_ref[...] = dya_e[h8:h8 + tm].astype(BF16)
        ds = ds_e[h8:h8 + tm]
        dzga = ds * qa
        dzgb = ds * qb
        dp_ref[:, dm.OFF_GA:dm.OFF_GA + d] = dzga.astype(BF16)
        dp_ref[:, dm.OFF_GB:dm.OFF_GB + d] = dzgb.astype(BF16)
        dbg_ref[0:1, 0:d] += _colsum(dzga)
        dbg_ref[0:1, d:2 * d] += _colsum(dzgb)
        dyb = (ds * g_b).astype(BF16)
        dyb_ref[...] = dyb
        dbb = _mm(dyb, wpbt_ref[...])
        dp_ref[:, dm.OFF_UB:dm.OFF_UB + d] = (dbb * mixed * dgu_ref[...].astype(F32)).astype(BF16)
        dmb = (dbb * gu).astype(BF16)
        vb = vn.astype(BF16)
        dbs = jnp.zeros((CHUNK, CHUNK), F32)
        for cc in range(tm // CHUNK):
            r0 = cc * CHUNK
            dbs = dbs + _mm(dmb[r0:r0 + CHUNK, :], sel_ref[...])
            for h in range(dm.H):
                c0 = h * CHUNK
                blk = dmb[r0:r0 + CHUNK, c0:c0 + CHUNK]
                dvnm_ref[r0:r0 + CHUNK, c0:c0 + CHUNK] = _mm(wst_ref[h], blk)
                dws_ref[h] += lax.dot_general(blk, vb[r0:r0 + CHUNK, c0:c0 + CHUNK], NT_DIMS, preferred_element_type=F32)
        dbs_ref[...] += dbs
        dvn = dvnm_ref[...]
        dvn_ref[0:1, :] += _colsum(dvn * xhv)
        dvn_ref[1:2, :] += _colsum(dvn)
        dp_ref[:, dm.OFF_VB:dm.OFF_VB + d] = (_ln_bwd(dvn * vng_ref[...], xhv, rgv_ref[...].astype(F32))).astype(BF16)

        @pl.when(i == nt - 1)
        def _():
            exchange.finish()

    full = lambda i: (0, 0)
    tile = lambda i: (i, 0)
    hcc = _resident((dm.H, CHUNK, CHUNK))
    tok = lambda w, dt: jax.ShapeDtypeStruct((t, w), dt)
    return pl.pallas_call(
        body, name="mixer_bwd", grid=(nt,),
        in_specs=[pl.BlockSpec((tm, d), tile), pl.BlockSpec((h8, d), prev_f), pl.BlockSpec((h8, d), next_f),
                  pl.BlockSpec((tm, 3 * wa), tile)] + [pl.BlockSpec((tm, gw), gate_map(tile, k)) for k in range(4)]
        + [pl.BlockSpec((hb, wa), prev_b)] + [pl.BlockSpec((hb, gw), gate_map(prev_b, k)) for k in range(2)]
        + [pl.BlockSpec((hb, wa), next_b)] + [pl.BlockSpec((hb, gw), gate_map(next_b, k)) for k in range(2)]
        + [pl.BlockSpec((tm, d), tile)] * 6
        + [pl.BlockSpec((tm, wa), tile), _resident((d, d)), _resident((d, wa)), _resident((d, d)),
           _resident((h8, wa)), _resident((1, d)), _resident((1, d)), hcc, hcc, _resident((CHUNK, d)),
           _resident((d, CHUNK))] + [HBM_SPEC] * nx,
        out_specs=(pl.BlockSpec((tm, npj), tile), pl.BlockSpec((tm, wa), tile), pl.BlockSpec((tm, d), tile),
                   pl.BlockSpec((tm, d), tile), pl.BlockSpec((tm, d), tile),
                   pl.BlockSpec((dm.H, CHUNK, CHUNK), lambda i: (0, 0, 0)), pl.BlockSpec((CHUNK, CHUNK), full),
                   pl.BlockSpec((h8, wa), full), pl.BlockSpec((h8, 2 * d), full), pl.BlockSpec((h8, d), full))
        + (HBM_SPEC,) * nx,
        out_shape=(tok(npj, BF16), tok(wa, BF16), tok(d, BF16), tok(d, BF16), tok(d, BF16),
                   jax.ShapeDtypeStruct((dm.H, CHUNK, CHUNK), F32), jax.ShapeDtypeStruct((CHUNK, CHUNK), F32),
                   jax.ShapeDtypeStruct((h8, wa), F32), jax.ShapeDtypeStruct((h8, 2 * d), F32),
                   jax.ShapeDtypeStruct((h8, d), F32)) + _Exchange.out_shapes(xsrcs, xrows),
        scratch_shapes=[pltpu.VMEM((tm, d), F32), pltpu.VMEM((tm, d), F32)] + _Exchange.semaphores(nx),
        compiler_params=_params(),
    )(dr1, dr1, dr1, *([p] * 11), ya, yb, gu_s, dgu_s, xhv_s, rgv_s, cv_s, w_ot, w_pat, w_pbt, conv_w8, vng, vnb, ws_b, wst_b, bias_s, head_sel, *xsrcs)


def _input_grad(dp, dr1, w_int, xchg, dm, tm):
    t, d, npj = dm.T, dm.D, dm.NP
    nt = t // tm
    nx = len(xchg)
    xsrcs, xrows = [a_ for a_, _ in xchg], [r for _, r in xchg]

    def body(dp_ref, dr_ref, w_ref, *refs):
        dx_ref = refs[nx]
        exchange = _Exchange(refs[:nx], refs[nx + 1:2 * nx + 1], xrows, *refs[2 * nx + 1:])
        i = pl.program_id(0)

        @pl.when(i == 0)
        def _():
            exchange.start()

        dx_ref[...] = ALPHA * dr_ref[...] + _mm(dp_ref[...], w_ref[...])

        @pl.when(i == nt - 1)
        def _():
            exchange.finish()

    return pl.pallas_call(
        body, name="input_grad", grid=(nt,),
        in_specs=[pl.BlockSpec((tm, npj), lambda i: (i, 0)), pl.BlockSpec((tm, d), lambda i: (i, 0)), _resident((npj, d))]
        + [HBM_SPEC] * nx,
        out_specs=(pl.BlockSpec((tm, d), lambda i: (i, 0)),) + (HBM_SPEC,) * nx,
        out_shape=(jax.ShapeDtypeStruct((t, d), F32),) + _Exchange.out_shapes(xsrcs, xrows),
        scratch_shapes=_Exchange.semaphores(nx),
        compiler_params=_params(),
    )(dp, dr1, w_int, *xsrcs)


def _adamw_math(w, g, m, v):
    nm = ADAM_B1 * m + (1.0 - ADAM_B1) * g
    nv = ADAM_B2 * v + (1.0 - ADAM_B2) * (g * g)
    delta = -ADAM_LR * ((nm / (1.0 - ADAM_B1 ** ADAM_STEP)) / (jnp.sqrt(nv / (1.0 - ADAM_B2 ** ADAM_STEP)) + ADAM_EPS) + ADAM_WD * w)
    return delta, nm, nv


def _adamw(w, g, m, v, name):
    _, rows, cols = w.shape
    tr = 256 if rows % 256 == 0 else rows
    from_slots = g.ndim == 3

    def body(w_ref, g_ref, m_ref, v_ref, go_ref, d_ref, nm_ref, nv_ref):
        if from_slots:
            g_ = g_ref[0].astype(F32)
            for k in range(1, N_DEV):
                g_ = g_ + g_ref[k].astype(F32)
        else:
            g_ = g_ref[...]
        go_ref[0] = g_
        d_ref[0], nm_ref[0], nv_ref[0] = _adamw_math(w_ref[0], g_, m_ref[0], v_ref[0])

    spec = pl.BlockSpec((1, tr, cols), lambda i: (0, i, 0))
    g_spec = pl.BlockSpec((N_DEV, tr, cols), lambda i: (0, i, 0)) if from_slots else pl.BlockSpec((tr, cols), lambda i: (i, 0))
    shp = jax.ShapeDtypeStruct((1, rows, cols), F32)
    return pl.pallas_call(
        body, name=name, grid=(rows // tr,), in_specs=[spec, g_spec, spec, spec], out_specs=(spec,) * 4, out_shape=(shp,) * 4,
        compiler_params=_params(),
    )(w, g, m, v)


def _adamw_small(ws, gs, ms, vs):
    n = len(ws)

    def body(*refs):
        ins, outs = refs[:4 * n], refs[4 * n:]
        for k in range(n):
            w_ref, g_ref, m_ref, v_ref = ins[k], ins[n + k], ins[2 * n + k], ins[3 * n + k]
            outs[k][...], outs[n + k][...], outs[2 * n + k][...] = _adamw_math(w_ref[...], g_ref[...], m_ref[...], v_ref[...])

    shapes = tuple(jax.ShapeDtypeStruct(w.shape, F32) for w in ws)
    out = pl.pallas_call(body, name="adamw_small", out_shape=shapes * 3, compiler_params=_params(()))(*ws, *gs, *ms, *vs)
    return out[:n], out[n:2 * n], out[2 * n:]


def _to_slab(parts):
    flat = jnp.concatenate([q.reshape(-1) for q in parts])
    pad = (-flat.shape[0]) % (F32_SUBLANES * LANES)
    return jnp.pad(flat, (0, pad)).reshape(-1, LANES)


def _from_slab(slab, shapes):
    flat = slab.reshape(-1)
    out, off = [], 0
    for s in shapes:
        n = math.prod(s)
        out.append(flat[off:off + n].reshape(s))
        off += n
    return out


def kernel(x, w_in, b_gate, conv_w, v_norm_g, v_norm_b, w_s, b_s, w_pa, w_pb, w_o, ln1_g, ln1_b, w_ff1, w_ff2, ln2_g, ln2_b, loss_target, m_w_in, m_b_gate, m_conv_w, m_v_norm_g, m_v_norm_b, m_w_s, m_b_s, m_w_pa, m_w_pb, m_w_o, m_ln1_g, m_ln1_b, m_w_ff1, m_w_ff2, m_ln2_g, m_ln2_b, v_w_in, v_b_gate, v_conv_w, v_v_norm_g, v_v_norm_b, v_w_s, v_b_s, v_w_pa, v_w_pb, v_w_o, v_ln1_g, v_ln1_b, v_w_ff1, v_w_ff2, v_ln2_g, v_ln2_b):
    t, d = x.shape[1], x.shape[2]
    dm = _Dims(t, d)
    tm = 256 if t % 256 == 0 else CHUNK
    tm_big = 512 if t % 512 == 0 else tm
    tt = 1024 if t % 1024 == 0 else tm
    me = 4 * lax.axis_index("x") + 2 * lax.axis_index("y") + lax.axis_index("c")
    x2, tgt = x[0], loss_target[0]

    conv_bits = lax.bitcast_convert_type(conv_w[0], BF16).reshape(-1)
    conv_blk = jnp.pad(conv_bits, (0, dm.conv_rows * d - conv_bits.shape[0])).reshape(dm.conv_rows, d)
    w_int, conv_g = _all_gather([w_in[0].T.astype(BF16), conv_blk])
    wa8 = dm.WA // N_DEV
    conv_all = lax.bitcast_convert_type(conv_g.reshape(N_DEV, -1)[:, :3 * wa8 * 2].reshape(N_DEV, 3, wa8, 2), F32)
    conv_full = jnp.transpose(conv_all, (1, 0, 2)).reshape(3, dm.WA)
    conv_w8 = jnp.pad(conv_full, ((0, F32_SUBLANES - 3), (0, 0)))
    ws_b = w_s[0].astype(BF16)
    wst_b = jnp.transpose(w_s[0], (0, 2, 1)).astype(BF16)
    bias_s = jnp.repeat(b_s[0].T, CHUNK, axis=1)
    head_sel = (jnp.arange(d)[:, None] // CHUNK == jnp.arange(CHUNK)[None, :]).astype(BF16)

    p, w_pa_f, w_pb_f, w_o_f, w_1t, w_2 = _proj_in(
        x2, w_int, b_gate, [w_pa[0].astype(BF16), w_pb[0].astype(BF16), w_o[0].astype(BF16), w_ff1[0].T.astype(BF16),
                            w_ff2[0].astype(BF16)], dm, tm_big)
    ya, yb, r1, gu_s, dgu_s, xhv_s, rgv_s, cv_s, s_m = _mixer_fwd(
        p, x2, w_pa_f, w_pb_f, w_o_f, conv_w8, v_norm_g, v_norm_b, ws_b, bias_s, dm, tm)
    relu, x1b, dr2, dr2b, sums2 = _ffn_fwd(r1, tgt, w_1t, w_2, ln1_g, ln1_b, ln2_g, ln2_b, dm, tm_big)
    dr1, dh1, sums1 = _ffn_bwd(r1, relu, dr2, w_2, w_1t, ln1_g, dm, tm_big)
    fb = min(1024, dm.DFF)
    rows = dm.shard_rows
    g_ff1t, g_ff2 = _wgrad([(dh1, x1b, False), (relu, dr2b, True)], tt, fb, "ffn_wgrad")
    dp, a_m, dya, bb_m, dyb, g_ws, g_bs_t, g_cw, g_bg, g_vn, got_ff1t, got_ff2 = _mixer_bwd(
        dr1, p, ya, yb, gu_s, dgu_s, xhv_s, rgv_s, cv_s, w_o_f.T, w_pa_f.T, w_pb_f.T, conv_w8, v_norm_g, v_norm_b, ws_b, wst_b, bias_s, head_sel,
        [(g_ff1t, rows[4]), (g_ff2, rows[5])], dm, tm)
    (g_pa,) = _wgrad([(a_m, dya, False)], tt, dm.WA, "w_pa_grad")
    g_o, g_pb = _wgrad([(s_m, dr1, False), (bb_m, dyb, False)], tt, d, "w_o_pb_grad")
    g_int, got_pa, got_pb, got_o = _wgrad([(dp, x2, False)], tt, 17 * LANES, "w_in_grad",
                                          xchg=[(g_pa, rows[1]), (g_pb, rows[2]), (g_o, rows[3])])
    small_parts = [g_bg[0], g_cw[0:3], g_vn[0], g_vn[1], g_ws, g_bs_t[:, :dm.H].T,
                   sums1[0], sums1[1], sums2[1], sums2[2], sums2[0]]
    grad_x, got_int, got_s = _input_grad(dp, dr1, w_int, [(g_int, rows[0]), (_to_slab(small_parts), None)], dm, tm_big)

    ssum = _sum_slots(got_s, 256, "sum_small")
    (s_bg, s_cw, s_vng, s_vnb, s_ws, s_bs, s_l1g, s_l1b, s_l2g, s_l2b, s_sq) = _from_slab(
        ssum, [(1, 2 * d), (3, dm.WA), (1, d), (1, d), (1, dm.H, CHUNK, CHUNK), (1, dm.H, CHUNK),
               (1, d), (1, d), (1, d), (1, d), (d,)])
    loss = 0.5 * jnp.sum(s_sq) / d
    s_cw = lax.dynamic_slice(s_cw, (0, me * wa8), (3, wa8))[None]
    big_g = [_sum_slots(got_int, 256, "sum_grads_w_in").T, got_pa, got_pb, got_o,
             _sum_slots(got_ff1t, 256, "sum_grads_w_ff1").T, got_ff2]
    big_w = [(w_in, m_w_in, v_w_in), (w_pa, m_w_pa, v_w_pa), (w_pb, m_w_pb, v_w_pb), (w_o, m_w_o, v_w_o),
             (w_ff1, m_w_ff1, v_w_ff1), (w_ff2, m_w_ff2, v_w_ff2)]
    big_out = [_adamw(w, g, m, v, "adamw_%d" % k) for k, ((w, m, v), g) in enumerate(zip(big_w, big_g))]
    small_w = [(b_gate, m_b_gate, v_b_gate), (conv_w, m_conv_w, v_conv_w), (v_norm_g, m_v_norm_g, v_v_norm_g),
               (v_norm_b, m_v_norm_b, v_v_norm_b), (w_s, m_w_s, v_w_s), (b_s, m_b_s, v_b_s), (ln1_g, m_ln1_g, v_ln1_g),
               (ln1_b, m_ln1_b, v_ln1_b), (ln2_g, m_ln2_g, v_ln2_g), (ln2_b, m_ln2_b, v_ln2_b)]
    small_g = [s_bg, s_cw, s_vng, s_vnb, s_ws, s_bs, s_l1g, s_l1b, s_l2g, s_l2b]
    small_out = list(zip(*_adamw_small([w for w, _, _ in small_w], small_g, [m for _, m, _ in small_w],
                                       [v for _, _, v in small_w])))

    order = [("b", 0), ("s", 0), ("s", 1), ("s", 2), ("s", 3), ("s", 4), ("s", 5), ("b", 1), ("b", 2), ("b", 3),
             ("s", 6), ("s", 7), ("b", 4), ("b", 5), ("s", 8), ("s", 9)]
    grads, deltas, new_m, new_v = [], [], [], []
    for kind, k in order:
        if kind == "b":
            g, dl, nm, nv = big_out[k]
        else:
            g, (dl, nm, nv) = small_g[k], small_out[k]
        grads.append(g)
        deltas.append(dl)
        new_m.append(nm)
        new_v.append(nv)
    return (loss, grad_x[None], *grads, *deltas, *new_m, *new_v)
```

```python
import math

import jax
import jax.numpy as jnp
from jax import lax
from jax.experimental import pallas as pl
from jax.experimental.pallas import tpu as pltpu

F32 = jnp.float32
BF16 = jnp.bfloat16
N_DEV = 8
CHUNK = 128
LN_EPS = 1e-5
ALPHA = 2.0 ** 0.25
ADAM_LR, ADAM_B1, ADAM_B2, ADAM_EPS, ADAM_WD, ADAM_STEP = 0.001, 0.9, 0.999, 1e-08, 0.01, 10
F32_SUBLANES = 8
BF16_SUBLANES = 16
LANES = 128
VMEM_LIMIT = 56 * 1024 * 1024
MESH = pl.DeviceIdType.MESH
NT_DIMS = (((1,), (1,)), ((), ()))
TN_DIMS = (((0,), (0,)), ((), ()))
HBM_SPEC = pl.BlockSpec(memory_space=pltpu.HBM)


class _Dims:
    def __init__(self, t, d):
        self.T, self.D = t, d
        self.WA = 3 * d // 2
        self.NP = 3 * self.WA + 4 * d
        self.DFF = 4 * d
        self.H = d // CHUNK
        self.OFF_CA, self.OFF_HA = self.WA, 2 * self.WA
        self.OFF_UB = 3 * self.WA
        self.OFF_VB = self.OFF_UB + d
        self.OFF_GA = self.OFF_VB + d
        self.OFF_GB = self.OFF_GA + d
        self.shard_rows = (self.NP // N_DEV, self.WA // N_DEV, d // N_DEV, d // N_DEV, self.DFF // N_DEV, self.DFF // N_DEV)
        self.conv_rows = BF16_SUBLANES * max(1, -(-(3 * (self.WA // N_DEV) * 2) // (BF16_SUBLANES * d)))


def _params(sem=("arbitrary",), vmem=VMEM_LIMIT):
    return pltpu.CompilerParams(dimension_semantics=sem, vmem_limit_bytes=vmem)


def _mesh_pos():
    return lax.axis_index("x"), lax.axis_index("y"), lax.axis_index("c")


def _resident(shape):
    zeros = (0,) * len(shape)
    return pl.BlockSpec(shape, lambda *_: zeros, pipeline_mode=pl.Buffered(1))


class _TwoLevelGather:
    def __init__(self, shard_refs, out_refs, send_sems, recv_sems, local_sems):
        self.n = len(shard_refs)
        self.shard_refs, self.out_refs = shard_refs, out_refs
        self.send_sems, self.recv_sems, self.local_sems = send_sems, recv_sems, local_sems
        x, y, c = _mesh_pos()
        self.c = c
        self.me, self.sibling = (x, y, c), (x, y, 1 - c)
        self.near = [(1 - x, y), (x, 1 - y)]
        self.far = (1 - x, 1 - y)
        self.relay_from = (jnp.where(c == 0, 1 - x, x), jnp.where(c == 0, y, 1 - y))
        self.relay_to = (jnp.where(c == 0, x, 1 - x), jnp.where(c == 0, 1 - y, y))

    def _slot(self, a, px, py, pc):
        rows = self.shard_refs[a].shape[0]
        return self.out_refs[a].at[pl.ds((4 * px + 2 * py + pc) * rows, rows), :]

    def _copy(self, a, k, block, to, src=None):
        return pltpu.make_async_remote_copy(
            src_ref=self._slot(a, *block) if src is None else src, dst_ref=self._slot(a, *block),
            send_sem=self.send_sems.at[7 * a + k], recv_sem=self.recv_sems.at[7 * a + k], device_id=to, device_id_type=MESH)

    def _mine(self):
        return [pltpu.make_async_copy(self.shard_refs[a], self._slot(a, *self.me), self.local_sems.at[a]) for a in range(self.n)]

    def _first(self):
        out = []
        for a in range(self.n):
            out.append(self._copy(a, 0, self.me, self.sibling, src=self.shard_refs[a]))
            out += [self._copy(a, 1 + j, self.me, (*chip, self.c), src=self.shard_refs[a]) for j, chip in enumerate(self.near)]
        return out

    def _passed(self, a):
        return ([self._copy(a, 3 + j, (*chip, self.c), self.sibling) for j, chip in enumerate(self.near)]
                + [self._copy(a, 6, (*self.relay_from, self.c), (*self.relay_to, self.c))])

    def _far_passed(self, a):
        return self._copy(a, 5, (*self.far, self.c), self.sibling)

    def start(self):
        for cp in self._mine() + self._first():
            cp.start()

    def forward(self):
        for a in range(self.n):
            for j, chip in enumerate(self.near):
                self._copy(a, 1 + j, (*chip, self.c), self.me).wait_recv()
            for cp in self._passed(a):
                cp.start()

    def forward_relayed(self):
        for a in range(self.n):
            self._copy(a, 6, (*self.far, self.c), self.me).wait_recv()
            self._far_passed(a).start()

    def finish(self):
        for a in range(self.n):
            self._copy(a, 0, self.sibling, self.me).wait_recv()
            for j, chip in enumerate(self.near):
                self._copy(a, 3 + j, (*chip, 1 - self.c), self.me).wait_recv()
            self._copy(a, 5, (*self.far, 1 - self.c), self.me).wait_recv()
        for cp in self._first():
            cp.wait_send()
        for a in range(self.n):
            for cp in self._passed(a) + [self._far_passed(a)]:
                cp.wait_send()
        for cp in self._mine():
            cp.wait()

    @staticmethod
    def out_shapes(shards):
        return tuple(jax.ShapeDtypeStruct((N_DEV * s.shape[0], s.shape[1]), s.dtype) for s in shards)

    @staticmethod
    def semaphores(n):
        return [pltpu.SemaphoreType.DMA((7 * n,)), pltpu.SemaphoreType.DMA((7 * n,)), pltpu.SemaphoreType.DMA((n,))]


def _all_gather(shards):
    n = len(shards)

    def body(*refs):
        gather = _TwoLevelGather(refs[:n], refs[n:2 * n], *refs[2 * n:])
        gather.start()
        gather.forward()
        gather.forward_relayed()
        gather.finish()

    return pl.pallas_call(
        body, name="all_gather_w_in", out_shape=_TwoLevelGather.out_shapes(shards),
        in_specs=[HBM_SPEC] * n, out_specs=(HBM_SPEC,) * n, scratch_shapes=_TwoLevelGather.semaphores(n),
    )(*shards)


class _Exchange:
    def __init__(self, src_refs, recv_refs, rows, send_sems, recv_sems, local_sems):
        x, y, c = _mesh_pos()
        me = 4 * x + 2 * y + c
        self.own, self.sends, self.arrivals = [], [], []
        for a, (src, recv) in enumerate(zip(src_refs, recv_refs)):
            def blk(k, src=src, r=rows[a]):
                return src if r is None else src.at[pl.ds(k * r, r), :]

            self.own.append(pltpu.make_async_copy(blk(me), recv.at[me], local_sems.at[a]))
            for rel in range(1, N_DEV):
                px = 1 - x if rel & 4 else x
                py = 1 - y if rel & 2 else y
                pc = 1 - c if rel & 1 else c
                peer = 4 * px + 2 * py + pc
                sem = dict(send_sem=send_sems.at[7 * a + rel - 1], recv_sem=recv_sems.at[7 * a + rel - 1],
                           device_id=(px, py, pc), device_id_type=MESH)
                self.sends.append(pltpu.make_async_remote_copy(src_ref=blk(peer), dst_ref=recv.at[me], **sem))
                self.arrivals.append(pltpu.make_async_remote_copy(src_ref=blk(me), dst_ref=recv.at[peer], **sem))

    def start(self):
        for cp in self.own + self.sends:
            cp.start()

    def finish(self):
        for cp in self.arrivals:
            cp.wait_recv()
        for cp in self.sends:
            cp.wait_send()
        for cp in self.own:
            cp.wait()

    @staticmethod
    def out_shapes(srcs, rows):
        return tuple(jax.ShapeDtypeStruct((N_DEV, s.shape[0] if r is None else r, s.shape[1]), s.dtype) for s, r in zip(srcs, rows))

    @staticmethod
    def semaphores(n):
        return [pltpu.SemaphoreType.DMA((7 * n,)), pltpu.SemaphoreType.DMA((7 * n,)), pltpu.SemaphoreType.DMA((n,))]


def _sum_slots(slots, tile_rows, name):
    _, rows, cols = slots.shape
    tr = rows
    if N_DEV * rows * cols * slots.dtype.itemsize > 8 * 1024 * 1024:
        tr = next(c for c in (256, 192, 128, 64, 32, 16) if c <= tile_rows and rows % c == 0)

    def body(s_ref, o_ref):
        acc = s_ref[0].astype(F32)
        for k in range(1, N_DEV):
            acc = acc + s_ref[k].astype(F32)
        o_ref[...] = acc

    return pl.pallas_call(
        body, name=name, grid=(rows // tr,),
        in_specs=[pl.BlockSpec((N_DEV, tr, cols), lambda i: (0, i, 0))],
        out_specs=pl.BlockSpec((tr, cols), lambda i: (i, 0)),
        out_shape=jax.ShapeDtypeStruct((rows, cols), F32),
        compiler_params=_params(),
    )(slots)


def _gelu_and_grad(x):
    k0 = math.sqrt(2.0 / math.pi)
    k1 = 0.044715
    a = k1 * (x * x)
    half = 1.0 / (1.0 + jnp.exp((-2.0 * k0) * x * (1.0 + a)))
    g = x * half
    return g, half + g * (1.0 - half) * (2.0 * k0 + (6.0 * k0) * a)


def _ln_stats(r):
    mu = jnp.mean(r, axis=-1, keepdims=True)
    rc = r - mu
    var = jnp.mean(rc * rc, axis=-1, keepdims=True)
    rstd = lax.rsqrt(var + LN_EPS)
    return rc * rstd, rstd


def _ln_bwd(dxh, xh, rstd):
    return rstd * (dxh - jnp.mean(dxh, axis=-1, keepdims=True) - xh * jnp.mean(dxh * xh, axis=-1, keepdims=True))


def _colsum(a):
    return jnp.sum(a, axis=0, keepdims=True)


def _mm(a, b):
    return jnp.dot(a, b, preferred_element_type=F32)


def _halo_maps(tm, t, unit):
    per, last = tm // unit, t // unit - 1
    return (lambda i: (jnp.maximum(i * per - 1, 0), 0)), (lambda i: (jnp.minimum((i + 1) * per, last), 0))


def _proj_in(x2, w_int, b_gate, shards, dm, tm):
    t, d, npj = dm.T, dm.D, dm.NP
    cw = d // 2
    nt = t // tm
    n = len(shards)

    def body(x_ref, w_ref, bg_ref, *refs):
        p_ref = refs[n]
        gather = _TwoLevelGather(refs[:n], refs[n + 1:2 * n + 1], *refs[2 * n + 1:])
        i = pl.program_id(0)

        @pl.when(i == 0)
        def _():
            gather.start()

        @pl.when(i == nt // 2)
        def _():
            gather.forward()

        @pl.when(i == 3 * nt // 4)
        def _():
            gather.forward_relayed()

        xb = x_ref[...].astype(BF16)
        for blk in range(npj // cw):
            lo = blk * cw
            acc = lax.dot_general(xb, w_ref[lo:lo + cw, :], NT_DIMS, preferred_element_type=F32)
            if lo >= dm.OFF_GA:
                acc = jax.nn.sigmoid(acc + bg_ref[:, lo - dm.OFF_GA:lo - dm.OFF_GA + cw])
            p_ref[:, lo:lo + cw] = acc.astype(BF16)

        @pl.when(i == nt - 1)
        def _():
            gather.finish()

    return pl.pallas_call(
        body, name="proj_in", grid=(nt,),
        in_specs=[pl.BlockSpec((tm, d), lambda i: (i, 0)), _resident((npj, d)), _resident((1, 2 * d))] + [HBM_SPEC] * n,
        out_specs=(pl.BlockSpec((tm, npj), lambda i: (i, 0)),) + (HBM_SPEC,) * n,
        out_shape=(jax.ShapeDtypeStruct((t, npj), BF16),) + _TwoLevelGather.out_shapes(shards),
        scratch_shapes=_TwoLevelGather.semaphores(n),
        compiler_params=_params(),
    )(x2, w_int, b_gate, *shards)


def _row_neighbours(ext, tm):
    h, n = F32_SUBLANES, ext.shape[0]
    return pltpu.roll(ext, 1, 0)[h:h + tm], pltpu.roll(ext, n - 1, 0)[h:h + tm]


def _spatial_mix(vn, ws_ref, bias_ref, mixed_ref, dm, tm):
    vb = vn.astype(BF16)
    for cc in range(tm // CHUNK):
        r0 = cc * CHUNK
        for h in range(dm.H):
            c0 = h * CHUNK
            m = _mm(ws_ref[h], vb[r0:r0 + CHUNK, c0:c0 + CHUNK])
            mixed_ref[r0:r0 + CHUNK, c0:c0 + CHUNK] = m + bias_ref[:, c0:c0 + CHUNK]
    return mixed_ref[...]


def _mixer_fwd(p, x2, w_pa, w_pb, w_o, conv_w8, vng, vnb, ws_b, bias_s, dm, tm):
    t, d, wa, npj = dm.T, dm.D, dm.WA, dm.NP
    nt = t // tm
    hb = BF16_SUBLANES
    prev_map, next_map = _halo_maps(tm, t, hb)

    def body(p_ref, pp_ref, pn_ref, x_ref, wpa_ref, wpb_ref, wo_ref, cw_ref, vng_ref, vnb_ref, ws_ref, bias_ref,
             ya_ref, yb_ref, r1_ref, gu_ref, dgu_ref, xhv_ref, rgv_ref, cv_ref, s_ref, mixed_ref):
        i = pl.program_id(0)

        def col(ref, lo, width):
            return ref[:, lo:lo + width].astype(F32)

        ch = col(p_ref, dm.OFF_CA, wa) * col(p_ref, dm.OFF_HA, wa)
        chp = (col(pp_ref, dm.OFF_CA, wa) * col(pp_ref, dm.OFF_HA, wa))[hb - F32_SUBLANES:hb]
        chn = (col(pn_ref, dm.OFF_CA, wa) * col(pn_ref, dm.OFF_HA, wa))[0:F32_SUBLANES]
        ch_e = jnp.concatenate([jnp.where(i == 0, 0.0, chp), ch, jnp.where(i == nt - 1, 0.0, chn)], axis=0)
        up, dn = _row_neighbours(ch_e, tm)
        cv = cw_ref[0:1, :] * up + cw_ref[1:2, :] * ch + cw_ref[2:3, :] * dn
        cv_ref[...] = cv.astype(BF16)
        ya = _mm((col(p_ref, 0, wa) * cv).astype(BF16), wpa_ref[...])
        gv, dgelu_v = _gelu_and_grad(col(p_ref, dm.OFF_VB, d))
        xhv, rstdv = _ln_stats(gv)
        xhv_ref[...] = xhv.astype(BF16)
        rgv_ref[...] = (rstdv * dgelu_v).astype(BF16)
        mixed = _spatial_mix(xhv * vng_ref[...] + vnb_ref[...], ws_ref, bias_ref, mixed_ref, dm, tm)
        gu, dgelu_u = _gelu_and_grad(col(p_ref, dm.OFF_UB, d))
        gu_ref[...] = gu.astype(BF16)
        dgu_ref[...] = dgelu_u.astype(BF16)
        yb = _mm((gu * mixed).astype(BF16), wpb_ref[...])
        sb = (col(p_ref, dm.OFF_GA, d) * ya + col(p_ref, dm.OFF_GB, d) * yb).astype(BF16)
        s_ref[...] = sb
        mix = _mm(sb, wo_ref[...])
        ya_ref[...] = ya.astype(BF16)
        yb_ref[...] = yb.astype(BF16)
        r1_ref[...] = ALPHA * x_ref[...] + mix

    tile = lambda i: (i, 0)
    return pl.pallas_call(
        body, name="mixer_fwd", grid=(nt,),
        in_specs=[pl.BlockSpec((tm, npj), tile), pl.BlockSpec((hb, npj), prev_map), pl.BlockSpec((hb, npj), next_map),
                  pl.BlockSpec((tm, d), tile), _resident((wa, d)), _resident((d, d)), _resident((d, d)),
                  _resident((F32_SUBLANES, wa)), _resident((1, d)), _resident((1, d)),
                  _resident((dm.H, CHUNK, CHUNK)), _resident((CHUNK, d))],
        out_specs=(pl.BlockSpec((tm, d), tile),) * 7 + (pl.BlockSpec((tm, wa), tile), pl.BlockSpec((tm, d), tile)),
        out_shape=(jax.ShapeDtypeStruct((t, d), BF16), jax.ShapeDtypeStruct((t, d), BF16), jax.ShapeDtypeStruct((t, d), F32))
        + (jax.ShapeDtypeStruct((t, d), BF16),) * 4 + (jax.ShapeDtypeStruct((t, wa), BF16), jax.ShapeDtypeStruct((t, d), BF16)),
        scratch_shapes=[pltpu.VMEM((tm, d), F32)],
        compiler_params=_params(),
    )(p, p, p, x2, w_pa, w_pb, w_o, conv_w8, vng, vnb, ws_b, bias_s)


def _ffn_fwd(r1, tgt, w1t, w2, ln1g, ln1b, ln2g, ln2b, dm, tm):
    t, d, dff = dm.T, dm.D, dm.DFF
    fc = dff // N_DEV

    def body(r1_ref, tgt_ref, w1t_ref, w2_ref, g1_ref, b1_ref, g2_ref, b2_ref, relu_ref, x1_ref, dr2_ref, dr2b_ref, sums_ref):
        @pl.when(pl.program_id(0) == 0)
        def _():
            sums_ref[...] = jnp.zeros_like(sums_ref)

        xh1, _ = _ln_stats(r1_ref[...])
        x1 = xh1 * g1_ref[...] + b1_ref[...]
        x1b = x1.astype(BF16)
        x1_ref[...] = x1b
        ffn = jnp.zeros((tm, d), F32)
        for k in range(dff // fc):
            ks = slice(k * fc, (k + 1) * fc)
            r = jnp.maximum(lax.dot_general(x1b, w1t_ref[ks, :], NT_DIMS, preferred_element_type=F32), 0.0)
            relu_ref[:, ks] = r.astype(BF16)
            ffn = ffn + _mm((r * r).astype(BF16), w2_ref[ks, :])
        xh2, rstd2 = _ln_stats(ALPHA * x1 + ffn)
        diff = xh2 * g2_ref[...] + b2_ref[...] - tgt_ref[...]
        dy = diff * (1.0 / d)
        dr2 = _ln_bwd(dy * g2_ref[...], xh2, rstd2)
        dr2_ref[...] = dr2
        dr2b_ref[...] = dr2.astype(BF16)
        sums_ref[0:1, :] += _colsum(diff * diff)
        sums_ref[1:2, :] += _colsum(dy * xh2)
        sums_ref[2:3, :] += _colsum(dy)

    tile = lambda i: (i, 0)
    vec = _resident((1, d))
    return pl.pallas_call(
        body, name="ffn_fwd", grid=(t // tm,),
        in_specs=[pl.BlockSpec((tm, d), tile), pl.BlockSpec((tm, d), tile), _resident((dff, d)), _resident((dff, d)),
                  vec, vec, vec, vec],
        out_specs=(pl.BlockSpec((tm, dff), tile), pl.BlockSpec((tm, d), tile), pl.BlockSpec((tm, d), tile),
                   pl.BlockSpec((tm, d), tile), pl.BlockSpec((F32_SUBLANES, d), lambda i: (0, 0))),
        out_shape=(jax.ShapeDtypeStruct((t, dff), BF16), jax.ShapeDtypeStruct((t, d), BF16), jax.ShapeDtypeStruct((t, d), F32),
                   jax.ShapeDtypeStruct((t, d), BF16), jax.ShapeDtypeStruct((F32_SUBLANES, d), F32)),
        compiler_params=_params(),
    )(r1, tgt, w1t, w2, ln1g, ln1b, ln2g, ln2b)


def _ffn_bwd(r1, relu, dr2, w2, w1t, ln1g, dm, tm):
    t, d, dff = dm.T, dm.D, dm.DFF
    fc = dff // N_DEV

    def body(r1_ref, relu_ref, dr2_ref, w2_ref, w1t_ref, g1_ref, dr1_ref, dh_ref, sums_ref):
        @pl.when(pl.program_id(0) == 0)
        def _():
            sums_ref[...] = jnp.zeros_like(sums_ref)

        xh1, rstd1 = _ln_stats(r1_ref[...])
        dr2 = dr2_ref[...]
        dr2b = dr2.astype(BF16)
        dx1 = ALPHA * dr2
        for k in range(dff // fc):
            ks = slice(k * fc, (k + 1) * fc)
            dact = lax.dot_general(dr2b, w2_ref[ks, :], NT_DIMS, preferred_element_type=F32)
            dhb = (dact * (2.0 * relu_ref[:, ks].astype(F32))).astype(BF16)
            dh_ref[:, ks] = dhb
            dx1 = dx1 + _mm(dhb, w1t_ref[ks, :])
        dr1_ref[...] = _ln_bwd(dx1 * g1_ref[...], xh1, rstd1)
        sums_ref[0:1, :] += _colsum(dx1 * xh1)
        sums_ref[1:2, :] += _colsum(dx1)

    tile = lambda i: (i, 0)
    return pl.pallas_call(
        body, name="ffn_bwd", grid=(t // tm,),
        in_specs=[pl.BlockSpec((tm, d), tile), pl.BlockSpec((tm, dff), tile), pl.BlockSpec((tm, d), tile),
                  _resident((dff, d)), _resident((dff, d)), _resident((1, d))],
        out_specs=(pl.BlockSpec((tm, d), tile), pl.BlockSpec((tm, dff), tile), pl.BlockSpec((F32_SUBLANES, d), lambda i: (0, 0))),
        out_shape=(jax.ShapeDtypeStruct((t, d), F32), jax.ShapeDtypeStruct((t, dff), BF16),
                   jax.ShapeDtypeStruct((F32_SUBLANES, d), F32)),
        compiler_params=_params(),
    )(r1, relu, dr2, w2, w1t, ln1g)


def _wgrad(pairs, tt, fb, name, xchg=()):
    n, m = len(pairs), len(xchg)
    t, f = pairs[0][0].shape
    d = pairs[0][1].shape[1]
    squares = [sq for _, _, sq in pairs]
    nj, ni = f // fb, t // tt
    xrows = [r for _, r in xchg]

    def body(*refs):
        ins, xsrc = refs[:2 * n], refs[2 * n:2 * n + m]
        outs, xrecv = refs[2 * n + m:3 * n + m], refs[3 * n + m:3 * n + 2 * m]
        accs, sems = refs[3 * n + 2 * m:4 * n + 2 * m], refs[4 * n + 2 * m:]
        j, i = pl.program_id(0), pl.program_id(1)
        exchange = _Exchange(xsrc, xrecv, xrows, *sems) if m else None

        if m:
            @pl.when(jnp.logical_and(j == 0, i == 0))
            def _():
                exchange.start()

        @pl.when(i == 0)
        def _():
            for acc in accs:
                acc[...] = jnp.zeros_like(acc)

        for q in range(n):
            lhs = ins[2 * q][...]
            if squares[q]:
                lf = lhs.astype(F32)
                lhs = (lf * lf).astype(BF16)
            accs[q][...] += lax.dot_general(lhs, ins[2 * q + 1][...].astype(BF16), TN_DIMS, preferred_element_type=F32)

        @pl.when(i == ni - 1)
        def _():
            for q in range(n):
                outs[q][...] = accs[q][...].astype(BF16)

        if m:
            @pl.when(jnp.logical_and(j == nj - 1, i == ni - 1))
            def _():
                exchange.finish()

    lhs_spec = pl.BlockSpec((tt, fb), lambda j, i: (i, j))
    rhs_spec = pl.BlockSpec((tt, d), lambda j, i: (i, 0))
    out_spec = pl.BlockSpec((fb, d), lambda j, i: (j, 0))
    xsrcs = [a_ for a_, _ in xchg]
    return pl.pallas_call(
        body, name=name, grid=(nj, ni),
        in_specs=[lhs_spec, rhs_spec] * n + [HBM_SPEC] * m, out_specs=(out_spec,) * n + (HBM_SPEC,) * m,
        out_shape=(jax.ShapeDtypeStruct((f, d), BF16),) * n + _Exchange.out_shapes(xsrcs, xrows),
        scratch_shapes=[pltpu.VMEM((fb, d), F32)] * n + (_Exchange.semaphores(m) if m else []),
        compiler_params=_params(("arbitrary", "arbitrary")),
    )(*[a_ for lhs, rhs, _ in pairs for a_ in (lhs, rhs)], *xsrcs)


def _mixer_bwd(dr1, p, ya, yb, gu_s, dgu_s, xhv_s, rgv_s, cv_s, w_ot, w_pat, w_pbt, conv_w8, vng, vnb, ws_b, wst_b, bias_s, head_sel, xchg, dm, tm):
    t, d, wa, npj = dm.T, dm.D, dm.WA, dm.NP
    nt = t // tm
    h8, hb = F32_SUBLANES, BF16_SUBLANES
    ext = tm + 2 * h8
    prev_f, next_f = _halo_maps(tm, t, h8)
    prev_b, next_b = _halo_maps(tm, t, hb)
    nx = len(xchg)
    xsrcs, xrows = [a_ for a_, _ in xchg], [r for _, r in xchg]

    gw = d // 2

    def gate_map(rows, k):
        return lambda i: (rows(i)[0], dm.OFF_GA // gw + k)

    def body(dr_ref, drp_ref, drn_ref, pc_ref, pg0_ref, pg1_ref, pg2_ref, pg3_ref, ppc_ref, ppg0_ref, ppg1_ref,
             pnc_ref, png0_ref, png1_ref, ya_ref, yb_ref, gu_ref, dgu_ref, xhv_ref, rgv_ref, cv_ref,
             wot_ref, wpat_ref, wpbt_ref,
             cw_ref, vng_ref, vnb_ref, ws_ref, wst_ref, bias_ref, sel_ref,
             *refs):
        xsrc = refs[:nx]
        dp_ref, a_ref, dya_ref, bb_ref, dyb_ref, dws_ref, dbs_ref, dcw_ref, dbg_ref, dvn_ref = refs[nx:nx + 10]
        xrecv = refs[nx + 10:2 * nx + 10]
        mixed_ref, dvnm_ref = refs[2 * nx + 10:2 * nx + 12]
        exchange = _Exchange(xsrc, xrecv, xrows, *refs[2 * nx + 12:])
        i = pl.program_id(0)

        @pl.when(i == 0)
        def _():
            exchange.start()
            for ref in (dws_ref, dbs_ref, dcw_ref, dbg_ref, dvn_ref):
                ref[...] = jnp.zeros_like(ref)

        def col(ref, lo, width):
            return ref[:, lo:lo + width].astype(F32)

        def ext_rows(prev_blk, center, next_blk):
            return jnp.concatenate([prev_blk, center, next_blk], axis=0)

        def lanes(*refs):
            return jnp.concatenate([r[...].astype(F32) for r in refs], axis=1)

        xhv = xhv_ref[...].astype(F32)
        vn = xhv * vng_ref[...] + vnb_ref[...]
        mixed = _spatial_mix(vn, ws_ref, bias_ref, mixed_ref, dm, tm)
        gu = gu_ref[...].astype(F32)
        bb_ref[...] = (gu * mixed).astype(BF16)
        g_a, g_b = lanes(pg0_ref, pg1_ref), lanes(pg2_ref, pg3_ref)
        y_a, y_b = ya_ref[...].astype(F32), yb_ref[...].astype(F32)
        qa = y_a * g_a * (1.0 - g_a)
        qb = y_b * g_b * (1.0 - g_b)
        cv = cv_ref[...].astype(F32)
        b_a, c_a, h_a = col(pc_ref, 0, wa), col(pc_ref, dm.OFF_CA, wa), col(pc_ref, dm.OFF_HA, wa)
        a_ref[...] = (b_a * cv).astype(BF16)
        ch = c_a * h_a
        row = lax.broadcasted_iota(jnp.int32, (ext, 1), 0) + (i * tm - h8)
        inside = jnp.logical_and(row >= 0, row < t)
        dr_e = ext_rows(drp_ref[...], dr_ref[...], drn_ref[...])
        ds_e = _mm(dr_e.astype(BF16), wot_ref[...])
        dya_e = ds_e * ext_rows(lanes(ppg0_ref, ppg1_ref)[hb - h8:hb], g_a, lanes(png0_ref, png1_ref)[0:h8])
        da_e = _mm(dya_e.astype(BF16), wpat_ref[...])
        dcv_e = jnp.where(inside, da_e * ext_rows(col(ppc_ref, 0, wa)[hb - h8:hb], b_a, col(pnc_ref, 0, wa)[0:h8]), 0.0)
        dcv, (dcv_up, dcv_dn) = dcv_e[h8:h8 + tm], _row_neighbours(dcv_e, tm)
        w0, w1, w2 = cw_ref[0:1, :], cw_ref[1:2, :], cw_ref[2:3, :]
        dp_ref[:, 0:wa] = (da_e[h8:h8 + tm] * cv).astype(BF16)
        dch = w0 * dcv_dn + w1 * dcv + w2 * dcv_up
        dp_ref[:, dm.OFF_CA:dm.OFF_CA + wa] = (dch * h_a).astype(BF16)
        dp_ref[:, dm.OFF_HA:dm.OFF_HA + wa] = (dch * c_a).astype(BF16)
        dcw_ref[0:1, :] += _colsum(dcv_dn * ch)
        dcw_ref[1:2, :] += _colsum(dcv * ch)
        dcw_ref[2:3, :] += _colsum(dcv_up * ch)
        dya_ref[...] = dya_e[h8:h8 + tm].astype(BF16)
        ds = ds_e[h8:h8 + tm]
        dzga = ds * qa
        dzgb = ds * qb
        dp_ref[:, dm.OFF_GA:dm.OFF_GA + d] = dzga.astype(BF16)
        dp_ref[:, dm.OFF_GB:dm.OFF_GB + d] = dzgb.astype(BF16)
        dbg_ref[0:1, 0:d] += _colsum(dzga)
        dbg_ref[0:1, d:2 * d] += _colsum(dzgb)
        dyb = (ds * g_b).astype(BF16)
        dyb_ref[...] = dyb
        dbb = _mm(dyb, wpbt_ref[...])
        dp_ref[:, dm.OFF_UB:dm.OFF_UB + d] = (dbb * mixed * dgu_ref[...].astype(F32)).astype(BF16)
        dmb = (dbb * gu).astype(BF16)
        vb = vn.astype(BF16)
        dbs = jnp.zeros((CHUNK, CHUNK), F32)
        for cc in range(tm // CHUNK):
            r0 = cc * CHUNK
            dbs = dbs + _mm(dmb[r0:r0 + CHUNK, :], sel_ref[...])
            for h in range(dm.H):
                c0 = h * CHUNK
                blk = dmb[r0:r0 + CHUNK, c0:c0 + CHUNK]
                dvnm_ref[r0:r0 + CHUNK, c0:c0 + CHUNK] = _mm(wst_ref[h], blk)
                dws_ref[h] += lax.dot_general(blk, vb[r0:r0 + CHUNK, c0:c0 + CHUNK], NT_DIMS, preferred_element_type=F32)
        dbs_ref[...] += dbs
        dvn = dvnm_ref[...]
        dvn_ref[0:1, :] += _colsum(dvn * xhv)
        dvn_ref[1:2, :] += _colsum(dvn)
        dp_ref[:, dm.OFF_VB:dm.OFF_VB + d] = (_ln_bwd(dvn * vng_ref[...], xhv, rgv_ref[...].astype(F32))).astype(BF16)

        @pl.when(i == nt - 1)
        def _():
            exchange.finish()

    full = lambda i: (0, 0)
    tile = lambda i: (i, 0)
    hcc = _resident((dm.H, CHUNK, CHUNK))
    tok = lambda w, dt: jax.ShapeDtypeStruct((t, w), dt)
    return pl.pallas_call(
        body, name="mixer_bwd", grid=(nt,),
        in_specs=[pl.BlockSpec((tm, d), tile), pl.BlockSpec((h8, d), prev_f), pl.BlockSpec((h8, d), next_f),
                  pl.BlockSpec((tm, 3 * wa), tile)] + [pl.BlockSpec((tm, gw), gate_map(tile, k)) for k in range(4)]
        + [pl.BlockSpec((hb, wa), prev_b)] + [pl.BlockSpec((hb, gw), gate_map(prev_b, k)) for k in range(2)]
        + [pl.BlockSpec((hb, wa), next_b)] + [pl.BlockSpec((hb, gw), gate_map(next_b, k)) for k in range(2)]
        + [pl.BlockSpec((tm, d), tile)] * 6
        + [pl.BlockSpec((tm, wa), tile), _resident((d, d)), _resident((d, wa)), _resident((d, d)),
           _resident((h8, wa)), _resident((1, d)), _resident((1, d)), hcc, hcc, _resident((CHUNK, d)),
           _resident((d, CHUNK))] + [HBM_SPEC] * nx,
        out_specs=(pl.BlockSpec((tm, npj), tile), pl.BlockSpec((tm, wa), tile), pl.BlockSpec((tm, d), tile),
                   pl.BlockSpec((tm, d), tile), pl.BlockSpec((tm, d), tile),
                   pl.BlockSpec((dm.H, CHUNK, CHUNK), lambda i: (0, 0, 0)), pl.BlockSpec((CHUNK, CHUNK), full),
                   pl.BlockSpec((h8, wa), full), pl.BlockSpec((h8, 2 * d), full), pl.BlockSpec((h8, d), full))
        + (HBM_SPEC,) * nx,
        out_shape=(tok(npj, BF16), tok(wa, BF16), tok(d, BF16), tok(d, BF16), tok(d, BF16),
                   jax.ShapeDtypeStruct((dm.H, CHUNK, CHUNK), F32), jax.ShapeDtypeStruct((CHUNK, CHUNK), F32),
                   jax.ShapeDtypeStruct((h8, wa), F32), jax.ShapeDtypeStruct((h8, 2 * d), F32),
                   jax.ShapeDtypeStruct((h8, d), F32)) + _Exchange.out_shapes(xsrcs, xrows),
        scratch_shapes=[pltpu.VMEM((tm, d), F32), pltpu.VMEM((tm, d), F32)] + _Exchange.semaphores(nx),
        compiler_params=_params(),
    )(dr1, dr1, dr1, *([p] * 11), ya, yb, gu_s, dgu_s, xhv_s, rgv_s, cv_s, w_ot, w_pat, w_pbt, conv_w8, vng, vnb, ws_b, wst_b, bias_s, head_sel, *xsrcs)


def _input_grad(dp, dr1, w_int, xchg, dm, tm):
    t, d, npj = dm.T, dm.D, dm.NP
    nt = t // tm
    nx = len(xchg)
    xsrcs, xrows = [a_ for a_, _ in xchg], [r for _, r in xchg]

    def body(dp_ref, dr_ref, w_ref, *refs):
        dx_ref = refs[nx]
        exchange = _Exchange(refs[:nx], refs[nx + 1:2 * nx + 1], xrows, *refs[2 * nx + 1:])
        i = pl.program_id(0)

        @pl.when(i == 0)
        def _():
            exchange.start()

        dx_ref[...] = ALPHA * dr_ref[...] + _mm(dp_ref[...], w_ref[...])

        @pl.when(i == nt - 1)
        def _():
            exchange.finish()

    return pl.pallas_call(
        body, name="input_grad", grid=(nt,),
        in_specs=[pl.BlockSpec((tm, npj), lambda i: (i, 0)), pl.BlockSpec((tm, d), lambda i: (i, 0)), _resident((npj, d))]
        + [HBM_SPEC] * nx,
        out_specs=(pl.BlockSpec((tm, d), lambda i: (i, 0)),) + (HBM_SPEC,) * nx,
        out_shape=(jax.ShapeDtypeStruct((t, d), F32),) + _Exchange.out_shapes(xsrcs, xrows),
        scratch_shapes=_Exchange.semaphores(nx),
        compiler_params=_params(),
    )(dp, dr1, w_int, *xsrcs)


def _adamw_math(w, g, m, v):
    nm = ADAM_B1 * m + (1.0 - ADAM_B1) * g
    nv = ADAM_B2 * v + (1.0 - ADAM_B2) * (g * g)
    delta = -ADAM_LR * ((nm / (1.0 - ADAM_B1 ** ADAM_STEP)) / (jnp.sqrt(nv / (1.0 - ADAM_B2 ** ADAM_STEP)) + ADAM_EPS) + ADAM_WD * w)
    return delta, nm, nv


def _adamw(w, g, m, v, name):
    _, rows, cols = w.shape
    tr = 256 if rows % 256 == 0 else rows
    from_slots = g.ndim == 3

    def body(w_ref, g_ref, m_ref, v_ref, go_ref, d_ref, nm_ref, nv_ref):
        if from_slots:
            g_ = g_ref[0].astype(F32)
            for k in range(1, N_DEV):
                g_ = g_ + g_ref[k].astype(F32)
        else:
            g_ = g_ref[...]
        go_ref[0] = g_
        d_ref[0], nm_ref[0], nv_ref[0] = _adamw_math(w_ref[0], g_, m_ref[0], v_ref[0])

    spec = pl.BlockSpec((1, tr, cols), lambda i: (0, i, 0))
    g_spec = pl.BlockSpec((N_DEV, tr, cols), lambda i: (0, i, 0)) if from_slots else pl.BlockSpec((tr, cols), lambda i: (i, 0))
    shp = jax.ShapeDtypeStruct((1, rows, cols), F32)
    return pl.pallas_call(
        body, name=name, grid=(rows // tr,), in_specs=[spec, g_spec, spec, spec], out_specs=(spec,) * 4, out_shape=(shp,) * 4,
        compiler_params=_params(),
    )(w, g, m, v)


def _adamw_small(ws, gs, ms, vs):
    n = len(ws)

    def body(*refs):
        ins, outs = refs[:4 * n], refs[4 * n:]
        for k in range(n):
            w_ref, g_ref, m_ref, v_ref = ins[k], ins[n + k], ins[2 * n + k], ins[3 * n + k]
            outs[k][...], outs[n + k][...], outs[2 * n + k][...] = _adamw_math(w_ref[...], g_ref[...], m_ref[...], v_ref[...])

    shapes = tuple(jax.ShapeDtypeStruct(w.shape, F32) for w in ws)
    out = pl.pallas_call(body, name="adamw_small", out_shape=shapes * 3, compiler_params=_params(()))(*ws, *gs, *ms, *vs)
    return out[:n], out[n:2 * n], out[2 * n:]


def _to_slab(parts):
    flat = jnp.concatenate([q.reshape(-1) for q in parts])
    pad = (-flat.shape[0]) % (F32_SUBLANES * LANES)
    return jnp.pad(flat, (0, pad)).reshape(-1, LANES)


def _from_slab(slab, shapes):
    flat = slab.reshape(-1)
    out, off = [], 0
    for s in shapes:
        n = math.prod(s)
        out.append(flat[off:off + n].reshape(s))
        off += n
    return out


def kernel(x, w_in, b_gate, conv_w, v_norm_g, v_norm_b, w_s, b_s, w_pa, w_pb, w_o, ln1_g, ln1_b, w_ff1, w_ff2, ln2_g, ln2_b, loss_target, m_w_in, m_b_gate, m_conv_w, m_v_norm_g, m_v_norm_b, m_w_s, m_b_s, m_w_pa, m_w_pb, m_w_o, m_ln1_g, m_ln1_b, m_w_ff1, m_w_ff2, m_ln2_g, m_ln2_b, v_w_in, v_b_gate, v_conv_w, v_v_norm_g, v_v_norm_b, v_w_s, v_b_s, v_w_pa, v_w_pb, v_w_o, v_ln1_g, v_ln1_b, v_w_ff1, v_w_ff2, v_ln2_g, v_ln2_b):
    t, d = x.shape[1], x.shape[2]
    dm = _Dims(t, d)
    tm = 256 if t % 256 == 0 else CHUNK
    tm_big = 512 if t % 512 == 0 else tm
    tt = 1024 if t % 1024 == 0 else tm
    me = 4 * lax.axis_index("x") + 2 * lax.axis_index("y") + lax.axis_index("c")
    x2, tgt = x[0], loss_target[0]

    conv_bits = lax.bitcast_convert_type(conv_w[0], BF16).reshape(-1)
    conv_blk = jnp.pad(conv_bits, (0, dm.conv_rows * d - conv_bits.shape[0])).reshape(dm.conv_rows, d)
    w_int, conv_g = _all_gather([w_in[0].T.astype(BF16), conv_blk])
    wa8 = dm.WA // N_DEV
    conv_all = lax.bitcast_convert_type(conv_g.reshape(N_DEV, -1)[:, :3 * wa8 * 2].reshape(N_DEV, 3, wa8, 2), F32)
    conv_full = jnp.transpose(conv_all, (1, 0, 2)).reshape(3, dm.WA)
    conv_w8 = jnp.pad(conv_full, ((0, F32_SUBLANES - 3), (0, 0)))
    ws_b = w_s[0].astype(BF16)
    wst_b = jnp.transpose(w_s[0], (0, 2, 1)).astype(BF16)
    bias_s = jnp.repeat(b_s[0].T, CHUNK, axis=1)
    head_sel = (jnp.arange(d)[:, None] // CHUNK == jnp.arange(CHUNK)[None, :]).astype(BF16)

    p, w_pa_f, w_pb_f, w_o_f, w_1t, w_2 = _proj_in(
        x2, w_int, b_gate, [w_pa[0].astype(BF16), w_pb[0].astype(BF16), w_o[0].astype(BF16), w_ff1[0].T.astype(BF16),
                            w_ff2[0].astype(BF16)], dm, tm_big)
    ya, yb, r1, gu_s, dgu_s, xhv_s, rgv_s, cv_s, s_m = _mixer_fwd(
        p, x2, w_pa_f, w_pb_f, w_o_f, conv_w8, v_norm_g, v_norm_b, ws_b, bias_s, dm, tm)
    relu, x1b, dr2, dr2b, sums2 = _ffn_fwd(r1, tgt, w_1t, w_2, ln1_g, ln1_b, ln2_g, ln2_b, dm, tm_big)
    dr1, dh1, sums1 = _ffn_bwd(r1, relu, dr2, w_2, w_1t, ln1_g, dm, tm_big)
    fb = min(1024, dm.DFF)
    rows = dm.shard_rows
    g_ff1t, g_ff2 = _wgrad([(dh1, x1b, False), (relu, dr2b, True)], tt, fb, "ffn_wgrad")
    dp, a_m, dya, bb_m, dyb, g_ws, g_bs_t, g_cw, g_bg, g_vn, got_ff1t, got_ff2 = _mixer_bwd(
        dr1, p, ya, yb, gu_s, dgu_s, xhv_s, rgv_s, cv_s, w_o_f.T, w_pa_f.T, w_pb_f.T, conv_w8, v_norm_g, v_norm_b, ws_b, wst_b, bias_s, head_sel,
        [(g_ff1t, rows[4]), (g_ff2, rows[5])], dm, tm)
    (g_pa,) = _wgrad([(a_m, dya, False)], tt, dm.WA, "w_pa_grad")
    g_o, g_pb = _wgrad([(s_m, dr1, False), (bb_m, dyb, False)], tt, d, "w_o_pb_grad")
    g_int, got_pa, got_pb, got_o = _wgrad([(dp, x2, False)], tt, 17 * LANES, "w_in_grad",
                                          xchg=[(g_pa, rows[1]), (g_pb, rows[2]), (g_o, rows[3])])
    small_parts = [g_bg[0], g_cw[0:3], g_vn[0], g_vn[1], g_ws, g_bs_t[:, :dm.H].T,
                   sums1[0], sums1[1], sums2[1], sums2[2], sums2[0]]
    grad_x, got_int, got_s = _input_grad(dp, dr1, w_int, [(g_int, rows[0]), (_to_slab(small_parts), None)], dm, tm_big)

    ssum = _sum_slots(got_s, 256, "sum_small")
    (s_bg, s_cw, s_vng, s_vnb, s_ws, s_bs, s_l1g, s_l1b, s_l2g, s_l2b, s_sq) = _from_slab(
        ssum, [(1, 2 * d), (3, dm.WA), (1, d), (1, d), (1, dm.H, CHUNK, CHUNK), (1, dm.H, CHUNK),
               (1, d), (1, d), (1, d), (1, d), (d,)])
    loss = 0.5 * jnp.sum(s_sq) / d
    s_cw = lax.dynamic_slice(s_cw, (0, me * wa8), (3, wa8))[None]
    big_g = [_sum_slots(got_int, 256, "sum_grads_w_in").T, got_pa, got_pb, got_o,
             _sum_slots(got_ff1t, 256, "sum_grads_w_ff1").T, got_ff2]
    big_w = [(w_in, m_w_in, v_w_in), (w_pa, m_w_pa, v_w_pa), (w_pb, m_w_pb, v_w_pb), (w_o, m_w_o, v_w_o),
             (w_ff1, m_w_ff1, v_w_ff1), (w_ff2, m_w_ff2, v_w_ff2)]
    big_out = [_adamw(w, g, m, v, "adamw_%d" % k) for k, ((w, m, v), g) in enumerate(zip(big_w, big_g))]
    small_w = [(b_gate, m_b_gate, v_b_gate), (conv_w, m_conv_w, v_conv_w), (v_norm_g, m_v_norm_g, v_v_norm_g),
               (v_norm_b, m_v_norm_b, v_v_norm_b), (w_s, m_w_s, v_w_s), (b_s, m_b_s, v_b_s), (ln1_g, m_ln1_g, v_ln1_g),
               (ln1_b, m_ln1_b, v_ln1_b), (ln2_g, m_ln2_g, v_ln2_g), (ln2_b, m_ln2_b, v_ln2_b)]
    small_g = [s_bg, s_cw, s_vng, s_vnb, s_ws, s_bs, s_l1g, s_l1b, s_l2g, s_l2b]
    small_out = list(zip(*_adamw_small([w for w, _, _ in small_w], small_g, [m for _, m, _ in small_w],
                                       [v for _, _, v in small_w])))

    order = [("b", 0), ("s", 0), ("s", 1), ("s", 2), ("s", 3), ("s", 4), ("s", 5), ("b", 1), ("b", 2), ("b", 3),
             ("s", 6), ("s", 7), ("b", 4), ("b", 5), ("s", 8), ("s", 9)]
    grads, deltas, new_m, new_v = [], [], [], []
    for kind, k in order:
        if kind == "b":
            g, dl, nm, nv = big_out[k]
        else:
            g, (dl, nm, nv) = small_g[k], small_out[k]
        grads.append(g)
        deltas.append(dl)
        new_m.append(nm)
        new_v.append(nv)
    return (loss, grad_x[None], *grads, *deltas, *new_m, *new_v)
```

```python
import math

import jax
import jax.numpy as jnp
from jax import lax
from jax.experimental import pallas as pl
from jax.experimental.pallas import tpu as pltpu

F32 = jnp.float32
BF16 = jnp.bfloat16
N_DEV = 8
CHUNK = 128
LN_EPS = 1e-5
ALPHA = 2.0 ** 0.25
ADAM_LR, ADAM_B1, ADAM_B2, ADAM_EPS, ADAM_WD, ADAM_STEP = 0.001, 0.9, 0.999, 1e-08, 0.01, 10
F32_SUBLANES = 8
BF16_SUBLANES = 16
LANES = 128
VMEM_LIMIT = 56 * 1024 * 1024
MESH = pl.DeviceIdType.MESH
NT_DIMS = (((1,), (1,)), ((), ()))
TN_DIMS = (((0,), (0,)), ((), ()))
HBM_SPEC = pl.BlockSpec(memory_space=pltpu.HBM)


class _Dims:
    def __init__(self, t, d):
        self.T, self.D = t, d
        self.WA = 3 * d // 2
        self.NP = 3 * self.WA + 4 * d
        self.DFF = 4 * d
        self.H = d // CHUNK
        self.OFF_CA, self.OFF_HA = self.WA, 2 * self.WA
        self.OFF_UB = 3 * self.WA
        self.OFF_VB = self.OFF_UB + d
        self.OFF_GA = self.OFF_VB + d
        self.OFF_GB = self.OFF_GA + d
        self.shard_rows = (self.NP // N_DEV, self.WA // N_DEV, d // N_DEV, d // N_DEV, self.DFF // N_DEV, self.DFF // N_DEV)
        self.conv_rows = BF16_SUBLANES * max(1, -(-(3 * (self.WA // N_DEV) * 2) // (BF16_SUBLANES * d)))


def _params(sem=("arbitrary",), vmem=VMEM_LIMIT):
    return pltpu.CompilerParams(dimension_semantics=sem, vmem_limit_bytes=vmem)


def _mesh_pos():
    return lax.axis_index("x"), lax.axis_index("y"), lax.axis_index("c")


def _resident(shape):
    zeros = (0,) * len(shape)
    return pl.BlockSpec(shape, lambda *_: zeros, pipeline_mode=pl.Buffered(1))


class _TwoLevelGather:
    def __init__(self, shard_refs, out_refs, send_sems, recv_sems, local_sems):
        self.n = len(shard_refs)
        self.shard_refs, self.out_refs = shard_refs, out_refs
        self.send_sems, self.recv_sems, self.local_sems = send_sems, recv_sems, local_sems
        x, y, c = _mesh_pos()
        self.c = c
        self.me, self.sibling = (x, y, c), (x, y, 1 - c)
        self.near = [(1 - x, y), (x, 1 - y)]
        self.far = (1 - x, 1 - y)
        self.relay_from = (jnp.where(c == 0, 1 - x, x), jnp.where(c == 0, y, 1 - y))
        self.relay_to = (jnp.where(c == 0, x, 1 - x), jnp.where(c == 0, 1 - y, y))

    def _slot(self, a, px, py, pc):
        rows = self.shard_refs[a].shape[0]
        return self.out_refs[a].at[pl.ds((4 * px + 2 * py + pc) * rows, rows), :]

    def _copy(self, a, k, block, to, src=None):
        return pltpu.make_async_remote_copy(
            src_ref=self._slot(a, *block) if src is None else src, dst_ref=self._slot(a, *block),
            send_sem=self.send_sems.at[7 * a + k], recv_sem=self.recv_sems.at[7 * a + k], device_id=to, device_id_type=MESH)

    def _mine(self):
        return [pltpu.make_async_copy(self.shard_refs[a], self._slot(a, *self.me), self.local_sems.at[a]) for a in range(self.n)]

    def _first(self):
        out = []
        for a in range(self.n):
            out.append(self._copy(a, 0, self.me, self.sibling, src=self.shard_refs[a]))
            out += [self._copy(a, 1 + j, self.me, (*chip, self.c), src=self.shard_refs[a]) for j, chip in enumerate(self.near)]
        return out

    def _passed(self, a):
        return ([self._copy(a, 3 + j, (*chip, self.c), self.sibling) for j, chip in enumerate(self.near)]
                + [self._copy(a, 6, (*self.relay_from, self.c), (*self.relay_to, self.c))])

    def _far_passed(self, a):
        return self._copy(a, 5, (*self.far, self.c), self.sibling)

    def start(self):
        for cp in self._mine() + self._first():
            cp.start()

    def forward(self):
        for a in range(self.n):
            for j, chip in enumerate(self.near):
                self._copy(a, 1 + j, (*chip, self.c), self.me).wait_recv()
            for cp in self._passed(a):
                cp.start()

    def forward_relayed(self):
        for a in range(self.n):
            self._copy(a, 6, (*self.far, self.c), self.me).wait_recv()
            self._far_passed(a).start()

    def finish(self):
        for a in range(self.n):
            self._copy(a, 0, self.sibling, self.me).wait_recv()
            for j, chip in enumerate(self.near):
                self._copy(a, 3 + j, (*chip, 1 - self.c), self.me).wait_recv()
            self._copy(a, 5, (*self.far, 1 - self.c), self.me).wait_recv()
        for cp in self._first():
            cp.wait_send()
        for a in range(self.n):
            for cp in self._passed(a) + [self._far_passed(a)]:
                cp.wait_send()
        for cp in self._mine():
            cp.wait()

    @staticmethod
    def out_shapes(shards):
        return tuple(jax.ShapeDtypeStruct((N_DEV * s.shape[0], s.shape[1]), s.dtype) for s in shards)

    @staticmethod
    def semaphores(n):
        return [pltpu.SemaphoreType.DMA((7 * n,)), pltpu.SemaphoreType.DMA((7 * n,)), pltpu.SemaphoreType.DMA((n,))]


def _all_gather(shards):
    n = len(shards)

    def body(*refs):
        gather = _TwoLevelGather(refs[:n], refs[n:2 * n], *refs[2 * n:])
        gather.start()
        gather.forward()
        gather.forward_relayed()
        gather.finish()

    return pl.pallas_call(
        body, name="all_gather_w_in", out_shape=_TwoLevelGather.out_shapes(shards),
        in_specs=[HBM_SPEC] * n, out_specs=(HBM_SPEC,) * n, scratch_shapes=_TwoLevelGather.semaphores(n),
    )(*shards)


class _Exchange:
    def __init__(self, src_refs, recv_refs, rows, send_sems, recv_sems, local_sems):
        x, y, c = _mesh_pos()
        me = 4 * x + 2 * y + c
        self.own, self.sends, self.arrivals = [], [], []
        for a, (src, recv) in enumerate(zip(src_refs, recv_refs)):
            def blk(k, src=src, r=rows[a]):
                return src if r is None else src.at[pl.ds(k * r, r), :]

            self.own.append(pltpu.make_async_copy(blk(me), recv.at[me], local_sems.at[a]))
            for rel in range(1, N_DEV):
                px = 1 - x if rel & 4 else x
                py = 1 - y if rel & 2 else y
                pc = 1 - c if rel & 1 else c
                peer = 4 * px + 2 * py + pc
                sem = dict(send_sem=send_sems.at[7 * a + rel - 1], recv_sem=recv_sems.at[7 * a + rel - 1],
                           device_id=(px, py, pc), device_id_type=MESH)
                self.sends.append(pltpu.make_async_remote_copy(src_ref=blk(peer), dst_ref=recv.at[me], **sem))
                self.arrivals.append(pltpu.make_async_remote_copy(src_ref=blk(me), dst_ref=recv.at[peer], **sem))

    def start(self):
        for cp in self.own + self.sends:
            cp.start()

    def finish(self):
        for cp in self.arrivals:
            cp.wait_recv()
        for cp in self.sends:
            cp.wait_send()
        for cp in self.own:
            cp.wait()

    @staticmethod
    def out_shapes(srcs, rows):
        return tuple(jax.ShapeDtypeStruct((N_DEV, s.shape[0] if r is None else r, s.shape[1]), s.dtype) for s, r in zip(srcs, rows))

    @staticmethod
    def semaphores(n):
        return [pltpu.SemaphoreType.DMA((7 * n,)), pltpu.SemaphoreType.DMA((7 * n,)), pltpu.SemaphoreType.DMA((n,))]


def _sum_slots(slots, tile_rows, name):
    _, rows, cols = slots.shape
    tr = rows
    if N_DEV * rows * cols * slots.dtype.itemsize > 8 * 1024 * 1024:
        tr = next(c for c in (256, 192, 128, 64, 32, 16) if c <= tile_rows and rows % c == 0)

    def body(s_ref, o_ref):
        acc = s_ref[0].astype(F32)
        for k in range(1, N_DEV):
            acc = acc + s_ref[k].astype(F32)
        o_ref[...] = acc

    return pl.pallas_call(
        body, name=name, grid=(rows // tr,),
        in_specs=[pl.BlockSpec((N_DEV, tr, cols), lambda i: (0, i, 0))],
        out_specs=pl.BlockSpec((tr, cols), lambda i: (i, 0)),
        out_shape=jax.ShapeDtypeStruct((rows, cols), F32),
        compiler_params=_params(),
    )(slots)


def _gelu_and_grad(x):
    k0 = math.sqrt(2.0 / math.pi)
    k1 = 0.044715
    a = k1 * (x * x)
    half = 1.0 / (1.0 + jnp.exp((-2.0 * k0) * x * (1.0 + a)))
    g = x * half
    return g, half + g * (1.0 - half) * (2.0 * k0 + (6.0 * k0) * a)


def _ln_stats(r):
    mu = jnp.mean(r, axis=-1, keepdims=True)
    rc = r - mu
    var = jnp.mean(rc * rc, axis=-1, keepdims=True)
    rstd = lax.rsqrt(var + LN_EPS)
    return rc * rstd, rstd


def _ln_bwd(dxh, xh, rstd):
    return rstd * (dxh - jnp.mean(dxh, axis=-1, keepdims=True) - xh * jnp.mean(dxh * xh, axis=-1, keepdims=True))


def _colsum(a):
    return jnp.sum(a, axis=0, keepdims=True)


def _mm(a, b):
    return jnp.dot(a, b, preferred_element_type=F32)


def _halo_maps(tm, t, unit):
    per, last = tm // unit, t // unit - 1
    return (lambda i: (jnp.maximum(i * per - 1, 0), 0)), (lambda i: (jnp.minimum((i + 1) * per, last), 0))


def _proj_in(x2, w_int, b_gate, shards, dm, tm):
    t, d, npj = dm.T, dm.D, dm.NP
    cw = d // 2
    nt = t // tm
    n = len(shards)

    def body(x_ref, w_ref, bg_ref, *refs):
        p_ref = refs[n]
        gather = _TwoLevelGather(refs[:n], refs[n + 1:2 * n + 1], *refs[2 * n + 1:])
        i = pl.program_id(0)

        @pl.when(i == 0)
        def _():
            gather.start()

        @pl.when(i == nt // 2)
        def _():
            gather.forward()

        @pl.when(i == 3 * nt // 4)
        def _():
            gather.forward_relayed()

        xb = x_ref[...].astype(BF16)
        for blk in range(npj // cw):
            lo = blk * cw
            acc = lax.dot_general(xb, w_ref[lo:lo + cw, :], NT_DIMS, preferred_element_type=F32)
            if lo >= dm.OFF_GA:
                acc = jax.nn.sigmoid(acc + bg_ref[:, lo - dm.OFF_GA:lo - dm.OFF_GA + cw])
            p_ref[:, lo:lo + cw] = acc.astype(BF16)

        @pl.when(i == nt - 1)
        def _():
            gather.finish()

    return pl.pallas_call(
        body, name="proj_in", grid=(nt,),
        in_specs=[pl.BlockSpec((tm, d), lambda i: (i, 0)), _resident((npj, d)), _resident((1, 2 * d))] + [HBM_SPEC] * n,
        out_specs=(pl.BlockSpec((tm, npj), lambda i: (i, 0)),) + (HBM_SPEC,) * n,
        out_shape=(jax.ShapeDtypeStruct((t, npj), BF16),) + _TwoLevelGather.out_shapes(shards),
        scratch_shapes=_TwoLevelGather.semaphores(n),
        compiler_params=_params(),
    )(x2, w_int, b_gate, *shards)


def _row_neighbours(ext, tm):
    h, n = F32_SUBLANES, ext.shape[0]
    return pltpu.roll(ext, 1, 0)[h:h + tm], pltpu.roll(ext, n - 1, 0)[h:h + tm]


def _spatial_mix(vn, ws_ref, bias_ref, mixed_ref, dm, tm):
    vb = vn.astype(BF16)
    for cc in range(tm // CHUNK):
        r0 = cc * CHUNK
        for h in range(dm.H):
            c0 = h * CHUNK
            m = _mm(ws_ref[h], vb[r0:r0 + CHUNK, c0:c0 + CHUNK])
            mixed_ref[r0:r0 + CHUNK, c0:c0 + CHUNK] = m + bias_ref[:, c0:c0 + CHUNK]
    return mixed_ref[...]


def _mixer_fwd(p, x2, w_pa, w_pb, w_o, conv_w8, vng, vnb, ws_b, bias_s, dm, tm):
    t, d, wa, npj = dm.T, dm.D, dm.WA, dm.NP
    nt = t // tm
    hb = BF16_SUBLANES
    prev_map, next_map = _halo_maps(tm, t, hb)

    def body(p_ref, pp_ref, pn_ref, x_ref, wpa_ref, wpb_ref, wo_ref, cw_ref, vng_ref, vnb_ref, ws_ref, bias_ref,
             ya_ref, yb_ref, r1_ref, gu_ref, dgu_ref, xhv_ref, rgv_ref, cv_ref, s_ref, mixed_ref):
        i = pl.program_id(0)

        def col(ref, lo, width):
            return ref[:, lo:lo + width].astype(F32)

        ch = col(p_ref, dm.OFF_CA, wa) * col(p_ref, dm.OFF_HA, wa)
        chp = (col(pp_ref, dm.OFF_CA, wa) * col(pp_ref, dm.OFF_HA, wa))[hb - F32_SUBLANES:hb]
        chn = (col(pn_ref, dm.OFF_CA, wa) * col(pn_ref, dm.OFF_HA, wa))[0:F32_SUBLANES]
        ch_e = jnp.concatenate([jnp.where(i == 0, 0.0, chp), ch, jnp.where(i == nt - 1, 0.0, chn)], axis=0)
        up, dn = _row_neighbours(ch_e, tm)
        cv = cw_ref[0:1, :] * up + cw_ref[1:2, :] * ch + cw_ref[2:3, :] * dn
        cv_ref[...] = cv.astype(BF16)
        ya = _mm((col(p_ref, 0, wa) * cv).astype(BF16), wpa_ref[...])
        gv, dgelu_v = _gelu_and_grad(col(p_ref, dm.OFF_VB, d))
        xhv, rstdv = _ln_stats(gv)
        xhv_ref[...] = xhv.astype(BF16)
        rgv_ref[...] = (rstdv * dgelu_v).astype(BF16)
        mixed = _spatial_mix(xhv * vng_ref[...] + vnb_ref[...], ws_ref, bias_ref, mixed_ref, dm, tm)
        gu, dgelu_u = _gelu_and_grad(col(p_ref, dm.OFF_UB, d))
        gu_ref[...] = gu.astype(BF16)
        dgu_ref[...] = dgelu_u.astype(BF16)
        yb = _mm((gu * mixed).astype(BF16), wpb_ref[...])
        sb = (col(p_ref, dm.OFF_GA, d) * ya + col(p_ref, dm.OFF_GB, d) * yb).astype(BF16)
        s_ref[...] = sb
        mix = _mm(sb, wo_ref[...])
        ya_ref[...] = ya.astype(BF16)
        yb_ref[...] = yb.astype(BF16)
        r1_ref[...] = ALPHA * x_ref[...] + mix

    tile = lambda i: (i, 0)
    return pl.pallas_call(
        body, name="mixer_fwd", grid=(nt,),
        in_specs=[pl.BlockSpec((tm, npj), tile), pl.BlockSpec((hb, npj), prev_map), pl.BlockSpec((hb, npj), next_map),
                  pl.BlockSpec((tm, d), tile), _resident((wa, d)), _resident((d, d)), _resident((d, d)),
                  _resident((F32_SUBLANES, wa)), _resident((1, d)), _resident((1, d)),
                  _resident((dm.H, CHUNK, CHUNK)), _resident((CHUNK, d))],
        out_specs=(pl.BlockSpec((tm, d), tile),) * 7 + (pl.BlockSpec((tm, wa), tile), pl.BlockSpec((tm, d), tile)),
        out_shape=(jax.ShapeDtypeStruct((t, d), BF16), jax.ShapeDtypeStruct((t, d), BF16), jax.ShapeDtypeStruct((t, d), F32))
        + (jax.ShapeDtypeStruct((t, d), BF16),) * 4 + (jax.ShapeDtypeStruct((t, wa), BF16), jax.ShapeDtypeStruct((t, d), BF16)),
        scratch_shapes=[pltpu.VMEM((tm, d), F32)],
        compiler_params=_params(),
    )(p, p, p, x2, w_pa, w_pb, w_o, conv_w8, vng, vnb, ws_b, bias_s)


HEAD_WINDOW = 2


def _after(value, earlier):
    bits = lax.shift_right_logical(lax.shift_right_logical(pltpu.bitcast(earlier, jnp.uint32), jnp.uint32(16)), jnp.uint32(16))
    return value + pltpu.bitcast(bits, F32)


def _ffn_fwd(r1, tgt, w1t, w2, ln1g, ln1b, ln2g, ln2b, dm, tm):
    t, d, dff = dm.T, dm.D, dm.DFF
    nt = t // tm
    fc = dff // N_DEV
    hr = tm // N_DEV

    def body(r1_ref, tgt_ref, w1t_ref, w2_ref, g1_ref, b1_ref, g2_ref, b2_ref, relu_ref, x1_ref, dr2_ref, dr2b_ref, sums_ref,
             r2_ref):
        i = pl.program_id(0)

        def head(c, live):
            rs = slice(c * hr, (c + 1) * hr)
            xh2, rstd2 = _ln_stats(r2_ref[rs, :])
            diff = xh2 * g2_ref[...] + b2_ref[...] - tgt_ref[rs, :]
            dy = diff * (1.0 / d)
            dr2 = _ln_bwd(dy * g2_ref[...], xh2, rstd2)
            dr2_ref[rs, :] = dr2
            dr2b_ref[rs, :] = dr2.astype(BF16)
            sums_ref[0:1, :] += live * _colsum(diff * diff)
            sums_ref[1:2, :] += live * _colsum(dy * xh2)
            sums_ref[2:3, :] += live * _colsum(dy)
            return dr2

        @pl.when(i == 0)
        def _():
            sums_ref[...] = jnp.zeros_like(sums_ref)
            r2_ref[...] = jnp.zeros_like(r2_ref)

        @pl.when(i < nt)
        def _():
            live = (i > 0).astype(F32)
            xh1, _ = _ln_stats(r1_ref[...])
            x1 = xh1 * g1_ref[...] + b1_ref[...]
            x1b = x1.astype(BF16)
            x1_ref[...] = x1b
            ffn = jnp.zeros((tm, d), F32)
            dones = []
            for k in range(N_DEV):
                ks = slice(k * fc, (k + 1) * fc)
                r = jnp.maximum(lax.dot_general(x1b, w1t_ref[ks, :], NT_DIMS, preferred_element_type=F32), 0.0)
                if k >= HEAD_WINDOW:
                    r = jnp.concatenate([_after(r[:hr, :], dones[k - HEAD_WINDOW][:, :fc]), r[hr:]], axis=0)
                relu_ref[:, ks] = r.astype(BF16)
                ffn = ffn + _mm((r * r).astype(BF16), w2_ref[ks, :])
                dones.append(head(k, live))
            r2_ref[...] = ALPHA * x1 + ffn

        @pl.when(i == nt)
        def _():
            for c in range(N_DEV):
                head(c, 1.0)

    tile = lambda i: (jnp.minimum(i, nt - 1), 0)
    late = lambda i: (jnp.maximum(i - 1, 0), 0)
    vec = _resident((1, d))
    return pl.pallas_call(
        body, name="ffn_fwd", grid=(nt + 1,),
        in_specs=[pl.BlockSpec((tm, d), tile), pl.BlockSpec((tm, d), late), _resident((dff, d)), _resident((dff, d)),
                  vec, vec, vec, vec],
        out_specs=(pl.BlockSpec((tm, dff), tile), pl.BlockSpec((tm, d), tile), pl.BlockSpec((tm, d), late),
                   pl.BlockSpec((tm, d), late), pl.BlockSpec((F32_SUBLANES, d), lambda i: (0, 0))),
        out_shape=(jax.ShapeDtypeStruct((t, dff), BF16), jax.ShapeDtypeStruct((t, d), BF16), jax.ShapeDtypeStruct((t, d), F32),
                   jax.ShapeDtypeStruct((t, d), BF16), jax.ShapeDtypeStruct((F32_SUBLANES, d), F32)),
        scratch_shapes=[pltpu.VMEM((tm, d), F32)],
        compiler_params=_params(),
    )(r1, tgt, w1t, w2, ln1g, ln1b, ln2g, ln2b)


def _ffn_bwd(r1, relu, dr2, w2, w1t, ln1g, dm, tm):
    t, d, dff = dm.T, dm.D, dm.DFF
    fc = dff // N_DEV

    def body(r1_ref, relu_ref, dr2_ref, w2_ref, w1t_ref, g1_ref, dr1_ref, dh_ref, sums_ref):
        @pl.when(pl.program_id(0) == 0)
        def _():
            sums_ref[...] = jnp.zeros_like(sums_ref)

        xh1, rstd1 = _ln_stats(r1_ref[...])
        dr2 = dr2_ref[...]
        dr2b = dr2.astype(BF16)
        dx1 = ALPHA * dr2
        for k in range(dff // fc):
            ks = slice(k * fc, (k + 1) * fc)
            dact = lax.dot_general(dr2b, w2_ref[ks, :], NT_DIMS, preferred_element_type=F32)
            dhb = (dact * (2.0 * relu_ref[:, ks].astype(F32))).astype(BF16)
            dh_ref[:, ks] = dhb
            dx1 = dx1 + _mm(dhb, w1t_ref[ks, :])
        dr1_ref[...] = _ln_bwd(dx1 * g1_ref[...], xh1, rstd1)
        sums_ref[0:1, :] += _colsum(dx1 * xh1)
        sums_ref[1:2, :] += _colsum(dx1)

    tile = lambda i: (i, 0)
    return pl.pallas_call(
        body, name="ffn_bwd", grid=(t // tm,),
        in_specs=[pl.BlockSpec((tm, d), tile), pl.BlockSpec((tm, dff), tile), pl.BlockSpec((tm, d), tile),
                  _resident((dff, d)), _resident((dff, d)), _resident((1, d))],
        out_specs=(pl.BlockSpec((tm, d), tile), pl.BlockSpec((tm, dff), tile), pl.BlockSpec((F32_SUBLANES, d), lambda i: (0, 0))),
        out_shape=(jax.ShapeDtypeStruct((t, d), F32), jax.ShapeDtypeStruct((t, dff), BF16),
                   jax.ShapeDtypeStruct((F32_SUBLANES, d), F32)),
        compiler_params=_params(),
    )(r1, relu, dr2, w2, w1t, ln1g)


def _wgrad(pairs, tt, fb, name, xchg=()):
    n, m = len(pairs), len(xchg)
    t, f = pairs[0][0].shape
    d = pairs[0][1].shape[1]
    squares = [sq for _, _, sq in pairs]
    nj, ni = f // fb, t // tt
    xrows = [r for _, r in xchg]

    def body(*refs):
        ins, xsrc = refs[:2 * n], refs[2 * n:2 * n + m]
        outs, xrecv = refs[2 * n + m:3 * n + m], refs[3 * n + m:3 * n + 2 * m]
        accs, sems = refs[3 * n + 2 * m:4 * n + 2 * m], refs[4 * n + 2 * m:]
        j, i = pl.program_id(0), pl.program_id(1)
        exchange = _Exchange(xsrc, xrecv, xrows, *sems) if m else None

        if m:
            @pl.when(jnp.logical_and(j == 0, i == 0))
            def _():
                exchange.start()

        @pl.when(i == 0)
        def _():
            for acc in accs:
                acc[...] = jnp.zeros_like(acc)

        for q in range(n):
            lhs = ins[2 * q][...]
            if squares[q]:
                lf = lhs.astype(F32)
                lhs = (lf * lf).astype(BF16)
            accs[q][...] += lax.dot_general(lhs, ins[2 * q + 1][...].astype(BF16), TN_DIMS, preferred_element_type=F32)

        @pl.when(i == ni - 1)
        def _():
            for q in range(n):
                outs[q][...] = accs[q][...].astype(BF16)

        if m:
            @pl.when(jnp.logical_and(j == nj - 1, i == ni - 1))
            def _():
                exchange.finish()

    lhs_spec = pl.BlockSpec((tt, fb), lambda j, i: (i, j))
    rhs_spec = pl.BlockSpec((tt, d), lambda j, i: (i, 0))
    out_spec = pl.BlockSpec((fb, d), lambda j, i: (j, 0))
    xsrcs = [a_ for a_, _ in xchg]
    return pl.pallas_call(
        body, name=name, grid=(nj, ni),
        in_specs=[lhs_spec, rhs_spec] * n + [HBM_SPEC] * m, out_specs=(out_spec,) * n + (HBM_SPEC,) * m,
        out_shape=(jax.ShapeDtypeStruct((f, d), BF16),) * n + _Exchange.out_shapes(xsrcs, xrows),
        scratch_shapes=[pltpu.VMEM((fb, d), F32)] * n + (_Exchange.semaphores(m) if m else []),
        compiler_params=_params(("arbitrary", "arbitrary")),
    )(*[a_ for lhs, rhs, _ in pairs for a_ in (lhs, rhs)], *xsrcs)


def _mixer_bwd(dr1, p, ya, yb, gu_s, dgu_s, xhv_s, rgv_s, cv_s, w_ot, w_pat, w_pbt, conv_w8, vng, vnb, ws_b, wst_b, bias_s, head_sel, xchg, dm, tm):
    t, d, wa, npj = dm.T, dm.D, dm.WA, dm.NP
    nt = t // tm
    h8, hb = F32_SUBLANES, BF16_SUBLANES
    ext = tm + 2 * h8
    prev_f, next_f = _halo_maps(tm, t, h8)
    prev_b, next_b = _halo_maps(tm, t, hb)
    nx = len(xchg)
    xsrcs, xrows = [a_ for a_, _ in xchg], [r for _, r in xchg]

    gw = d // 2

    def gate_map(rows, k):
        return lambda i: (rows(i)[0], dm.OFF_GA // gw + k)

    def body(dr_ref, drp_ref, drn_ref, pc_ref, pg0_ref, pg1_ref, pg2_ref, pg3_ref, ppc_ref, ppg0_ref, ppg1_ref,
             pnc_ref, png0_ref, png1_ref, ya_ref, yb_ref, gu_ref, dgu_ref, xhv_ref, rgv_ref, cv_ref,
             wot_ref, wpat_ref, wpbt_ref,
             cw_ref, vng_ref, vnb_ref, ws_ref, wst_ref, bias_ref, sel_ref,
             *refs):
        xsrc = refs[:nx]
        dp_ref, a_ref, dya_ref, bb_ref, dyb_ref, dws_ref, dbs_ref, dcw_ref, dbg_ref, dvn_ref = refs[nx:nx + 10]
        xrecv = refs[nx + 10:2 * nx + 10]
        mixed_ref, dvnm_ref = refs[2 * nx + 10:2 * nx + 12]
        exchange = _Exchange(xsrc, xrecv, xrows, *refs[2 * nx + 12:])
        i = pl.program_id(0)

        @pl.when(i == 0)
        def _():
            exchange.start()
            for ref in (dws_ref, dbs_ref, dcw_ref, dbg_ref, dvn_ref):
                ref[...] = jnp.zeros_like(ref)

        def col(ref, lo, width):
            return ref[:, lo:lo + width].astype(F32)

        def ext_rows(prev_blk, center, next_blk):
            return jnp.concatenate([prev_blk, center, next_blk], axis=0)

        def lanes(*refs):
            return jnp.concatenate([r[...].astype(F32) for r in refs], axis=1)

        xhv = xhv_ref[...].astype(F32)
        vn = xhv * vng_ref[...] + vnb_ref[...]
        mixed = _spatial_mix(vn, ws_ref, bias_ref, mixed_ref, dm, tm)
        gu = gu_ref[...].astype(F32)
        bb_ref[...] = (gu * mixed).astype(BF16)
        g_a, g_b = lanes(pg0_ref, pg1_ref), lanes(pg2_ref, pg3_ref)
        y_a, y_b = ya_ref[...].astype(F32), yb_ref[...].astype(F32)
        qa = y_a * g_a * (1.0 - g_a)
        qb = y_b * g_b * (1.0 - g_b)
        cv = cv_ref[...].astype(F32)
        b_a, c_a, h_a = col(pc_ref, 0, wa), col(pc_ref, dm.OFF_CA, wa), col(pc_ref, dm.OFF_HA, wa)
        a_ref[...] = (b_a * cv).astype(BF16)
        ch = c_a * h_a
        row = lax.broadcasted_iota(jnp.int32, (ext, 1), 0) + (i * tm - h8)
        inside = jnp.logical_and(row >= 0, row < t)
        dr_e = ext_rows(drp_ref[...], dr_ref[...], drn_ref[...])
        ds_e = _mm(dr_e.astype(BF16), wot_ref[...])
        dya_e = ds_e * ext_rows(lanes(ppg0_ref, ppg1_ref)[hb - h8:hb], g_a, lanes(png0_ref, png1_ref)[0:h8])
        da_e = _mm(dya_e.astype(BF16), wpat_ref[...])
        dcv_e = jnp.where(inside, da_e * ext_rows(col(ppc_ref, 0, wa)[hb - h8:hb], b_a, col(pnc_ref, 0, wa)[0:h8]), 0.0)
        dcv, (dcv_up, dcv_dn) = dcv_e[h8:h8 + tm], _row_neighbours(dcv_e, tm)
        w0, w1, w2 = cw_ref[0:1, :], cw_ref[1:2, :], cw_ref[2:3, :]
        dp_ref[:, 0:wa] = (da_e[h8:h8 + tm] * cv).astype(BF16)
        dch = w0 * dcv_dn + w1 * dcv + w2 * dcv_up
        dp_ref[:, dm.OFF_CA:dm.OFF_CA + wa] = (dch * h_a).astype(BF16)
        dp_ref[:, dm.OFF_HA:dm.OFF_HA + wa] = (dch * c_a).astype(BF16)
        dcw_ref[0:1, :] += _colsum(dcv_dn * ch)
        dcw_ref[1:2, :] += _colsum(dcv * ch)
        dcw_ref[2:3, :] += _colsum(dcv_up * ch)
        dya_ref[...] = dya_e[h8:h8 + tm].astype(BF16)
        ds = ds_e[h8:h8 + tm]
        dzga = ds * qa
        dzgb = ds * qb
        dp_ref[:, dm.OFF_GA:dm.OFF_GA + d] = dzga.astype(BF16)
        dp_ref[:, dm.OFF_GB:dm.OFF_GB + d] = dzgb.astype(BF16)
        dbg_ref[0:1, 0:d] += _colsum(dzga)
        dbg_ref[0:1, d:2 * d] += _colsum(dzgb)
        dyb = (ds * g_b).astype(BF16)
        dyb_ref[...] = dyb
        dbb = _mm(dyb, wpbt_ref[...])
        dp_ref[:, dm.OFF_UB:dm.OFF_UB + d] = (dbb * mixed * dgu_ref[...].astype(F32)).astype(BF16)
        dmb = (dbb * gu).astype(BF16)
        vb = vn.astype(BF16)
        dbs = jnp.zeros((CHUNK, CHUNK), F32)
        for cc in range(tm // CHUNK):
            r0 = cc * CHUNK
            dbs = dbs + _mm(dmb[r0:r0 + CHUNK, :], sel_ref[...])
            for h in range(dm.H):
                c0 = h * CHUNK
                blk = dmb[r0:r0 + CHUNK, c0:c0 + CHUNK]
                dvnm_ref[r0:r0 + CHUNK, c0:c0 + CHUNK] = _mm(wst_ref[h], blk)
                dws_ref[h] += lax.dot_general(blk, vb[r0:r0 + CHUNK, c0:c0 + CHUNK], NT_DIMS, preferred_element_type=F32)
        dbs_ref[...] += dbs
        dvn = dvnm_ref[...]
        dvn_ref[0:1, :] += _colsum(dvn * xhv)
        dvn_ref[1:2, :] += _colsum(dvn)
        dp_ref[:, dm.OFF_VB:dm.OFF_VB + d] = (_ln_bwd(dvn * vng_ref[...], xhv, rgv_ref[...].astype(F32))).astype(BF16)

        @pl.when(i == nt - 1)
        def _():
            exchange.finish()

    full = lambda i: (0, 0)
    tile = lambda i: (i, 0)
    hcc = _resident((dm.H, CHUNK, CHUNK))
    tok = lambda w, dt: jax.ShapeDtypeStruct((t, w), dt)
    return pl.pallas_call(
        body, name="mixer_bwd", grid=(nt,),
        in_specs=[pl.BlockSpec((tm, d), tile), pl.BlockSpec((h8, d), prev_f), pl.BlockSpec((h8, d), next_f),
                  pl.BlockSpec((tm, 3 * wa), tile)] + [pl.BlockSpec((tm, gw), gate_map(tile, k)) for k in range(4)]
        + [pl.BlockSpec((hb, wa), prev_b)] + [pl.BlockSpec((hb, gw), gate_map(prev_b, k)) for k in range(2)]
        + [pl.BlockSpec((hb, wa), next_b)] + [pl.BlockSpec((hb, gw), gate_map(next_b, k)) for k in range(2)]
        + [pl.BlockSpec((tm, d), tile)] * 6
        + [pl.BlockSpec((tm, wa), tile), _resident((d, d)), _resident((d, wa)), _resident((d, d)),
           _resident((h8, wa)), _resident((1, d)), _resident((1, d)), hcc, hcc, _resident((CHUNK, d)),
           _resident((d, CHUNK))] + [HBM_SPEC] * nx,
        out_specs=(pl.BlockSpec((tm, npj), tile), pl.BlockSpec((tm, wa), tile), pl.BlockSpec((tm, d), tile),
                   pl.BlockSpec((tm, d), tile), pl.BlockSpec((tm, d), tile),
                   pl.BlockSpec((dm.H, CHUNK, CHUNK), lambda i: (0, 0, 0)), pl.BlockSpec((CHUNK, CHUNK), full),
                   pl.BlockSpec((h8, wa), full), pl.BlockSpec((h8, 2 * d), full), pl.BlockSpec((h8, d), full))
        + (HBM_SPEC,) * nx,
        out_shape=(tok(npj, BF16), tok(wa, BF16), tok(d, BF16), tok(d, BF16), tok(d, BF16),
                   jax.ShapeDtypeStruct((dm.H, CHUNK, CHUNK), F32), jax.ShapeDtypeStruct((CHUNK, CHUNK), F32),
                   jax.ShapeDtypeStruct((h8, wa), F32), jax.ShapeDtypeStruct((h8, 2 * d), F32),
                   jax.ShapeDtypeStruct((h8, d), F32)) + _Exchange.out_shapes(xsrcs, xrows),
        scratch_shapes=[pltpu.VMEM((tm, d), F32), pltpu.VMEM((tm, d), F32)] + _Exchange.semaphores(nx),
        compiler_params=_params(),
    )(dr1, dr1, dr1, *([p] * 11), ya, yb, gu_s, dgu_s, xhv_s, rgv_s, cv_s, w_ot, w_pat, w_pbt, conv_w8, vng, vnb, ws_b, wst_b, bias_s, head_sel, *xsrcs)


def _input_grad(dp, dr1, w_int, xchg, dm, tm):
    t, d, npj = dm.T, dm.D, dm.NP
    nt = t // tm
    nx = len(xchg)
    xsrcs, xrows = [a_ for a_, _ in xchg], [r for _, r in xchg]

    def body(dp_ref, dr_ref, w_ref, *refs):
        dx_ref = refs[nx]
        exchange = _Exchange(refs[:nx], refs[nx + 1:2 * nx + 1], xrows, *refs[2 * nx + 1:])
        i = pl.program_id(0)

        @pl.when(i == 0)
        def _():
            exchange.start()

        dx_ref[...] = ALPHA * dr_ref[...] + _mm(dp_ref[...], w_ref[...])

        @pl.when(i == nt - 1)
        def _():
            exchange.finish()

    return pl.pallas_call(
        body, name="input_grad", grid=(nt,),
        in_specs=[pl.BlockSpec((tm, npj), lambda i: (i, 0)), pl.BlockSpec((tm, d), lambda i: (i, 0)), _resident((npj, d))]
        + [HBM_SPEC] * nx,
        out_specs=(pl.BlockSpec((tm, d), lambda i: (i, 0)),) + (HBM_SPEC,) * nx,
        out_shape=(jax.ShapeDtypeStruct((t, d), F32),) + _Exchange.out_shapes(xsrcs, xrows),
        scratch_shapes=_Exchange.semaphores(nx),
        compiler_params=_params(),
    )(dp, dr1, w_int, *xsrcs)


def _adamw_math(w, g, m, v):
    nm = ADAM_B1 * m + (1.0 - ADAM_B1) * g
    nv = ADAM_B2 * v + (1.0 - ADAM_B2) * (g * g)
    delta = -ADAM_LR * ((nm / (1.0 - ADAM_B1 ** ADAM_STEP)) / (jnp.sqrt(nv / (1.0 - ADAM_B2 ** ADAM_STEP)) + ADAM_EPS) + ADAM_WD * w)
    return delta, nm, nv


def _adamw(w, g, m, v, name):
    _, rows, cols = w.shape
    tr = 256 if rows % 256 == 0 else rows
    from_slots = g.ndim == 3

    def body(w_ref, g_ref, m_ref, v_ref, go_ref, d_ref, nm_ref, nv_ref):
        if from_slots:
            g_ = g_ref[0].astype(F32)
            for k in range(1, N_DEV):
                g_ = g_ + g_ref[k].astype(F32)
        else:
            g_ = g_ref[...]
        go_ref[0] = g_
        d_ref[0], nm_ref[0], nv_ref[0] = _adamw_math(w_ref[0], g_, m_ref[0], v_ref[0])

    spec = pl.BlockSpec((1, tr, cols), lambda i: (0, i, 0))
    g_spec = pl.BlockSpec((N_DEV, tr, cols), lambda i: (0, i, 0)) if from_slots else pl.BlockSpec((tr, cols), lambda i: (i, 0))
    shp = jax.ShapeDtypeStruct((1, rows, cols), F32)
    return pl.pallas_call(
        body, name=name, grid=(rows // tr,), in_specs=[spec, g_spec, spec, spec], out_specs=(spec,) * 4, out_shape=(shp,) * 4,
        compiler_params=_params(),
    )(w, g, m, v)


def _adamw_small(ws, gs, ms, vs):
    n = len(ws)

    def body(*refs):
        ins, outs = refs[:4 * n], refs[4 * n:]
        for k in range(n):
            w_ref, g_ref, m_ref, v_ref = ins[k], ins[n + k], ins[2 * n + k], ins[3 * n + k]
            outs[k][...], outs[n + k][...], outs[2 * n + k][...] = _adamw_math(w_ref[...], g_ref[...], m_ref[...], v_ref[...])

    shapes = tuple(jax.ShapeDtypeStruct(w.shape, F32) for w in ws)
    out = pl.pallas_call(body, name="adamw_small", out_shape=shapes * 3, compiler_params=_params(()))(*ws, *gs, *ms, *vs)
    return out[:n], out[n:2 * n], out[2 * n:]


def _to_slab(parts):
    flat = jnp.concatenate([q.reshape(-1) for q in parts])
    pad = (-flat.shape[0]) % (F32_SUBLANES * LANES)
    return jnp.pad(flat, (0, pad)).reshape(-1, LANES)


def _from_slab(slab, shapes):
    flat = slab.reshape(-1)
    out, off = [], 0
    for s in shapes:
        n = math.prod(s)
        out.append(flat[off:off + n].reshape(s))
        off += n
    return out


def kernel(x, w_in, b_gate, conv_w, v_norm_g, v_norm_b, w_s, b_s, w_pa, w_pb, w_o, ln1_g, ln1_b, w_ff1, w_ff2, ln2_g, ln2_b, loss_target, m_w_in, m_b_gate, m_conv_w, m_v_norm_g, m_v_norm_b, m_w_s, m_b_s, m_w_pa, m_w_pb, m_w_o, m_ln1_g, m_ln1_b, m_w_ff1, m_w_ff2, m_ln2_g, m_ln2_b, v_w_in, v_b_gate, v_conv_w, v_v_norm_g, v_v_norm_b, v_w_s, v_b_s, v_w_pa, v_w_pb, v_w_o, v_ln1_g, v_ln1_b, v_w_ff1, v_w_ff2, v_ln2_g, v_ln2_b):
    t, d = x.shape[1], x.shape[2]
    dm = _Dims(t, d)
    tm = 256 if t % 256 == 0 else CHUNK
    tm_big = 512 if t % 512 == 0 else tm
    tt = 1024 if t % 1024 == 0 else tm
    me = 4 * lax.axis_index("x") + 2 * lax.axis_index("y") + lax.axis_index("c")
    x2, tgt = x[0], loss_target[0]

    conv_bits = lax.bitcast_convert_type(conv_w[0], BF16).reshape(-1)
    conv_blk = jnp.pad(conv_bits, (0, dm.conv_rows * d - conv_bits.shape[0])).reshape(dm.conv_rows, d)
    w_int, conv_g = _all_gather([w_in[0].T.astype(BF16), conv_blk])
    wa8 = dm.WA // N_DEV
    conv_all = lax.bitcast_convert_type(conv_g.reshape(N_DEV, -1)[:, :3 * wa8 * 2].reshape(N_DEV, 3, wa8, 2), F32)
    conv_full = jnp.transpose(conv_all, (1, 0, 2)).reshape(3, dm.WA)
    conv_w8 = jnp.pad(conv_full, ((0, F32_SUBLANES - 3), (0, 0)))
    ws_b = w_s[0].astype(BF16)
    wst_b = jnp.transpose(w_s[0], (0, 2, 1)).astype(BF16)
    bias_s = jnp.repeat(b_s[0].T, CHUNK, axis=1)
    head_sel = (jnp.arange(d)[:, None] // CHUNK == jnp.arange(CHUNK)[None, :]).astype(BF16)

    p, w_pa_f, w_pb_f, w_o_f, w_1t, w_2 = _proj_in(
        x2, w_int, b_gate, [w_pa[0].astype(BF16), w_pb[0].astype(BF16), w_o[0].astype(BF16), w_ff1[0].T.astype(BF16),
                            w_ff2[0].astype(BF16)], dm, tm_big)
    ya, yb, r1, gu_s, dgu_s, xhv_s, rgv_s, cv_s, s_m = _mixer_fwd(
        p, x2, w_pa_f, w_pb_f, w_o_f, conv_w8, v_norm_g, v_norm_b, ws_b, bias_s, dm, tm)
    relu, x1b, dr2, dr2b, sums2 = _ffn_fwd(r1, tgt, w_1t, w_2, ln1_g, ln1_b, ln2_g, ln2_b, dm, tm_big)
    dr1, dh1, sums1 = _ffn_bwd(r1, relu, dr2, w_2, w_1t, ln1_g, dm, tm_big)
    fb = min(1024, dm.DFF)
    rows = dm.shard_rows
    g_ff1t, g_ff2 = _wgrad([(dh1, x1b, False), (relu, dr2b, True)], tt, fb, "ffn_wgrad")
    dp, a_m, dya, bb_m, dyb, g_ws, g_bs_t, g_cw, g_bg, g_vn, got_ff1t, got_ff2 = _mixer_bwd(
        dr1, p, ya, yb, gu_s, dgu_s, xhv_s, rgv_s, cv_s, w_o_f.T, w_pa_f.T, w_pb_f.T, conv_w8, v_norm_g, v_norm_b, ws_b, wst_b, bias_s, head_sel,
        [(g_ff1t, rows[4]), (g_ff2, rows[5])], dm, tm)
    (g_pa,) = _wgrad([(a_m, dya, False)], tt, dm.WA, "w_pa_grad")
    g_o, g_pb = _wgrad([(s_m, dr1, False), (bb_m, dyb, False)], tt, d, "w_o_pb_grad")
    g_int, got_pa, got_pb, got_o = _wgrad([(dp, x2, False)], tt, 17 * LANES, "w_in_grad",
                                          xchg=[(g_pa, rows[1]), (g_pb, rows[2]), (g_o, rows[3])])
    small_parts = [g_bg[0], g_cw[0:3], g_vn[0], g_vn[1], g_ws, g_bs_t[:, :dm.H].T,
                   sums1[0], sums1[1], sums2[1], sums2[2], sums2[0]]
    grad_x, got_int, got_s = _input_grad(dp, dr1, w_int, [(g_int, rows[0]), (_to_slab(small_parts), None)], dm, tm_big)

    ssum = _sum_slots(got_s, 256, "sum_small")
    (s_bg, s_cw, s_vng, s_vnb, s_ws, s_bs, s_l1g, s_l1b, s_l2g, s_l2b, s_sq) = _from_slab(
        ssum, [(1, 2 * d), (3, dm.WA), (1, d), (1, d), (1, dm.H, CHUNK, CHUNK), (1, dm.H, CHUNK),
               (1, d), (1, d), (1, d), (1, d), (d,)])
    loss = 0.5 * jnp.sum(s_sq) / d
    s_cw = lax.dynamic_slice(s_cw, (0, me * wa8), (3, wa8))[None]
    big_g = [_sum_slots(got_int, 256, "sum_grads_w_in").T, got_pa, got_pb, got_o,
             _sum_slots(got_ff1t, 256, "sum_grads_w_ff1").T, got_ff2]
    big_w = [(w_in, m_w_in, v_w_in), (w_pa, m_w_pa, v_w_pa), (w_pb, m_w_pb, v_w_pb), (w_o, m_w_o, v_w_o),
             (w_ff1, m_w_ff1, v_w_ff1), (w_ff2, m_w_ff2, v_w_ff2)]
    big_out = [_adamw(w, g, m, v, "adamw_%d" % k) for k, ((w, m, v), g) in enumerate(zip(big_w, big_g))]
    small_w = [(b_gate, m_b_gate, v_b_gate), (conv_w, m_conv_w, v_conv_w), (v_norm_g, m_v_norm_g, v_v_norm_g),
               (v_norm_b, m_v_norm_b, v_v_norm_b), (w_s, m_w_s, v_w_s), (b_s, m_b_s, v_b_s), (ln1_g, m_ln1_g, v_ln1_g),
               (ln1_b, m_ln1_b, v_ln1_b), (ln2_g, m_ln2_g, v_ln2_g), (ln2_b, m_ln2_b, v_ln2_b)]
    small_g = [s_bg, s_cw, s_vng, s_vnb, s_ws, s_bs, s_l1g, s_l1b, s_l2g, s_l2b]
    small_out = list(zip(*_adamw_small([w for w, _, _ in small_w], small_g, [m for _, m, _ in small_w],
                                       [v for _, _, v in small_w])))

    order = [("b", 0), ("s", 0), ("s", 1), ("s", 2), ("s", 3), ("s", 4), ("s", 5), ("b", 1), ("b", 2), ("b", 3),
             ("s", 6), ("s", 7), ("b", 4), ("b", 5), ("s", 8), ("s", 9)]
    grads, deltas, new_m, new_v = [], [], [], []
    for kind, k in order:
        if kind == "b":
            g, dl, nm, nv = big_out[k]
        else:
            g, (dl, nm, nv) = small_g[k], small_out[k]
        grads.append(g)
        deltas.append(dl)
        new_m.append(nm)
        new_v.append(nv)
    return (loss, grad_x[None], *grads, *deltas, *new_m, *new_v)
```

```python
import math

import jax
import jax.numpy as jnp
from jax import lax
from jax.experimental import pallas as pl
from jax.experimental.pallas import tpu as pltpu

F32 = jnp.float32
BF16 = jnp.bfloat16
N_DEV = 8
CHUNK = 128
LN_EPS = 1e-5
ALPHA = 2.0 ** 0.25
ADAM_LR, ADAM_B1, ADAM_B2, ADAM_EPS, ADAM_WD, ADAM_STEP = 0.001, 0.9, 0.999, 1e-08, 0.01, 10
F32_SUBLANES = 8
BF16_SUBLANES = 16
LANES = 128
VMEM_LIMIT = 56 * 1024 * 1024
MESH = pl.DeviceIdType.MESH
NT_DIMS = (((1,), (1,)), ((), ()))
TN_DIMS = (((0,), (0,)), ((), ()))
HBM_SPEC = pl.BlockSpec(memory_space=pltpu.HBM)


class _Dims:
    def __init__(self, t, d):
        self.T, self.D = t, d
        self.WA = 3 * d // 2
        self.NP = 3 * self.WA + 4 * d
        self.DFF = 4 * d
        self.H = d // CHUNK
        self.OFF_CA, self.OFF_HA = self.WA, 2 * self.WA
        self.OFF_UB = 3 * self.WA
        self.OFF_VB = self.OFF_UB + d
        self.OFF_GA = self.OFF_VB + d
        self.OFF_GB = self.OFF_GA + d
        self.shard_rows = (self.NP // N_DEV, self.WA // N_DEV, d // N_DEV, d // N_DEV, self.DFF // N_DEV, self.DFF // N_DEV)
        self.conv_rows = BF16_SUBLANES * max(1, -(-(3 * (self.WA // N_DEV) * 2) // (BF16_SUBLANES * d)))


def _params(sem=("arbitrary",), vmem=VMEM_LIMIT):
    return pltpu.CompilerParams(dimension_semantics=sem, vmem_limit_bytes=vmem)


def _mesh_pos():
    return lax.axis_index("x"), lax.axis_index("y"), lax.axis_index("c")


def _resident(shape):
    zeros = (0,) * len(shape)
    return pl.BlockSpec(shape, lambda *_: zeros, pipeline_mode=pl.Buffered(1))


class _TwoLevelGather:
    def __init__(self, shard_refs, out_refs, send_sems, recv_sems, local_sems):
        self.n = len(shard_refs)
        self.shard_refs, self.out_refs = shard_refs, out_refs
        self.send_sems, self.recv_sems, self.local_sems = send_sems, recv_sems, local_sems
        x, y, c = _mesh_pos()
        self.c = c
        self.me, self.sibling = (x, y, c), (x, y, 1 - c)
        self.near = [(1 - x, y), (x, 1 - y)]
        self.far = (1 - x, 1 - y)
        self.relay_from = (jnp.where(c == 0, 1 - x, x), jnp.where(c == 0, y, 1 - y))
        self.relay_to = (jnp.where(c == 0, x, 1 - x), jnp.where(c == 0, 1 - y, y))

    def _slot(self, a, px, py, pc):
        rows = self.shard_refs[a].shape[0]
        return self.out_refs[a].at[pl.ds((4 * px + 2 * py + pc) * rows, rows), :]

    def _copy(self, a, k, block, to, src=None):
        return pltpu.make_async_remote_copy(
            src_ref=self._slot(a, *block) if src is None else src, dst_ref=self._slot(a, *block),
            send_sem=self.send_sems.at[7 * a + k], recv_sem=self.recv_sems.at[7 * a + k], device_id=to, device_id_type=MESH)

    def _mine(self):
        return [pltpu.make_async_copy(self.shard_refs[a], self._slot(a, *self.me), self.local_sems.at[a]) for a in range(self.n)]

    def _first(self):
        out = []
        for a in range(self.n):
            out.append(self._copy(a, 0, self.me, self.sibling, src=self.shard_refs[a]))
            out += [self._copy(a, 1 + j, self.me, (*chip, self.c), src=self.shard_refs[a]) for j, chip in enumerate(self.near)]
        return out

    def _passed(self, a):
        return ([self._copy(a, 3 + j, (*chip, self.c), self.sibling) for j, chip in enumerate(self.near)]
                + [self._copy(a, 6, (*self.relay_from, self.c), (*self.relay_to, self.c))])

    def _far_passed(self, a):
        return self._copy(a, 5, (*self.far, self.c), self.sibling)

    def start(self):
        for cp in self._mine() + self._first():
            cp.start()

    def forward(self):
        for a in range(self.n):
            for j, chip in enumerate(self.near):
                self._copy(a, 1 + j, (*chip, self.c), self.me).wait_recv()
            for cp in self._passed(a):
                cp.start()

    def forward_relayed(self):
        for a in range(self.n):
            self._copy(a, 6, (*self.far, self.c), self.me).wait_recv()
            self._far_passed(a).start()

    def finish(self):
        for a in range(self.n):
            self._copy(a, 0, self.sibling, self.me).wait_recv()
            for j, chip in enumerate(self.near):
                self._copy(a, 3 + j, (*chip, 1 - self.c), self.me).wait_recv()
            self._copy(a, 5, (*self.far, 1 - self.c), self.me).wait_recv()
        for cp in self._first():
            cp.wait_send()
        for a in range(self.n):
            for cp in self._passed(a) + [self._far_passed(a)]:
                cp.wait_send()
        for cp in self._mine():
            cp.wait()

    @staticmethod
    def out_shapes(shards):
        return tuple(jax.ShapeDtypeStruct((N_DEV * s.shape[0], s.shape[1]), s.dtype) for s in shards)

    @staticmethod
    def semaphores(n):
        return [pltpu.SemaphoreType.DMA((7 * n,)), pltpu.SemaphoreType.DMA((7 * n,)), pltpu.SemaphoreType.DMA((n,))]


def _all_gather(shards):
    n = len(shards)

    def body(*refs):
        gather = _TwoLevelGather(refs[:n], refs[n:2 * n], *refs[2 * n:])
        gather.start()
        gather.forward()
        gather.forward_relayed()
        gather.finish()

    return pl.pallas_call(
        body, name="all_gather_w_in", out_shape=_TwoLevelGather.out_shapes(shards),
        in_specs=[HBM_SPEC] * n, out_specs=(HBM_SPEC,) * n, scratch_shapes=_TwoLevelGather.semaphores(n),
    )(*shards)


class _Exchange:
    def __init__(self, src_refs, recv_refs, rows, send_sems, recv_sems, local_sems):
        x, y, c = _mesh_pos()
        me = 4 * x + 2 * y + c
        self.own, self.sends, self.arrivals = [], [], []
        for a, (src, recv) in enumerate(zip(src_refs, recv_refs)):
            def blk(k, src=src, r=rows[a]):
                return src if r is None else src.at[pl.ds(k * r, r), :]

            self.own.append(pltpu.make_async_copy(blk(me), recv.at[me], local_sems.at[a]))
            for rel in range(1, N_DEV):
                px = 1 - x if rel & 4 else x
                py = 1 - y if rel & 2 else y
                pc = 1 - c if rel & 1 else c
                peer = 4 * px + 2 * py + pc
                sem = dict(send_sem=send_sems.at[7 * a + rel - 1], recv_sem=recv_sems.at[7 * a + rel - 1],
                           device_id=(px, py, pc), device_id_type=MESH)
                self.sends.append(pltpu.make_async_remote_copy(src_ref=blk(peer), dst_ref=recv.at[me], **sem))
                self.arrivals.append(pltpu.make_async_remote_copy(src_ref=blk(me), dst_ref=recv.at[peer], **sem))

    def start(self):
        for cp in self.own + self.sends:
            cp.start()

    def finish(self):
        for cp in self.arrivals:
            cp.wait_recv()
        for cp in self.sends:
            cp.wait_send()
        for cp in self.own:
            cp.wait()

    @staticmethod
    def out_shapes(srcs, rows):
        return tuple(jax.ShapeDtypeStruct((N_DEV, s.shape[0] if r is None else r, s.shape[1]), s.dtype) for s, r in zip(srcs, rows))

    @staticmethod
    def semaphores(n):
        return [pltpu.SemaphoreType.DMA((7 * n,)), pltpu.SemaphoreType.DMA((7 * n,)), pltpu.SemaphoreType.DMA((n,))]


def _sum_slots(slots, tile_rows, name):
    _, rows, cols = slots.shape
    tr = rows
    if N_DEV * rows * cols * slots.dtype.itemsize > 8 * 1024 * 1024:
        tr = next(c for c in (256, 192, 128, 64, 32, 16) if c <= tile_rows and rows % c == 0)

    def body(s_ref, o_ref):
        acc = s_ref[0].astype(F32)
        for k in range(1, N_DEV):
            acc = acc + s_ref[k].astype(F32)
        o_ref[...] = acc

    return pl.pallas_call(
        body, name=name, grid=(rows // tr,),
        in_specs=[pl.BlockSpec((N_DEV, tr, cols), lambda i: (0, i, 0))],
        out_specs=pl.BlockSpec((tr, cols), lambda i: (i, 0)),
        out_shape=jax.ShapeDtypeStruct((rows, cols), F32),
        compiler_params=_params(),
    )(slots)


def _gelu_and_grad(x):
    k0 = math.sqrt(2.0 / math.pi)
    k1 = 0.044715
    a = k1 * (x * x)
    half = 1.0 / (1.0 + jnp.exp((-2.0 * k0) * x * (1.0 + a)))
    g = x * half
    return g, half + g * (1.0 - half) * (2.0 * k0 + (6.0 * k0) * a)


def _ln_stats(r):
    mu = jnp.mean(r, axis=-1, keepdims=True)
    rc = r - mu
    var = jnp.mean(rc * rc, axis=-1, keepdims=True)
    rstd = lax.rsqrt(var + LN_EPS)
    return rc * rstd, rstd


def _ln_bwd(dxh, xh, rstd):
    return rstd * (dxh - jnp.mean(dxh, axis=-1, keepdims=True) - xh * jnp.mean(dxh * xh, axis=-1, keepdims=True))


def _colsum(a):
    return jnp.sum(a, axis=0, keepdims=True)


def _mm(a, b):
    return jnp.dot(a, b, preferred_element_type=F32)


def _halo_maps(tm, t, unit):
    per, last = tm // unit, t // unit - 1
    return (lambda i: (jnp.maximum(i * per - 1, 0), 0)), (lambda i: (jnp.minimum((i + 1) * per, last), 0))


def _proj_in(x2, w_int, b_gate, shards, dm, tm):
    t, d, npj = dm.T, dm.D, dm.NP
    cw = d // 2
    nt = t // tm
    n = len(shards)

    def body(x_ref, w_ref, bg_ref, *refs):
        p_ref = refs[n]
        gather = _TwoLevelGather(refs[:n], refs[n + 1:2 * n + 1], *refs[2 * n + 1:])
        i = pl.program_id(0)

        @pl.when(i == 0)
        def _():
            gather.start()

        @pl.when(i == nt // 2)
        def _():
            gather.forward()

        @pl.when(i == 3 * nt // 4)
        def _():
            gather.forward_relayed()

        xb = x_ref[...].astype(BF16)
        for blk in range(npj // cw):
            lo = blk * cw
            acc = lax.dot_general(xb, w_ref[lo:lo + cw, :], NT_DIMS, preferred_element_type=F32)
            if lo >= dm.OFF_GA:
                acc = jax.nn.sigmoid(acc + bg_ref[:, lo - dm.OFF_GA:lo - dm.OFF_GA + cw])
            p_ref[:, lo:lo + cw] = acc.astype(BF16)

        @pl.when(i == nt - 1)
        def _():
            gather.finish()

    return pl.pallas_call(
        body, name="proj_in", grid=(nt,),
        in_specs=[pl.BlockSpec((tm, d), lambda i: (i, 0)), _resident((npj, d)), _resident((1, 2 * d))] + [HBM_SPEC] * n,
        out_specs=(pl.BlockSpec((tm, npj), lambda i: (i, 0)),) + (HBM_SPEC,) * n,
        out_shape=(jax.ShapeDtypeStruct((t, npj), BF16),) + _TwoLevelGather.out_shapes(shards),
        scratch_shapes=_TwoLevelGather.semaphores(n),
        compiler_params=_params(),
    )(x2, w_int, b_gate, *shards)


def _row_neighbours(ext, tm):
    h, n = F32_SUBLANES, ext.shape[0]
    return pltpu.roll(ext, 1, 0)[h:h + tm], pltpu.roll(ext, n - 1, 0)[h:h + tm]


def _spatial_mix(vn, ws_ref, bias_ref, mixed_ref, dm, tm):
    vb = vn.astype(BF16)
    for cc in range(tm // CHUNK):
        r0 = cc * CHUNK
        for h in range(dm.H):
            c0 = h * CHUNK
            m = _mm(ws_ref[h], vb[r0:r0 + CHUNK, c0:c0 + CHUNK])
            mixed_ref[r0:r0 + CHUNK, c0:c0 + CHUNK] = m + bias_ref[:, c0:c0 + CHUNK]
    return mixed_ref[...]


def _mixer_fwd(p, x2, w_pa, w_pb, w_o, conv_w8, vng, vnb, ws_b, bias_s, dm, tm):
    t, d, wa, npj = dm.T, dm.D, dm.WA, dm.NP
    nt = t // tm
    hb = BF16_SUBLANES
    prev_map, next_map = _halo_maps(tm, t, hb)

    def body(p_ref, pp_ref, pn_ref, x_ref, wpa_ref, wpb_ref, wo_ref, cw_ref, vng_ref, vnb_ref, ws_ref, bias_ref,
             ya_ref, yb_ref, r1_ref, gu_ref, dgu_ref, xhv_ref, rgv_ref, cv_ref, s_ref, mixed_ref):
        i = pl.program_id(0)

        def col(ref, lo, width):
            return ref[:, lo:lo + width].astype(F32)

        ch = col(p_ref, dm.OFF_CA, wa) * col(p_ref, dm.OFF_HA, wa)
        chp = (col(pp_ref, dm.OFF_CA, wa) * col(pp_ref, dm.OFF_HA, wa))[hb - F32_SUBLANES:hb]
        chn = (col(pn_ref, dm.OFF_CA, wa) * col(pn_ref, dm.OFF_HA, wa))[0:F32_SUBLANES]
        ch_e = jnp.concatenate([jnp.where(i == 0, 0.0, chp), ch, jnp.where(i == nt - 1, 0.0, chn)], axis=0)
        up, dn = _row_neighbours(ch_e, tm)
        cv = cw_ref[0:1, :] * up + cw_ref[1:2, :] * ch + cw_ref[2:3, :] * dn
        cv_ref[...] = cv.astype(BF16)
        ya = _mm((col(p_ref, 0, wa) * cv).astype(BF16), wpa_ref[...])
        gv, dgelu_v = _gelu_and_grad(col(p_ref, dm.OFF_VB, d))
        xhv, rstdv = _ln_stats(gv)
        xhv_ref[...] = xhv.astype(BF16)
        rgv_ref[...] = (rstdv * dgelu_v).astype(BF16)
        mixed = _spatial_mix(xhv * vng_ref[...] + vnb_ref[...], ws_ref, bias_ref, mixed_ref, dm, tm)
        gu, dgelu_u = _gelu_and_grad(col(p_ref, dm.OFF_UB, d))
        gu_ref[...] = gu.astype(BF16)
        dgu_ref[...] = dgelu_u.astype(BF16)
        yb = _mm((gu * mixed).astype(BF16), wpb_ref[...])
        sb = (col(p_ref, dm.OFF_GA, d) * ya + col(p_ref, dm.OFF_GB, d) * yb).astype(BF16)
        s_ref[...] = sb
        mix = _mm(sb, wo_ref[...])
        ya_ref[...] = ya.astype(BF16)
        yb_ref[...] = yb.astype(BF16)
        r1_ref[...] = ALPHA * x_ref[...] + mix

    tile = lambda i: (i, 0)
    return pl.pallas_call(
        body, name="mixer_fwd", grid=(nt,),
        in_specs=[pl.BlockSpec((tm, npj), tile), pl.BlockSpec((hb, npj), prev_map), pl.BlockSpec((hb, npj), next_map),
                  pl.BlockSpec((tm, d), tile), _resident((wa, d)), _resident((d, d)), _resident((d, d)),
                  _resident((F32_SUBLANES, wa)), _resident((1, d)), _resident((1, d)),
                  _resident((dm.H, CHUNK, CHUNK)), _resident((CHUNK, d))],
        out_specs=(pl.BlockSpec((tm, d), tile),) * 7 + (pl.BlockSpec((tm, wa), tile), pl.BlockSpec((tm, d), tile)),
        out_shape=(jax.ShapeDtypeStruct((t, d), BF16), jax.ShapeDtypeStruct((t, d), BF16), jax.ShapeDtypeStruct((t, d), F32))
        + (jax.ShapeDtypeStruct((t, d), BF16),) * 4 + (jax.ShapeDtypeStruct((t, wa), BF16), jax.ShapeDtypeStruct((t, d), BF16)),
        scratch_shapes=[pltpu.VMEM((tm, d), F32)],
        compiler_params=_params(),
    )(p, p, p, x2, w_pa, w_pb, w_o, conv_w8, vng, vnb, ws_b, bias_s)


def _ffn_fwd(r1, tgt, w1t, w2, ln1g, ln1b, ln2g, ln2b, dm, tm):
    t, d, dff = dm.T, dm.D, dm.DFF
    fc = dff // N_DEV

    def body(r1_ref, tgt_ref, w1t_ref, w2_ref, g1_ref, b1_ref, g2_ref, b2_ref, relu_ref, x1_ref, dr2_ref, dr2b_ref, sums_ref):
        @pl.when(pl.program_id(0) == 0)
        def _():
            sums_ref[...] = jnp.zeros_like(sums_ref)

        xh1, _ = _ln_stats(r1_ref[...])
        x1 = xh1 * g1_ref[...] + b1_ref[...]
        x1b = x1.astype(BF16)
        x1_ref[...] = x1b
        ffn = jnp.zeros((tm, d), F32)
        for k in range(dff // fc):
            ks = slice(k * fc, (k + 1) * fc)
            r = jnp.maximum(lax.dot_general(x1b, w1t_ref[ks, :], NT_DIMS, preferred_element_type=F32), 0.0)
            relu_ref[:, ks] = r.astype(BF16)
            ffn = ffn + _mm((r * r).astype(BF16), w2_ref[ks, :])
        xh2, rstd2 = _ln_stats(ALPHA * x1 + ffn)
        diff = xh2 * g2_ref[...] + b2_ref[...] - tgt_ref[...]
        dy = diff * (1.0 / d)
        dr2 = _ln_bwd(dy * g2_ref[...], xh2, rstd2)
        dr2_ref[...] = dr2
        dr2b_ref[...] = dr2.astype(BF16)
        sums_ref[0:1, :] += _colsum(diff * diff)
        sums_ref[1:2, :] += _colsum(dy * xh2)
        sums_ref[2:3, :] += _colsum(dy)

    tile = lambda i: (i, 0)
    vec = _resident((1, d))
    return pl.pallas_call(
        body, name="ffn_fwd", grid=(t // tm,),
        in_specs=[pl.BlockSpec((tm, d), tile), pl.BlockSpec((tm, d), tile), _resident((dff, d)), _resident((dff, d)),
                  vec, vec, vec, vec],
        out_specs=(pl.BlockSpec((tm, dff), tile), pl.BlockSpec((tm, d), tile), pl.BlockSpec((tm, d), tile),
                   pl.BlockSpec((tm, d), tile), pl.BlockSpec((F32_SUBLANES, d), lambda i: (0, 0))),
        out_shape=(jax.ShapeDtypeStruct((t, dff), BF16), jax.ShapeDtypeStruct((t, d), BF16), jax.ShapeDtypeStruct((t, d), F32),
                   jax.ShapeDtypeStruct((t, d), BF16), jax.ShapeDtypeStruct((F32_SUBLANES, d), F32)),
        compiler_params=_params(),
    )(r1, tgt, w1t, w2, ln1g, ln1b, ln2g, ln2b)


def _ffn_bwd(r1, relu, dr2, w2, w1t, ln1g, dm, tm):
    t, d, dff = dm.T, dm.D, dm.DFF
    fc = dff // N_DEV

    def body(r1_ref, relu_ref, dr2_ref, w2_ref, w1t_ref, g1_ref, dr1_ref, dh_ref, sums_ref):
        @pl.when(pl.program_id(0) == 0)
        def _():
            sums_ref[...] = jnp.zeros_like(sums_ref)

        xh1, rstd1 = _ln_stats(r1_ref[...])
        dr2 = dr2_ref[...]
        dr2b = dr2.astype(BF16)
        dx1 = ALPHA * dr2
        for k in range(dff // fc):
            ks = slice(k * fc, (k + 1) * fc)
            dact = lax.dot_general(dr2b, w2_ref[ks, :], NT_DIMS, preferred_element_type=F32)
            dhb = (dact * (2.0 * relu_ref[:, ks].astype(F32))).astype(BF16)
            dh_ref[:, ks] = dhb
            dx1 = dx1 + _mm(dhb, w1t_ref[ks, :])
        dr1_ref[...] = _ln_bwd(dx1 * g1_ref[...], xh1, rstd1)
        sums_ref[0:1, :] += _colsum(dx1 * xh1)
        sums_ref[1:2, :] += _colsum(dx1)

    tile = lambda i: (i, 0)
    return pl.pallas_call(
        body, name="ffn_bwd", grid=(t // tm,),
        in_specs=[pl.BlockSpec((tm, d), tile), pl.BlockSpec((tm, dff), tile), pl.BlockSpec((tm, d), tile),
                  _resident((dff, d)), _resident((dff, d)), _resident((1, d))],
        out_specs=(pl.BlockSpec((tm, d), tile), pl.BlockSpec((tm, dff), tile), pl.BlockSpec((F32_SUBLANES, d), lambda i: (0, 0))),
        out_shape=(jax.ShapeDtypeStruct((t, d), F32), jax.ShapeDtypeStruct((t, dff), BF16),
                   jax.ShapeDtypeStruct((F32_SUBLANES, d), F32)),
        compiler_params=_params(),
    )(r1, relu, dr2, w2, w1t, ln1g)


def _wgrad(pairs, tt, fb, name, xchg=()):
    n, m = len(pairs), len(xchg)
    t, f = pairs[0][0].shape
    d = pairs[0][1].shape[1]
    squares = [sq for _, _, sq in pairs]
    nj, ni = f // fb, t // tt
    xrows = [r for _, r in xchg]

    def body(*refs):
        ins, xsrc = refs[:2 * n], refs[2 * n:2 * n + m]
        outs, xrecv = refs[2 * n + m:3 * n + m], refs[3 * n + m:3 * n + 2 * m]
        accs, sems = refs[3 * n + 2 * m:4 * n + 2 * m], refs[4 * n + 2 * m:]
        j, i = pl.program_id(0), pl.program_id(1)
        exchange = _Exchange(xsrc, xrecv, xrows, *sems) if m else None

        if m:
            @pl.when(jnp.logical_and(j == 0, i == 0))
            def _():
                exchange.start()

        @pl.when(i == 0)
        def _():
            for acc in accs:
                acc[...] = jnp.zeros_like(acc)

        for q in range(n):
            lhs = ins[2 * q][...]
            if squares[q]:
                lf = lhs.astype(F32)
                lhs = (lf * lf).astype(BF16)
            accs[q][...] += lax.dot_general(lhs, ins[2 * q + 1][...].astype(BF16), TN_DIMS, preferred_element_type=F32)

        @pl.when(i == ni - 1)
        def _():
            for q in range(n):
                outs[q][...] = accs[q][...].astype(BF16)

        if m:
            @pl.when(jnp.logical_and(j == nj - 1, i == ni - 1))
            def _():
                exchange.finish()

    lhs_spec = pl.BlockSpec((tt, fb), lambda j, i: (i, j))
    rhs_spec = pl.BlockSpec((tt, d), lambda j, i: (i, 0))
    out_spec = pl.BlockSpec((fb, d), lambda j, i: (j, 0))
    xsrcs = [a_ for a_, _ in xchg]
    return pl.pallas_call(
        body, name=name, grid=(nj, ni),
        in_specs=[lhs_spec, rhs_spec] * n + [HBM_SPEC] * m, out_specs=(out_spec,) * n + (HBM_SPEC,) * m,
        out_shape=(jax.ShapeDtypeStruct((f, d), BF16),) * n + _Exchange.out_shapes(xsrcs, xrows),
        scratch_shapes=[pltpu.VMEM((fb, d), F32)] * n + (_Exchange.semaphores(m) if m else []),
        compiler_params=_params(("arbitrary", "arbitrary")),
    )(*[a_ for lhs, rhs, _ in pairs for a_ in (lhs, rhs)], *xsrcs)


RING_SLOTS = 3


def _mixer_bwd(dr1, p, ya, yb, gu_s, dgu_s, xhv_s, rgv_s, cv_s, w_ot, w_pat, w_pbt, conv_w8, vng, vnb, ws_b, wst_b, bias_s, head_sel, xchg, dm, tm):
    t, d, wa, npj = dm.T, dm.D, dm.WA, dm.NP
    nt = t // tm
    h8, hb = F32_SUBLANES, BF16_SUBLANES
    ext = tm + 2 * h8
    prev_f, next_f = _halo_maps(tm, t, h8)
    prev_b, next_b = _halo_maps(tm, t, hb)
    nx = len(xchg)
    xsrcs, xrows = [a_ for a_, _ in xchg], [r for _, r in xchg]

    gw = d // 2

    def gate_map(rows, k):
        return lambda i: (rows(i)[0], dm.OFF_GA // gw + k)

    def body(dr_ref, drp_ref, drn_ref, p_hbm, pg0_ref, pg1_ref, pg2_ref, pg3_ref, ppc_ref, ppg0_ref, ppg1_ref,
             pnc_ref, png0_ref, png1_ref, ya_ref, yb_ref, gu_ref, dgu_ref, xhv_ref, rgv_ref, cv_ref,
             wot_ref, wpat_ref, wpbt_ref,
             cw_ref, vng_ref, vnb_ref, ws_ref, wst_ref, bias_ref, sel_ref,
             *refs):
        xsrc = refs[:nx]
        dp_ref, a_ref, dya_ref, bb_ref, dyb_ref, dws_ref, dbs_ref, dcw_ref, dbg_ref, dvn_ref = refs[nx:nx + 10]
        xrecv = refs[nx + 10:2 * nx + 10]
        mixed_ref, dvnm_ref, ring_ref, ring_sems = refs[2 * nx + 10:2 * nx + 14]
        exchange = _Exchange(xsrc, xrecv, xrows, *refs[2 * nx + 14:])
        i = pl.program_id(0)

        def fetch(j):
            slot = j % RING_SLOTS
            return pltpu.make_async_copy(p_hbm.at[pl.ds(j * tm, tm), pl.ds(0, 3 * wa)], ring_ref.at[slot], ring_sems.at[slot])

        @pl.when(i == 0)
        def _():
            exchange.start()
            for j in range(min(RING_SLOTS - 1, nt)):
                fetch(j).start()
            for ref in (dws_ref, dbs_ref, dcw_ref, dbg_ref, dvn_ref):
                ref[...] = jnp.zeros_like(ref)

        @pl.when(i + RING_SLOTS - 1 < nt)
        def _():
            fetch(i + RING_SLOTS - 1).start()

        fetch(i).wait()
        pc_ref = ring_ref.at[i % RING_SLOTS]

        def col(ref, lo, width):
            return ref[:, lo:lo + width].astype(F32)

        def ext_rows(prev_blk, center, next_blk):
            return jnp.concatenate([prev_blk, center, next_blk], axis=0)

        def lanes(*refs):
            return jnp.concatenate([r[...].astype(F32) for r in refs], axis=1)

        xhv = xhv_ref[...].astype(F32)
        vn = xhv * vng_ref[...] + vnb_ref[...]
        mixed = _spatial_mix(vn, ws_ref, bias_ref, mixed_ref, dm, tm)
        gu = gu_ref[...].astype(F32)
        bb_ref[...] = (gu * mixed).astype(BF16)
        g_a, g_b = lanes(pg0_ref, pg1_ref), lanes(pg2_ref, pg3_ref)
        y_a, y_b = ya_ref[...].astype(F32), yb_ref[...].astype(F32)
        qa = y_a * g_a * (1.0 - g_a)
        qb = y_b * g_b * (1.0 - g_b)
        cv = cv_ref[...].astype(F32)
        b_a, c_a, h_a = col(pc_ref, 0, wa), col(pc_ref, dm.OFF_CA, wa), col(pc_ref, dm.OFF_HA, wa)
        a_ref[...] = (b_a * cv).astype(BF16)
        ch = c_a * h_a
        row = lax.broadcasted_iota(jnp.int32, (ext, 1), 0) + (i * tm - h8)
        inside = jnp.logical_and(row >= 0, row < t)
        dr_e = ext_rows(drp_ref[...], dr_ref[...], drn_ref[...])
        ds_e = _mm(dr_e.astype(BF16), wot_ref[...])
        dya_e = ds_e * ext_rows(lanes(ppg0_ref, ppg1_ref)[hb - h8:hb], g_a, lanes(png0_ref, png1_ref)[0:h8])
        da_e = _mm(dya_e.astype(BF16), wpat_ref[...])
        dcv_e = jnp.where(inside, da_e * ext_rows(col(ppc_ref, 0, wa)[hb - h8:hb], b_a, col(pnc_ref, 0, wa)[0:h8]), 0.0)
        dcv, (dcv_up, dcv_dn) = dcv_e[h8:h8 + tm], _row_neighbours(dcv_e, tm)
        w0, w1, w2 = cw_ref[0:1, :], cw_ref[1:2, :], cw_ref[2:3, :]
        dp_ref[:, 0:wa] = (da_e[h8:h8 + tm] * cv).astype(BF16)
        dch = w0 * dcv_dn + w1 * dcv + w2 * dcv_up
        dp_ref[:, dm.OFF_CA:dm.OFF_CA + wa] = (dch * h_a).astype(BF16)
        dp_ref[:, dm.OFF_HA:dm.OFF_HA + wa] = (dch * c_a).astype(BF16)
        dcw_ref[0:1, :] += _colsum(dcv_dn * ch)
        dcw_ref[1:2, :] += _colsum(dcv * ch)
        dcw_ref[2:3, :] += _colsum(dcv_up * ch)
        dya_ref[...] = dya_e[h8:h8 + tm].astype(BF16)
        ds = ds_e[h8:h8 + tm]
        dzga = ds * qa
        dzgb = ds * qb
        dp_ref[:, dm.OFF_GA:dm.OFF_GA + d] = dzga.astype(BF16)
        dp_ref[:, dm.OFF_GB:dm.OFF_GB + d] = dzgb.astype(BF16)
        dbg_ref[0:1, 0:d] += _colsum(dzga)
        dbg_ref[0:1, d:2 * d] += _colsum(dzgb)
        dyb = (ds * g_b).astype(BF16)
        dyb_ref[...] = dyb
        dbb = _mm(dyb, wpbt_ref[...])
        dp_ref[:, dm.OFF_UB:dm.OFF_UB + d] = (dbb * mixed * dgu_ref[...].astype(F32)).astype(BF16)
        dmb = (dbb * gu).astype(BF16)
        vb = vn.astype(BF16)
        dbs = jnp.zeros((CHUNK, CHUNK), F32)
        for cc in range(tm // CHUNK):
            r0 = cc * CHUNK
            dbs = dbs + _mm(dmb[r0:r0 + CHUNK, :], sel_ref[...])
            for h in range(dm.H):
                c0 = h * CHUNK
                blk = dmb[r0:r0 + CHUNK, c0:c0 + CHUNK]
                dvnm_ref[r0:r0 + CHUNK, c0:c0 + CHUNK] = _mm(wst_ref[h], blk)
                dws_ref[h] += lax.dot_general(blk, vb[r0:r0 + CHUNK, c0:c0 + CHUNK], NT_DIMS, preferred_element_type=F32)
        dbs_ref[...] += dbs
        dvn = dvnm_ref[...]
        dvn_ref[0:1, :] += _colsum(dvn * xhv)
        dvn_ref[1:2, :] += _colsum(dvn)
        dp_ref[:, dm.OFF_VB:dm.OFF_VB + d] = (_ln_bwd(dvn * vng_ref[...], xhv, rgv_ref[...].astype(F32))).astype(BF16)

        @pl.when(i == nt - 1)
        def _():
            exchange.finish()

    full = lambda i: (0, 0)
    tile = lambda i: (i, 0)
    hcc = _resident((dm.H, CHUNK, CHUNK))
    tok = lambda w, dt: jax.ShapeDtypeStruct((t, w), dt)
    return pl.pallas_call(
        body, name="mixer_bwd", grid=(nt,),
        in_specs=[pl.BlockSpec((tm, d), tile), pl.BlockSpec((h8, d), prev_f), pl.BlockSpec((h8, d), next_f),
                  HBM_SPEC] + [pl.BlockSpec((tm, gw), gate_map(tile, k)) for k in range(4)]
        + [pl.BlockSpec((hb, wa), prev_b)] + [pl.BlockSpec((hb, gw), gate_map(prev_b, k)) for k in range(2)]
        + [pl.BlockSpec((hb, wa), next_b)] + [pl.BlockSpec((hb, gw), gate_map(next_b, k)) for k in range(2)]
        + [pl.BlockSpec((tm, d), tile)] * 6
        + [pl.BlockSpec((tm, wa), tile), _resident((d, d)), _resident((d, wa)), _resident((d, d)),
           _resident((h8, wa)), _resident((1, d)), _resident((1, d)), hcc, hcc, _resident((CHUNK, d)),
           _resident((d, CHUNK))] + [HBM_SPEC] * nx,
        out_specs=(pl.BlockSpec((tm, npj), tile), pl.BlockSpec((tm, wa), tile), pl.BlockSpec((tm, d), tile),
                   pl.BlockSpec((tm, d), tile), pl.BlockSpec((tm, d), tile),
                   pl.BlockSpec((dm.H, CHUNK, CHUNK), lambda i: (0, 0, 0)), pl.BlockSpec((CHUNK, CHUNK), full),
                   pl.BlockSpec((h8, wa), full), pl.BlockSpec((h8, 2 * d), full), pl.BlockSpec((h8, d), full))
        + (HBM_SPEC,) * nx,
        out_shape=(tok(npj, BF16), tok(wa, BF16), tok(d, BF16), tok(d, BF16), tok(d, BF16),
                   jax.ShapeDtypeStruct((dm.H, CHUNK, CHUNK), F32), jax.ShapeDtypeStruct((CHUNK, CHUNK), F32),
                   jax.ShapeDtypeStruct((h8, wa), F32), jax.ShapeDtypeStruct((h8, 2 * d), F32),
                   jax.ShapeDtypeStruct((h8, d), F32)) + _Exchange.out_shapes(xsrcs, xrows),
        scratch_shapes=[pltpu.VMEM((tm, d), F32), pltpu.VMEM((tm, d), F32), pltpu.VMEM((RING_SLOTS, tm, 3 * wa), BF16),
                        pltpu.SemaphoreType.DMA((RING_SLOTS,))] + _Exchange.semaphores(nx),
        compiler_params=_params(),
    )(dr1, dr1, dr1, *([p] * 11), ya, yb, gu_s, dgu_s, xhv_s, rgv_s, cv_s, w_ot, w_pat, w_pbt, conv_w8, vng, vnb, ws_b, wst_b, bias_s, head_sel, *xsrcs)


def _input_grad(dp, dr1, w_int, xchg, dm, tm):
    t, d, npj = dm.T, dm.D, dm.NP
    nt = t // tm
    nx = len(xchg)
    xsrcs, xrows = [a_ for a_, _ in xchg], [r for _, r in xchg]

    def body(dp_ref, dr_ref, w_ref, *refs):
        dx_ref = refs[nx]
        exchange = _Exchange(refs[:nx], refs[nx + 1:2 * nx + 1], xrows, *refs[2 * nx + 1:])
        i = pl.program_id(0)

        @pl.when(i == 0)
        def _():
            exchange.start()

        dx_ref[...] = ALPHA * dr_ref[...] + _mm(dp_ref[...], w_ref[...])

        @pl.when(i == nt - 1)
        def _():
            exchange.finish()

    return pl.pallas_call(
        body, name="input_grad", grid=(nt,),
        in_specs=[pl.BlockSpec((tm, npj), lambda i: (i, 0)), pl.BlockSpec((tm, d), lambda i: (i, 0)), _resident((npj, d))]
        + [HBM_SPEC] * nx,
        out_specs=(pl.BlockSpec((tm, d), lambda i: (i, 0)),) + (HBM_SPEC,) * nx,
        out_shape=(jax.ShapeDtypeStruct((t, d), F32),) + _Exchange.out_shapes(xsrcs, xrows),
        scratch_shapes=_Exchange.semaphores(nx),
        compiler_params=_params(),
    )(dp, dr1, w_int, *xsrcs)


def _adamw_math(w, g, m, v):
    nm = ADAM_B1 * m + (1.0 - ADAM_B1) * g
    nv = ADAM_B2 * v + (1.0 - ADAM_B2) * (g * g)
    delta = -ADAM_LR * ((nm / (1.0 - ADAM_B1 ** ADAM_STEP)) / (jnp.sqrt(nv / (1.0 - ADAM_B2 ** ADAM_STEP)) + ADAM_EPS) + ADAM_WD * w)
    return delta, nm, nv


def _adamw(w, g, m, v, name):
    _, rows, cols = w.shape
    tr = 256 if rows % 256 == 0 else rows
    from_slots = g.ndim == 3

    def body(w_ref, g_ref, m_ref, v_ref, go_ref, d_ref, nm_ref, nv_ref):
        if from_slots:
            g_ = g_ref[0].astype(F32)
            for k in range(1, N_DEV):
                g_ = g_ + g_ref[k].astype(F32)
        else:
            g_ = g_ref[...]
        go_ref[0] = g_
        d_ref[0], nm_ref[0], nv_ref[0] = _adamw_math(w_ref[0], g_, m_ref[0], v_ref[0])

    spec = pl.BlockSpec((1, tr, cols), lambda i: (0, i, 0))
    g_spec = pl.BlockSpec((N_DEV, tr, cols), lambda i: (0, i, 0)) if from_slots else pl.BlockSpec((tr, cols), lambda i: (i, 0))
    shp = jax.ShapeDtypeStruct((1, rows, cols), F32)
    return pl.pallas_call(
        body, name=name, grid=(rows // tr,), in_specs=[spec, g_spec, spec, spec], out_specs=(spec,) * 4, out_shape=(shp,) * 4,
        compiler_params=_params(),
    )(w, g, m, v)


def _adamw_small(ws, gs, ms, vs):
    n = len(ws)

    def body(*refs):
        ins, outs = refs[:4 * n], refs[4 * n:]
        for k in range(n):
            w_ref, g_ref, m_ref, v_ref = ins[k], ins[n + k], ins[2 * n + k], ins[3 * n + k]
            outs[k][...], outs[n + k][...], outs[2 * n + k][...] = _adamw_math(w_ref[...], g_ref[...], m_ref[...], v_ref[...])

    shapes = tuple(jax.ShapeDtypeStruct(w.shape, F32) for w in ws)
    out = pl.pallas_call(body, name="adamw_small", out_shape=shapes * 3, compiler_params=_params(()))(*ws, *gs, *ms, *vs)
    return out[:n], out[n:2 * n], out[2 * n:]


def _to_slab(parts):
    flat = jnp.concatenate([q.reshape(-1) for q in parts])
    pad = (-flat.shape[0]) % (F32_SUBLANES * LANES)
    return jnp.pad(flat, (0, pad)).reshape(-1, LANES)


def _from_slab(slab, shapes):
    flat = slab.reshape(-1)
    out, off = [], 0
    for s in shapes:
        n = math.prod(s)
        out.append(flat[off:off + n].reshape(s))
        off += n
    return out


def kernel(x, w_in, b_gate, conv_w, v_norm_g, v_norm_b, w_s, b_s, w_pa, w_pb, w_o, ln1_g, ln1_b, w_ff1, w_ff2, ln2_g, ln2_b, loss_target, m_w_in, m_b_gate, m_conv_w, m_v_norm_g, m_v_norm_b, m_w_s, m_b_s, m_w_pa, m_w_pb, m_w_o, m_ln1_g, m_ln1_b, m_w_ff1, m_w_ff2, m_ln2_g, m_ln2_b, v_w_in, v_b_gate, v_conv_w, v_v_norm_g, v_v_norm_b, v_w_s, v_b_s, v_w_pa, v_w_pb, v_w_o, v_ln1_g, v_ln1_b, v_w_ff1, v_w_ff2, v_ln2_g, v_ln2_b):
    t, d = x.shape[1], x.shape[2]
    dm = _Dims(t, d)
    tm = 256 if t % 256 == 0 else CHUNK
    tm_big = 512 if t % 512 == 0 else tm
    tt = 1024 if t % 1024 == 0 else tm
    me = 4 * lax.axis_index("x") + 2 * lax.axis_index("y") + lax.axis_index("c")
    x2, tgt = x[0], loss_target[0]

    conv_bits = lax.bitcast_convert_type(conv_w[0], BF16).reshape(-1)
    conv_blk = jnp.pad(conv_bits, (0, dm.conv_rows * d - conv_bits.shape[0])).reshape(dm.conv_rows, d)
    w_int, conv_g = _all_gather([w_in[0].T.astype(BF16), conv_blk])
    wa8 = dm.WA // N_DEV
    conv_all = lax.bitcast_convert_type(conv_g.reshape(N_DEV, -1)[:, :3 * wa8 * 2].reshape(N_DEV, 3, wa8, 2), F32)
    conv_full = jnp.transpose(conv_all, (1, 0, 2)).reshape(3, dm.WA)
    conv_w8 = jnp.pad(conv_full, ((0, F32_SUBLANES - 3), (0, 0)))
    ws_b = w_s[0].astype(BF16)
    wst_b = jnp.transpose(w_s[0], (0, 2, 1)).astype(BF16)
    bias_s = jnp.repeat(b_s[0].T, CHUNK, axis=1)
    head_sel = (jnp.arange(d)[:, None] // CHUNK == jnp.arange(CHUNK)[None, :]).astype(BF16)

    p, w_pa_f, w_pb_f, w_o_f, w_1t, w_2 = _proj_in(
        x2, w_int, b_gate, [w_pa[0].astype(BF16), w_pb[0].astype(BF16), w_o[0].astype(BF16), w_ff1[0].T.astype(BF16),
                            w_ff2[0].astype(BF16)], dm, tm_big)
    ya, yb, r1, gu_s, dgu_s, xhv_s, rgv_s, cv_s, s_m = _mixer_fwd(
        p, x2, w_pa_f, w_pb_f, w_o_f, conv_w8, v_norm_g, v_norm_b, ws_b, bias_s, dm, tm)
    relu, x1b, dr2, dr2b, sums2 = _ffn_fwd(r1, tgt, w_1t, w_2, ln1_g, ln1_b, ln2_g, ln2_b, dm, tm_big)
    dr1, dh1, sums1 = _ffn_bwd(r1, relu, dr2, w_2, w_1t, ln1_g, dm, tm_big)
    fb = min(1024, dm.DFF)
    rows = dm.shard_rows
    g_ff1t, g_ff2 = _wgrad([(dh1, x1b, False), (relu, dr2b, True)], tt, fb, "ffn_wgrad")
    dp, a_m, dya, bb_m, dyb, g_ws, g_bs_t, g_cw, g_bg, g_vn, got_ff1t, got_ff2 = _mixer_bwd(
        dr1, p, ya, yb, gu_s, dgu_s, xhv_s, rgv_s, cv_s, w_o_f.T, w_pa_f.T, w_pb_f.T, conv_w8, v_norm_g, v_norm_b, ws_b, wst_b, bias_s, head_sel,
        [(g_ff1t, rows[4]), (g_ff2, rows[5])], dm, tm)
    (g_pa,) = _wgrad([(a_m, dya, False)], tt, dm.WA, "w_pa_grad")
    g_o, g_pb = _wgrad([(s_m, dr1, False), (bb_m, dyb, False)], tt, d, "w_o_pb_grad")
    g_int, got_pa, got_pb, got_o = _wgrad([(dp, x2, False)], tt, 17 * LANES, "w_in_grad",
                                          xchg=[(g_pa, rows[1]), (g_pb, rows[2]), (g_o, rows[3])])
    small_parts = [g_bg[0], g_cw[0:3], g_vn[0], g_vn[1], g_ws, g_bs_t[:, :dm.H].T,
                   sums1[0], sums1[1], sums2[1], sums2[2], sums2[0]]
    grad_x, got_int, got_s = _input_grad(dp, dr1, w_int, [(g_int, rows[0]), (_to_slab(small_parts), None)], dm, tm_big)

    ssum = _sum_slots(got_s, 256, "sum_small")
    (s_bg, s_cw, s_vng, s_vnb, s_ws, s_bs, s_l1g, s_l1b, s_l2g, s_l2b, s_sq) = _from_slab(
        ssum, [(1, 2 * d), (3, dm.WA), (1, d), (1, d), (1, dm.H, CHUNK, CHUNK), (1, dm.H, CHUNK),
               (1, d), (1, d), (1, d), (1, d), (d,)])
    loss = 0.5 * jnp.sum(s_sq) / d
    s_cw = lax.dynamic_slice(s_cw, (0, me * wa8), (3, wa8))[None]
    big_g = [_sum_slots(got_int, 256, "sum_grads_w_in").T, got_pa, got_pb, got_o,
             _sum_slots(got_ff1t, 256, "sum_grads_w_ff1").T, got_ff2]
    big_w = [(w_in, m_w_in, v_w_in), (w_pa, m_w_pa, v_w_pa), (w_pb, m_w_pb, v_w_pb), (w_o, m_w_o, v_w_o),
             (w_ff1, m_w_ff1, v_w_ff1), (w_ff2, m_w_ff2, v_w_ff2)]
    big_out = [_adamw(w, g, m, v, "adamw_%d" % k) for k, ((w, m, v), g) in enumerate(zip(big_w, big_g))]
    small_w = [(b_gate, m_b_gate, v_b_gate), (conv_w, m_conv_w, v_conv_w), (v_norm_g, m_v_norm_g, v_v_norm_g),
               (v_norm_b, m_v_norm_b, v_v_norm_b), (w_s, m_w_s, v_w_s), (b_s, m_b_s, v_b_s), (ln1_g, m_ln1_g, v_ln1_g),
               (ln1_b, m_ln1_b, v_ln1_b), (ln2_g, m_ln2_g, v_ln2_g), (ln2_b, m_ln2_b, v_ln2_b)]
    small_g = [s_bg, s_cw, s_vng, s_vnb, s_ws, s_bs, s_l1g, s_l1b, s_l2g, s_l2b]
    small_out = list(zip(*_adamw_small([w for w, _, _ in small_w], small_g, [m for _, m, _ in small_w],
                                       [v for _, _, v in small_w])))

    order = [("b", 0), ("s", 0), ("s", 1), ("s", 2), ("s", 3), ("s", 4), ("s", 5), ("b", 1), ("b", 2), ("b", 3),
             ("s", 6), ("s", 7), ("b", 4), ("b", 5), ("s", 8), ("s", 9)]
    grads, deltas, new_m, new_v = [], [], [], []
    for kind, k in order:
        if kind == "b":
            g, dl, nm, nv = big_out[k]
        else:
            g, (dl, nm, nv) = small_g[k], small_out[k]
        grads.append(g)
        deltas.append(dl)
        new_m.append(nm)
        new_v.append(nv)
    return (loss, grad_x[None], *grads, *deltas, *new_m, *new_v)
```

```python
import math

import jax
import jax.numpy as jnp
from jax import lax
from jax.experimental import pallas as pl
from jax.experimental.pallas import tpu as pltpu

F32 = jnp.float32
BF16 = jnp.bfloat16
N_DEV = 8
CHUNK = 128
LN_EPS = 1e-5
ALPHA = 2.0 ** 0.25
ADAM_LR, ADAM_B1, ADAM_B2, ADAM_EPS, ADAM_WD, ADAM_STEP = 0.001, 0.9, 0.999, 1e-08, 0.01, 10
F32_SUBLANES = 8
BF16_SUBLANES = 16
LANES = 128
VMEM_LIMIT = 56 * 1024 * 1024
MESH = pl.DeviceIdType.MESH
NT_DIMS = (((1,), (1,)), ((), ()))
TN_DIMS = (((0,), (0,)), ((), ()))
HBM_SPEC = pl.BlockSpec(memory_space=pltpu.HBM)


class _Dims:
    def __init__(self, t, d):
        self.T, self.D = t, d
        self.WA = 3 * d // 2
        self.NP = 3 * self.WA + 4 * d
        self.DFF = 4 * d
        self.H = d // CHUNK
        self.OFF_CA, self.OFF_HA = self.WA, 2 * self.WA
        self.OFF_UB = 3 * self.WA
        self.OFF_VB = self.OFF_UB + d
        self.OFF_GA = self.OFF_VB + d
        self.OFF_GB = self.OFF_GA + d
        self.shard_rows = (self.NP // N_DEV, self.WA // N_DEV, d // N_DEV, d // N_DEV, self.DFF // N_DEV, self.DFF // N_DEV)
        self.conv_rows = BF16_SUBLANES * max(1, -(-(3 * (self.WA // N_DEV) * 2) // (BF16_SUBLANES * d)))


def _params(sem=("arbitrary",), vmem=VMEM_LIMIT):
    return pltpu.CompilerParams(dimension_semantics=sem, vmem_limit_bytes=vmem)


def _mesh_pos():
    return lax.axis_index("x"), lax.axis_index("y"), lax.axis_index("c")


def _resident(shape):
    zeros = (0,) * len(shape)
    return pl.BlockSpec(shape, lambda *_: zeros, pipeline_mode=pl.Buffered(1))


class _TwoLevelGather:
    def __init__(self, shard_refs, out_refs, send_sems, recv_sems, local_sems):
        self.n = len(shard_refs)
        self.shard_refs, self.out_refs = shard_refs, out_refs
        self.send_sems, self.recv_sems, self.local_sems = send_sems, recv_sems, local_sems
        x, y, c = _mesh_pos()
        self.c = c
        self.me, self.sibling = (x, y, c), (x, y, 1 - c)
        self.near = [(1 - x, y), (x, 1 - y)]
        self.far = (1 - x, 1 - y)
        self.relay_from = (jnp.where(c == 0, 1 - x, x), jnp.where(c == 0, y, 1 - y))
        self.relay_to = (jnp.where(c == 0, x, 1 - x), jnp.where(c == 0, 1 - y, y))

    def _slot(self, a, px, py, pc):
        rows = self.shard_refs[a].shape[0]
        return self.out_refs[a].at[pl.ds((4 * px + 2 * py + pc) * rows, rows), :]

    def _copy(self, a, k, block, to, src=None):
        return pltpu.make_async_remote_copy(
            src_ref=self._slot(a, *block) if src is None else src, dst_ref=self._slot(a, *block),
            send_sem=self.send_sems.at[7 * a + k], recv_sem=self.recv_sems.at[7 * a + k], device_id=to, device_id_type=MESH)

    def _mine(self):
        return [pltpu.make_async_copy(self.shard_refs[a], self._slot(a, *self.me), self.local_sems.at[a]) for a in range(self.n)]

    def _first(self):
        out = []
        for a in range(self.n):
            out.append(self._copy(a, 0, self.me, self.sibling, src=self.shard_refs[a]))
            out += [self._copy(a, 1 + j, self.me, (*chip, self.c), src=self.shard_refs[a]) for j, chip in enumerate(self.near)]
        return out

    def _passed(self, a):
        return ([self._copy(a, 3 + j, (*chip, self.c), self.sibling) for j, chip in enumerate(self.near)]
                + [self._copy(a, 6, (*self.relay_from, self.c), (*self.relay_to, self.c))])

    def _far_passed(self, a):
        return self._copy(a, 5, (*self.far, self.c), self.sibling)

    def start(self):
        for cp in self._mine() + self._first():
            cp.start()

    def forward(self):
        for a in range(self.n):
            for j, chip in enumerate(self.near):
                self._copy(a, 1 + j, (*chip, self.c), self.me).wait_recv()
            for cp in self._passed(a):
                cp.start()

    def forward_relayed(self):
        for a in range(self.n):
            self._copy(a, 6, (*self.far, self.c), self.me).wait_recv()
            self._far_passed(a).start()

    def finish(self):
        for a in range(self.n):
            self._copy(a, 0, self.sibling, self.me).wait_recv()
            for j, chip in enumerate(self.near):
                self._copy(a, 3 + j, (*chip, 1 - self.c), self.me).wait_recv()
            self._copy(a, 5, (*self.far, 1 - self.c), self.me).wait_recv()
        for cp in self._first():
            cp.wait_send()
        for a in range(self.n):
            for cp in self._passed(a) + [self._far_passed(a)]:
                cp.wait_send()
        for cp in self._mine():
            cp.wait()

    @staticmethod
    def out_shapes(shards):
        return tuple(jax.ShapeDtypeStruct((N_DEV * s.shape[0], s.shape[1]), s.dtype) for s in shards)

    @staticmethod
    def semaphores(n):
        return [pltpu.SemaphoreType.DMA((7 * n,)), pltpu.SemaphoreType.DMA((7 * n,)), pltpu.SemaphoreType.DMA((n,))]


def _all_gather(shards):
    n = len(shards)

    def body(*refs):
        gather = _TwoLevelGather(refs[:n], refs[n:2 * n], *refs[2 * n:])
        gather.start()
        gather.forward()
        gather.forward_relayed()
        gather.finish()

    return pl.pallas_call(
        body, name="all_gather_w_in", out_shape=_TwoLevelGather.out_shapes(shards),
        in_specs=[HBM_SPEC] * n, out_specs=(HBM_SPEC,) * n, scratch_shapes=_TwoLevelGather.semaphores(n),
    )(*shards)


class _Exchange:
    def __init__(self, src_refs, recv_refs, rows, send_sems, recv_sems, local_sems):
        x, y, c = _mesh_pos()
        me = 4 * x + 2 * y + c
        self.own, self.sends, self.arrivals = [], [], []
        for a, (src, recv) in enumerate(zip(src_refs, recv_refs)):
            def blk(k, src=src, r=rows[a]):
                return src if r is None else src.at[pl.ds(k * r, r), :]

            self.own.append(pltpu.make_async_copy(blk(me), recv.at[me], local_sems.at[a]))
            for rel in range(1, N_DEV):
                px = 1 - x if rel & 4 else x
                py = 1 - y if rel & 2 else y
                pc = 1 - c if rel & 1 else c
                peer = 4 * px + 2 * py + pc
                sem = dict(send_sem=send_sems.at[7 * a + rel - 1], recv_sem=recv_sems.at[7 * a + rel - 1],
                           device_id=(px, py, pc), device_id_type=MESH)
                self.sends.append(pltpu.make_async_remote_copy(src_ref=blk(peer), dst_ref=recv.at[me], **sem))
                self.arrivals.append(pltpu.make_async_remote_copy(src_ref=blk(me), dst_ref=recv.at[peer], **sem))

    def start(self):
        for cp in self.own + self.sends:
            cp.start()

    def finish(self):
        for cp in self.arrivals:
            cp.wait_recv()
        for cp in self.sends:
            cp.wait_send()
        for cp in self.own:
            cp.wait()

    @staticmethod
    def out_shapes(srcs, rows):
        return tuple(jax.ShapeDtypeStruct((N_DEV, s.shape[0] if r is None else r, s.shape[1]), s.dtype) for s, r in zip(srcs, rows))

    @staticmethod
    def semaphores(n):
        return [pltpu.SemaphoreType.DMA((7 * n,)), pltpu.SemaphoreType.DMA((7 * n,)), pltpu.SemaphoreType.DMA((n,))]


def _sum_slots(slots, tile_rows, name):
    _, rows, cols = slots.shape
    tr = rows
    if N_DEV * rows * cols * slots.dtype.itemsize > 8 * 1024 * 1024:
        tr = next(c for c in (256, 192, 128, 64, 32, 16) if c <= tile_rows and rows % c == 0)

    def body(s_ref, o_ref):
        acc = s_ref[0].astype(F32)
        for k in range(1, N_DEV):
            acc = acc + s_ref[k].astype(F32)
        o_ref[...] = acc

    return pl.pallas_call(
        body, name=name, grid=(rows // tr,),
        in_specs=[pl.BlockSpec((N_DEV, tr, cols), lambda i: (0, i, 0))],
        out_specs=pl.BlockSpec((tr, cols), lambda i: (i, 0)),
        out_shape=jax.ShapeDtypeStruct((rows, cols), F32),
        compiler_params=_params(),
    )(slots)


def _gelu_and_grad(x):
    k0 = math.sqrt(2.0 / math.pi)
    k1 = 0.044715
    a = k1 * (x * x)
    half = 1.0 / (1.0 + jnp.exp((-2.0 * k0) * x * (1.0 + a)))
    g = x * half
    return g, half + g * (1.0 - half) * (2.0 * k0 + (6.0 * k0) * a)


def _ln_stats(r):
    mu = jnp.mean(r, axis=-1, keepdims=True)
    rc = r - mu
    var = jnp.mean(rc * rc, axis=-1, keepdims=True)
    rstd = lax.rsqrt(var + LN_EPS)
    return rc * rstd, rstd


def _ln_bwd(dxh, xh, rstd):
    return rstd * (dxh - jnp.mean(dxh, axis=-1, keepdims=True) - xh * jnp.mean(dxh * xh, axis=-1, keepdims=True))


def _colsum(a):
    return jnp.sum(a, axis=0, keepdims=True)


def _mm(a, b):
    return jnp.dot(a, b, preferred_element_type=F32)


RING_SLOTS = 3


def _ring_advance(src_hbm, ring_ref, sems, i, nt):
    _, rows, cols = ring_ref.shape

    def fetch(j):
        slot = j % RING_SLOTS
        return pltpu.make_async_copy(src_hbm.at[pl.ds(j * rows, rows), pl.ds(0, cols)], ring_ref.at[slot], sems.at[slot])

    @pl.when(i == 0)
    def _():
        for j in range(min(RING_SLOTS - 1, nt)):
            fetch(j).start()

    @pl.when(i + RING_SLOTS - 1 < nt)
    def _():
        fetch(i + RING_SLOTS - 1).start()

    fetch(i).wait()
    return ring_ref.at[i % RING_SLOTS]


def _ring_scratch(rows, cols, dtype):
    return [pltpu.VMEM((RING_SLOTS, rows, cols), dtype), pltpu.SemaphoreType.DMA((RING_SLOTS,))]


def _halo_maps(tm, t, unit):
    per, last = tm // unit, t // unit - 1
    return (lambda i: (jnp.maximum(i * per - 1, 0), 0)), (lambda i: (jnp.minimum((i + 1) * per, last), 0))


def _proj_in(x2, w_int, b_gate, shards, dm, tm):
    t, d, npj = dm.T, dm.D, dm.NP
    cw = d // 2
    nt = t // tm
    n = len(shards)

    def body(x_ref, w_ref, bg_ref, *refs):
        p_ref = refs[n]
        gather = _TwoLevelGather(refs[:n], refs[n + 1:2 * n + 1], *refs[2 * n + 1:])
        i = pl.program_id(0)

        @pl.when(i == 0)
        def _():
            gather.start()

        @pl.when(i == nt // 2)
        def _():
            gather.forward()

        @pl.when(i == 3 * nt // 4)
        def _():
            gather.forward_relayed()

        xb = x_ref[...].astype(BF16)
        for blk in range(npj // cw):
            lo = blk * cw
            acc = lax.dot_general(xb, w_ref[lo:lo + cw, :], NT_DIMS, preferred_element_type=F32)
            if lo >= dm.OFF_GA:
                acc = jax.nn.sigmoid(acc + bg_ref[:, lo - dm.OFF_GA:lo - dm.OFF_GA + cw])
            p_ref[:, lo:lo + cw] = acc.astype(BF16)

        @pl.when(i == nt - 1)
        def _():
            gather.finish()

    return pl.pallas_call(
        body, name="proj_in", grid=(nt,),
        in_specs=[pl.BlockSpec((tm, d), lambda i: (i, 0)), _resident((npj, d)), _resident((1, 2 * d))] + [HBM_SPEC] * n,
        out_specs=(pl.BlockSpec((tm, npj), lambda i: (i, 0)),) + (HBM_SPEC,) * n,
        out_shape=(jax.ShapeDtypeStruct((t, npj), BF16),) + _TwoLevelGather.out_shapes(shards),
        scratch_shapes=_TwoLevelGather.semaphores(n),
        compiler_params=_params(),
    )(x2, w_int, b_gate, *shards)


def _row_neighbours(ext, tm):
    h, n = F32_SUBLANES, ext.shape[0]
    return pltpu.roll(ext, 1, 0)[h:h + tm], pltpu.roll(ext, n - 1, 0)[h:h + tm]


def _spatial_mix(vn, ws_ref, bias_ref, mixed_ref, dm, tm):
    vb = vn.astype(BF16)
    for cc in range(tm // CHUNK):
        r0 = cc * CHUNK
        for h in range(dm.H):
            c0 = h * CHUNK
            m = _mm(ws_ref[h], vb[r0:r0 + CHUNK, c0:c0 + CHUNK])
            mixed_ref[r0:r0 + CHUNK, c0:c0 + CHUNK] = m + bias_ref[:, c0:c0 + CHUNK]
    return mixed_ref[...]


def _mixer_fwd(p, x2, w_pa, w_pb, w_o, conv_w8, vng, vnb, ws_b, bias_s, dm, tm):
    t, d, wa, npj = dm.T, dm.D, dm.WA, dm.NP
    nt = t // tm
    hb = BF16_SUBLANES
    prev_map, next_map = _halo_maps(tm, t, hb)

    def body(p_hbm, pp_ref, pn_ref, x_ref, wpa_ref, wpb_ref, wo_ref, cw_ref, vng_ref, vnb_ref, ws_ref, bias_ref,
             ya_ref, yb_ref, r1_ref, gu_ref, dgu_ref, xhv_ref, rgv_ref, cv_ref, s_ref, mixed_ref, ring_ref, ring_sems):
        i = pl.program_id(0)
        p_ref = _ring_advance(p_hbm, ring_ref, ring_sems, i, nt)

        def col(ref, lo, width):
            return ref[:, lo:lo + width].astype(F32)

        ch = col(p_ref, dm.OFF_CA, wa) * col(p_ref, dm.OFF_HA, wa)
        chp = (col(pp_ref, dm.OFF_CA, wa) * col(pp_ref, dm.OFF_HA, wa))[hb - F32_SUBLANES:hb]
        chn = (col(pn_ref, dm.OFF_CA, wa) * col(pn_ref, dm.OFF_HA, wa))[0:F32_SUBLANES]
        ch_e = jnp.concatenate([jnp.where(i == 0, 0.0, chp), ch, jnp.where(i == nt - 1, 0.0, chn)], axis=0)
        up, dn = _row_neighbours(ch_e, tm)
        cv = cw_ref[0:1, :] * up + cw_ref[1:2, :] * ch + cw_ref[2:3, :] * dn
        cv_ref[...] = cv.astype(BF16)
        ya = _mm((col(p_ref, 0, wa) * cv).astype(BF16), wpa_ref[...])
        gv, dgelu_v = _gelu_and_grad(col(p_ref, dm.OFF_VB, d))
        xhv, rstdv = _ln_stats(gv)
        xhv_ref[...] = xhv.astype(BF16)
        rgv_ref[...] = (rstdv * dgelu_v).astype(BF16)
        mixed = _spatial_mix(xhv * vng_ref[...] + vnb_ref[...], ws_ref, bias_ref, mixed_ref, dm, tm)
        gu, dgelu_u = _gelu_and_grad(col(p_ref, dm.OFF_UB, d))
        gu_ref[...] = gu.astype(BF16)
        dgu_ref[...] = dgelu_u.astype(BF16)
        yb = _mm((gu * mixed).astype(BF16), wpb_ref[...])
        sb = (col(p_ref, dm.OFF_GA, d) * ya + col(p_ref, dm.OFF_GB, d) * yb).astype(BF16)
        s_ref[...] = sb
        mix = _mm(sb, wo_ref[...])
        ya_ref[...] = ya.astype(BF16)
        yb_ref[...] = yb.astype(BF16)
        r1_ref[...] = ALPHA * x_ref[...] + mix

    tile = lambda i: (i, 0)
    return pl.pallas_call(
        body, name="mixer_fwd", grid=(nt,),
        in_specs=[HBM_SPEC, pl.BlockSpec((hb, npj), prev_map), pl.BlockSpec((hb, npj), next_map),
                  pl.BlockSpec((tm, d), tile), _resident((wa, d)), _resident((d, d)), _resident((d, d)),
                  _resident((F32_SUBLANES, wa)), _resident((1, d)), _resident((1, d)),
                  _resident((dm.H, CHUNK, CHUNK)), _resident((CHUNK, d))],
        out_specs=(pl.BlockSpec((tm, d), tile),) * 7 + (pl.BlockSpec((tm, wa), tile), pl.BlockSpec((tm, d), tile)),
        out_shape=(jax.ShapeDtypeStruct((t, d), BF16), jax.ShapeDtypeStruct((t, d), BF16), jax.ShapeDtypeStruct((t, d), F32))
        + (jax.ShapeDtypeStruct((t, d), BF16),) * 4 + (jax.ShapeDtypeStruct((t, wa), BF16), jax.ShapeDtypeStruct((t, d), BF16)),
        scratch_shapes=[pltpu.VMEM((tm, d), F32)] + _ring_scratch(tm, npj, BF16),
        compiler_params=_params(),
    )(p, p, p, x2, w_pa, w_pb, w_o, conv_w8, vng, vnb, ws_b, bias_s)


def _ffn_fwd(r1, tgt, w1t, w2, ln1g, ln1b, ln2g, ln2b, dm, tm):
    t, d, dff = dm.T, dm.D, dm.DFF
    fc = dff // N_DEV

    def body(r1_ref, tgt_ref, w1t_ref, w2_ref, g1_ref, b1_ref, g2_ref, b2_ref, relu_ref, x1_ref, dr2_ref, dr2b_ref, sums_ref):
        @pl.when(pl.program_id(0) == 0)
        def _():
            sums_ref[...] = jnp.zeros_like(sums_ref)

        xh1, _ = _ln_stats(r1_ref[...])
        x1 = xh1 * g1_ref[...] + b1_ref[...]
        x1b = x1.astype(BF16)
        x1_ref[...] = x1b
        ffn = jnp.zeros((tm, d), F32)
        for k in range(dff // fc):
            ks = slice(k * fc, (k + 1) * fc)
            r = jnp.maximum(lax.dot_general(x1b, w1t_ref[ks, :], NT_DIMS, preferred_element_type=F32), 0.0)
            relu_ref[:, ks] = r.astype(BF16)
            ffn = ffn + _mm((r * r).astype(BF16), w2_ref[ks, :])
        xh2, rstd2 = _ln_stats(ALPHA * x1 + ffn)
        diff = xh2 * g2_ref[...] + b2_ref[...] - tgt_ref[...]
        dy = diff * (1.0 / d)
        dr2 = _ln_bwd(dy * g2_ref[...], xh2, rstd2)
        dr2_ref[...] = dr2
        dr2b_ref[...] = dr2.astype(BF16)
        sums_ref[0:1, :] += _colsum(diff * diff)
        sums_ref[1:2, :] += _colsum(dy * xh2)
        sums_ref[2:3, :] += _colsum(dy)

    tile = lambda i: (i, 0)
    vec = _resident((1, d))
    return pl.pallas_call(
        body, name="ffn_fwd", grid=(t // tm,),
        in_specs=[pl.BlockSpec((tm, d), tile), pl.BlockSpec((tm, d), tile), _resident((dff, d)), _resident((dff, d)),
                  vec, vec, vec, vec],
        out_specs=(pl.BlockSpec((tm, dff), tile), pl.BlockSpec((tm, d), tile), pl.BlockSpec((tm, d), tile),
                   pl.BlockSpec((tm, d), tile), pl.BlockSpec((F32_SUBLANES, d), lambda i: (0, 0))),
        out_shape=(jax.ShapeDtypeStruct((t, dff), BF16), jax.ShapeDtypeStruct((t, d), BF16), jax.ShapeDtypeStruct((t, d), F32),
                   jax.ShapeDtypeStruct((t, d), BF16), jax.ShapeDtypeStruct((F32_SUBLANES, d), F32)),
        compiler_params=_params(),
    )(r1, tgt, w1t, w2, ln1g, ln1b, ln2g, ln2b)


def _ffn_bwd(r1, relu, dr2, w2, w1t, ln1g, dm, tm):
    t, d, dff = dm.T, dm.D, dm.DFF
    fc = dff // N_DEV

    def body(r1_ref, relu_ref, dr2_ref, w2_ref, w1t_ref, g1_ref, dr1_ref, dh_ref, sums_ref):
        @pl.when(pl.program_id(0) == 0)
        def _():
            sums_ref[...] = jnp.zeros_like(sums_ref)

        xh1, rstd1 = _ln_stats(r1_ref[...])
        dr2 = dr2_ref[...]
        dr2b = dr2.astype(BF16)
        dx1 = ALPHA * dr2
        for k in range(dff // fc):
            ks = slice(k * fc, (k + 1) * fc)
            dact = lax.dot_general(dr2b, w2_ref[ks, :], NT_DIMS, preferred_element_type=F32)
            dhb = (dact * (2.0 * relu_ref[:, ks].astype(F32))).astype(BF16)
            dh_ref[:, ks] = dhb
            dx1 = dx1 + _mm(dhb, w1t_ref[ks, :])
        dr1_ref[...] = _ln_bwd(dx1 * g1_ref[...], xh1, rstd1)
        sums_ref[0:1, :] += _colsum(dx1 * xh1)
        sums_ref[1:2, :] += _colsum(dx1)

    tile = lambda i: (i, 0)
    return pl.pallas_call(
        body, name="ffn_bwd", grid=(t // tm,),
        in_specs=[pl.BlockSpec((tm, d), tile), pl.BlockSpec((tm, dff), tile), pl.BlockSpec((tm, d), tile),
                  _resident((dff, d)), _resident((dff, d)), _resident((1, d))],
        out_specs=(pl.BlockSpec((tm, d), tile), pl.BlockSpec((tm, dff), tile), pl.BlockSpec((F32_SUBLANES, d), lambda i: (0, 0))),
        out_shape=(jax.ShapeDtypeStruct((t, d), F32), jax.ShapeDtypeStruct((t, dff), BF16),
                   jax.ShapeDtypeStruct((F32_SUBLANES, d), F32)),
        compiler_params=_params(),
    )(r1, relu, dr2, w2, w1t, ln1g)


def _wgrad(pairs, tt, fb, name, xchg=()):
    n, m = len(pairs), len(xchg)
    t, f = pairs[0][0].shape
    d = pairs[0][1].shape[1]
    squares = [sq for _, _, sq in pairs]
    nj, ni = f // fb, t // tt
    xrows = [r for _, r in xchg]

    def body(*refs):
        ins, xsrc = refs[:2 * n], refs[2 * n:2 * n + m]
        outs, xrecv = refs[2 * n + m:3 * n + m], refs[3 * n + m:3 * n + 2 * m]
        accs, sems = refs[3 * n + 2 * m:4 * n + 2 * m], refs[4 * n + 2 * m:]
        j, i = pl.program_id(0), pl.program_id(1)
        exchange = _Exchange(xsrc, xrecv, xrows, *sems) if m else None

        if m:
            @pl.when(jnp.logical_and(j == 0, i == 0))
            def _():
                exchange.start()

        @pl.when(i == 0)
        def _():
            for acc in accs:
                acc[...] = jnp.zeros_like(acc)

        for q in range(n):
            lhs = ins[2 * q][...]
            if squares[q]:
                lf = lhs.astype(F32)
                lhs = (lf * lf).astype(BF16)
            accs[q][...] += lax.dot_general(lhs, ins[2 * q + 1][...].astype(BF16), TN_DIMS, preferred_element_type=F32)

        @pl.when(i == ni - 1)
        def _():
            for q in range(n):
                outs[q][...] = accs[q][...].astype(BF16)

        if m:
            @pl.when(jnp.logical_and(j == nj - 1, i == ni - 1))
            def _():
                exchange.finish()

    lhs_spec = pl.BlockSpec((tt, fb), lambda j, i: (i, j))
    rhs_spec = pl.BlockSpec((tt, d), lambda j, i: (i, 0))
    out_spec = pl.BlockSpec((fb, d), lambda j, i: (j, 0))
    xsrcs = [a_ for a_, _ in xchg]
    return pl.pallas_call(
        body, name=name, grid=(nj, ni),
        in_specs=[lhs_spec, rhs_spec] * n + [HBM_SPEC] * m, out_specs=(out_spec,) * n + (HBM_SPEC,) * m,
        out_shape=(jax.ShapeDtypeStruct((f, d), BF16),) * n + _Exchange.out_shapes(xsrcs, xrows),
        scratch_shapes=[pltpu.VMEM((fb, d), F32)] * n + (_Exchange.semaphores(m) if m else []),
        compiler_params=_params(("arbitrary", "arbitrary")),
    )(*[a_ for lhs, rhs, _ in pairs for a_ in (lhs, rhs)], *xsrcs)


def _mixer_bwd(dr1, p, ya, yb, gu_s, dgu_s, xhv_s, rgv_s, cv_s, w_ot, w_pat, w_pbt, conv_w8, vng, vnb, ws_b, wst_b, bias_s, head_sel, xchg, dm, tm):
    t, d, wa, npj = dm.T, dm.D, dm.WA, dm.NP
    nt = t // tm
    h8, hb = F32_SUBLANES, BF16_SUBLANES
    ext = tm + 2 * h8
    prev_f, next_f = _halo_maps(tm, t, h8)
    prev_b, next_b = _halo_maps(tm, t, hb)
    nx = len(xchg)
    xsrcs, xrows = [a_ for a_, _ in xchg], [r for _, r in xchg]

    gw = d // 2

    def gate_map(rows, k):
        return lambda i: (rows(i)[0], dm.OFF_GA // gw + k)

    def body(dr_ref, drp_ref, drn_ref, p_hbm, pg0_ref, pg1_ref, pg2_ref, pg3_ref, ppc_ref, ppg0_ref, ppg1_ref,
             pnc_ref, png0_ref, png1_ref, ya_ref, yb_ref, gu_ref, dgu_ref, xhv_ref, rgv_ref, cv_ref,
             wot_ref, wpat_ref, wpbt_ref,
             cw_ref, vng_ref, vnb_ref, ws_ref, wst_ref, bias_ref, sel_ref,
             *refs):
        xsrc = refs[:nx]
        dp_ref, a_ref, dya_ref, bb_ref, dyb_ref, dws_ref, dbs_ref, dcw_ref, dbg_ref, dvn_ref = refs[nx:nx + 10]
        xrecv = refs[nx + 10:2 * nx + 10]
        mixed_ref, dvnm_ref, ring_ref, ring_sems = refs[2 * nx + 10:2 * nx + 14]
        exchange = _Exchange(xsrc, xrecv, xrows, *refs[2 * nx + 14:])
        i = pl.program_id(0)

        @pl.when(i == 0)
        def _():
            exchange.start()
            for ref in (dws_ref, dbs_ref, dcw_ref, dbg_ref, dvn_ref):
                ref[...] = jnp.zeros_like(ref)

        pc_ref = _ring_advance(p_hbm, ring_ref, ring_sems, i, nt)

        def col(ref, lo, width):
            return ref[:, lo:lo + width].astype(F32)

        def ext_rows(prev_blk, center, next_blk):
            return jnp.concatenate([prev_blk, center, next_blk], axis=0)

        def lanes(*refs):
            return jnp.concatenate([r[...].astype(F32) for r in refs], axis=1)

        xhv = xhv_ref[...].astype(F32)
        vn = xhv * vng_ref[...] + vnb_ref[...]
        mixed = _spatial_mix(vn, ws_ref, bias_ref, mixed_ref, dm, tm)
        gu = gu_ref[...].astype(F32)
        bb_ref[...] = (gu * mixed).astype(BF16)
        g_a, g_b = lanes(pg0_ref, pg1_ref), lanes(pg2_ref, pg3_ref)
        y_a, y_b = ya_ref[...].astype(F32), yb_ref[...].astype(F32)
        qa = y_a * g_a * (1.0 - g_a)
        qb = y_b * g_b * (1.0 - g_b)
        cv = cv_ref[...].astype(F32)
        b_a, c_a, h_a = col(pc_ref, 0, wa), col(pc_ref, dm.OFF_CA, wa), col(pc_ref, dm.OFF_HA, wa)
        a_ref[...] = (b_a * cv).astype(BF16)
        ch = c_a * h_a
        row = lax.broadcasted_iota(jnp.int32, (ext, 1), 0) + (i * tm - h8)
        inside = jnp.logical_and(row >= 0, row < t)
        dr_e = ext_rows(drp_ref[...], dr_ref[...], drn_ref[...])
        ds_e = _mm(dr_e.astype(BF16), wot_ref[...])
        dya_e = ds_e * ext_rows(lanes(ppg0_ref, ppg1_ref)[hb - h8:hb], g_a, lanes(png0_ref, png1_ref)[0:h8])
        da_e = _mm(dya_e.astype(BF16), wpat_ref[...])
        dcv_e = jnp.where(inside, da_e * ext_rows(col(ppc_ref, 0, wa)[hb - h8:hb], b_a, col(pnc_ref, 0, wa)[0:h8]), 0.0)
        dcv, (dcv_up, dcv_dn) = dcv_e[h8:h8 + tm], _row_neighbours(dcv_e, tm)
        w0, w1, w2 = cw_ref[0:1, :], cw_ref[1:2, :], cw_ref[2:3, :]
        dp_ref[:, 0:wa] = (da_e[h8:h8 + tm] * cv).astype(BF16)
        dch = w0 * dcv_dn + w1 * dcv + w2 * dcv_up
        dp_ref[:, dm.OFF_CA:dm.OFF_CA + wa] = (dch * h_a).astype(BF16)
        dp_ref[:, dm.OFF_HA:dm.OFF_HA + wa] = (dch * c_a).astype(BF16)
        dcw_ref[0:1, :] += _colsum(dcv_dn * ch)
        dcw_ref[1:2, :] += _colsum(dcv * ch)
        dcw_ref[2:3, :] += _colsum(dcv_up * ch)
        dya_ref[...] = dya_e[h8:h8 + tm].astype(BF16)
        ds = ds_e[h8:h8 + tm]
        dzga = ds * qa
        dzgb = ds * qb
        dp_ref[:, dm.OFF_GA:dm.OFF_GA + d] = dzga.astype(BF16)
        dp_ref[:, dm.OFF_GB:dm.OFF_GB + d] = dzgb.astype(BF16)
        dbg_ref[0:1, 0:d] += _colsum(dzga)
        dbg_ref[0:1, d:2 * d] += _colsum(dzgb)
        dyb = (ds * g_b).astype(BF16)
        dyb_ref[...] = dyb
        dbb = _mm(dyb, wpbt_ref[...])
        dp_ref[:, dm.OFF_UB:dm.OFF_UB + d] = (dbb * mixed * dgu_ref[...].astype(F32)).astype(BF16)
        dmb = (dbb * gu).astype(BF16)
        vb = vn.astype(BF16)
        dbs = jnp.zeros((CHUNK, CHUNK), F32)
        for cc in range(tm // CHUNK):
            r0 = cc * CHUNK
            dbs = dbs + _mm(dmb[r0:r0 + CHUNK, :], sel_ref[...])
            for h in range(dm.H):
                c0 = h * CHUNK
                blk = dmb[r0:r0 + CHUNK, c0:c0 + CHUNK]
                dvnm_ref[r0:r0 + CHUNK, c0:c0 + CHUNK] = _mm(wst_ref[h], blk)
                dws_ref[h] += lax.dot_general(blk, vb[r0:r0 + CHUNK, c0:c0 + CHUNK], NT_DIMS, preferred_element_type=F32)
        dbs_ref[...] += dbs
        dvn = dvnm_ref[...]
        dvn_ref[0:1, :] += _colsum(dvn * xhv)
        dvn_ref[1:2, :] += _colsum(dvn)
        dp_ref[:, dm.OFF_VB:dm.OFF_VB + d] = (_ln_bwd(dvn * vng_ref[...], xhv, rgv_ref[...].astype(F32))).astype(BF16)

        @pl.when(i == nt - 1)
        def _():
            exchange.finish()

    full = lambda i: (0, 0)
    tile = lambda i: (i, 0)
    hcc = _resident((dm.H, CHUNK, CHUNK))
    tok = lambda w, dt: jax.ShapeDtypeStruct((t, w), dt)
    return pl.pallas_call(
        body, name="mixer_bwd", grid=(nt,),
        in_specs=[pl.BlockSpec((tm, d), tile), pl.BlockSpec((h8, d), prev_f), pl.BlockSpec((h8, d), next_f),
                  HBM_SPEC] + [pl.BlockSpec((tm, gw), gate_map(tile, k)) for k in range(4)]
        + [pl.BlockSpec((hb, wa), prev_b)] + [pl.BlockSpec((hb, gw), gate_map(prev_b, k)) for k in range(2)]
        + [pl.BlockSpec((hb, wa), next_b)] + [pl.BlockSpec((hb, gw), gate_map(next_b, k)) for k in range(2)]
        + [pl.BlockSpec((tm, d), tile)] * 6
        + [pl.BlockSpec((tm, wa), tile), _resident((d, d)), _resident((d, wa)), _resident((d, d)),
           _resident((h8, wa)), _resident((1, d)), _resident((1, d)), hcc, hcc, _resident((CHUNK, d)),
           _resident((d, CHUNK))] + [HBM_SPEC] * nx,
        out_specs=(pl.BlockSpec((tm, npj), tile), pl.BlockSpec((tm, wa), tile), pl.BlockSpec((tm, d), tile),
                   pl.BlockSpec((tm, d), tile), pl.BlockSpec((tm, d), tile),
                   pl.BlockSpec((dm.H, CHUNK, CHUNK), lambda i: (0, 0, 0)), pl.BlockSpec((CHUNK, CHUNK), full),
                   pl.BlockSpec((h8, wa), full), pl.BlockSpec((h8, 2 * d), full), pl.BlockSpec((h8, d), full))
        + (HBM_SPEC,) * nx,
        out_shape=(tok(npj, BF16), tok(wa, BF16), tok(d, BF16), tok(d, BF16), tok(d, BF16),
                   jax.ShapeDtypeStruct((dm.H, CHUNK, CHUNK), F32), jax.ShapeDtypeStruct((CHUNK, CHUNK), F32),
                   jax.ShapeDtypeStruct((h8, wa), F32), jax.ShapeDtypeStruct((h8, 2 * d), F32),
                   jax.ShapeDtypeStruct((h8, d), F32)) + _Exchange.out_shapes(xsrcs, xrows),
        scratch_shapes=[pltpu.VMEM((tm, d), F32), pltpu.VMEM((tm, d), F32)] + _ring_scratch(tm, 3 * wa, BF16) + _Exchange.semaphores(nx),
        compiler_params=_params(),
    )(dr1, dr1, dr1, *([p] * 11), ya, yb, gu_s, dgu_s, xhv_s, rgv_s, cv_s, w_ot, w_pat, w_pbt, conv_w8, vng, vnb, ws_b, wst_b, bias_s, head_sel, *xsrcs)


def _input_grad(dp, dr1, w_int, xchg, dm, tm):
    t, d, npj = dm.T, dm.D, dm.NP
    nt = t // tm
    nx = len(xchg)
    xsrcs, xrows = [a_ for a_, _ in xchg], [r for _, r in xchg]

    def body(dp_ref, dr_ref, w_ref, *refs):
        dx_ref = refs[nx]
        exchange = _Exchange(refs[:nx], refs[nx + 1:2 * nx + 1], xrows, *refs[2 * nx + 1:])
        i = pl.program_id(0)

        @pl.when(i == 0)
        def _():
            exchange.start()

        dx_ref[...] = ALPHA * dr_ref[...] + _mm(dp_ref[...], w_ref[...])

        @pl.when(i == nt - 1)
        def _():
            exchange.finish()

    return pl.pallas_call(
        body, name="input_grad", grid=(nt,),
        in_specs=[pl.BlockSpec((tm, npj), lambda i: (i, 0)), pl.BlockSpec((tm, d), lambda i: (i, 0)), _resident((npj, d))]
        + [HBM_SPEC] * nx,
        out_specs=(pl.BlockSpec((tm, d), lambda i: (i, 0)),) + (HBM_SPEC,) * nx,
        out_shape=(jax.ShapeDtypeStruct((t, d), F32),) + _Exchange.out_shapes(xsrcs, xrows),
        scratch_shapes=_Exchange.semaphores(nx),
        compiler_params=_params(),
    )(dp, dr1, w_int, *xsrcs)


def _adamw_math(w, g, m, v):
    nm = ADAM_B1 * m + (1.0 - ADAM_B1) * g
    nv = ADAM_B2 * v + (1.0 - ADAM_B2) * (g * g)
    delta = -ADAM_LR * ((nm / (1.0 - ADAM_B1 ** ADAM_STEP)) / (jnp.sqrt(nv / (1.0 - ADAM_B2 ** ADAM_STEP)) + ADAM_EPS) + ADAM_WD * w)
    return delta, nm, nv


def _adamw(w, g, m, v, name):
    _, rows, cols = w.shape
    tr = 256 if rows % 256 == 0 else rows
    from_slots = g.ndim == 3

    def body(w_ref, g_ref, m_ref, v_ref, go_ref, d_ref, nm_ref, nv_ref):
        if from_slots:
            g_ = g_ref[0].astype(F32)
            for k in range(1, N_DEV):
                g_ = g_ + g_ref[k].astype(F32)
        else:
            g_ = g_ref[...]
        go_ref[0] = g_
        d_ref[0], nm_ref[0], nv_ref[0] = _adamw_math(w_ref[0], g_, m_ref[0], v_ref[0])

    spec = pl.BlockSpec((1, tr, cols), lambda i: (0, i, 0))
    g_spec = pl.BlockSpec((N_DEV, tr, cols), lambda i: (0, i, 0)) if from_slots else pl.BlockSpec((tr, cols), lambda i: (i, 0))
    shp = jax.ShapeDtypeStruct((1, rows, cols), F32)
    return pl.pallas_call(
        body, name=name, grid=(rows // tr,), in_specs=[spec, g_spec, spec, spec], out_specs=(spec,) * 4, out_shape=(shp,) * 4,
        compiler_params=_params(),
    )(w, g, m, v)


def _adamw_small(ws, gs, ms, vs):
    n = len(ws)

    def body(*refs):
        ins, outs = refs[:4 * n], refs[4 * n:]
        for k in range(n):
            w_ref, g_ref, m_ref, v_ref = ins[k], ins[n + k], ins[2 * n + k], ins[3 * n + k]
            outs[k][...], outs[n + k][...], outs[2 * n + k][...] = _adamw_math(w_ref[...], g_ref[...], m_ref[...], v_ref[...])

    shapes = tuple(jax.ShapeDtypeStruct(w.shape, F32) for w in ws)
    out = pl.pallas_call(body, name="adamw_small", out_shape=shapes * 3, compiler_params=_params(()))(*ws, *gs, *ms, *vs)
    return out[:n], out[n:2 * n], out[2 * n:]


def _to_slab(parts):
    flat = jnp.concatenate([q.reshape(-1) for q in parts])
    pad = (-flat.shape[0]) % (F32_SUBLANES * LANES)
    return jnp.pad(flat, (0, pad)).reshape(-1, LANES)


def _from_slab(slab, shapes):
    flat = slab.reshape(-1)
    out, off = [], 0
    for s in shapes:
        n = math.prod(s)
        out.append(flat[off:off + n].reshape(s))
        off += n
    return out


def kernel(x, w_in, b_gate, conv_w, v_norm_g, v_norm_b, w_s, b_s, w_pa, w_pb, w_o, ln1_g, ln1_b, w_ff1, w_ff2, ln2_g, ln2_b, loss_target, m_w_in, m_b_gate, m_conv_w, m_v_norm_g, m_v_norm_b, m_w_s, m_b_s, m_w_pa, m_w_pb, m_w_o, m_ln1_g, m_ln1_b, m_w_ff1, m_w_ff2, m_ln2_g, m_ln2_b, v_w_in, v_b_gate, v_conv_w, v_v_norm_g, v_v_norm_b, v_w_s, v_b_s, v_w_pa, v_w_pb, v_w_o, v_ln1_g, v_ln1_b, v_w_ff1, v_w_ff2, v_ln2_g, v_ln2_b):
    t, d = x.shape[1], x.shape[2]
    dm = _Dims(t, d)
    tm = 256 if t % 256 == 0 else CHUNK
    tm_big = 512 if t % 512 == 0 else tm
    tt = 1024 if t % 1024 == 0 else tm
    me = 4 * lax.axis_index("x") + 2 * lax.axis_index("y") + lax.axis_index("c")
    x2, tgt = x[0], loss_target[0]

    conv_bits = lax.bitcast_convert_type(conv_w[0], BF16).reshape(-1)
    conv_blk = jnp.pad(conv_bits, (0, dm.conv_rows * d - conv_bits.shape[0])).reshape(dm.conv_rows, d)
    w_int, conv_g = _all_gather([w_in[0].T.astype(BF16), conv_blk])
    wa8 = dm.WA // N_DEV
    conv_all = lax.bitcast_convert_type(conv_g.reshape(N_DEV, -1)[:, :3 * wa8 * 2].reshape(N_DEV, 3, wa8, 2), F32)
    conv_full = jnp.transpose(conv_all, (1, 0, 2)).reshape(3, dm.WA)
    conv_w8 = jnp.pad(conv_full, ((0, F32_SUBLANES - 3), (0, 0)))
    ws_b = w_s[0].astype(BF16)
    wst_b = jnp.transpose(w_s[0], (0, 2, 1)).astype(BF16)
    bias_s = jnp.repeat(b_s[0].T, CHUNK, axis=1)
    head_sel = (jnp.arange(d)[:, None] // CHUNK == jnp.arange(CHUNK)[None, :]).astype(BF16)

    p, w_pa_f, w_pb_f, w_o_f, w_1t, w_2 = _proj_in(
        x2, w_int, b_gate, [w_pa[0].astype(BF16), w_pb[0].astype(BF16), w_o[0].astype(BF16), w_ff1[0].T.astype(BF16),
                            w_ff2[0].astype(BF16)], dm, tm_big)
    ya, yb, r1, gu_s, dgu_s, xhv_s, rgv_s, cv_s, s_m = _mixer_fwd(
        p, x2, w_pa_f, w_pb_f, w_o_f, conv_w8, v_norm_g, v_norm_b, ws_b, bias_s, dm, tm)
    relu, x1b, dr2, dr2b, sums2 = _ffn_fwd(r1, tgt, w_1t, w_2, ln1_g, ln1_b, ln2_g, ln2_b, dm, tm_big)
    dr1, dh1, sums1 = _ffn_bwd(r1, relu, dr2, w_2, w_1t, ln1_g, dm, tm_big)
    fb = min(1024, dm.DFF)
    rows = dm.shard_rows
    g_ff1t, g_ff2 = _wgrad([(dh1, x1b, False), (relu, dr2b, True)], tt, fb, "ffn_wgrad")
    dp, a_m, dya, bb_m, dyb, g_ws, g_bs_t, g_cw, g_bg, g_vn, got_ff1t, got_ff2 = _mixer_bwd(
        dr1, p, ya, yb, gu_s, dgu_s, xhv_s, rgv_s, cv_s, w_o_f.T, w_pa_f.T, w_pb_f.T, conv_w8, v_norm_g, v_norm_b, ws_b, wst_b, bias_s, head_sel,
        [(g_ff1t, rows[4]), (g_ff2, rows[5])], dm, tm)
    (g_pa,) = _wgrad([(a_m, dya, False)], tt, dm.WA, "w_pa_grad")
    g_o, g_pb = _wgrad([(s_m, dr1, False), (bb_m, dyb, False)], tt, d, "w_o_pb_grad")
    g_int, got_pa, got_pb, got_o = _wgrad([(dp, x2, False)], tt, 17 * LANES, "w_in_grad",
                                          xchg=[(g_pa, rows[1]), (g_pb, rows[2]), (g_o, rows[3])])
    small_parts = [g_bg[0], g_cw[0:3], g_vn[0], g_vn[1], g_ws, g_bs_t[:, :dm.H].T,
                   sums1[0], sums1[1], sums2[1], sums2[2], sums2[0]]
    grad_x, got_int, got_s = _input_grad(dp, dr1, w_int, [(g_int, rows[0]), (_to_slab(small_parts), None)], dm, tm_big)

    ssum = _sum_slots(got_s, 256, "sum_small")
    (s_bg, s_cw, s_vng, s_vnb, s_ws, s_bs, s_l1g, s_l1b, s_l2g, s_l2b, s_sq) = _from_slab(
        ssum, [(1, 2 * d), (3, dm.WA), (1, d), (1, d), (1, dm.H, CHUNK, CHUNK), (1, dm.H, CHUNK),
               (1, d), (1, d), (1, d), (1, d), (d,)])
    loss = 0.5 * jnp.sum(s_sq) / d
    s_cw = lax.dynamic_slice(s_cw, (0, me * wa8), (3, wa8))[None]
    big_g = [_sum_slots(got_int, 256, "sum_grads_w_in").T, got_pa, got_pb, got_o,
             _sum_slots(got_ff1t, 256, "sum_grads_w_ff1").T, got_ff2]
    big_w = [(w_in, m_w_in, v_w_in), (w_pa, m_w_pa, v_w_pa), (w_pb, m_w_pb, v_w_pb), (w_o, m_w_o, v_w_o),
             (w_ff1, m_w_ff1, v_w_ff1), (w_ff2, m_w_ff2, v_w_ff2)]
    big_out = [_adamw(w, g, m, v, "adamw_%d" % k) for k, ((w, m, v), g) in enumerate(zip(big_w, big_g))]
    small_w = [(b_gate, m_b_gate, v_b_gate), (conv_w, m_conv_w, v_conv_w), (v_norm_g, m_v_norm_g, v_v_norm_g),
               (v_norm_b, m_v_norm_b, v_v_norm_b), (w_s, m_w_s, v_w_s), (b_s, m_b_s, v_b_s), (ln1_g, m_ln1_g, v_ln1_g),
               (ln1_b, m_ln1_b, v_ln1_b), (ln2_g, m_ln2_g, v_ln2_g), (ln2_b, m_ln2_b, v_ln2_b)]
    small_g = [s_bg, s_cw, s_vng, s_vnb, s_ws, s_bs, s_l1g, s_l1b, s_l2g, s_l2b]
    small_out = list(zip(*_adamw_small([w for w, _, _ in small_w], small_g, [m for _, m, _ in small_w],
                                       [v for _, _, v in small_w])))

    order = [("b", 0), ("s", 0), ("s", 1), ("s", 2), ("s", 3), ("s", 4), ("s", 5), ("b", 1), ("b", 2), ("b", 3),
             ("s", 6), ("s", 7), ("b", 4), ("b", 5), ("s", 8), ("s", 9)]
    grads, deltas, new_m, new_v = [], [], [], []
    for kind, k in order:
        if kind == "b":
            g, dl, nm, nv = big_out[k]
        else:
            g, (dl, nm, nv) = small_g[k], small_out[k]
        grads.append(g)
        deltas.append(dl)
        new_m.append(nm)
        new_v.append(nv)
    return (loss, grad_x[None], *grads, *deltas, *new_m, *new_v)
```

```python
import math

import jax
import jax.numpy as jnp
from jax import lax
from jax.experimental import pallas as pl
from jax.experimental.pallas import tpu as pltpu

F32 = jnp.float32
BF16 = jnp.bfloat16
N_DEV = 8
CHUNK = 128
LN_EPS = 1e-5
ALPHA = 2.0 ** 0.25
ADAM_LR, ADAM_B1, ADAM_B2, ADAM_EPS, ADAM_WD, ADAM_STEP = 0.001, 0.9, 0.999, 1e-08, 0.01, 10
F32_SUBLANES = 8
BF16_SUBLANES = 16
LANES = 128
VMEM_LIMIT = 56 * 1024 * 1024
MESH = pl.DeviceIdType.MESH
NT_DIMS = (((1,), (1,)), ((), ()))
TN_DIMS = (((0,), (0,)), ((), ()))
HBM_SPEC = pl.BlockSpec(memory_space=pltpu.HBM)


class _Dims:
    def __init__(self, t, d):
        self.T, self.D = t, d
        self.WA = 3 * d // 2
        self.NP = 3 * self.WA + 4 * d
        self.DFF = 4 * d
        self.H = d // CHUNK
        self.OFF_CA, self.OFF_HA = self.WA, 2 * self.WA
        self.OFF_UB = 3 * self.WA
        self.OFF_VB = self.OFF_UB + d
        self.OFF_GA = self.OFF_VB + d
        self.OFF_GB = self.OFF_GA + d
        self.shard_rows = (self.NP // N_DEV, self.WA // N_DEV, d // N_DEV, d // N_DEV, self.DFF // N_DEV, self.DFF // N_DEV)
        self.conv_rows = BF16_SUBLANES * max(1, -(-(3 * (self.WA // N_DEV) * 2) // (BF16_SUBLANES * d)))


def _params(sem=("arbitrary",), vmem=VMEM_LIMIT):
    return pltpu.CompilerParams(dimension_semantics=sem, vmem_limit_bytes=vmem)


def _mesh_pos():
    return lax.axis_index("x"), lax.axis_index("y"), lax.axis_index("c")


def _resident(shape):
    zeros = (0,) * len(shape)
    return pl.BlockSpec(shape, lambda *_: zeros, pipeline_mode=pl.Buffered(1))


class _TwoLevelGather:
    def __init__(self, shard_refs, out_refs, send_sems, recv_sems, local_sems):
        self.n = len(shard_refs)
        self.shard_refs, self.out_refs = shard_refs, out_refs
        self.send_sems, self.recv_sems, self.local_sems = send_sems, recv_sems, local_sems
        x, y, c = _mesh_pos()
        self.c = c
        self.me, self.sibling = (x, y, c), (x, y, 1 - c)
        self.near = [(1 - x, y), (x, 1 - y)]
        self.far = (1 - x, 1 - y)
        self.relay_from = (jnp.where(c == 0, 1 - x, x), jnp.where(c == 0, y, 1 - y))
        self.relay_to = (jnp.where(c == 0, x, 1 - x), jnp.where(c == 0, 1 - y, y))

    def _slot(self, a, px, py, pc):
        rows = self.shard_refs[a].shape[0]
        return self.out_refs[a].at[pl.ds((4 * px + 2 * py + pc) * rows, rows), :]

    def _copy(self, a, k, block, to, src=None):
        return pltpu.make_async_remote_copy(
            src_ref=self._slot(a, *block) if src is None else src, dst_ref=self._slot(a, *block),
            send_sem=self.send_sems.at[7 * a + k], recv_sem=self.recv_sems.at[7 * a + k], device_id=to, device_id_type=MESH)

    def _mine(self):
        return [pltpu.make_async_copy(self.shard_refs[a], self._slot(a, *self.me), self.local_sems.at[a]) for a in range(self.n)]

    def _first(self):
        out = []
        for a in range(self.n):
            out.append(self._copy(a, 0, self.me, self.sibling, src=self.shard_refs[a]))
            out += [self._copy(a, 1 + j, self.me, (*chip, self.c), src=self.shard_refs[a]) for j, chip in enumerate(self.near)]
        return out

    def _passed(self, a):
        return ([self._copy(a, 3 + j, (*chip, self.c), self.sibling) for j, chip in enumerate(self.near)]
                + [self._copy(a, 6, (*self.relay_from, self.c), (*self.relay_to, self.c))])

    def _far_passed(self, a):
        return self._copy(a, 5, (*self.far, self.c), self.sibling)

    def start(self):
        for cp in self._mine() + self._first():
            cp.start()

    def forward(self):
        for a in range(self.n):
            for j, chip in enumerate(self.near):
                self._copy(a, 1 + j, (*chip, self.c), self.me).wait_recv()
            for cp in self._passed(a):
                cp.start()

    def forward_relayed(self):
        for a in range(self.n):
            self._copy(a, 6, (*self.far, self.c), self.me).wait_recv()
            self._far_passed(a).start()

    def finish(self):
        for a in range(self.n):
            self._copy(a, 0, self.sibling, self.me).wait_recv()
            for j, chip in enumerate(self.near):
                self._copy(a, 3 + j, (*chip, 1 - self.c), self.me).wait_recv()
            self._copy(a, 5, (*self.far, 1 - self.c), self.me).wait_recv()
        for cp in self._first():
            cp.wait_send()
        for a in range(self.n):
            for cp in self._passed(a) + [self._far_passed(a)]:
                cp.wait_send()
        for cp in self._mine():
            cp.wait()

    @staticmethod
    def out_shapes(shards):
        return tuple(jax.ShapeDtypeStruct((N_DEV * s.shape[0], s.shape[1]), s.dtype) for s in shards)

    @staticmethod
    def semaphores(n):
        return [pltpu.SemaphoreType.DMA((7 * n,)), pltpu.SemaphoreType.DMA((7 * n,)), pltpu.SemaphoreType.DMA((n,))]


def _all_gather(shards):
    n = len(shards)

    def body(*refs):
        gather = _TwoLevelGather(refs[:n], refs[n:2 * n], *refs[2 * n:])
        gather.start()
        gather.forward()
        gather.forward_relayed()
        gather.finish()

    return pl.pallas_call(
        body, name="all_gather_w_in", out_shape=_TwoLevelGather.out_shapes(shards),
        in_specs=[HBM_SPEC] * n, out_specs=(HBM_SPEC,) * n, scratch_shapes=_TwoLevelGather.semaphores(n),
    )(*shards)


class _Exchange:
    def __init__(self, src_refs, recv_refs, rows, send_sems, recv_sems, local_sems):
        x, y, c = _mesh_pos()
        me = 4 * x + 2 * y + c
        self.own, self.sends, self.arrivals = [], [], []
        for a, (src, recv) in enumerate(zip(src_refs, recv_refs)):
            def blk(k, src=src, r=rows[a]):
                return src if r is None else src.at[pl.ds(k * r, r), :]

            self.own.append(pltpu.make_async_copy(blk(me), recv.at[me], local_sems.at[a]))
            for rel in range(1, N_DEV):
                px = 1 - x if rel & 4 else x
                py = 1 - y if rel & 2 else y
                pc = 1 - c if rel & 1 else c
                peer = 4 * px + 2 * py + pc
                sem = dict(send_sem=send_sems.at[7 * a + rel - 1], recv_sem=recv_sems.at[7 * a + rel - 1],
                           device_id=(px, py, pc), device_id_type=MESH)
                self.sends.append(pltpu.make_async_remote_copy(src_ref=blk(peer), dst_ref=recv.at[me], **sem))
                self.arrivals.append(pltpu.make_async_remote_copy(src_ref=blk(me), dst_ref=recv.at[peer], **sem))

    def start(self):
        for cp in self.own + self.sends:
            cp.start()

    def finish(self):
        for cp in self.arrivals:
            cp.wait_recv()
        for cp in self.sends:
            cp.wait_send()
        for cp in self.own:
            cp.wait()

    @staticmethod
    def out_shapes(srcs, rows):
        return tuple(jax.ShapeDtypeStruct((N_DEV, s.shape[0] if r is None else r, s.shape[1]), s.dtype) for s, r in zip(srcs, rows))

    @staticmethod
    def semaphores(n):
        return [pltpu.SemaphoreType.DMA((7 * n,)), pltpu.SemaphoreType.DMA((7 * n,)), pltpu.SemaphoreType.DMA((n,))]


def _sum_slots(slots, tile_rows, name):
    _, rows, cols = slots.shape
    tr = rows
    if N_DEV * rows * cols * slots.dtype.itemsize > 8 * 1024 * 1024:
        tr = next(c for c in (256, 192, 128, 64, 32, 16) if c <= tile_rows and rows % c == 0)

    def body(s_ref, o_ref):
        acc = s_ref[0].astype(F32)
        for k in range(1, N_DEV):
            acc = acc + s_ref[k].astype(F32)
        o_ref[...] = acc

    return pl.pallas_call(
        body, name=name, grid=(rows // tr,),
        in_specs=[pl.BlockSpec((N_DEV, tr, cols), lambda i: (0, i, 0))],
        out_specs=pl.BlockSpec((tr, cols), lambda i: (i, 0)),
        out_shape=jax.ShapeDtypeStruct((rows, cols), F32),
        compiler_params=_params(),
    )(slots)


def _gelu_and_grad(x):
    k0 = math.sqrt(2.0 / math.pi)
    k1 = 0.044715
    a = k1 * (x * x)
    half = 1.0 / (1.0 + jnp.exp((-2.0 * k0) * x * (1.0 + a)))
    g = x * half
    return g, half + g * (1.0 - half) * (2.0 * k0 + (6.0 * k0) * a)


def _ln_stats(r):
    mu = jnp.mean(r, axis=-1, keepdims=True)
    rc = r - mu
    var = jnp.mean(rc * rc, axis=-1, keepdims=True)
    rstd = lax.rsqrt(var + LN_EPS)
    return rc * rstd, rstd


def _ln_bwd(dxh, xh, rstd):
    return rstd * (dxh - jnp.mean(dxh, axis=-1, keepdims=True) - xh * jnp.mean(dxh * xh, axis=-1, keepdims=True))


def _colsum(a):
    return jnp.sum(a, axis=0, keepdims=True)


def _mm(a, b):
    return jnp.dot(a, b, preferred_element_type=F32)


def _halo_maps(tm, t, unit):
    per, last = tm // unit, t // unit - 1
    return (lambda i: (jnp.maximum(i * per - 1, 0), 0)), (lambda i: (jnp.minimum((i + 1) * per, last), 0))


def _proj_in(x2, w_int, b_gate, shards, dm, tm):
    t, d, npj = dm.T, dm.D, dm.NP
    cw = d // 2
    nt = t // tm
    n = len(shards)

    def body(x_ref, w_ref, bg_ref, *refs):
        p_ref = refs[n]
        gather = _TwoLevelGather(refs[:n], refs[n + 1:2 * n + 1], *refs[2 * n + 1:])
        i = pl.program_id(0)

        @pl.when(i == 0)
        def _():
            gather.start()

        @pl.when(i == nt // 2)
        def _():
            gather.forward()

        @pl.when(i == 3 * nt // 4)
        def _():
            gather.forward_relayed()

        xb = x_ref[...].astype(BF16)
        for blk in range(npj // cw):
            lo = blk * cw
            acc = lax.dot_general(xb, w_ref[lo:lo + cw, :], NT_DIMS, preferred_element_type=F32)
            if lo >= dm.OFF_GA:
                acc = jax.nn.sigmoid(acc + bg_ref[:, lo - dm.OFF_GA:lo - dm.OFF_GA + cw])
            p_ref[:, lo:lo + cw] = acc.astype(BF16)

        @pl.when(i == nt - 1)
        def _():
            gather.finish()

    return pl.pallas_call(
        body, name="proj_in", grid=(nt,),
        in_specs=[pl.BlockSpec((tm, d), lambda i: (i, 0)), _resident((npj, d)), _resident((1, 2 * d))] + [HBM_SPEC] * n,
        out_specs=(pl.BlockSpec((tm, npj), lambda i: (i, 0)),) + (HBM_SPEC,) * n,
        out_shape=(jax.ShapeDtypeStruct((t, npj), BF16),) + _TwoLevelGather.out_shapes(shards),
        scratch_shapes=_TwoLevelGather.semaphores(n),
        compiler_params=_params(),
    )(x2, w_int, b_gate, *shards)


def _row_neighbours(ext, tm):
    h, n = F32_SUBLANES, ext.shape[0]
    return pltpu.roll(ext, 1, 0)[h:h + tm], pltpu.roll(ext, n - 1, 0)[h:h + tm]


def _spatial_mix(vn, ws_ref, bias_ref, mixed_ref, dm, tm):
    vb = vn.astype(BF16)
    for cc in range(tm // CHUNK):
        r0 = cc * CHUNK
        for h in range(dm.H):
            c0 = h * CHUNK
            m = _mm(ws_ref[h], vb[r0:r0 + CHUNK, c0:c0 + CHUNK])
            mixed_ref[r0:r0 + CHUNK, c0:c0 + CHUNK] = m + bias_ref[:, c0:c0 + CHUNK]
    return mixed_ref[...]


def _mixer_fwd(p, x2, w_pa, w_pb, w_o, conv_w8, vng, vnb, ws_b, bias_s, dm, tm):
    t, d, wa, npj = dm.T, dm.D, dm.WA, dm.NP
    nt = t // tm
    hb = BF16_SUBLANES
    prev_map, next_map = _halo_maps(tm, t, hb)

    def body(p_ref, pp_ref, pn_ref, x_ref, wpa_ref, wpb_ref, wo_ref, cw_ref, vng_ref, vnb_ref, ws_ref, bias_ref,
             ya_ref, yb_ref, r1_ref, gu_ref, dgu_ref, xhv_ref, rgv_ref, cv_ref, s_ref, mixed_ref):
        i = pl.program_id(0)

        def col(ref, lo, width):
            return ref[:, lo:lo + width].astype(F32)

        ch = col(p_ref, dm.OFF_CA, wa) * col(p_ref, dm.OFF_HA, wa)
        chp = (col(pp_ref, dm.OFF_CA, wa) * col(pp_ref, dm.OFF_HA, wa))[hb - F32_SUBLANES:hb]
        chn = (col(pn_ref, dm.OFF_CA, wa) * col(pn_ref, dm.OFF_HA, wa))[0:F32_SUBLANES]
        ch_e = jnp.concatenate([jnp.where(i == 0, 0.0, chp), ch, jnp.where(i == nt - 1, 0.0, chn)], axis=0)
        up, dn = _row_neighbours(ch_e, tm)
        cv = cw_ref[0:1, :] * up + cw_ref[1:2, :] * ch + cw_ref[2:3, :] * dn
        cv_ref[...] = cv.astype(BF16)
        ya = _mm((col(p_ref, 0, wa) * cv).astype(BF16), wpa_ref[...])
        gv, dgelu_v = _gelu_and_grad(col(p_ref, dm.OFF_VB, d))
        xhv, rstdv = _ln_stats(gv)
        xhv_ref[...] = xhv.astype(BF16)
        rgv_ref[...] = (rstdv * dgelu_v).astype(BF16)
        mixed = _spatial_mix(xhv * vng_ref[...] + vnb_ref[...], ws_ref, bias_ref, mixed_ref, dm, tm)
        gu, dgelu_u = _gelu_and_grad(col(p_ref, dm.OFF_UB, d))
        gu_ref[...] = gu.astype(BF16)
        dgu_ref[...] = dgelu_u.astype(BF16)
        yb = _mm((gu * mixed).astype(BF16), wpb_ref[...])
        sb = (col(p_ref, dm.OFF_GA, d) * ya + col(p_ref, dm.OFF_GB, d) * yb).astype(BF16)
        s_ref[...] = sb
        mix = _mm(sb, wo_ref[...])
        ya_ref[...] = ya.astype(BF16)
        yb_ref[...] = yb.astype(BF16)
        r1_ref[...] = ALPHA * x_ref[...] + mix

    tile = lambda i: (i, 0)
    return pl.pallas_call(
        body, name="mixer_fwd", grid=(nt,),
        in_specs=[pl.BlockSpec((tm, npj), tile), pl.BlockSpec((hb, npj), prev_map), pl.BlockSpec((hb, npj), next_map),
                  pl.BlockSpec((tm, d), tile), _resident((wa, d)), _resident((d, d)), _resident((d, d)),
                  _resident((F32_SUBLANES, wa)), _resident((1, d)), _resident((1, d)),
                  _resident((dm.H, CHUNK, CHUNK)), _resident((CHUNK, d))],
        out_specs=(pl.BlockSpec((tm, d), tile),) * 7 + (pl.BlockSpec((tm, wa), tile), pl.BlockSpec((tm, d), tile)),
        out_shape=(jax.ShapeDtypeStruct((t, d), BF16), jax.ShapeDtypeStruct((t, d), BF16), jax.ShapeDtypeStruct((t, d), F32))
        + (jax.ShapeDtypeStruct((t, d), BF16),) * 4 + (jax.ShapeDtypeStruct((t, wa), BF16), jax.ShapeDtypeStruct((t, d), BF16)),
        scratch_shapes=[pltpu.VMEM((tm, d), F32)],
        compiler_params=_params(),
    )(p, p, p, x2, w_pa, w_pb, w_o, conv_w8, vng, vnb, ws_b, bias_s)


HEAD_WINDOW = 2


def _after(value, earlier):
    bits = lax.shift_right_logical(lax.shift_right_logical(pltpu.bitcast(earlier, jnp.uint32), jnp.uint32(16)), jnp.uint32(16))
    return value + pltpu.bitcast(bits, F32)


def _ffn_fwd(r1, tgt, w1t, w2, ln1g, ln1b, ln2g, ln2b, dm, tm):
    t, d, dff = dm.T, dm.D, dm.DFF
    nt = t // tm
    fc = dff // N_DEV
    hr = tm // N_DEV

    def body(r1_ref, tgt_ref, w1t_ref, w2_ref, g1_ref, b1_ref, g2_ref, b2_ref, relu_ref, x1_ref, dr2_ref, dr2b_ref, sums_ref,
             r2_ref):
        i = pl.program_id(0)

        def head(c, live):
            rs = slice(c * hr, (c + 1) * hr)
            xh2, rstd2 = _ln_stats(r2_ref[rs, :])
            diff = xh2 * g2_ref[...] + b2_ref[...] - tgt_ref[rs, :]
            dy = diff * (1.0 / d)
            dr2 = _ln_bwd(dy * g2_ref[...], xh2, rstd2)
            dr2_ref[rs, :] = dr2
            dr2b_ref[rs, :] = dr2.astype(BF16)
            sums_ref[0:1, :] += live * _colsum(diff * diff)
            sums_ref[1:2, :] += live * _colsum(dy * xh2)
            sums_ref[2:3, :] += live * _colsum(dy)
            return dr2

        @pl.when(i == 0)
        def _():
            sums_ref[...] = jnp.zeros_like(sums_ref)
            r2_ref[...] = jnp.zeros_like(r2_ref)

        @pl.when(i < nt)
        def _():
            live = (i > 0).astype(F32)
            xh1, _ = _ln_stats(r1_ref[...])
            x1 = xh1 * g1_ref[...] + b1_ref[...]
            x1b = x1.astype(BF16)
            x1_ref[...] = x1b
            ffn = jnp.zeros((tm, d), F32)
            dones = []
            for k in range(N_DEV):
                ks = slice(k * fc, (k + 1) * fc)
                r = jnp.maximum(lax.dot_general(x1b, w1t_ref[ks, :], NT_DIMS, preferred_element_type=F32), 0.0)
                if k >= HEAD_WINDOW:
                    r = jnp.concatenate([_after(r[:hr, :], dones[k - HEAD_WINDOW][:, :fc]), r[hr:]], axis=0)
                relu_ref[:, ks] = r.astype(BF16)
                ffn = ffn + _mm((r * r).astype(BF16), w2_ref[ks, :])
                dones.append(head(k, live))
            r2_ref[...] = ALPHA * x1 + ffn

        @pl.when(i == nt)
        def _():
            for c in range(N_DEV):
                head(c, 1.0)

    tile = lambda i: (jnp.minimum(i, nt - 1), 0)
    late = lambda i: (jnp.maximum(i - 1, 0), 0)
    vec = _resident((1, d))
    return pl.pallas_call(
        body, name="ffn_fwd", grid=(nt + 1,),
        in_specs=[pl.BlockSpec((tm, d), tile), pl.BlockSpec((tm, d), late), _resident((dff, d)), _resident((dff, d)),
                  vec, vec, vec, vec],
        out_specs=(pl.BlockSpec((tm, dff), tile), pl.BlockSpec((tm, d), tile), pl.BlockSpec((tm, d), late),
                   pl.BlockSpec((tm, d), late), pl.BlockSpec((F32_SUBLANES, d), lambda i: (0, 0))),
        out_shape=(jax.ShapeDtypeStruct((t, dff), BF16), jax.ShapeDtypeStruct((t, d), BF16), jax.ShapeDtypeStruct((t, d), F32),
                   jax.ShapeDtypeStruct((t, d), BF16), jax.ShapeDtypeStruct((F32_SUBLANES, d), F32)),
        scratch_shapes=[pltpu.VMEM((tm, d), F32)],
        compiler_params=_params(),
    )(r1, tgt, w1t, w2, ln1g, ln1b, ln2g, ln2b)


def _ffn_bwd(r1, relu, dr2, w2, w1t, ln1g, dm, tm):
    t, d, dff = dm.T, dm.D, dm.DFF
    fc = dff // N_DEV

    def body(r1_ref, relu_ref, dr2_ref, w2_ref, w1t_ref, g1_ref, dr1_ref, dh_ref, sums_ref):
        @pl.when(pl.program_id(0) == 0)
        def _():
            sums_ref[...] = jnp.zeros_like(sums_ref)

        xh1, rstd1 = _ln_stats(r1_ref[...])
        dr2 = dr2_ref[...]
        dr2b = dr2.astype(BF16)
        dx1 = ALPHA * dr2
        for k in range(dff // fc):
            ks = slice(k * fc, (k + 1) * fc)
            dact = lax.dot_general(dr2b, w2_ref[ks, :], NT_DIMS, preferred_element_type=F32)
            dhb = (dact * (2.0 * relu_ref[:, ks].astype(F32))).astype(BF16)
            dh_ref[:, ks] = dhb
            dx1 = dx1 + _mm(dhb, w1t_ref[ks, :])
        dr1_ref[...] = _ln_bwd(dx1 * g1_ref[...], xh1, rstd1)
        sums_ref[0:1, :] += _colsum(dx1 * xh1)
        sums_ref[1:2, :] += _colsum(dx1)

    tile = lambda i: (i, 0)
    return pl.pallas_call(
        body, name="ffn_bwd", grid=(t // tm,),
        in_specs=[pl.BlockSpec((tm, d), tile), pl.BlockSpec((tm, dff), tile), pl.BlockSpec((tm, d), tile),
                  _resident((dff, d)), _resident((dff, d)), _resident((1, d))],
        out_specs=(pl.BlockSpec((tm, d), tile), pl.BlockSpec((tm, dff), tile), pl.BlockSpec((F32_SUBLANES, d), lambda i: (0, 0))),
        out_shape=(jax.ShapeDtypeStruct((t, d), F32), jax.ShapeDtypeStruct((t, dff), BF16),
                   jax.ShapeDtypeStruct((F32_SUBLANES, d), F32)),
        compiler_params=_params(),
    )(r1, relu, dr2, w2, w1t, ln1g)


def _wgrad(pairs, tt, fb, name, xchg=()):
    n, m = len(pairs), len(xchg)
    t, f = pairs[0][0].shape
    d = pairs[0][1].shape[1]
    squares = [sq for _, _, sq in pairs]
    nj, ni = f // fb, t // tt
    xrows = [r for _, r in xchg]

    def body(*refs):
        ins, xsrc = refs[:2 * n], refs[2 * n:2 * n + m]
        outs, xrecv = refs[2 * n + m:3 * n + m], refs[3 * n + m:3 * n + 2 * m]
        accs, sems = refs[3 * n + 2 * m:4 * n + 2 * m], refs[4 * n + 2 * m:]
        j, i = pl.program_id(0), pl.program_id(1)
        exchange = _Exchange(xsrc, xrecv, xrows, *sems) if m else None

        if m:
            @pl.when(jnp.logical_and(j == 0, i == 0))
            def _():
                exchange.start()

        @pl.when(i == 0)
        def _():
            for acc in accs:
                acc[...] = jnp.zeros_like(acc)

        for q in range(n):
            lhs = ins[2 * q][...]
            if squares[q]:
                lf = lhs.astype(F32)
                lhs = (lf * lf).astype(BF16)
            accs[q][...] += lax.dot_general(lhs, ins[2 * q + 1][...].astype(BF16), TN_DIMS, preferred_element_type=F32)

        @pl.when(i == ni - 1)
        def _():
            for q in range(n):
                outs[q][...] = accs[q][...].astype(BF16)

        if m:
            @pl.when(jnp.logical_and(j == nj - 1, i == ni - 1))
            def _():
                exchange.finish()

    lhs_spec = pl.BlockSpec((tt, fb), lambda j, i: (i, j))
    rhs_spec = pl.BlockSpec((tt, d), lambda j, i: (i, 0))
    out_spec = pl.BlockSpec((fb, d), lambda j, i: (j, 0))
    xsrcs = [a_ for a_, _ in xchg]
    return pl.pallas_call(
        body, name=name, grid=(nj, ni),
        in_specs=[lhs_spec, rhs_spec] * n + [HBM_SPEC] * m, out_specs=(out_spec,) * n + (HBM_SPEC,) * m,
        out_shape=(jax.ShapeDtypeStruct((f, d), BF16),) * n + _Exchange.out_shapes(xsrcs, xrows),
        scratch_shapes=[pltpu.VMEM((fb, d), F32)] * n + (_Exchange.semaphores(m) if m else []),
        compiler_params=_params(("arbitrary", "arbitrary")),
    )(*[a_ for lhs, rhs, _ in pairs for a_ in (lhs, rhs)], *xsrcs)


RING_SLOTS = 3


def _mixer_bwd(dr1, p, ya, yb, gu_s, dgu_s, xhv_s, rgv_s, cv_s, w_ot, w_pat, w_pbt, conv_w8, vng, vnb, ws_b, wst_b, bias_s, head_sel, xchg, dm, tm):
    t, d, wa, npj = dm.T, dm.D, dm.WA, dm.NP
    nt = t // tm
    h8, hb = F32_SUBLANES, BF16_SUBLANES
    ext = tm + 2 * h8
    prev_f, next_f = _halo_maps(tm, t, h8)
    prev_b, next_b = _halo_maps(tm, t, hb)
    nx = len(xchg)
    xsrcs, xrows = [a_ for a_, _ in xchg], [r for _, r in xchg]

    gw = d // 2

    def gate_map(rows, k):
        return lambda i: (rows(i)[0], dm.OFF_GA // gw + k)

    def body(dr_ref, drp_ref, drn_ref, p_hbm, pg0_ref, pg1_ref, pg2_ref, pg3_ref, ppc_ref, ppg0_ref, ppg1_ref,
             pnc_ref, png0_ref, png1_ref, ya_ref, yb_ref, gu_ref, dgu_ref, xhv_ref, rgv_ref, cv_ref,
             wot_ref, wpat_ref, wpbt_ref,
             cw_ref, vng_ref, vnb_ref, ws_ref, wst_ref, bias_ref, sel_ref,
             *refs):
        xsrc = refs[:nx]
        dp_ref, a_ref, dya_ref, bb_ref, dyb_ref, dws_ref, dbs_ref, dcw_ref, dbg_ref, dvn_ref = refs[nx:nx + 10]
        xrecv = refs[nx + 10:2 * nx + 10]
        mixed_ref, dvnm_ref, ring_ref, ring_sems = refs[2 * nx + 10:2 * nx + 14]
        exchange = _Exchange(xsrc, xrecv, xrows, *refs[2 * nx + 14:])
        i = pl.program_id(0)

        def fetch(j):
            slot = j % RING_SLOTS
            return pltpu.make_async_copy(p_hbm.at[pl.ds(j * tm, tm), pl.ds(0, 3 * wa)], ring_ref.at[slot], ring_sems.at[slot])

        @pl.when(i == 0)
        def _():
            exchange.start()
            for j in range(min(RING_SLOTS - 1, nt)):
                fetch(j).start()
            for ref in (dws_ref, dbs_ref, dcw_ref, dbg_ref, dvn_ref):
                ref[...] = jnp.zeros_like(ref)

        @pl.when(i + RING_SLOTS - 1 < nt)
        def _():
            fetch(i + RING_SLOTS - 1).start()

        fetch(i).wait()
        pc_ref = ring_ref.at[i % RING_SLOTS]

        def col(ref, lo, width):
            return ref[:, lo:lo + width].astype(F32)

        def ext_rows(prev_blk, center, next_blk):
            return jnp.concatenate([prev_blk, center, next_blk], axis=0)

        def lanes(*refs):
            return jnp.concatenate([r[...].astype(F32) for r in refs], axis=1)

        xhv = xhv_ref[...].astype(F32)
        vn = xhv * vng_ref[...] + vnb_ref[...]
        mixed = _spatial_mix(vn, ws_ref, bias_ref, mixed_ref, dm, tm)
        gu = gu_ref[...].astype(F32)
        bb_ref[...] = (gu * mixed).astype(BF16)
        g_a, g_b = lanes(pg0_ref, pg1_ref), lanes(pg2_ref, pg3_ref)
        y_a, y_b = ya_ref[...].astype(F32), yb_ref[...].astype(F32)
        qa = y_a * g_a * (1.0 - g_a)
        qb = y_b * g_b * (1.0 - g_b)
        cv = cv_ref[...].astype(F32)
        b_a, c_a, h_a = col(pc_ref, 0, wa), col(pc_ref, dm.OFF_CA, wa), col(pc_ref, dm.OFF_HA, wa)
        a_ref[...] = (b_a * cv).astype(BF16)
        ch = c_a * h_a
        row = lax.broadcasted_iota(jnp.int32, (ext, 1), 0) + (i * tm - h8)
        inside = jnp.logical_and(row >= 0, row < t)
        dr_e = ext_rows(drp_ref[...], dr_ref[...], drn_ref[...])
        ds_e = _mm(dr_e.astype(BF16), wot_ref[...])
        dya_e = ds_e * ext_rows(lanes(ppg0_ref, ppg1_ref)[hb - h8:hb], g_a, lanes(png0_ref, png1_ref)[0:h8])
        da_e = _mm(dya_e.astype(BF16), wpat_ref[...])
        dcv_e = jnp.where(inside, da_e * ext_rows(col(ppc_ref, 0, wa)[hb - h8:hb], b_a, col(pnc_ref, 0, wa)[0:h8]), 0.0)
        dcv, (dcv_up, dcv_dn) = dcv_e[h8:h8 + tm], _row_neighbours(dcv_e, tm)
        w0, w1, w2 = cw_ref[0:1, :], cw_ref[1:2, :], cw_ref[2:3, :]
        dp_ref[:, 0:wa] = (da_e[h8:h8 + tm] * cv).astype(BF16)
        dch = w0 * dcv_dn + w1 * dcv + w2 * dcv_up
        dp_ref[:, dm.OFF_CA:dm.OFF_CA + wa] = (dch * h_a).astype(BF16)
        dp_ref[:, dm.OFF_HA:dm.OFF_HA + wa] = (dch * c_a).astype(BF16)
        dcw_ref[0:1, :] += _colsum(dcv_dn * ch)
        dcw_ref[1:2, :] += _colsum(dcv * ch)
        dcw_ref[2:3, :] += _colsum(dcv_up * ch)
        dya_ref[...] = dya_e[h8:h8 + tm].astype(BF16)
        ds = ds_e[h8:h8 + tm]
        dzga = ds * qa
        dzgb = ds * qb
        dp_ref[:, dm.OFF_GA:dm.OFF_GA + d] = dzga.astype(BF16)
        dp_ref[:, dm.OFF_GB:dm.OFF_GB + d] = dzgb.astype(BF16)
        dbg_ref[0:1, 0:d] += _colsum(dzga)
        dbg_ref[0:1, d:2 * d] += _colsum(dzgb)
        dyb = (ds * g_b).astype(BF16)
        dyb_ref[...] = dyb
        dbb = _mm(dyb, wpbt_ref[...])
        dp_ref[:, dm.OFF_UB:dm.OFF_UB + d] = (dbb * mixed * dgu_ref[...].astype(F32)).astype(BF16)
        dmb = (dbb * gu).astype(BF16)
        vb = vn.astype(BF16)
        dbs = jnp.zeros((CHUNK, CHUNK), F32)
        for cc in range(tm // CHUNK):
            r0 = cc * CHUNK
            dbs = dbs + _mm(dmb[r0:r0 + CHUNK, :], sel_ref[...])
            for h in range(dm.H):
                c0 = h * CHUNK
                blk = dmb[r0:r0 + CHUNK, c0:c0 + CHUNK]
                dvnm_ref[r0:r0 + CHUNK, c0:c0 + CHUNK] = _mm(wst_ref[h], blk)
                dws_ref[h] += lax.dot_general(blk, vb[r0:r0 + CHUNK, c0:c0 + CHUNK], NT_DIMS, preferred_element_type=F32)
        dbs_ref[...] += dbs
        dvn = dvnm_ref[...]
        dvn_ref[0:1, :] += _colsum(dvn * xhv)
        dvn_ref[1:2, :] += _colsum(dvn)
        dp_ref[:, dm.OFF_VB:dm.OFF_VB + d] = (_ln_bwd(dvn * vng_ref[...], xhv, rgv_ref[...].astype(F32))).astype(BF16)

        @pl.when(i == nt - 1)
        def _():
            exchange.finish()

    full = lambda i: (0, 0)
    tile = lambda i: (i, 0)
    hcc = _resident((dm.H, CHUNK, CHUNK))
    tok = lambda w, dt: jax.ShapeDtypeStruct((t, w), dt)
    return pl.pallas_call(
        body, name="mixer_bwd", grid=(nt,),
        in_specs=[pl.BlockSpec((tm, d), tile), pl.BlockSpec((h8, d), prev_f), pl.BlockSpec((h8, d), next_f),
                  HBM_SPEC] + [pl.BlockSpec((tm, gw), gate_map(tile, k)) for k in range(4)]
        + [pl.BlockSpec((hb, wa), prev_b)] + [pl.BlockSpec((hb, gw), gate_map(prev_b, k)) for k in range(2)]
        + [pl.BlockSpec((hb, wa), next_b)] + [pl.BlockSpec((hb, gw), gate_map(next_b, k)) for k in range(2)]
        + [pl.BlockSpec((tm, d), tile)] * 6
        + [pl.BlockSpec((tm, wa), tile), _resident((d, d)), _resident((d, wa)), _resident((d, d)),
           _resident((h8, wa)), _resident((1, d)), _resident((1, d)), hcc, hcc, _resident((CHUNK, d)),
           _resident((d, CHUNK))] + [HBM_SPEC] * nx,
        out_specs=(pl.BlockSpec((tm, npj), tile), pl.BlockSpec((tm, wa), tile), pl.BlockSpec((tm, d), tile),
                   pl.BlockSpec((tm, d), tile), pl.BlockSpec((tm, d), tile),
                   pl.BlockSpec((dm.H, CHUNK, CHUNK), lambda i: (0, 0, 0)), pl.BlockSpec((CHUNK, CHUNK), full),
                   pl.BlockSpec((h8, wa), full), pl.BlockSpec((h8, 2 * d), full), pl.BlockSpec((h8, d), full))
        + (HBM_SPEC,) * nx,
        out_shape=(tok(npj, BF16), tok(wa, BF16), tok(d, BF16), tok(d, BF16), tok(d, BF16),
                   jax.ShapeDtypeStruct((dm.H, CHUNK, CHUNK), F32), jax.ShapeDtypeStruct((CHUNK, CHUNK), F32),
                   jax.ShapeDtypeStruct((h8, wa), F32), jax.ShapeDtypeStruct((h8, 2 * d), F32),
                   jax.ShapeDtypeStruct((h8, d), F32)) + _Exchange.out_shapes(xsrcs, xrows),
        scratch_shapes=[pltpu.VMEM((tm, d), F32), pltpu.VMEM((tm, d), F32), pltpu.VMEM((RING_SLOTS, tm, 3 * wa), BF16),
                        pltpu.SemaphoreType.DMA((RING_SLOTS,))] + _Exchange.semaphores(nx),
        compiler_params=_params(),
    )(dr1, dr1, dr1, *([p] * 11), ya, yb, gu_s, dgu_s, xhv_s, rgv_s, cv_s, w_ot, w_pat, w_pbt, conv_w8, vng, vnb, ws_b, wst_b, bias_s, head_sel, *xsrcs)


def _input_grad(dp, dr1, w_int, xchg, dm, tm):
    t, d, npj = dm.T, dm.D, dm.NP
    nt = t // tm
    nx = len(xchg)
    xsrcs, xrows = [a_ for a_, _ in xchg], [r for _, r in xchg]

    def body(dp_ref, dr_ref, w_ref, *refs):
        dx_ref = refs[nx]
        exchange = _Exchange(refs[:nx], refs[nx + 1:2 * nx + 1], xrows, *refs[2 * nx + 1:])
        i = pl.program_id(0)

        @pl.when(i == 0)
        def _():
            exchange.start()

        dx_ref[...] = ALPHA * dr_ref[...] + _mm(dp_ref[...], w_ref[...])

        @pl.when(i == nt - 1)
        def _():
            exchange.finish()

    return pl.pallas_call(
        body, name="input_grad", grid=(nt,),
        in_specs=[pl.BlockSpec((tm, npj), lambda i: (i, 0)), pl.BlockSpec((tm, d), lambda i: (i, 0)), _resident((npj, d))]
        + [HBM_SPEC] * nx,
        out_specs=(pl.BlockSpec((tm, d), lambda i: (i, 0)),) + (HBM_SPEC,) * nx,
        out_shape=(jax.ShapeDtypeStruct((t, d), F32),) + _Exchange.out_shapes(xsrcs, xrows),
        scratch_shapes=_Exchange.semaphores(nx),
        compiler_params=_params(),
    )(dp, dr1, w_int, *xsrcs)


def _adamw_math(w, g, m, v):
    nm = ADAM_B1 * m + (1.0 - ADAM_B1) * g
    nv = ADAM_B2 * v + (1.0 - ADAM_B2) * (g * g)
    delta = -ADAM_LR * ((nm / (1.0 - ADAM_B1 ** ADAM_STEP)) / (jnp.sqrt(nv / (1.0 - ADAM_B2 ** ADAM_STEP)) + ADAM_EPS) + ADAM_WD * w)
    return delta, nm, nv


def _adamw(w, g, m, v, name):
    _, rows, cols = w.shape
    tr = 256 if rows % 256 == 0 else rows
    from_slots = g.ndim == 3

    def body(w_ref, g_ref, m_ref, v_ref, go_ref, d_ref, nm_ref, nv_ref):
        if from_slots:
            g_ = g_ref[0].astype(F32)
            for k in range(1, N_DEV):
                g_ = g_ + g_ref[k].astype(F32)
        else:
            g_ = g_ref[...]
        go_ref[0] = g_
        d_ref[0], nm_ref[0], nv_ref[0] = _adamw_math(w_ref[0], g_, m_ref[0], v_ref[0])

    spec = pl.BlockSpec((1, tr, cols), lambda i: (0, i, 0))
    g_spec = pl.BlockSpec((N_DEV, tr, cols), lambda i: (0, i, 0)) if from_slots else pl.BlockSpec((tr, cols), lambda i: (i, 0))
    shp = jax.ShapeDtypeStruct((1, rows, cols), F32)
    return pl.pallas_call(
        body, name=name, grid=(rows // tr,), in_specs=[spec, g_spec, spec, spec], out_specs=(spec,) * 4, out_shape=(shp,) * 4,
        compiler_params=_params(),
    )(w, g, m, v)


def _adamw_small(ws, gs, ms, vs):
    n = len(ws)

    def body(*refs):
        ins, outs = refs[:4 * n], refs[4 * n:]
        for k in range(n):
            w_ref, g_ref, m_ref, v_ref = ins[k], ins[n + k], ins[2 * n + k], ins[3 * n + k]
            outs[k][...], outs[n + k][...], outs[2 * n + k][...] = _adamw_math(w_ref[...], g_ref[...], m_ref[...], v_ref[...])

    shapes = tuple(jax.ShapeDtypeStruct(w.shape, F32) for w in ws)
    out = pl.pallas_call(body, name="adamw_small", out_shape=shapes * 3, compiler_params=_params(()))(*ws, *gs, *ms, *vs)
    return out[:n], out[n:2 * n], out[2 * n:]


def _to_slab(parts):
    flat = jnp.concatenate([q.reshape(-1) for q in parts])
    pad = (-flat.shape[0]) % (F32_SUBLANES * LANES)
    return jnp.pad(flat, (0, pad)).reshape(-1, LANES)


def _from_slab(slab, shapes):
    flat = slab.reshape(-1)
    out, off = [], 0
    for s in shapes:
        n = math.prod(s)
        out.append(flat[off:off + n].reshape(s))
        off += n
    return out


def kernel(x, w_in, b_gate, conv_w, v_norm_g, v_norm_b, w_s, b_s, w_pa, w_pb, w_o, ln1_g, ln1_b, w_ff1, w_ff2, ln2_g, ln2_b, loss_target, m_w_in, m_b_gate, m_conv_w, m_v_norm_g, m_v_norm_b, m_w_s, m_b_s, m_w_pa, m_w_pb, m_w_o, m_ln1_g, m_ln1_b, m_w_ff1, m_w_ff2, m_ln2_g, m_ln2_b, v_w_in, v_b_gate, v_conv_w, v_v_norm_g, v_v_norm_b, v_w_s, v_b_s, v_w_pa, v_w_pb, v_w_o, v_ln1_g, v_ln1_b, v_w_ff1, v_w_ff2, v_ln2_g, v_ln2_b):
    t, d = x.shape[1], x.shape[2]
    dm = _Dims(t, d)
    tm = 256 if t % 256 == 0 else CHUNK
    tm_big = 512 if t % 512 == 0 else tm
    tt = 1024 if t % 1024 == 0 else tm
    me = 4 * lax.axis_index("x") + 2 * lax.axis_index("y") + lax.axis_index("c")
    x2, tgt = x[0], loss_target[0]

    conv_bits = lax.bitcast_convert_type(conv_w[0], BF16).reshape(-1)
    conv_blk = jnp.pad(conv_bits, (0, dm.conv_rows * d - conv_bits.shape[0])).reshape(dm.conv_rows, d)
    w_int, conv_g = _all_gather([w_in[0].T.astype(BF16), conv_blk])
    wa8 = dm.WA // N_DEV
    conv_all = lax.bitcast_convert_type(conv_g.reshape(N_DEV, -1)[:, :3 * wa8 * 2].reshape(N_DEV, 3, wa8, 2), F32)
    conv_full = jnp.transpose(conv_all, (1, 0, 2)).reshape(3, dm.WA)
    conv_w8 = jnp.pad(conv_full, ((0, F32_SUBLANES - 3), (0, 0)))
    ws_b = w_s[0].astype(BF16)
    wst_b = jnp.transpose(w_s[0], (0, 2, 1)).astype(BF16)
    bias_s = jnp.repeat(b_s[0].T, CHUNK, axis=1)
    head_sel = (jnp.arange(d)[:, None] // CHUNK == jnp.arange(CHUNK)[None, :]).astype(BF16)

    p, w_pa_f, w_pb_f, w_o_f, w_1t, w_2 = _proj_in(
        x2, w_int, b_gate, [w_pa[0].astype(BF16), w_pb[0].astype(BF16), w_o[0].astype(BF16), w_ff1[0].T.astype(BF16),
                            w_ff2[0].astype(BF16)], dm, tm_big)
    ya, yb, r1, gu_s, dgu_s, xhv_s, rgv_s, cv_s, s_m = _mixer_fwd(
        p, x2, w_pa_f, w_pb_f, w_o_f, conv_w8, v_norm_g, v_norm_b, ws_b, bias_s, dm, tm)
    relu, x1b, dr2, dr2b, sums2 = _ffn_fwd(r1, tgt, w_1t, w_2, ln1_g, ln1_b, ln2_g, ln2_b, dm, tm_big)
    dr1, dh1, sums1 = _ffn_bwd(r1, relu, dr2, w_2, w_1t, ln1_g, dm, tm_big)
    fb = min(1024, dm.DFF)
    rows = dm.shard_rows
    g_ff1t, g_ff2 = _wgrad([(dh1, x1b, False), (relu, dr2b, True)], tt, fb, "ffn_wgrad")
    dp, a_m, dya, bb_m, dyb, g_ws, g_bs_t, g_cw, g_bg, g_vn, got_ff1t, got_ff2 = _mixer_bwd(
        dr1, p, ya, yb, gu_s, dgu_s, xhv_s, rgv_s, cv_s, w_o_f.T, w_pa_f.T, w_pb_f.T, conv_w8, v_norm_g, v_norm_b, ws_b, wst_b, bias_s, head_sel,
        [(g_ff1t, rows[4]), (g_ff2, rows[5])], dm, tm)
    (g_pa,) = _wgrad([(a_m, dya, False)], tt, dm.WA, "w_pa_grad")
    g_o, g_pb = _wgrad([(s_m, dr1, False), (bb_m, dyb, False)], tt, d, "w_o_pb_grad")
    g_int, got_pa, got_pb, got_o = _wgrad([(dp, x2, False)], tt, 17 * LANES, "w_in_grad",
                                          xchg=[(g_pa, rows[1]), (g_pb, rows[2]), (g_o, rows[3])])
    small_parts = [g_bg[0], g_cw[0:3], g_vn[0], g_vn[1], g_ws, g_bs_t[:, :dm.H].T,
                   sums1[0], sums1[1], sums2[1], sums2[2], sums2[0]]
    grad_x, got_int, got_s = _input_grad(dp, dr1, w_int, [(g_int, rows[0]), (_to_slab(small_parts), None)], dm, tm_big)

    ssum = _sum_slots(got_s, 256, "sum_small")
    (s_bg, s_cw, s_vng, s_vnb, s_ws, s_bs, s_l1g, s_l1b, s_l2g, s_l2b, s_sq) = _from_slab(
        ssum, [(1, 2 * d), (3, dm.WA), (1, d), (1, d), (1, dm.H, CHUNK, CHUNK), (1, dm.H, CHUNK),
               (1, d), (1, d), (1, d), (1, d), (d,)])
    loss = 0.5 * jnp.sum(s_sq) / d
    s_cw = lax.dynamic_slice(s_cw, (0, me * wa8), (3, wa8))[None]
    big_g = [_sum_slots(got_int, 256, "sum_grads_w_in").T, got_pa, got_pb, got_o,
             _sum_slots(got_ff1t, 256, "sum_grads_w_ff1").T, got_ff2]
    big_w = [(w_in, m_w_in, v_w_in), (w_pa, m_w_pa, v_w_pa), (w_pb, m_w_pb, v_w_pb), (w_o, m_w_o, v_w_o),
             (w_ff1, m_w_ff1, v_w_ff1), (w_ff2, m_w_ff2, v_w_ff2)]
    big_out = [_adamw(w, g, m, v, "adamw_%d" % k) for k, ((w, m, v), g) in enumerate(zip(big_w, big_g))]
    small_w = [(b_gate, m_b_gate, v_b_gate), (conv_w, m_conv_w, v_conv_w), (v_norm_g, m_v_norm_g, v_v_norm_g),
               (v_norm_b, m_v_norm_b, v_v_norm_b), (w_s, m_w_s, v_w_s), (b_s, m_b_s, v_b_s), (ln1_g, m_ln1_g, v_ln1_g),
               (ln1_b, m_ln1_b, v_ln1_b), (ln2_g, m_ln2_g, v_ln2_g), (ln2_b, m_ln2_b, v_ln2_b)]
    small_g = [s_bg, s_cw, s_vng, s_vnb, s_ws, s_bs, s_l1g, s_l1b, s_l2g, s_l2b]
    small_out = list(zip(*_adamw_small([w for w, _, _ in small_w], small_g, [m for _, m, _ in small_w],
                                       [v for _, _, v in small_w])))

    order = [("b", 0), ("s", 0), ("s", 1), ("s", 2), ("s", 3), ("s", 4), ("s", 5), ("b", 1), ("b", 2), ("b", 3),
             ("s", 6), ("s", 7), ("b", 4), ("b", 5), ("s", 8), ("s", 9)]
    grads, deltas, new_m, new_v = [], [], [], []
    for kind, k in order:
        if kind == "b":
            g, dl, nm, nv = big_out[k]
        else:
            g, (dl, nm, nv) = small_g[k], small_out[k]
        grads.append(g)
        deltas.append(dl)
        new_m.append(nm)
        new_v.append(nv)
    return (loss, grad_x[None], *grads, *deltas, *new_m, *new_v)
```
